```python
import jax, jax.numpy as jnp
from jax import lax
import numpy as np

D_MODEL = 2048
BATCH = 4
SEQ = 2048
DEPTH = 1

N_META = 16
GRID_W = 64
MIX_WIDTH = D_MODEL
N_HEADS = 16
HEAD_DIM = 64
ATTN_WIDTH = N_HEADS * HEAD_DIM
CONV_WIDTH = MIX_WIDTH - ATTN_WIDTH
CONV_K = 3
WIN_ROWS = 8
WIN_COLS = 16
N_GROUPS = 4
EXPERTS_PER_GROUP = 8
N_EXPERTS = N_GROUPS * EXPERTS_PER_GROUP
TOP_K = 2
EXPERT_FF = 1024
MOE_BLOCK = 128
EPS = 1e-6

PROJ_WIDTHS = (ATTN_WIDTH, ATTN_WIDTH, ATTN_WIDTH, CONV_WIDTH, CONV_WIDTH, CONV_WIDTH)
PROJ_SPLITS = tuple(int(s) for s in np.cumsum(PROJ_WIDTHS)[:-1])
PROJ_TOTAL = int(sum(PROJ_WIDTHS))

kernel_name = "hymba_na_shortconv_hmoe_encoder"


def rms_norm(x, w):
    xf = x.astype(jnp.float32)
    y = xf * lax.rsqrt(jnp.mean(xf * xf, axis=-1, keepdims=True) + EPS)
    return (y * w.astype(jnp.float32)).astype(x.dtype)


def neighbourhood_attention(q, k, v, rel_bias, meta_bias):
    bsz, length, nh, hd = q.shape
    rows = (length - N_META) // GRID_W
    wr = min(WIN_ROWS, rows)
    scale = hd ** -0.5
    qf = q.astype(jnp.float32) * scale
    kf = k.astype(jnp.float32)
    vf = v.astype(jnp.float32)
    qm, km, vm = qf[:, :N_META], kf[:, :N_META], vf[:, :N_META]
    qg = qf[:, N_META:].reshape(bsz, rows, GRID_W, nh, hd)
    kg = kf[:, N_META:].reshape(bsz, rows, GRID_W, nh, hd)
    vg = vf[:, N_META:].reshape(bsz, rows, GRID_W, nh, hd)

    r = jnp.arange(rows)
    c = jnp.arange(GRID_W)
    row_start = jnp.clip(r - wr // 2, 0, rows - wr)
    row_idx = row_start[:, None] + jnp.arange(wr)[None, :]
    col_start = jnp.clip(c - WIN_COLS // 2, 0, GRID_W - WIN_COLS)
    col_mask = (c[None, :] >= col_start[:, None]) & (c[None, :] < col_start[:, None] + WIN_COLS)

    k_rows = kg[:, row_idx]
    v_rows = vg[:, row_idx]

    dr = row_idx - r[:, None] + (WIN_ROWS - 1)
    dc = jnp.clip(c[None, :] - c[:, None], -(WIN_COLS - 1), WIN_COLS - 1) + (WIN_COLS - 1)
    bias = rel_bias.astype(jnp.float32)[:, dr[:, None, :, None], dc[None, :, None, :]]
    bias = jnp.where(col_mask[None, None, :, None, :], bias, -jnp.inf)

    s_grid = jnp.einsum('brqhd,brwkhd->bhrqwk', qg, k_rows) + bias[None]
    s_meta = jnp.einsum('brqhd,bmhd->bhrqm', qg, km) + meta_bias.astype(jnp.float32)[None, :, None, None, :]
    s = jnp.concatenate([s_grid.reshape(bsz, nh, rows, GRID_W, wr * GRID_W), s_meta], axis=-1)
    p = jax.nn.softmax(s, axis=-1)
    p_grid = p[..., :wr * GRID_W].reshape(bsz, nh, rows, GRID_W, wr, GRID_W)
    p_meta = p[..., wr * GRID_W:]
    o_grid = (jnp.einsum('bhrqwk,brwkhd->brqhd', p_grid, v_rows)
              + jnp.einsum('bhrqm,bmhd->brqhd', p_meta, vm))
    o_grid = o_grid.reshape(bsz, rows * GRID_W, nh, hd)

    p_mm = jax.nn.softmax(jnp.einsum('bmhd,bnhd->bhmn', qm, km), axis=-1)
    o_meta = jnp.einsum('bhmn,bnhd->bmhd', p_mm, vm)
    return jnp.concatenate([o_meta, o_grid], axis=1).astype(q.dtype)


def short_conv_mixer(gate_b, gate_c, h, conv_w):
    u = gate_c * h
    up = jnp.pad(u, ((0, 0), (1, 1), (0, 0)))
    y = up[:, :-2] * conv_w[0] + up[:, 1:-1] * conv_w[1] + up[:, 2:] * conv_w[2]
    return gate_b * y


def hierarchical_moe(x, w_rg, b_rg, w_re, b_re, w_gate, w_up, w_down):
    t, d = x.shape
    xf = x.astype(jnp.float32)
    g_prob = jax.nn.softmax(xf @ w_rg.astype(jnp.float32) + b_rg.astype(jnp.float32), axis=-1)
    g_idx = jnp.argmax(g_prob, axis=-1).astype(jnp.int32)
    g_w = jnp.max(g_prob, axis=-1)
    e_logits = (xf @ w_re.astype(jnp.float32) + b_re.astype(jnp.float32)).reshape(t, N_GROUPS, EXPERTS_PER_GROUP)
    e_logits = e_logits[jnp.arange(t), g_idx]
    top_p, top_j = lax.top_k(jax.nn.softmax(e_logits, axis=-1), TOP_K)
    weights = g_w[:, None] * top_p / jnp.sum(top_p, axis=-1, keepdims=True)
    experts = g_idx[:, None] * EXPERTS_PER_GROUP + top_j.astype(jnp.int32)

    n_assign = t * TOP_K
    e_flat = experts.reshape(-1)
    w_flat = weights.reshape(-1)
    tok_flat = jnp.repeat(jnp.arange(t, dtype=jnp.int32), TOP_K)
    order = jnp.argsort(e_flat)
    e_sorted = e_flat[order]
    counts = jnp.bincount(e_flat, length=N_EXPERTS)
    padded = (counts + MOE_BLOCK - 1) // MOE_BLOCK * MOE_BLOCK
    start_sorted = jnp.cumsum(counts) - counts
    seg_end = jnp.cumsum(padded)
    start_pad = seg_end - padded
    dest = start_pad[e_sorted] + (jnp.arange(n_assign) - start_sorted[e_sorted])
    n_blocks = (n_assign + N_EXPERTS * (MOE_BLOCK - 1) + MOE_BLOCK - 1) // MOE_BLOCK
    n_rows = n_blocks * MOE_BLOCK
    tok_buf = jnp.zeros((n_rows,), jnp.int32).at[dest].set(tok_flat[order])
    w_buf = jnp.zeros((n_rows,), jnp.float32).at[dest].set(w_flat[order])
    x_buf = x[tok_buf].reshape(n_blocks, MOE_BLOCK, d)
    block_expert = jnp.minimum(
        jnp.searchsorted(seg_end, jnp.arange(n_blocks) * MOE_BLOCK, side='right'), N_EXPERTS - 1)

    def expert_block(args):
        xb, e = args
        hb = jax.nn.silu(xb @ w_gate[e]) * (xb @ w_up[e])
        return hb @ w_down[e]

    y_buf = lax.map(expert_block, (x_buf, block_expert)).reshape(n_rows, d)
    y_buf = y_buf * w_buf[:, None].astype(y_buf.dtype)
    return jnp.zeros_like(x).at[tok_buf].add(y_buf.astype(x.dtype))


def setup_inputs(seed: int = 0) -> dict:
    key = jax.random.key(seed)
    ks = jax.random.split(key, 24)
    f32 = jnp.float32
    nrm = lambda k, shape, s: jax.random.normal(k, shape, f32) * s
    return {
        "x": nrm(ks[0], (BATCH, SEQ, D_MODEL), 1.0),
        "meta_tokens": nrm(ks[1], (N_META, D_MODEL), 1.0),
        "mix_norm_w": 1.0 + nrm(ks[2], (DEPTH, D_MODEL), 0.02),
        "w_in": nrm(ks[3], (DEPTH, D_MODEL, PROJ_TOTAL), D_MODEL ** -0.5),
        "q_norm_w": 1.0 + nrm(ks[4], (DEPTH, HEAD_DIM), 0.02),
        "k_norm_w": 1.0 + nrm(ks[5], (DEPTH, HEAD_DIM), 0.02),
        "rel_bias": nrm(ks[6], (DEPTH, N_HEADS, 2 * WIN_ROWS - 1, 2 * WIN_COLS - 1), 0.1),
        "meta_bias": nrm(ks[7], (DEPTH, N_HEADS, N_META), 0.1),
        "conv_w": nrm(ks[8], (DEPTH, CONV_K, CONV_WIDTH), CONV_K ** -0.5),
        "attn_out_norm_w": 1.0 + nrm(ks[9], (DEPTH, ATTN_WIDTH), 0.02),
        "conv_out_norm_w": 1.0 + nrm(ks[10], (DEPTH, CONV_WIDTH), 0.02),
        "w_out": nrm(ks[11], (DEPTH, MIX_WIDTH, D_MODEL), MIX_WIDTH ** -0.5),
        "ffn_norm_w": 1.0 + nrm(ks[12], (DEPTH, D_MODEL), 0.02),
        "w_router_group": nrm(ks[13], (DEPTH, D_MODEL, N_GROUPS), D_MODEL ** -0.5),
        "b_router_group": nrm(ks[14], (DEPTH, N_GROUPS), 0.01),
        "w_router_expert": nrm(ks[15], (DEPTH, D_MODEL, N_EXPERTS), D_MODEL ** -0.5),
        "b_router_expert": nrm(ks[16], (DEPTH, N_EXPERTS), 0.01),
        "w_gate": nrm(ks[17], (DEPTH, N_EXPERTS, D_MODEL, EXPERT_FF), D_MODEL ** -0.5),
        "w_up": nrm(ks[18], (DEPTH, N_EXPERTS, D_MODEL, EXPERT_FF), D_MODEL ** -0.5),
        "w_down": nrm(ks[19], (DEPTH, N_EXPERTS, EXPERT_FF, D_MODEL), EXPERT_FF ** -0.5),
    }


def reference(x, meta_tokens, mix_norm_w, w_in, q_norm_w, k_norm_w, rel_bias, meta_bias, conv_w,
              attn_out_norm_w, conv_out_norm_w, w_out, ffn_norm_w, w_router_group, b_router_group,
              w_router_expert, b_router_expert, w_gate, w_up, w_down):
    bsz, seq, d = x.shape
    meta = jnp.broadcast_to(meta_tokens[None].astype(x.dtype), (bsz, N_META, d))
    h = jnp.concatenate([meta, x], axis=1)
    length = h.shape[1]
    for l in range(DEPTH):
        xn = rms_norm(h, mix_norm_w[l])
        proj = xn @ w_in[l]
        q, k, v, gate_b, gate_c, hc = jnp.split(proj, PROJ_SPLITS, axis=-1)
        q = rms_norm(q.reshape(bsz, length, N_HEADS, HEAD_DIM), q_norm_w[l])
        k = rms_norm(k.reshape(bsz, length, N_HEADS, HEAD_DIM), k_norm_w[l])
        v = v.reshape(bsz, length, N_HEADS, HEAD_DIM)
        a = neighbourhood_attention(q, k, v, rel_bias[l], meta_bias[l]).reshape(bsz, length, ATTN_WIDTH)
        c = short_conv_mixer(gate_b, gate_c, hc, conv_w[l])
        mixed = jnp.concatenate([rms_norm(a, attn_out_norm_w[l]), rms_norm(c, conv_out_norm_w[l])], axis=-1)
        h = h + mixed @ w_out[l]
        hn = rms_norm(h, ffn_norm_w[l]).reshape(bsz * length, d)
        h = h + hierarchical_moe(hn, w_router_group[l], b_router_group[l], w_router_expert[l],
                                 b_router_expert[l], w_gate[l], w_up[l], w_down[l]).reshape(bsz, length, d)
    return h[:, N_META:]
```

```python
import functools

import jax
import jax.numpy as jnp
from jax import lax
from jax.experimental import pallas as pl
from jax.experimental.pallas import tpu as pltpu

F32 = jnp.float32
BF16 = jnp.bfloat16

D_MODEL = 2048
N_META = 16
GRID_W = 64
N_HEADS = 16
HEAD_DIM = 64
ATTN_WIDTH = N_HEADS * HEAD_DIM
CONV_WIDTH = D_MODEL - ATTN_WIDTH
PROJ_TOTAL = 3 * ATTN_WIDTH + 3 * CONV_WIDTH
WIN_ROWS = 8
WIN_COLS = 16
N_GROUPS = 4
EXPERTS_PER_GROUP = 8
N_EXPERTS = N_GROUPS * EXPERTS_PER_GROUP
TOP_K = 2
EXPERT_FF = 1024
EPS = 1e-6

LANES = 128
VMEM_LIMIT = 52 * 1024 * 1024

INPROJ_TM = 1024
INPROJ_TN = 512
OUTPROJ_TM = 256
ROUTE_TM = 512
EXPERT_ROW_BLOCK = 256
EXPERT_CAP = 1024
EXPERT_FF_CHUNK = 256
COMBINE_TM = 256


def _params(*sem):
    return pltpu.CompilerParams(dimension_semantics=sem, vmem_limit_bytes=VMEM_LIMIT)


def _inproj_body(x_ref, nw_ref, w_ref, o_ref, xn_ref):
    @pl.when(pl.program_id(1) == 0)
    def _():
        x = x_ref[...]
        ms = jnp.mean(x * x, axis=-1, keepdims=True)
        xn_ref[...] = (x * lax.rsqrt(ms + EPS) * nw_ref[...]).astype(BF16)

    o_ref[...] = jnp.dot(xn_ref[...], w_ref[...], preferred_element_type=F32)


def _inproj(x2d, norm_w, w_bf16, tm):
    m = x2d.shape[0]
    tn = INPROJ_TN
    return pl.pallas_call(
        _inproj_body,
        grid=(m // tm, PROJ_TOTAL // tn),
        in_specs=[pl.BlockSpec((tm, D_MODEL), lambda i, j: (i, 0)),
                  pl.BlockSpec((1, D_MODEL), lambda i, j: (0, 0)),
                  pl.BlockSpec((D_MODEL, tn), lambda i, j: (0, j))],
        out_specs=pl.BlockSpec((tm, tn), lambda i, j: (i, j)),
        out_shape=jax.ShapeDtypeStruct((m, PROJ_TOTAL), F32),
        scratch_shapes=[pltpu.VMEM((tm, D_MODEL), BF16)],
        compiler_params=_params("arbitrary", "arbitrary"),
        name="inproj",
    )(x2d, norm_w.reshape(1, D_MODEL), w_bf16)


def _head_norm(x, w, lo):
    x2 = x * x
    s_lo = jnp.sum(jnp.where(lo, x2, 0.0), axis=-1, keepdims=True)
    s_hi = jnp.sum(jnp.where(lo, 0.0, x2), axis=-1, keepdims=True)
    ms = jnp.where(lo, s_lo, s_hi) * (1.0 / HEAD_DIM)
    return x * lax.rsqrt(ms + EPS) * w


def _attn_body(q_ref, k_ref, v_ref, km_ref, vm_ref, qw_ref, kw_ref, bias_ref, mb_ref, o_ref,
               qs, ks, vs, *, rows):
    lo = lax.broadcasted_iota(jnp.int32, (1, LANES), 1) < HEAD_DIM
    scale = HEAD_DIM ** -0.5
    chunk = 256
    seq = rows * GRID_W

    def prep(i, _):
        sl = pl.ds(pl.multiple_of(i * chunk, chunk), chunk)
        qs[sl, :] = (_head_norm(q_ref[sl, :], qw_ref[...], lo) * scale).astype(BF16)
        ks[sl, :] = _head_norm(k_ref[sl, :], kw_ref[...], lo).astype(BF16)
        vs[sl, :] = v_ref[sl, :].astype(BF16)
        return 0

    lax.fori_loop(0, seq // chunk, prep, 0)
    kmb = _head_norm(km_ref[...], kw_ref[...], lo).astype(BF16)
    vmb = vm_ref[...].astype(BF16)
    wr = min(WIN_ROWS, rows)
    nk = wr * GRID_W
    contract_last = (((1,), (1,)), ((), ()))

    def row_body(r, _):
        rs = jnp.clip(r - wr // 2, 0, rows - wr)
        si = rs - r + (WIN_ROWS - 1)
        q_r = qs[pl.ds(pl.multiple_of(r * GRID_W, GRID_W), GRID_W), :]
        kwin = ks[pl.ds(pl.multiple_of(rs * GRID_W, GRID_W), nk), :]
        vwin = vs[pl.ds(pl.multiple_of(rs * GRID_W, GRID_W), nk), :]
        outs = []
        for h in range(2):
            mask = lo if h == 0 else jnp.logical_not(lo)
            qh = jnp.where(mask, q_r, jnp.zeros_like(q_r))
            s = lax.dot_general(qh, kwin, contract_last, preferred_element_type=F32)
            s = s + bias_ref[h, si]
            sm = lax.dot_general(qh, kmb, contract_last, preferred_element_type=F32) + mb_ref[h]
            m = jnp.maximum(jnp.max(s, axis=-1, keepdims=True), jnp.max(sm, axis=-1, keepdims=True))
            p = jnp.exp(s - m)
            pm = jnp.exp(sm - m)
            l = jnp.sum(p, axis=-1, keepdims=True) + jnp.sum(pm, axis=-1, keepdims=True)
            o = (jnp.dot(p.astype(BF16), vwin, preferred_element_type=F32)
                 + jnp.dot(pm.astype(BF16), vmb, preferred_element_type=F32))
            outs.append(o / l)
        o_ref[pl.ds(pl.multiple_of(r * GRID_W, GRID_W), GRID_W), :] = jnp.where(lo, outs[0], outs[1])
        return 0

    lax.fori_loop(0, rows, row_body, 0)


def _bias_table(rel_bias, rows):
    wr = min(WIN_ROWS, rows)
    c = jnp.arange(GRID_W)
    col_start = jnp.clip(c - WIN_COLS // 2, 0, GRID_W - WIN_COLS)
    col_mask = (c[None, :] >= col_start[:, None]) & (c[None, :] < col_start[:, None] + WIN_COLS)
    dc = jnp.clip(c[None, :] - c[:, None], -(WIN_COLS - 1), WIN_COLS - 1) + (WIN_COLS - 1)
    dr = jnp.minimum(jnp.arange(WIN_ROWS)[:, None] + jnp.arange(wr)[None, :], 2 * WIN_ROWS - 2)
    tbl = rel_bias.astype(F32)[:, dr[:, None, :, None], dc[None, :, None, :]]
    tbl = jnp.where(col_mask[None, None, :, None, :], tbl, -jnp.inf)
    return tbl.reshape(N_HEADS, WIN_ROWS, GRID_W, wr * GRID_W)


def _attention(proj, proj_meta, q_norm_w, k_norm_w, rel_bias, meta_bias, bsz, seq):
    rows = seq // GRID_W
    nk = min(WIN_ROWS, rows) * GRID_W
    npairs = N_HEADS // 2
    qw = jnp.tile(q_norm_w.astype(F32), 2).reshape(1, LANES)
    kw = jnp.tile(k_norm_w.astype(F32), 2).reshape(1, LANES)
    bias = _bias_table(rel_bias, rows)
    mb = meta_bias.astype(F32).reshape(N_HEADS, 1, N_META)
    return pl.pallas_call(
        functools.partial(_attn_body, rows=rows),
        grid=(bsz, npairs),
        in_specs=[pl.BlockSpec((seq, LANES), lambda b, p: (b, p)),
                  pl.BlockSpec((seq, LANES), lambda b, p: (b, npairs + p)),
                  pl.BlockSpec((seq, LANES), lambda b, p: (b, 2 * npairs + p)),
                  pl.BlockSpec((N_META, LANES), lambda b, p: (0, npairs + p)),
                  pl.BlockSpec((N_META, LANES), lambda b, p: (0, 2 * npairs + p)),
                  pl.BlockSpec((1, LANES), lambda b, p: (0, 0)),
                  pl.BlockSpec((1, LANES), lambda b, p: (0, 0)),
                  pl.BlockSpec((2, WIN_ROWS, GRID_W, nk), lambda b, p: (p, 0, 0, 0)),
                  pl.BlockSpec((2, 1, N_META), lambda b, p: (p, 0, 0))],
        out_specs=pl.BlockSpec((seq, LANES), lambda b, p: (b, p)),
        out_shape=jax.ShapeDtypeStruct((bsz * seq, ATTN_WIDTH), F32),
        scratch_shapes=[pltpu.VMEM((seq, LANES), BF16)] * 3,
        compiler_params=_params("arbitrary", "arbitrary"),
        name="attention",
    )(proj, proj, proj, proj_meta, proj_meta, qw, kw, bias, mb)


def _conv_body(gb_ref, gc_ref, hc_ref, gcm_ref, hcm_ref, w_ref, o_ref):
    seq = gb_ref.shape[0]
    u = gc_ref[...] * hc_ref[...]
    u_meta_last = gcm_ref[N_META - 1:N_META, :] * hcm_ref[N_META - 1:N_META, :]
    row = lax.broadcasted_iota(jnp.int32, (seq, 1), 0)
    u_prev = jnp.where(row == 0, u_meta_last, pltpu.roll(u, 1, 0))
    u_next = jnp.where(row == seq - 1, 0.0, pltpu.roll(u, seq - 1, 0))
    w = w_ref[...]
    y = u_prev * w[0:1] + u * w[1:2] + u_next * w[2:3]
    o_ref[...] = gb_ref[...] * y


def _short_conv(proj, proj_meta, conv_w, bsz, seq):
    nct = CONV_WIDTH // LANES
    base = 3 * ATTN_WIDTH // LANES
    return pl.pallas_call(
        _conv_body,
        grid=(bsz, nct),
        in_specs=[pl.BlockSpec((seq, LANES), lambda b, c: (b, base + c)),
                  pl.BlockSpec((seq, LANES), lambda b, c: (b, base + nct + c)),
                  pl.BlockSpec((seq, LANES), lambda b, c: (b, base + 2 * nct + c)),
                  pl.BlockSpec((N_META, LANES), lambda b, c: (0, base + nct + c)),
                  pl.BlockSpec((N_META, LANES), lambda b, c: (0, base + 2 * nct + c)),
                  pl.BlockSpec((3, LANES), lambda b, c: (0, c))],
        out_specs=pl.BlockSpec((seq, LANES), lambda b, c: (b, c)),
        out_shape=jax.ShapeDtypeStruct((bsz * seq, CONV_WIDTH), F32),
        compiler_params=_params("arbitrary", "arbitrary"),
        name="short_conv",
    )(proj, proj, proj, proj_meta, proj_meta, conv_w.astype(F32))


def _rms(x, w):
    ms = jnp.mean(x * x, axis=-1, keepdims=True)
    return x * lax.rsqrt(ms + EPS) * w


def _outproj_body(a_ref, c_ref, x_ref, aw_ref, cw_ref, wo_ref, fw_ref, wr_ref, br_ref,
                  h1_ref, hn_ref, lg_ref):
    an = _rms(a_ref[...], aw_ref[...]).astype(BF16)
    cn = _rms(c_ref[...], cw_ref[...]).astype(BF16)
    mixed = (jnp.dot(an, wo_ref[0:ATTN_WIDTH, :], preferred_element_type=F32)
             + jnp.dot(cn, wo_ref[ATTN_WIDTH:D_MODEL, :], preferred_element_type=F32))
    h1 = x_ref[...] + mixed
    h1_ref[...] = h1
    hn = _rms(h1, fw_ref[...])
    hn_ref[...] = hn
    lg_ref[...] = jnp.dot(hn, wr_ref[...], preferred_element_type=F32,
                          precision=lax.Precision.HIGHEST) + br_ref[...]


def _outproj(a, c, x2d, aw, cw, wo_bf16, fw, w_router, b_router):
    t = x2d.shape[0]
    tm = OUTPROJ_TM
    row = lambda i: (i, 0)
    fixed = lambda i: (0, 0)
    return pl.pallas_call(
        _outproj_body,
        grid=(t // tm,),
        in_specs=[pl.BlockSpec((tm, ATTN_WIDTH), row),
                  pl.BlockSpec((tm, CONV_WIDTH), row),
                  pl.BlockSpec((tm, D_MODEL), row),
                  pl.BlockSpec((1, ATTN_WIDTH), fixed),
                  pl.BlockSpec((1, CONV_WIDTH), fixed),
                  pl.BlockSpec((D_MODEL, D_MODEL), fixed),
                  pl.BlockSpec((1, D_MODEL), fixed),
                  pl.BlockSpec((D_MODEL, LANES), fixed),
                  pl.BlockSpec((1, LANES), fixed)],
        out_specs=[pl.BlockSpec((tm, D_MODEL), row),
                   pl.BlockSpec((tm, D_MODEL), row),
                   pl.BlockSpec((tm, LANES), row)],
        out_shape=[jax.ShapeDtypeStruct((t, D_MODEL), F32),
                   jax.ShapeDtypeStruct((t, D_MODEL), F32),
                   jax.ShapeDtypeStruct((t, LANES), F32)],
        compiler_params=_params("arbitrary"),
        name="outproj",
    )(a, c, x2d, aw.reshape(1, -1), cw.reshape(1, -1), wo_bf16, fw.reshape(1, -1), w_router, b_router)


def _route_body(lg_ref, idx_ref, wt_ref, cnt_ref, run_ref):
    @pl.when(pl.program_id(0) == 0)
    def _():
        run_ref[...] = jnp.zeros_like(run_ref)

    logits = lg_ref[...]
    tm = logits.shape[0]
    lane = lax.broadcasted_iota(jnp.int32, (tm, LANES), 1)
    neg = -jnp.inf

    def first_argmax(v):
        m = jnp.max(v, axis=-1, keepdims=True)
        first = jnp.min(jnp.where(v == m, lane.astype(F32), float(LANES)), axis=-1, keepdims=True)
        return m, first.astype(jnp.int32)

    gl = jnp.where(lane < N_GROUPS, logits, neg)
    gmax, gidx = first_argmax(gl)
    g_w = 1.0 / jnp.sum(jnp.exp(gl - gmax), axis=-1, keepdims=True)
    first = N_GROUPS + gidx * EXPERTS_PER_GROUP
    el = jnp.where((lane >= first) & (lane < first + EXPERTS_PER_GROUP), logits, neg)
    m0, j0 = first_argmax(el)
    m1, j1 = first_argmax(jnp.where(lane == j0, neg, el))
    p1 = jnp.exp(m1 - m0)
    w0 = g_w / (1.0 + p1)
    w1 = g_w * p1 / (1.0 + p1)
    e0 = j0 - N_GROUPS
    e1 = j1 - N_GROUPS

    onehot = ((lane == e0) | (lane == e1)).astype(BF16)
    tri = (lax.broadcasted_iota(jnp.int32, (tm, tm), 0)
           > lax.broadcasted_iota(jnp.int32, (tm, tm), 1)).astype(BF16)
    before = jnp.dot(tri, onehot, preferred_element_type=F32) + run_ref[0:1, :]
    r0 = jnp.sum(jnp.where(lane == e0, before, 0.0), axis=-1, keepdims=True).astype(jnp.int32)
    r1 = jnp.sum(jnp.where(lane == e1, before, 0.0), axis=-1, keepdims=True).astype(jnp.int32)
    run = run_ref[0:1, :] + jnp.sum(onehot.astype(F32), axis=0, keepdims=True)
    run_ref[...] = jnp.broadcast_to(run, run_ref.shape)
    cnt_ref[...] = jnp.broadcast_to(run, cnt_ref.shape)

    zero = jnp.zeros_like(lane)
    idx_ref[...] = jnp.where(lane == 0, e0, jnp.where(lane == 1, e1,
                             jnp.where(lane == 2, r0, jnp.where(lane == 3, r1, zero))))
    wt_ref[...] = jnp.where(lane == 0, w0, jnp.where(lane == 1, w1, 0.0))


def _route(logits):
    t = logits.shape[0]
    tm = ROUTE_TM
    return pl.pallas_call(
        _route_body,
        grid=(t // tm,),
        in_specs=[pl.BlockSpec((tm, LANES), lambda i: (i, 0))],
        out_specs=[pl.BlockSpec((tm, LANES), lambda i: (i, 0)),
                   pl.BlockSpec((tm, LANES), lambda i: (i, 0)),
                   pl.BlockSpec((8, LANES), lambda i: (0, 0))],
        out_shape=[jax.ShapeDtypeStruct((t, LANES), jnp.int32),
                   jax.ShapeDtypeStruct((t, LANES), F32),
                   jax.ShapeDtypeStruct((8, LANES), F32)],
        scratch_shapes=[pltpu.VMEM((8, LANES), F32)],
        compiler_params=_params("arbitrary"),
        name="route",
    )(logits)


def _expert_body(sbe, sbs, sbn, tok, tail, hn_hbm, wg_ref, wu_ref, wd_ref, y_hbm,
                 xf, xb, acc, wgb, wub, wdb, gsem, osem):
    s = pl.program_id(0)
    c = pl.program_id(1)
    nch = pl.num_programs(1)
    n = sbn[s]
    start = sbs[s]
    nblk = n // EXPERT_ROW_BLOCK

    def row_copy(i):
        return pltpu.make_async_copy(hn_hbm.at[pl.ds(tok[start + i], 1)], xf.at[pl.ds(i, 1)], gsem)

    def out_copy(j):
        rows = pl.ds(pl.multiple_of(j * EXPERT_ROW_BLOCK, EXPERT_ROW_BLOCK), EXPERT_ROW_BLOCK)
        dst = pl.ds(pl.multiple_of(start + j * EXPERT_ROW_BLOCK, EXPERT_ROW_BLOCK), EXPERT_ROW_BLOCK)
        return pltpu.make_async_copy(acc.at[rows], y_hbm.at[dst], osem)

    @pl.when((s == pl.num_programs(0) - 1) & (c == 0))
    def _():
        first = tail[0]
        nfill = (y_hbm.shape[0] - first) // EXPERT_ROW_BLOCK
        acc[0:EXPERT_ROW_BLOCK, :] = jnp.zeros((EXPERT_ROW_BLOCK, D_MODEL), F32)

        def fill_copy(j):
            dst = pl.ds(pl.multiple_of(first + j * EXPERT_ROW_BLOCK, EXPERT_ROW_BLOCK), EXPERT_ROW_BLOCK)
            return pltpu.make_async_copy(acc.at[0:EXPERT_ROW_BLOCK], y_hbm.at[dst], osem)

        def issue(j, _):
            fill_copy(j).start()
            return 0
        lax.fori_loop(0, nfill, issue, 0)

        def drain(j, _):
            fill_copy(j).wait()
            return 0
        lax.fori_loop(0, nfill, drain, 0)

    @pl.when(n > 0)
    def _():
        @pl.when(c == 0)
        def _():
            def issue(i, _):
                row_copy(i).start()
                return 0
            lax.fori_loop(0, n, issue, 0)

            def drain(i, _):
                row_copy(i).wait()
                return 0
            lax.fori_loop(0, n, drain, 0)

            def cast(j, _):
                rows = pl.ds(pl.multiple_of(j * EXPERT_ROW_BLOCK, EXPERT_ROW_BLOCK), EXPERT_ROW_BLOCK)
                xb[rows, :] = xf[rows, :].astype(BF16)
                return 0
            lax.fori_loop(0, nblk, cast, 0)

        wgb[...] = wg_ref[0].astype(BF16)
        wub[...] = wu_ref[0].astype(BF16)
        wdb[...] = wd_ref[0].astype(BF16)

        def partial_out(j):
            rows = pl.ds(pl.multiple_of(j * EXPERT_ROW_BLOCK, EXPERT_ROW_BLOCK), EXPERT_ROW_BLOCK)
            x = xb[rows, :]
            g = jnp.dot(x, wgb[...], preferred_element_type=F32)
            u = jnp.dot(x, wub[...], preferred_element_type=F32)
            h = (jax.nn.silu(g) * u).astype(BF16)
            return rows, jnp.dot(h, wdb[...], preferred_element_type=F32)

        @pl.when(c == 0)
        def _():
            def blk(j, _):
                rows, y = partial_out(j)
                acc[rows, :] = y
                return 0
            lax.fori_loop(0, nblk, blk, 0)

        @pl.when(c > 0)
        def _():
            def blk(j, _):
                rows, y = partial_out(j)
                acc[rows, :] += y
                return 0
            lax.fori_loop(0, nblk, blk, 0)

        @pl.when(c == nch - 1)
        def _():
            def issue(j, _):
                out_copy(j).start()
                return 0
            lax.fori_loop(0, nblk, issue, 0)

            def drain(j, _):
                out_copy(j).wait()
                return 0
            lax.fori_loop(0, nblk, drain, 0)


def _experts(hn, w_gate, w_up, w_down, sb_expert, sb_start, sb_rows, tok_buf, tail, n_rows):
    n_sb = sb_expert.shape[0]
    nch = EXPERT_FF // EXPERT_FF_CHUNK
    fc = EXPERT_FF_CHUNK

    def chunk(s, c, sbn):
        return jnp.where(sbn[s] > 0, c, nch - 1)

    def up_map(s, c, sbe, sbs, sbn, tok, tail):
        return (sbe[s], 0, chunk(s, c, sbn))

    def down_map(s, c, sbe, sbs, sbn, tok, tail):
        return (sbe[s], chunk(s, c, sbn), 0)

    grid_spec = pltpu.PrefetchScalarGridSpec(
        num_scalar_prefetch=5,
        grid=(n_sb, nch),
        in_specs=[pl.BlockSpec(memory_space=pl.ANY),
                  pl.BlockSpec((1, D_MODEL, fc), up_map),
                  pl.BlockSpec((1, D_MODEL, fc), up_map),
                  pl.BlockSpec((1, fc, D_MODEL), down_map)],
        out_specs=pl.BlockSpec(memory_space=pl.ANY),
        scratch_shapes=[pltpu.VMEM((EXPERT_CAP, D_MODEL), F32),
                        pltpu.VMEM((EXPERT_CAP, D_MODEL), BF16),
                        pltpu.VMEM((EXPERT_CAP, D_MODEL), F32),
                        pltpu.VMEM((D_MODEL, fc), BF16),
                        pltpu.VMEM((D_MODEL, fc), BF16),
                        pltpu.VMEM((fc, D_MODEL), BF16),
                        pltpu.SemaphoreType.DMA(()),
                        pltpu.SemaphoreType.DMA(())],
    )
    return pl.pallas_call(
        _expert_body,
        grid_spec=grid_spec,
        out_shape=jax.ShapeDtypeStruct((n_rows, D_MODEL), F32),
        compiler_params=_params("arbitrary", "arbitrary"),
        name="experts",
    )(sb_expert, sb_start, sb_rows, tok_buf, tail, hn, w_gate, w_up, w_down)


def _combine_body(dest, h1_ref, wt_ref, y_hbm, o_ref, g, sem):
    i = pl.program_id(0)
    nsteps = pl.num_programs(0)
    tm = h1_ref.shape[0]

    def row_copy(step, slot, r, k):
        src = y_hbm.at[pl.ds(dest[(step * tm + r) * TOP_K + k], 1)]
        return pltpu.make_async_copy(src, g.at[slot, k, pl.ds(r, 1)], sem.at[slot])

    def issue(step, slot):
        def f(r, _):
            for k in range(TOP_K):
                row_copy(step, slot, r, k).start()
            return 0
        lax.fori_loop(0, tm, f, 0)

    @pl.when(i == 0)
    def _():
        issue(0, 0)

    @pl.when(i + 1 < nsteps)
    def _():
        issue(i + 1, (i + 1) % 2)

    slot = i % 2

    def drain(r, _):
        for k in range(TOP_K):
            row_copy(i, slot, r, k).wait()
        return 0
    lax.fori_loop(0, tm, drain, 0)

    w = wt_ref[...]
    o_ref[...] = h1_ref[...] + (w[:, 0:1] * g[slot, 0] + w[:, 1:2] * g[slot, 1])


def _combine(h1, wts, y_buf, dest_flat):
    t = h1.shape[0]
    tm = COMBINE_TM
    grid_spec = pltpu.PrefetchScalarGridSpec(
        num_scalar_prefetch=1,
        grid=(t // tm,),
        in_specs=[pl.BlockSpec((tm, D_MODEL), lambda i, d: (i, 0)),
                  pl.BlockSpec((tm, LANES), lambda i, d: (i, 0)),
                  pl.BlockSpec(memory_space=pl.ANY)],
        out_specs=pl.BlockSpec((tm, D_MODEL), lambda i, d: (i, 0)),
        scratch_shapes=[pltpu.VMEM((2, TOP_K, tm, D_MODEL), F32),
                        pltpu.SemaphoreType.DMA((2,))],
    )
    return pl.pallas_call(
        _combine_body,
        grid_spec=grid_spec,
        out_shape=jax.ShapeDtypeStruct((t, D_MODEL), F32),
        compiler_params=_params("arbitrary"),
        name="combine",
    )(dest_flat, h1, wts, y_buf)


def _dispatch_tables(idx, cnt, t):
    rb, cap = EXPERT_ROW_BLOCK, EXPERT_CAP
    n_assign = t * TOP_K
    n_rows = -(-(n_assign + N_EXPERTS * (rb - 1)) // rb) * rb
    n_sb = (n_rows + N_EXPERTS * (cap - rb)) // cap
    experts = idx[:, 0:TOP_K]
    ranks = idx[:, TOP_K:2 * TOP_K]
    counts = cnt[0, :N_EXPERTS].astype(jnp.int32)
    padded = (counts + rb - 1) // rb * rb
    seg_end = jnp.cumsum(padded)
    seg_start = seg_end - padded
    dest = (seg_start[experts] + ranks).reshape(-1)
    tok = jnp.repeat(jnp.arange(t, dtype=jnp.int32), TOP_K)
    tok_buf = jnp.zeros((n_rows,), jnp.int32).at[dest].set(tok)

    sb_per_expert = (padded + cap - 1) // cap
    sb_end = jnp.cumsum(sb_per_expert)
    total = sb_end[-1]
    s = jnp.arange(n_sb, dtype=jnp.int32)
    s_eff = jnp.minimum(s, total - 1)
    e = jnp.minimum(jnp.searchsorted(sb_end, s_eff, side='right'), N_EXPERTS - 1).astype(jnp.int32)
    local = s_eff - (sb_end[e] - sb_per_expert[e])
    sb_start = (seg_start[e] + local * cap).astype(jnp.int32)
    sb_rows = jnp.where(s < total, jnp.clip(padded[e] - local * cap, 0, cap), 0).astype(jnp.int32)
    tail = seg_end[-1:].astype(jnp.int32)
    return dest.astype(jnp.int32), tok_buf, e, sb_start, sb_rows, tail, n_rows


def kernel(x, meta_tokens, mix_norm_w, w_in, q_norm_w, k_norm_w, rel_bias, meta_bias, conv_w,
           attn_out_norm_w, conv_out_norm_w, w_out, ffn_norm_w, w_router_group, b_router_group,
           w_router_expert, b_router_expert, w_gate, w_up, w_down):
    bsz, seq, d = x.shape
    depth = mix_norm_w.shape[0]
    assert depth == 1 and d == D_MODEL and seq % GRID_W == 0
    t = bsz * seq
    x2d = x.reshape(t, d)
    l = 0

    w_in_b = w_in[l].astype(BF16)
    proj = _inproj(x2d, mix_norm_w[l], w_in_b, INPROJ_TM)
    proj_meta = _inproj(meta_tokens.astype(x.dtype), mix_norm_w[l], w_in_b, N_META)

    a = _attention(proj, proj_meta, q_norm_w[l], k_norm_w[l], rel_bias[l], meta_bias[l], bsz, seq)
    c = _short_conv(proj, proj_meta, conv_w[l], bsz, seq)

    w_router = jnp.zeros((d, LANES), F32)
    w_router = w_router.at[:, :N_GROUPS].set(w_router_group[l].astype(F32))
    w_router = w_router.at[:, N_GROUPS:N_GROUPS + N_EXPERTS].set(w_router_expert[l].astype(F32))
    b_router = jnp.zeros((1, LANES), F32)
    b_router = b_router.at[0, :N_GROUPS].set(b_router_group[l].astype(F32))
    b_router = b_router.at[0, N_GROUPS:N_GROUPS + N_EXPERTS].set(b_router_expert[l].astype(F32))
    h1, hn, logits = _outproj(a, c, x2d, attn_out_norm_w[l], conv_out_norm_w[l], w_out[l].astype(BF16),
                              ffn_norm_w[l], w_router, b_router)

    idx, wts, cnt = _route(logits)
    dest, tok_buf, sb_expert, sb_start, sb_rows, tail, n_rows = _dispatch_tables(idx, cnt, t)
    y_buf = _experts(hn, w_gate.reshape(N_EXPERTS, d, EXPERT_FF), w_up.reshape(N_EXPERTS, d, EXPERT_FF),
                     w_down.reshape(N_EXPERTS, EXPERT_FF, d), sb_expert, sb_start, sb_rows, tok_buf, tail,
                     n_rows)
    out = _combine(h1, wts, y_buf, dest)
    return out.reshape(bsz, seq, d)
```

```python
import functools

import jax
import jax.numpy as jnp
from jax import lax
from jax.experimental import pallas as pl
from jax.experimental.pallas import tpu as pltpu

F32 = jnp.float32
BF16 = jnp.bfloat16

D_MODEL = 2048
N_META = 16
GRID_W = 64
N_HEADS = 16
HEAD_DIM = 64
ATTN_WIDTH = N_HEADS * HEAD_DIM
CONV_WIDTH = D_MODEL - ATTN_WIDTH
PROJ_TOTAL = 3 * ATTN_WIDTH + 3 * CONV_WIDTH
WIN_ROWS = 8
WIN_COLS = 16
N_GROUPS = 4
EXPERTS_PER_GROUP = 8
N_EXPERTS = N_GROUPS * EXPERTS_PER_GROUP
TOP_K = 2
EXPERT_FF = 1024
EPS = 1e-6

LANES = 128
VMEM_LIMIT = 52 * 1024 * 1024

INPROJ_TM = 1024
INPROJ_TN = 512
OUTPROJ_TM = 256
ROUTE_TM = 512
EXPERT_ROW_BLOCK = 256
EXPERT_CAP = 1024
EXPERT_FF_CHUNK = 256
COMBINE_TM = 256


def _params(*sem):
    return pltpu.CompilerParams(dimension_semantics=sem, vmem_limit_bytes=VMEM_LIMIT)


def _inproj_body(x_ref, nw_ref, w_ref, o_ref, xn_ref):
    @pl.when(pl.program_id(1) == 0)
    def _():
        x = x_ref[...]
        ms = jnp.mean(x * x, axis=-1, keepdims=True)
        xn_ref[...] = (x * lax.rsqrt(ms + EPS) * nw_ref[...]).astype(BF16)

    o_ref[...] = jnp.dot(xn_ref[...], w_ref[...], preferred_element_type=F32)


def _inproj(x2d, norm_w, w_bf16, tm):
    m = x2d.shape[0]
    tn = INPROJ_TN
    return pl.pallas_call(
        _inproj_body,
        grid=(m // tm, PROJ_TOTAL // tn),
        in_specs=[pl.BlockSpec((tm, D_MODEL), lambda i, j: (i, 0)),
                  pl.BlockSpec((1, D_MODEL), lambda i, j: (0, 0)),
                  pl.BlockSpec((D_MODEL, tn), lambda i, j: (0, j))],
        out_specs=pl.BlockSpec((tm, tn), lambda i, j: (i, j)),
        out_shape=jax.ShapeDtypeStruct((m, PROJ_TOTAL), F32),
        scratch_shapes=[pltpu.VMEM((tm, D_MODEL), BF16)],
        compiler_params=_params("arbitrary", "arbitrary"),
        name="inproj",
    )(x2d, norm_w.reshape(1, D_MODEL), w_bf16)


def _head_norm(x, w, lo):
    x2 = x * x
    s_lo = jnp.sum(jnp.where(lo, x2, 0.0), axis=-1, keepdims=True)
    s_hi = jnp.sum(jnp.where(lo, 0.0, x2), axis=-1, keepdims=True)
    ms = jnp.where(lo, s_lo, s_hi) * (1.0 / HEAD_DIM)
    return x * lax.rsqrt(ms + EPS) * w


def _attn_body(q_ref, k_ref, v_ref, km_ref, vm_ref, qw_ref, kw_ref, bias_ref, mb_ref, o_ref,
               qs, ks, vs, pa, pb, pma, pmb, *, rows):
    lo = lax.broadcasted_iota(jnp.int32, (1, LANES), 1) < HEAD_DIM
    scale = HEAD_DIM ** -0.5
    chunk = 256
    seq = rows * GRID_W

    def prep(i, _):
        sl = pl.ds(pl.multiple_of(i * chunk, chunk), chunk)
        qs[sl, :] = (_head_norm(q_ref[sl, :], qw_ref[...], lo) * scale).astype(BF16)
        ks[sl, :] = _head_norm(k_ref[sl, :], kw_ref[...], lo).astype(BF16)
        vs[sl, :] = v_ref[sl, :].astype(BF16)
        return 0

    lax.fori_loop(0, seq // chunk, prep, 0)
    kmb = _head_norm(km_ref[...], kw_ref[...], lo).astype(BF16)
    vmb = vm_ref[...].astype(BF16)
    wr = min(WIN_ROWS, rows)
    nk = wr * GRID_W
    contract_last = (((1,), (1,)), ((), ()))

    def window_start(r):
        return jnp.clip(r - wr // 2, 0, rows - wr)

    def row_slice(r, n):
        return pl.ds(pl.multiple_of(r * GRID_W, GRID_W), n)

    def scores(r):
        rs = window_start(r)
        si = rs - r + (WIN_ROWS - 1)
        q_r = qs[row_slice(r, GRID_W), :]
        kwin = ks[row_slice(rs, nk), :]
        out = []
        for h in range(2):
            mask = lo if h == 0 else jnp.logical_not(lo)
            qh = jnp.where(mask, q_r, jnp.zeros_like(q_r))
            s = lax.dot_general(qh, kwin, contract_last, preferred_element_type=F32) + bias_ref[h, si]
            sm = lax.dot_general(qh, kmb, contract_last, preferred_element_type=F32) + mb_ref[h]
            out.append((s, sm))
        return out

    def store_softmax(row_scores, p_ref, pm_ref):
        for h, (s, sm) in enumerate(row_scores):
            m = jnp.maximum(jnp.max(s, axis=-1, keepdims=True), jnp.max(sm, axis=-1, keepdims=True))
            p = jnp.exp(s - m)
            pm = jnp.exp(sm - m)
            inv = 1.0 / (jnp.sum(p, axis=-1, keepdims=True) + jnp.sum(pm, axis=-1, keepdims=True))
            p_ref[h] = (p * inv).astype(BF16)
            pm_ref[h] = (pm * inv).astype(BF16)

    def weighted_values(r, p_ref, pm_ref):
        vwin = vs[row_slice(window_start(r), nk), :]
        outs = [jnp.dot(p_ref[h], vwin, preferred_element_type=F32)
                + jnp.dot(pm_ref[h], vmb, preferred_element_type=F32) for h in range(2)]
        o_ref[row_slice(r, GRID_W), :] = jnp.where(lo, outs[0], outs[1])

    pairs = rows // 2
    bufs = ((pa, pma), (pb, pmb))

    first = [scores(k) for k in range(2)]
    for k in range(2):
        store_softmax(first[k], *bufs[k])

    def pair_body(j, _):
        current = [scores(2 * j + k) for k in range(2)]
        for k in range(2):
            weighted_values(2 * j - 2 + k, *bufs[k])
        for k in range(2):
            store_softmax(current[k], *bufs[k])
        return 0

    lax.fori_loop(1, pairs, pair_body, 0)
    for k in range(2):
        weighted_values(rows - 2 + k, *bufs[k])


def _bias_table(rel_bias, rows):
    wr = min(WIN_ROWS, rows)
    c = jnp.arange(GRID_W)
    col_start = jnp.clip(c - WIN_COLS // 2, 0, GRID_W - WIN_COLS)
    col_mask = (c[None, :] >= col_start[:, None]) & (c[None, :] < col_start[:, None] + WIN_COLS)
    dc = jnp.clip(c[None, :] - c[:, None], -(WIN_COLS - 1), WIN_COLS - 1) + (WIN_COLS - 1)
    onehot = (dc[None] == jnp.arange(2 * WIN_COLS - 1)[:, None, None]).astype(F32)
    toep = jnp.einsum('hdc,cqk->hdqk', rel_bias.astype(F32), onehot, precision=lax.Precision.HIGHEST)
    toep = jnp.where(col_mask[None, None], toep, -jnp.inf)
    tbl = jnp.stack([toep[:, si:si + wr] for si in range(WIN_ROWS)], axis=1)
    return tbl.transpose(0, 1, 3, 2, 4).reshape(N_HEADS, WIN_ROWS, GRID_W, wr * GRID_W)


def _attention(proj, proj_meta, q_norm_w, k_norm_w, rel_bias, meta_bias, bsz, seq):
    rows = seq // GRID_W
    nk = min(WIN_ROWS, rows) * GRID_W
    npairs = N_HEADS // 2
    qw = jnp.tile(q_norm_w.astype(F32), 2).reshape(1, LANES)
    kw = jnp.tile(k_norm_w.astype(F32), 2).reshape(1, LANES)
    bias = _bias_table(rel_bias, rows)
    mb = meta_bias.astype(F32).reshape(N_HEADS, 1, N_META)
    return pl.pallas_call(
        functools.partial(_attn_body, rows=rows),
        grid=(bsz, npairs),
        in_specs=[pl.BlockSpec((seq, LANES), lambda b, p: (b, p)),
                  pl.BlockSpec((seq, LANES), lambda b, p: (b, npairs + p)),
                  pl.BlockSpec((seq, LANES), lambda b, p: (b, 2 * npairs + p)),
                  pl.BlockSpec((N_META, LANES), lambda b, p: (0, npairs + p)),
                  pl.BlockSpec((N_META, LANES), lambda b, p: (0, 2 * npairs + p)),
                  pl.BlockSpec((1, LANES), lambda b, p: (0, 0)),
                  pl.BlockSpec((1, LANES), lambda b, p: (0, 0)),
                  pl.BlockSpec((2, WIN_ROWS, GRID_W, nk), lambda b, p: (p, 0, 0, 0)),
                  pl.BlockSpec((2, 1, N_META), lambda b, p: (p, 0, 0))],
        out_specs=pl.BlockSpec((seq, LANES), lambda b, p: (b, p)),
        out_shape=jax.ShapeDtypeStruct((bsz * seq, ATTN_WIDTH), F32),
        scratch_shapes=([pltpu.VMEM((seq, LANES), BF16)] * 3
                        + [pltpu.VMEM((2, GRID_W, nk), BF16)] * 2
                        + [pltpu.VMEM((2, GRID_W, N_META), BF16)] * 2),
        compiler_params=_params("arbitrary", "arbitrary"),
        name="attention",
    )(proj, proj, proj, proj_meta, proj_meta, qw, kw, bias, mb)


def _conv_body(gb_ref, gc_ref, hc_ref, gcm_ref, hcm_ref, w_ref, o_ref):
    seq = gb_ref.shape[0]
    u = gc_ref[...] * hc_ref[...]
    u_meta_last = gcm_ref[N_META - 1:N_META, :] * hcm_ref[N_META - 1:N_META, :]
    row = lax.broadcasted_iota(jnp.int32, (seq, 1), 0)
    u_prev = jnp.where(row == 0, u_meta_last, pltpu.roll(u, 1, 0))
    u_next = jnp.where(row == seq - 1, 0.0, pltpu.roll(u, seq - 1, 0))
    w = w_ref[...]
    y = u_prev * w[0:1] + u * w[1:2] + u_next * w[2:3]
    o_ref[...] = gb_ref[...] * y


def _short_conv(proj, proj_meta, conv_w, bsz, seq):
    nct = CONV_WIDTH // LANES
    base = 3 * ATTN_WIDTH // LANES
    return pl.pallas_call(
        _conv_body,
        grid=(bsz, nct),
        in_specs=[pl.BlockSpec((seq, LANES), lambda b, c: (b, base + c)),
                  pl.BlockSpec((seq, LANES), lambda b, c: (b, base + nct + c)),
                  pl.BlockSpec((seq, LANES), lambda b, c: (b, base + 2 * nct + c)),
                  pl.BlockSpec((N_META, LANES), lambda b, c: (0, base + nct + c)),
                  pl.BlockSpec((N_META, LANES), lambda b, c: (0, base + 2 * nct + c)),
                  pl.BlockSpec((3, LANES), lambda b, c: (0, c))],
        out_specs=pl.BlockSpec((seq, LANES), lambda b, c: (b, c)),
        out_shape=jax.ShapeDtypeStruct((bsz * seq, CONV_WIDTH), F32),
        compiler_params=_params("arbitrary", "arbitrary"),
        name="short_conv",
    )(proj, proj, proj, proj_meta, proj_meta, conv_w.astype(F32))


def _rms(x, w):
    ms = jnp.mean(x * x, axis=-1, keepdims=True)
    return x * lax.rsqrt(ms + EPS) * w


def _outproj_body(a_ref, c_ref, x_ref, aw_ref, cw_ref, wo_ref, fw_ref, wr_ref, br_ref,
                  h1_ref, hn_ref, lg_ref):
    an = _rms(a_ref[...], aw_ref[...]).astype(BF16)
    cn = _rms(c_ref[...], cw_ref[...]).astype(BF16)
    mixed = (jnp.dot(an, wo_ref[0:ATTN_WIDTH, :], preferred_element_type=F32)
             + jnp.dot(cn, wo_ref[ATTN_WIDTH:D_MODEL, :], preferred_element_type=F32))
    h1 = x_ref[...] + mixed
    h1_ref[...] = h1
    hn = _rms(h1, fw_ref[...])
    hn_ref[...] = hn
    lg_ref[...] = jnp.dot(hn, wr_ref[...], preferred_element_type=F32,
                          precision=lax.Precision.HIGHEST) + br_ref[...]


def _outproj(a, c, x2d, aw, cw, wo_bf16, fw, w_router, b_router):
    t = x2d.shape[0]
    tm = OUTPROJ_TM
    row = lambda i: (i, 0)
    fixed = lambda i: (0, 0)
    return pl.pallas_call(
        _outproj_body,
        grid=(t // tm,),
        in_specs=[pl.BlockSpec((tm, ATTN_WIDTH), row),
                  pl.BlockSpec((tm, CONV_WIDTH), row),
                  pl.BlockSpec((tm, D_MODEL), row),
                  pl.BlockSpec((1, ATTN_WIDTH), fixed),
                  pl.BlockSpec((1, CONV_WIDTH), fixed),
                  pl.BlockSpec((D_MODEL, D_MODEL), fixed),
                  pl.BlockSpec((1, D_MODEL), fixed),
                  pl.BlockSpec((D_MODEL, LANES), fixed),
                  pl.BlockSpec((1, LANES), fixed)],
        out_specs=[pl.BlockSpec((tm, D_MODEL), row),
                   pl.BlockSpec((tm, D_MODEL), row),
                   pl.BlockSpec((tm, LANES), row)],
        out_shape=[jax.ShapeDtypeStruct((t, D_MODEL), F32),
                   jax.ShapeDtypeStruct((t, D_MODEL), F32),
                   jax.ShapeDtypeStruct((t, LANES), F32)],
        compiler_params=_params("arbitrary"),
        name="outproj",
    )(a, c, x2d, aw.reshape(1, -1), cw.reshape(1, -1), wo_bf16, fw.reshape(1, -1), w_router, b_router)


def _route_body(lg_ref, idx_ref, wt_ref, cnt_ref, run_ref):
    @pl.when(pl.program_id(0) == 0)
    def _():
        run_ref[...] = jnp.zeros_like(run_ref)

    logits = lg_ref[...]
    tm = logits.shape[0]
    lane = lax.broadcasted_iota(jnp.int32, (tm, LANES), 1)
    neg = -jnp.inf

    def first_argmax(v):
        m = jnp.max(v, axis=-1, keepdims=True)
        first = jnp.min(jnp.where(v == m, lane.astype(F32), float(LANES)), axis=-1, keepdims=True)
        return m, first.astype(jnp.int32)

    gl = jnp.where(lane < N_GROUPS, logits, neg)
    gmax, gidx = first_argmax(gl)
    g_w = 1.0 / jnp.sum(jnp.exp(gl - gmax), axis=-1, keepdims=True)
    first = N_GROUPS + gidx * EXPERTS_PER_GROUP
    el = jnp.where((lane >= first) & (lane < first + EXPERTS_PER_GROUP), logits, neg)
    m0, j0 = first_argmax(el)
    m1, j1 = first_argmax(jnp.where(lane == j0, neg, el))
    p1 = jnp.exp(m1 - m0)
    w0 = g_w / (1.0 + p1)
    w1 = g_w * p1 / (1.0 + p1)
    e0 = j0 - N_GROUPS
    e1 = j1 - N_GROUPS

    onehot = ((lane == e0) | (lane == e1)).astype(BF16)
    tri = (lax.broadcasted_iota(jnp.int32, (tm, tm), 0)
           > lax.broadcasted_iota(jnp.int32, (tm, tm), 1)).astype(BF16)
    before = jnp.dot(tri, onehot, preferred_element_type=F32) + run_ref[0:1, :]
    r0 = jnp.sum(jnp.where(lane == e0, before, 0.0), axis=-1, keepdims=True).astype(jnp.int32)
    r1 = jnp.sum(jnp.where(lane == e1, before, 0.0), axis=-1, keepdims=True).astype(jnp.int32)
    run = run_ref[0:1, :] + jnp.sum(onehot.astype(F32), axis=0, keepdims=True)
    run_ref[...] = jnp.broadcast_to(run, run_ref.shape)
    cnt_ref[...] = jnp.broadcast_to(run, cnt_ref.shape)

    zero = jnp.zeros_like(lane)
    idx_ref[...] = jnp.where(lane == 0, e0, jnp.where(lane == 1, e1,
                             jnp.where(lane == 2, r0, jnp.where(lane == 3, r1, zero))))
    wt_ref[...] = jnp.where(lane == 0, w0, jnp.where(lane == 1, w1, 0.0))


def _route(logits):
    t = logits.shape[0]
    tm = ROUTE_TM
    return pl.pallas_call(
        _route_body,
        grid=(t // tm,),
        in_specs=[pl.BlockSpec((tm, LANES), lambda i: (i, 0))],
        out_specs=[pl.BlockSpec((tm, LANES), lambda i: (i, 0)),
                   pl.BlockSpec((tm, LANES), lambda i: (i, 0)),
                   pl.BlockSpec((8, LANES), lambda i: (0, 0))],
        out_shape=[jax.ShapeDtypeStruct((t, LANES), jnp.int32),
                   jax.ShapeDtypeStruct((t, LANES), F32),
                   jax.ShapeDtypeStruct((8, LANES), F32)],
        scratch_shapes=[pltpu.VMEM((8, LANES), F32)],
        compiler_params=_params("arbitrary"),
        name="route",
    )(logits)


def _expert_body(sbe, sbs, sbn, tok, tail, hn_hbm, wg_ref, wu_ref, wd_ref, y_hbm,
                 xf, xb, acc, wgb, wub, wdb, gsem, osem):
    s = pl.program_id(0)
    c = pl.program_id(1)
    nch = pl.num_programs(1)
    n = sbn[s]
    start = sbs[s]
    nblk = n // EXPERT_ROW_BLOCK

    def row_copy(i):
        return pltpu.make_async_copy(hn_hbm.at[pl.ds(tok[start + i], 1)], xf.at[pl.ds(i, 1)], gsem)

    def out_copy(j):
        rows = pl.ds(pl.multiple_of(j * EXPERT_ROW_BLOCK, EXPERT_ROW_BLOCK), EXPERT_ROW_BLOCK)
        dst = pl.ds(pl.multiple_of(start + j * EXPERT_ROW_BLOCK, EXPERT_ROW_BLOCK), EXPERT_ROW_BLOCK)
        return pltpu.make_async_copy(acc.at[rows], y_hbm.at[dst], osem)

    @pl.when((s == pl.num_programs(0) - 1) & (c == 0))
    def _():
        first = tail[0]
        nfill = (y_hbm.shape[0] - first) // EXPERT_ROW_BLOCK
        acc[0:EXPERT_ROW_BLOCK, :] = jnp.zeros((EXPERT_ROW_BLOCK, D_MODEL), F32)

        def fill_copy(j):
            dst = pl.ds(pl.multiple_of(first + j * EXPERT_ROW_BLOCK, EXPERT_ROW_BLOCK), EXPERT_ROW_BLOCK)
            return pltpu.make_async_copy(acc.at[0:EXPERT_ROW_BLOCK], y_hbm.at[dst], osem)

        def issue(j, _):
            fill_copy(j).start()
            return 0
        lax.fori_loop(0, nfill, issue, 0)

        def drain(j, _):
            fill_copy(j).wait()
            return 0
        lax.fori_loop(0, nfill, drain, 0)

    @pl.when(n > 0)
    def _():
        @pl.when(c == 0)
        def _():
            def issue(i, _):
                row_copy(i).start()
                return 0
            lax.fori_loop(0, n, issue, 0)

            def drain(i, _):
                row_copy(i).wait()
                return 0
            lax.fori_loop(0, n, drain, 0)

            def cast(j, _):
                rows = pl.ds(pl.multiple_of(j * EXPERT_ROW_BLOCK, EXPERT_ROW_BLOCK), EXPERT_ROW_BLOCK)
                xb[rows, :] = xf[rows, :].astype(BF16)
                return 0
            lax.fori_loop(0, nblk, cast, 0)

        wgb[...] = wg_ref[0].astype(BF16)
        wub[...] = wu_ref[0].astype(BF16)
        wdb[...] = wd_ref[0].astype(BF16)

        def partial_out(j):
            rows = pl.ds(pl.multiple_of(j * EXPERT_ROW_BLOCK, EXPERT_ROW_BLOCK), EXPERT_ROW_BLOCK)
            x = xb[rows, :]
            g = jnp.dot(x, wgb[...], preferred_element_type=F32)
            u = jnp.dot(x, wub[...], preferred_element_type=F32)
            h = (jax.nn.silu(g) * u).astype(BF16)
            return rows, jnp.dot(h, wdb[...], preferred_element_type=F32)

        @pl.when(c == 0)
        def _():
            def blk(j, _):
                rows, y = partial_out(j)
                acc[rows, :] = y
                return 0
            lax.fori_loop(0, nblk, blk, 0)

        @pl.when(c > 0)
        def _():
            def blk(j, _):
                rows, y = partial_out(j)
                acc[rows, :] += y
                return 0
            lax.fori_loop(0, nblk, blk, 0)

        @pl.when(c == nch - 1)
        def _():
            def issue(j, _):
                out_copy(j).start()
                return 0
            lax.fori_loop(0, nblk, issue, 0)

            def drain(j, _):
                out_copy(j).wait()
                return 0
            lax.fori_loop(0, nblk, drain, 0)


def _experts(hn, w_gate, w_up, w_down, sb_expert, sb_start, sb_rows, tok_buf, tail, n_rows):
    n_sb = sb_expert.shape[0]
    nch = EXPERT_FF // EXPERT_FF_CHUNK
    fc = EXPERT_FF_CHUNK

    def chunk(s, c, sbn):
        return jnp.where(sbn[s] > 0, c, nch - 1)

    def up_map(s, c, sbe, sbs, sbn, tok, tail):
        return (sbe[s], 0, chunk(s, c, sbn))

    def down_map(s, c, sbe, sbs, sbn, tok, tail):
        return (sbe[s], chunk(s, c, sbn), 0)

    grid_spec = pltpu.PrefetchScalarGridSpec(
        num_scalar_prefetch=5,
        grid=(n_sb, nch),
        in_specs=[pl.BlockSpec(memory_space=pl.ANY),
                  pl.BlockSpec((1, D_MODEL, fc), up_map),
                  pl.BlockSpec((1, D_MODEL, fc), up_map),
                  pl.BlockSpec((1, fc, D_MODEL), down_map)],
        out_specs=pl.BlockSpec(memory_space=pl.ANY),
        scratch_shapes=[pltpu.VMEM((EXPERT_CAP, D_MODEL), F32),
                        pltpu.VMEM((EXPERT_CAP, D_MODEL), BF16),
                        pltpu.VMEM((EXPERT_CAP, D_MODEL), F32),
                        pltpu.VMEM((D_MODEL, fc), BF16),
                        pltpu.VMEM((D_MODEL, fc), BF16),
                        pltpu.VMEM((fc, D_MODEL), BF16),
                        pltpu.SemaphoreType.DMA(()),
                        pltpu.SemaphoreType.DMA(())],
    )
    return pl.pallas_call(
        _expert_body,
        grid_spec=grid_spec,
        out_shape=jax.ShapeDtypeStruct((n_rows, D_MODEL), F32),
        compiler_params=_params("arbitrary", "arbitrary"),
        name="experts",
    )(sb_expert, sb_start, sb_rows, tok_buf, tail, hn, w_gate, w_up, w_down)


def _combine_body(dest, h1_ref, wt_ref, y_hbm, o_ref, g, sem):
    i = pl.program_id(0)
    nsteps = pl.num_programs(0)
    tm = h1_ref.shape[0]

    def row_copy(step, slot, r, k):
        src = y_hbm.at[pl.ds(dest[(step * tm + r) * TOP_K + k], 1)]
        return pltpu.make_async_copy(src, g.at[slot, k, pl.ds(r, 1)], sem.at[slot])

    def issue(step, slot):
        def f(r, _):
            for k in range(TOP_K):
                row_copy(step, slot, r, k).start()
            return 0
        lax.fori_loop(0, tm, f, 0)

    @pl.when(i == 0)
    def _():
        issue(0, 0)

    @pl.when(i + 1 < nsteps)
    def _():
        issue(i + 1, (i + 1) % 2)

    slot = i % 2

    def drain(r, _):
        for k in range(TOP_K):
            row_copy(i, slot, r, k).wait()
        return 0
    lax.fori_loop(0, tm, drain, 0)

    w = wt_ref[...]
    o_ref[...] = h1_ref[...] + (w[:, 0:1] * g[slot, 0] + w[:, 1:2] * g[slot, 1])


def _combine(h1, wts, y_buf, dest_flat):
    t = h1.shape[0]
    tm = COMBINE_TM
    grid_spec = pltpu.PrefetchScalarGridSpec(
        num_scalar_prefetch=1,
        grid=(t // tm,),
        in_specs=[pl.BlockSpec((tm, D_MODEL), lambda i, d: (i, 0)),
                  pl.BlockSpec((tm, LANES), lambda i, d: (i, 0)),
                  pl.BlockSpec(memory_space=pl.ANY)],
        out_specs=pl.BlockSpec((tm, D_MODEL), lambda i, d: (i, 0)),
        scratch_shapes=[pltpu.VMEM((2, TOP_K, tm, D_MODEL), F32),
                        pltpu.SemaphoreType.DMA((2,))],
    )
    return pl.pallas_call(
        _combine_body,
        grid_spec=grid_spec,
        out_shape=jax.ShapeDtypeStruct((t, D_MODEL), F32),
        compiler_params=_params("arbitrary"),
        name="combine",
    )(dest_flat, h1, wts, y_buf)


def _dispatch_tables(idx, cnt, t):
    rb, cap = EXPERT_ROW_BLOCK, EXPERT_CAP
    n_assign = t * TOP_K
    n_rows = -(-(n_assign + N_EXPERTS * (rb - 1)) // rb) * rb
    n_sb = (n_rows + N_EXPERTS * (cap - rb)) // cap
    experts = idx[:, 0:TOP_K]
    ranks = idx[:, TOP_K:2 * TOP_K]
    counts = cnt[0, :N_EXPERTS].astype(jnp.int32)
    padded = (counts + rb - 1) // rb * rb
    seg_end = jnp.cumsum(padded)
    seg_start = seg_end - padded
    dest = (seg_start[experts] + ranks).reshape(-1)
    tok = jnp.repeat(jnp.arange(t, dtype=jnp.int32), TOP_K)
    tok_buf = jnp.zeros((n_rows,), jnp.int32).at[dest].set(tok)

    sb_per_expert = (padded + cap - 1) // cap
    sb_end = jnp.cumsum(sb_per_expert)
    total = sb_end[-1]
    s = jnp.arange(n_sb, dtype=jnp.int32)
    s_eff = jnp.minimum(s, total - 1)
    e = jnp.minimum(jnp.searchsorted(sb_end, s_eff, side='right'), N_EXPERTS - 1).astype(jnp.int32)
    local = s_eff - (sb_end[e] - sb_per_expert[e])
    sb_start = (seg_start[e] + local * cap).astype(jnp.int32)
    sb_rows = jnp.where(s < total, jnp.clip(padded[e] - local * cap, 0, cap), 0).astype(jnp.int32)
    tail = seg_end[-1:].astype(jnp.int32)
    return dest.astype(jnp.int32), tok_buf, e, sb_start, sb_rows, tail, n_rows


def kernel(x, meta_tokens, mix_norm_w, w_in, q_norm_w, k_norm_w, rel_bias, meta_bias, conv_w,
           attn_out_norm_w, conv_out_norm_w, w_out, ffn_norm_w, w_router_group, b_router_group,
           w_router_expert, b_router_expert, w_gate, w_up, w_down):
    bsz, seq, d = x.shape
    depth = mix_norm_w.shape[0]
    assert depth == 1 and d == D_MODEL and seq % GRID_W == 0
    t = bsz * seq
    x2d = x.reshape(t, d)
    l = 0

    w_in_b = w_in[l].astype(BF16)
    proj = _inproj(x2d, mix_norm_w[l], w_in_b, INPROJ_TM)
    proj_meta = _inproj(meta_tokens.astype(x.dtype), mix_norm_w[l], w_in_b, N_META)

    a = _attention(proj, proj_meta, q_norm_w[l], k_norm_w[l], rel_bias[l], meta_bias[l], bsz, seq)
    c = _short_conv(proj, proj_meta, conv_w[l], bsz, seq)

    w_router = jnp.zeros((d, LANES), F32)
    w_router = w_router.at[:, :N_GROUPS].set(w_router_group[l].astype(F32))
    w_router = w_router.at[:, N_GROUPS:N_GROUPS + N_EXPERTS].set(w_router_expert[l].astype(F32))
    b_router = jnp.zeros((1, LANES), F32)
    b_router = b_router.at[0, :N_GROUPS].set(b_router_group[l].astype(F32))
    b_router = b_router.at[0, N_GROUPS:N_GROUPS + N_EXPERTS].set(b_router_expert[l].astype(F32))
    h1, hn, logits = _outproj(a, c, x2d, attn_out_norm_w[l], conv_out_norm_w[l], w_out[l].astype(BF16),
                              ffn_norm_w[l], w_router, b_router)

    idx, wts, cnt = _route(logits)
    dest, tok_buf, sb_expert, sb_start, sb_rows, tail, n_rows = _dispatch_tables(idx, cnt, t)
    y_buf = _experts(hn, w_gate.reshape(N_EXPERTS, d, EXPERT_FF), w_up.reshape(N_EXPERTS, d, EXPERT_FF),
                     w_down.reshape(N_EXPERTS, EXPERT_FF, d), sb_expert, sb_start, sb_rows, tok_buf, tail,
                     n_rows)
    out = _combine(h1, wts, y_buf, dest)
    return out.reshape(bsz, seq, d)
```

```python
import functools

import jax
import jax.numpy as jnp
from jax import lax
from jax.experimental import pallas as pl
from jax.experimental.pallas import tpu as pltpu

F32 = jnp.float32
BF16 = jnp.bfloat16

D_MODEL = 2048
N_META = 16
GRID_W = 64
N_HEADS = 16
HEAD_DIM = 64
ATTN_WIDTH = N_HEADS * HEAD_DIM
CONV_WIDTH = D_MODEL - ATTN_WIDTH
PROJ_TOTAL = 3 * ATTN_WIDTH + 3 * CONV_WIDTH
WIN_ROWS = 8
WIN_COLS = 16
N_GROUPS = 4
EXPERTS_PER_GROUP = 8
N_EXPERTS = N_GROUPS * EXPERTS_PER_GROUP
TOP_K = 2
EXPERT_FF = 1024
EPS = 1e-6

LANES = 128
VMEM_LIMIT = 52 * 1024 * 1024

INPROJ_TM = 1024
INPROJ_TN = 512
OUTPROJ_TM = 256
ROUTE_TM = 512
EXPERT_ROW_BLOCK = 128
EXPERT_CAP = 1024
EXPERT_FF_CHUNK = 256
COMBINE_TM = 256


def _params(*sem):
    return pltpu.CompilerParams(dimension_semantics=sem, vmem_limit_bytes=VMEM_LIMIT)


def _inproj_body(x_ref, nw_ref, w_ref, o_ref, xn_ref):
    @pl.when(pl.program_id(1) == 0)
    def _():
        x = x_ref[...]
        ms = jnp.mean(x * x, axis=-1, keepdims=True)
        xn_ref[...] = (x * lax.rsqrt(ms + EPS) * nw_ref[...]).astype(BF16)

    o_ref[...] = jnp.dot(xn_ref[...], w_ref[...], preferred_element_type=F32)


def _inproj(x2d, norm_w, w_bf16, tm):
    m = x2d.shape[0]
    tn = INPROJ_TN
    return pl.pallas_call(
        _inproj_body,
        grid=(m // tm, PROJ_TOTAL // tn),
        in_specs=[pl.BlockSpec((tm, D_MODEL), lambda i, j: (i, 0)),
                  pl.BlockSpec((1, D_MODEL), lambda i, j: (0, 0)),
                  pl.BlockSpec((D_MODEL, tn), lambda i, j: (0, j))],
        out_specs=pl.BlockSpec((tm, tn), lambda i, j: (i, j)),
        out_shape=jax.ShapeDtypeStruct((m, PROJ_TOTAL), F32),
        scratch_shapes=[pltpu.VMEM((tm, D_MODEL), BF16)],
        compiler_params=_params("arbitrary", "arbitrary"),
        name="inproj",
    )(x2d, norm_w.reshape(1, D_MODEL), w_bf16)


def _head_norm(x, w, lo):
    x2 = x * x
    s_lo = jnp.sum(jnp.where(lo, x2, 0.0), axis=-1, keepdims=True)
    s_hi = jnp.sum(jnp.where(lo, 0.0, x2), axis=-1, keepdims=True)
    ms = jnp.where(lo, s_lo, s_hi) * (1.0 / HEAD_DIM)
    return x * lax.rsqrt(ms + EPS) * w


def _attn_body(q_ref, k_ref, v_ref, km_ref, vm_ref, qw_ref, kw_ref, bias_ref, mb_ref, o_ref,
               qs, ks, vs, pa, pb, pma, pmb, *, rows):
    lo = lax.broadcasted_iota(jnp.int32, (1, LANES), 1) < HEAD_DIM
    scale = HEAD_DIM ** -0.5
    chunk = 256
    seq = rows * GRID_W

    def prep(i, _):
        sl = pl.ds(pl.multiple_of(i * chunk, chunk), chunk)
        qs[sl, :] = (_head_norm(q_ref[sl, :], qw_ref[...], lo) * scale).astype(BF16)
        ks[sl, :] = _head_norm(k_ref[sl, :], kw_ref[...], lo).astype(BF16)
        vs[sl, :] = v_ref[sl, :].astype(BF16)
        return 0

    lax.fori_loop(0, seq // chunk, prep, 0)
    kmb = _head_norm(km_ref[...], kw_ref[...], lo).astype(BF16)
    vmb = vm_ref[...].astype(BF16)
    wr = min(WIN_ROWS, rows)
    nk = wr * GRID_W
    contract_last = (((1,), (1,)), ((), ()))

    def window_start(r):
        return jnp.clip(r - wr // 2, 0, rows - wr)

    def row_slice(r, n):
        return pl.ds(pl.multiple_of(r * GRID_W, GRID_W), n)

    def scores(r):
        rs = window_start(r)
        si = rs - r + (WIN_ROWS - 1)
        q_r = qs[row_slice(r, GRID_W), :]
        kwin = ks[row_slice(rs, nk), :]
        out = []
        for h in range(2):
            mask = lo if h == 0 else jnp.logical_not(lo)
            qh = jnp.where(mask, q_r, jnp.zeros_like(q_r))
            s = lax.dot_general(qh, kwin, contract_last, preferred_element_type=F32) + bias_ref[h, si]
            sm = lax.dot_general(qh, kmb, contract_last, preferred_element_type=F32) + mb_ref[h]
            out.append((s, sm))
        return out

    def store_softmax(row_scores, p_ref, pm_ref):
        for h, (s, sm) in enumerate(row_scores):
            m = jnp.maximum(jnp.max(s, axis=-1, keepdims=True), jnp.max(sm, axis=-1, keepdims=True))
            p = jnp.exp(s - m)
            pm = jnp.exp(sm - m)
            inv = 1.0 / (jnp.sum(p, axis=-1, keepdims=True) + jnp.sum(pm, axis=-1, keepdims=True))
            p_ref[h] = (p * inv).astype(BF16)
            pm_ref[h] = (pm * inv).astype(BF16)

    def weighted_values(r, p_ref, pm_ref):
        vwin = vs[row_slice(window_start(r), nk), :]
        outs = [jnp.dot(p_ref[h], vwin, preferred_element_type=F32)
                + jnp.dot(pm_ref[h], vmb, preferred_element_type=F32) for h in range(2)]
        o_ref[row_slice(r, GRID_W), :] = jnp.where(lo, outs[0], outs[1])

    pairs = rows // 2
    bufs = ((pa, pma), (pb, pmb))

    first = [scores(k) for k in range(2)]
    for k in range(2):
        store_softmax(first[k], *bufs[k])

    def pair_body(j, _):
        current = [scores(2 * j + k) for k in range(2)]
        for k in range(2):
            weighted_values(2 * j - 2 + k, *bufs[k])
        for k in range(2):
            store_softmax(current[k], *bufs[k])
        return 0

    lax.fori_loop(1, pairs, pair_body, 0)
    for k in range(2):
        weighted_values(rows - 2 + k, *bufs[k])


def _bias_table(rel_bias, rows):
    wr = min(WIN_ROWS, rows)
    c = jnp.arange(GRID_W)
    col_start = jnp.clip(c - WIN_COLS // 2, 0, GRID_W - WIN_COLS)
    col_mask = (c[None, :] >= col_start[:, None]) & (c[None, :] < col_start[:, None] + WIN_COLS)
    dc = jnp.clip(c[None, :] - c[:, None], -(WIN_COLS - 1), WIN_COLS - 1) + (WIN_COLS - 1)
    onehot = (dc[None] == jnp.arange(2 * WIN_COLS - 1)[:, None, None]).astype(F32)
    toep = jnp.einsum('hdc,cqk->hdqk', rel_bias.astype(F32), onehot, precision=lax.Precision.HIGHEST)
    toep = jnp.where(col_mask[None, None], toep, -jnp.inf)
    tbl = jnp.stack([toep[:, si:si + wr] for si in range(WIN_ROWS)], axis=1)
    return tbl.transpose(0, 1, 3, 2, 4).reshape(N_HEADS, WIN_ROWS, GRID_W, wr * GRID_W)


def _attention(proj, proj_meta, q_norm_w, k_norm_w, rel_bias, meta_bias, bsz, seq):
    rows = seq // GRID_W
    nk = min(WIN_ROWS, rows) * GRID_W
    npairs = N_HEADS // 2
    qw = jnp.tile(q_norm_w.astype(F32), 2).reshape(1, LANES)
    kw = jnp.tile(k_norm_w.astype(F32), 2).reshape(1, LANES)
    bias = _bias_table(rel_bias, rows)
    mb = meta_bias.astype(F32).reshape(N_HEADS, 1, N_META)
    return pl.pallas_call(
        functools.partial(_attn_body, rows=rows),
        grid=(bsz, npairs),
        in_specs=[pl.BlockSpec((seq, LANES), lambda b, p: (b, p)),
                  pl.BlockSpec((seq, LANES), lambda b, p: (b, npairs + p)),
                  pl.BlockSpec((seq, LANES), lambda b, p: (b, 2 * npairs + p)),
                  pl.BlockSpec((N_META, LANES), lambda b, p: (0, npairs + p)),
                  pl.BlockSpec((N_META, LANES), lambda b, p: (0, 2 * npairs + p)),
                  pl.BlockSpec((1, LANES), lambda b, p: (0, 0)),
                  pl.BlockSpec((1, LANES), lambda b, p: (0, 0)),
                  pl.BlockSpec((2, WIN_ROWS, GRID_W, nk), lambda b, p: (p, 0, 0, 0)),
                  pl.BlockSpec((2, 1, N_META), lambda b, p: (p, 0, 0))],
        out_specs=pl.BlockSpec((seq, LANES), lambda b, p: (b, p)),
        out_shape=jax.ShapeDtypeStruct((bsz * seq, ATTN_WIDTH), F32),
        scratch_shapes=([pltpu.VMEM((seq, LANES), BF16)] * 3
                        + [pltpu.VMEM((2, GRID_W, nk), BF16)] * 2
                        + [pltpu.VMEM((2, GRID_W, N_META), BF16)] * 2),
        compiler_params=_params("arbitrary", "arbitrary"),
        name="attention",
    )(proj, proj, proj, proj_meta, proj_meta, qw, kw, bias, mb)


def _conv_body(gb_ref, gc_ref, hc_ref, gcm_ref, hcm_ref, w_ref, o_ref):
    seq = gb_ref.shape[0]
    u = gc_ref[...] * hc_ref[...]
    u_meta_last = gcm_ref[N_META - 1:N_META, :] * hcm_ref[N_META - 1:N_META, :]
    row = lax.broadcasted_iota(jnp.int32, (seq, 1), 0)
    u_prev = jnp.where(row == 0, u_meta_last, pltpu.roll(u, 1, 0))
    u_next = jnp.where(row == seq - 1, 0.0, pltpu.roll(u, seq - 1, 0))
    w = w_ref[...]
    y = u_prev * w[0:1] + u * w[1:2] + u_next * w[2:3]
    o_ref[...] = gb_ref[...] * y


def _short_conv(proj, proj_meta, conv_w, bsz, seq):
    nct = CONV_WIDTH // LANES
    base = 3 * ATTN_WIDTH // LANES
    return pl.pallas_call(
        _conv_body,
        grid=(bsz, nct),
        in_specs=[pl.BlockSpec((seq, LANES), lambda b, c: (b, base + c)),
                  pl.BlockSpec((seq, LANES), lambda b, c: (b, base + nct + c)),
                  pl.BlockSpec((seq, LANES), lambda b, c: (b, base + 2 * nct + c)),
                  pl.BlockSpec((N_META, LANES), lambda b, c: (0, base + nct + c)),
                  pl.BlockSpec((N_META, LANES), lambda b, c: (0, base + 2 * nct + c)),
                  pl.BlockSpec((3, LANES), lambda b, c: (0, c))],
        out_specs=pl.BlockSpec((seq, LANES), lambda b, c: (b, c)),
        out_shape=jax.ShapeDtypeStruct((bsz * seq, CONV_WIDTH), F32),
        compiler_params=_params("arbitrary", "arbitrary"),
        name="short_conv",
    )(proj, proj, proj, proj_meta, proj_meta, conv_w.astype(F32))


def _rms(x, w):
    ms = jnp.mean(x * x, axis=-1, keepdims=True)
    return x * lax.rsqrt(ms + EPS) * w


def _pack_bf16_pairs(x):
    w = x.shape[1] // 2
    lo = lax.bitcast_convert_type(x[:, :w].astype(BF16).astype(F32), jnp.uint32)
    hi = lax.bitcast_convert_type(x[:, w:].astype(BF16).astype(F32), jnp.uint32)
    return (hi & jnp.uint32(0xFFFF0000)) | (lo >> 16)


def _unpack_bf16_pairs(p):
    lo = lax.bitcast_convert_type(p << 16, F32).astype(BF16)
    hi = lax.bitcast_convert_type(p & jnp.uint32(0xFFFF0000), F32).astype(BF16)
    return lo, hi


def _outproj_body(a_ref, c_ref, x_ref, aw_ref, cw_ref, wo_ref, fw_ref, wr_ref, br_ref,
                  h1_ref, hn_ref, lg_ref):
    an = _rms(a_ref[...], aw_ref[...]).astype(BF16)
    cn = _rms(c_ref[...], cw_ref[...]).astype(BF16)
    mixed = (jnp.dot(an, wo_ref[0:ATTN_WIDTH, :], preferred_element_type=F32)
             + jnp.dot(cn, wo_ref[ATTN_WIDTH:D_MODEL, :], preferred_element_type=F32))
    h1 = x_ref[...] + mixed
    h1_ref[...] = h1
    hn = _rms(h1, fw_ref[...])
    hn_ref[...] = _pack_bf16_pairs(hn)
    lg_ref[...] = jnp.dot(hn, wr_ref[...], preferred_element_type=F32,
                          precision=lax.Precision.HIGHEST) + br_ref[...]


def _outproj(a, c, x2d, aw, cw, wo_bf16, fw, w_router, b_router):
    t = x2d.shape[0]
    tm = OUTPROJ_TM
    row = lambda i: (i, 0)
    fixed = lambda i: (0, 0)
    return pl.pallas_call(
        _outproj_body,
        grid=(t // tm,),
        in_specs=[pl.BlockSpec((tm, ATTN_WIDTH), row),
                  pl.BlockSpec((tm, CONV_WIDTH), row),
                  pl.BlockSpec((tm, D_MODEL), row),
                  pl.BlockSpec((1, ATTN_WIDTH), fixed),
                  pl.BlockSpec((1, CONV_WIDTH), fixed),
                  pl.BlockSpec((D_MODEL, D_MODEL), fixed),
                  pl.BlockSpec((1, D_MODEL), fixed),
                  pl.BlockSpec((D_MODEL, LANES), fixed),
                  pl.BlockSpec((1, LANES), fixed)],
        out_specs=[pl.BlockSpec((tm, D_MODEL), row),
                   pl.BlockSpec((tm, D_MODEL // 2), row),
                   pl.BlockSpec((tm, LANES), row)],
        out_shape=[jax.ShapeDtypeStruct((t, D_MODEL), F32),
                   jax.ShapeDtypeStruct((t, D_MODEL // 2), jnp.uint32),
                   jax.ShapeDtypeStruct((t, LANES), F32)],
        compiler_params=_params("arbitrary"),
        name="outproj",
    )(a, c, x2d, aw.reshape(1, -1), cw.reshape(1, -1), wo_bf16, fw.reshape(1, -1), w_router, b_router)


def _route_body(lg_ref, idx_ref, wt_ref, cnt_ref, run_ref):
    @pl.when(pl.program_id(0) == 0)
    def _():
        run_ref[...] = jnp.zeros_like(run_ref)

    logits = lg_ref[...]
    tm = logits.shape[0]
    lane = lax.broadcasted_iota(jnp.int32, (tm, LANES), 1)
    neg = -jnp.inf

    def first_argmax(v):
        m = jnp.max(v, axis=-1, keepdims=True)
        first = jnp.min(jnp.where(v == m, lane.astype(F32), float(LANES)), axis=-1, keepdims=True)
        return m, first.astype(jnp.int32)

    gl = jnp.where(lane < N_GROUPS, logits, neg)
    gmax, gidx = first_argmax(gl)
    g_w = 1.0 / jnp.sum(jnp.exp(gl - gmax), axis=-1, keepdims=True)
    first = N_GROUPS + gidx * EXPERTS_PER_GROUP
    el = jnp.where((lane >= first) & (lane < first + EXPERTS_PER_GROUP), logits, neg)
    m0, j0 = first_argmax(el)
    m1, j1 = first_argmax(jnp.where(lane == j0, neg, el))
    p1 = jnp.exp(m1 - m0)
    w0 = g_w / (1.0 + p1)
    w1 = g_w * p1 / (1.0 + p1)
    e0 = j0 - N_GROUPS
    e1 = j1 - N_GROUPS

    onehot = ((lane == e0) | (lane == e1)).astype(BF16)
    tri = (lax.broadcasted_iota(jnp.int32, (tm, tm), 0)
           > lax.broadcasted_iota(jnp.int32, (tm, tm), 1)).astype(BF16)
    before = jnp.dot(tri, onehot, preferred_element_type=F32) + run_ref[0:1, :]
    r0 = jnp.sum(jnp.where(lane == e0, before, 0.0), axis=-1, keepdims=True).astype(jnp.int32)
    r1 = jnp.sum(jnp.where(lane == e1, before, 0.0), axis=-1, keepdims=True).astype(jnp.int32)
    run = run_ref[0:1, :] + jnp.sum(onehot.astype(F32), axis=0, keepdims=True)
    run_ref[...] = jnp.broadcast_to(run, run_ref.shape)
    cnt_ref[...] = jnp.broadcast_to(run, cnt_ref.shape)

    zero = jnp.zeros_like(lane)
    idx_ref[...] = jnp.where(lane == 0, e0, jnp.where(lane == 1, e1,
                             jnp.where(lane == 2, r0, jnp.where(lane == 3, r1, zero))))
    wt_ref[...] = jnp.where(lane == 0, w0, jnp.where(lane == 1, w1, 0.0))


def _route(logits):
    t = logits.shape[0]
    tm = ROUTE_TM
    return pl.pallas_call(
        _route_body,
        grid=(t // tm,),
        in_specs=[pl.BlockSpec((tm, LANES), lambda i: (i, 0))],
        out_specs=[pl.BlockSpec((tm, LANES), lambda i: (i, 0)),
                   pl.BlockSpec((tm, LANES), lambda i: (i, 0)),
                   pl.BlockSpec((8, LANES), lambda i: (0, 0))],
        out_shape=[jax.ShapeDtypeStruct((t, LANES), jnp.int32),
                   jax.ShapeDtypeStruct((t, LANES), F32),
                   jax.ShapeDtypeStruct((8, LANES), F32)],
        scratch_shapes=[pltpu.VMEM((8, LANES), F32)],
        compiler_params=_params("arbitrary"),
        name="route",
    )(logits)


def _expert_body(sbe, sbs, sbn, sbr, tok, tail, hn_hbm, wg_ref, wu_ref, wd_ref, y_hbm,
                 x32, xb, acc, wgb, wub, wdb, gsem, osem):
    s = pl.program_id(0)
    c = pl.program_id(1)
    nsb = pl.num_programs(0)
    nch = pl.num_programs(1)
    rb = EXPERT_ROW_BLOCK
    half = D_MODEL // 2
    n = sbn[s]
    start = sbs[s]
    nblk = n // rb
    slot = s % 2

    def for_each(count, fn):
        def body(i, _):
            fn(i)
            return 0
        lax.fori_loop(0, count, body, 0)

    def block_rows(j):
        return pl.ds(pl.multiple_of(j * rb, rb), rb)

    def row_copy(sb, buf, i):
        src = hn_hbm.at[pl.ds(tok[sbs[sb] + i], 1)]
        return pltpu.make_async_copy(src, x32.at[buf, pl.ds(i, 1)], gsem.at[buf])

    def out_copy(first_row, j):
        dst = pl.ds(pl.multiple_of(first_row + j * rb, rb), rb)
        return pltpu.make_async_copy(acc.at[block_rows(j)], y_hbm.at[dst], osem)

    @pl.when((s == 0) & (c == 0))
    def _():
        def clear(j):
            for buf in range(2):
                x32[buf, block_rows(j), :] = jnp.zeros((rb, half), jnp.uint32)
        for_each(EXPERT_CAP // rb, clear)
        for_each(sbr[0], lambda i: row_copy(0, 0, i).start())

    prev = jnp.maximum(s - 1, 0)

    @pl.when((c == 0) & (s > 0) & (sbn[prev] > 0))
    def _():
        for_each(sbn[prev] // rb, lambda j: out_copy(sbs[prev], j).wait())

    @pl.when((s == nsb - 1) & (c == 0))
    def _():
        first = tail[0]
        nfill = (y_hbm.shape[0] - first) // rb
        acc[0:rb, :] = jnp.zeros((rb, D_MODEL), F32)

        def fill_copy(j):
            dst = pl.ds(pl.multiple_of(first + j * rb, rb), rb)
            return pltpu.make_async_copy(acc.at[0:rb], y_hbm.at[dst], osem)

        for_each(nfill, lambda j: fill_copy(j).start())
        for_each(nfill, lambda j: fill_copy(j).wait())

    @pl.when(n > 0)
    def _():
        @pl.when(c == 0)
        def _():
            for_each(sbr[s], lambda i: row_copy(s, slot, i).wait())

            def unpack(j):
                rows = block_rows(j)
                lo, hi = _unpack_bf16_pairs(x32[slot, rows, :])
                xb[rows, 0:half] = lo
                xb[rows, half:D_MODEL] = hi
                acc[rows, :] = jnp.zeros((rb, D_MODEL), F32)
            for_each(nblk, unpack)

            nxt = jnp.minimum(s + 1, nsb - 1)

            @pl.when((s + 1 < nsb) & (sbn[nxt] > 0))
            def _():
                for_each(sbr[nxt], lambda i: row_copy(nxt, 1 - slot, i).start())

        wgb[...] = wg_ref[0].astype(BF16)
        wub[...] = wu_ref[0].astype(BF16)
        wdb[...] = wd_ref[0].astype(BF16)

        def mlp(first_row, m):
            rows = pl.ds(pl.multiple_of(first_row, rb), m)
            x = xb[rows, :]
            g = jnp.dot(x, wgb[...], preferred_element_type=F32)
            u = jnp.dot(x, wub[...], preferred_element_type=F32)
            h = (jax.nn.silu(g) * u).astype(BF16)
            acc[rows, :] += jnp.dot(h, wdb[...], preferred_element_type=F32)

        wide = n // (2 * rb)
        for_each(wide, lambda j: mlp(j * 2 * rb, 2 * rb))

        @pl.when(n % (2 * rb) != 0)
        def _():
            mlp(wide * 2 * rb, rb)

        @pl.when(c == nch - 1)
        def _():
            for_each(nblk, lambda j: out_copy(start, j).start())


def _experts(hn_packed, w_gate, w_up, w_down, sb_expert, sb_start, sb_rows, sb_real, tok_buf, tail, n_rows):
    n_sb = sb_expert.shape[0]
    nch = EXPERT_FF // EXPERT_FF_CHUNK
    fc = EXPERT_FF_CHUNK

    def chunk(s, c, sbn):
        return jnp.where(sbn[s] > 0, c, nch - 1)

    def up_map(s, c, sbe, sbs, sbn, *_):
        return (sbe[s], 0, chunk(s, c, sbn))

    def down_map(s, c, sbe, sbs, sbn, *_):
        return (sbe[s], chunk(s, c, sbn), 0)

    grid_spec = pltpu.PrefetchScalarGridSpec(
        num_scalar_prefetch=6,
        grid=(n_sb, nch),
        in_specs=[pl.BlockSpec(memory_space=pl.ANY),
                  pl.BlockSpec((1, D_MODEL, fc), up_map),
                  pl.BlockSpec((1, D_MODEL, fc), up_map),
                  pl.BlockSpec((1, fc, D_MODEL), down_map)],
        out_specs=pl.BlockSpec(memory_space=pl.ANY),
        scratch_shapes=[pltpu.VMEM((2, EXPERT_CAP, D_MODEL // 2), jnp.uint32),
                        pltpu.VMEM((EXPERT_CAP, D_MODEL), BF16),
                        pltpu.VMEM((EXPERT_CAP, D_MODEL), F32),
                        pltpu.VMEM((D_MODEL, fc), BF16),
                        pltpu.VMEM((D_MODEL, fc), BF16),
                        pltpu.VMEM((fc, D_MODEL), BF16),
                        pltpu.SemaphoreType.DMA((2,)),
                        pltpu.SemaphoreType.DMA(())],
    )
    return pl.pallas_call(
        _expert_body,
        grid_spec=grid_spec,
        out_shape=jax.ShapeDtypeStruct((n_rows, D_MODEL), F32),
        compiler_params=_params("arbitrary", "arbitrary"),
        name="experts",
    )(sb_expert, sb_start, sb_rows, sb_real, tok_buf, tail, hn_packed, w_gate, w_up, w_down)


def _combine_body(dest, h1_ref, wt_ref, y_hbm, o_ref, g, sem):
    i = pl.program_id(0)
    nsteps = pl.num_programs(0)
    tm = h1_ref.shape[0]

    def row_copy(step, slot, r, k):
        src = y_hbm.at[pl.ds(dest[(step * tm + r) * TOP_K + k], 1)]
        return pltpu.make_async_copy(src, g.at[slot, k, pl.ds(r, 1)], sem.at[slot])

    def issue(step, slot):
        def f(r, _):
            for k in range(TOP_K):
                row_copy(step, slot, r, k).start()
            return 0
        lax.fori_loop(0, tm, f, 0)

    @pl.when(i == 0)
    def _():
        issue(0, 0)

    @pl.when(i + 1 < nsteps)
    def _():
        issue(i + 1, (i + 1) % 2)

    slot = i % 2

    def drain(r, _):
        for k in range(TOP_K):
            row_copy(i, slot, r, k).wait()
        return 0
    lax.fori_loop(0, tm, drain, 0)

    w = wt_ref[...]
    o_ref[...] = h1_ref[...] + (w[:, 0:1] * g[slot, 0] + w[:, 1:2] * g[slot, 1])


def _combine(h1, wts, y_buf, dest_flat):
    t = h1.shape[0]
    tm = COMBINE_TM
    grid_spec = pltpu.PrefetchScalarGridSpec(
        num_scalar_prefetch=1,
        grid=(t // tm,),
        in_specs=[pl.BlockSpec((tm, D_MODEL), lambda i, d: (i, 0)),
                  pl.BlockSpec((tm, LANES), lambda i, d: (i, 0)),
                  pl.BlockSpec(memory_space=pl.ANY)],
        out_specs=pl.BlockSpec((tm, D_MODEL), lambda i, d: (i, 0)),
        scratch_shapes=[pltpu.VMEM((2, TOP_K, tm, D_MODEL), F32),
                        pltpu.SemaphoreType.DMA((2,))],
    )
    return pl.pallas_call(
        _combine_body,
        grid_spec=grid_spec,
        out_shape=jax.ShapeDtypeStruct((t, D_MODEL), F32),
        compiler_params=_params("arbitrary"),
        name="combine",
    )(dest_flat, h1, wts, y_buf)


def _dispatch_tables(idx, cnt, t):
    rb, cap = EXPERT_ROW_BLOCK, EXPERT_CAP
    n_assign = t * TOP_K
    n_rows = -(-(n_assign + N_EXPERTS * (rb - 1)) // rb) * rb
    n_sb = (n_rows + N_EXPERTS * (cap - rb)) // cap
    experts = idx[:, 0:TOP_K]
    ranks = idx[:, TOP_K:2 * TOP_K]
    counts = cnt[0, :N_EXPERTS].astype(jnp.int32)
    padded = (counts + rb - 1) // rb * rb
    seg_end = jnp.cumsum(padded)
    seg_start = seg_end - padded
    dest = (seg_start[experts] + ranks).reshape(-1)
    tok = jnp.repeat(jnp.arange(t, dtype=jnp.int32), TOP_K)
    tok_buf = jnp.zeros((n_rows,), jnp.int32).at[dest].set(tok)

    sb_per_expert = (padded + cap - 1) // cap
    sb_end = jnp.cumsum(sb_per_expert)
    total = sb_end[-1]
    s = jnp.arange(n_sb, dtype=jnp.int32)
    s_eff = jnp.minimum(s, total - 1)
    e = jnp.minimum(jnp.searchsorted(sb_end, s_eff, side='right'), N_EXPERTS - 1).astype(jnp.int32)
    local = s_eff - (sb_end[e] - sb_per_expert[e])
    sb_start = (seg_start[e] + local * cap).astype(jnp.int32)
    sb_rows = jnp.where(s < total, jnp.clip(padded[e] - local * cap, 0, cap), 0).astype(jnp.int32)
    sb_real = jnp.where(s < total, jnp.clip(counts[e] - local * cap, 0, cap), 0).astype(jnp.int32)
    tail = seg_end[-1:].astype(jnp.int32)
    return dest.astype(jnp.int32), tok_buf, e, sb_start, sb_rows, sb_real, tail, n_rows


def kernel(x, meta_tokens, mix_norm_w, w_in, q_norm_w, k_norm_w, rel_bias, meta_bias, conv_w,
           attn_out_norm_w, conv_out_norm_w, w_out, ffn_norm_w, w_router_group, b_router_group,
           w_router_expert, b_router_expert, w_gate, w_up, w_down):
    bsz, seq, d = x.shape
    depth = mix_norm_w.shape[0]
    assert depth == 1 and d == D_MODEL and seq % GRID_W == 0
    t = bsz * seq
    x2d = x.reshape(t, d)
    l = 0

    w_in_b = w_in[l].astype(BF16)
    proj = _inproj(x2d, mix_norm_w[l], w_in_b, INPROJ_TM)
    proj_meta = _inproj(meta_tokens.astype(x.dtype), mix_norm_w[l], w_in_b, N_META)

    a = _attention(proj, proj_meta, q_norm_w[l], k_norm_w[l], rel_bias[l], meta_bias[l], bsz, seq)
    c = _short_conv(proj, proj_meta, conv_w[l], bsz, seq)

    w_router = jnp.zeros((d, LANES), F32)
    w_router = w_router.at[:, :N_GROUPS].set(w_router_group[l].astype(F32))
    w_router = w_router.at[:, N_GROUPS:N_GROUPS + N_EXPERTS].set(w_router_expert[l].astype(F32))
    b_router = jnp.zeros((1, LANES), F32)
    b_router = b_router.at[0, :N_GROUPS].set(b_router_group[l].astype(F32))
    b_router = b_router.at[0, N_GROUPS:N_GROUPS + N_EXPERTS].set(b_router_expert[l].astype(F32))
    h1, hn, logits = _outproj(a, c, x2d, attn_out_norm_w[l], conv_out_norm_w[l], w_out[l].astype(BF16),
                              ffn_norm_w[l], w_router, b_router)

    idx, wts, cnt = _route(logits)
    dest, tok_buf, sb_expert, sb_start, sb_rows, sb_real, tail, n_rows = _dispatch_tables(idx, cnt, t)
    y_buf = _experts(hn, w_gate.reshape(N_EXPERTS, d, EXPERT_FF), w_up.reshape(N_EXPERTS, d, EXPERT_FF),
                     w_down.reshape(N_EXPERTS, EXPERT_FF, d), sb_expert, sb_start, sb_rows, sb_real,
                     tok_buf, tail, n_rows)
    out = _combine(h1, wts, y_buf, dest)
    return out.reshape(bsz, seq, d)
```

```python
import functools

import jax
import jax.numpy as jnp
from jax import lax
from jax.experimental import pallas as pl
from jax.experimental.pallas import tpu as pltpu

F32 = jnp.float32
BF16 = jnp.bfloat16

D_MODEL = 2048
N_META = 16
GRID_W = 64
N_HEADS = 16
HEAD_DIM = 64
ATTN_WIDTH = N_HEADS * HEAD_DIM
CONV_WIDTH = D_MODEL - ATTN_WIDTH
PROJ_TOTAL = 3 * ATTN_WIDTH + 3 * CONV_WIDTH
WIN_ROWS = 8
WIN_COLS = 16
N_GROUPS = 4
EXPERTS_PER_GROUP = 8
N_EXPERTS = N_GROUPS * EXPERTS_PER_GROUP
TOP_K = 2
EXPERT_FF = 1024
EPS = 1e-6

LANES = 128
VMEM_LIMIT = 52 * 1024 * 1024

INPROJ_TM = 1024
INPROJ_TN = 512
OUTPROJ_TM = 256
ROUTE_TM = 512
EXPERT_ROW_BLOCK = 128
EXPERT_CAP = 1024
EXPERT_FF_CHUNK = 256
GATHER_UNROLL = 8
COMBINE_TM = 256


def _params(*sem):
    return pltpu.CompilerParams(dimension_semantics=sem, vmem_limit_bytes=VMEM_LIMIT)


def _inproj_body(x_ref, nw_ref, w_ref, o_ref, xn_ref):
    @pl.when(pl.program_id(1) == 0)
    def _():
        x = x_ref[...]
        ms = jnp.mean(x * x, axis=-1, keepdims=True)
        xn_ref[...] = (x * lax.rsqrt(ms + EPS) * nw_ref[...]).astype(BF16)

    o_ref[...] = jnp.dot(xn_ref[...], w_ref[...], preferred_element_type=F32)


def _inproj(x2d, norm_w, w_bf16, tm):
    m = x2d.shape[0]
    tn = INPROJ_TN
    return pl.pallas_call(
        _inproj_body,
        grid=(m // tm, PROJ_TOTAL // tn),
        in_specs=[pl.BlockSpec((tm, D_MODEL), lambda i, j: (i, 0)),
                  pl.BlockSpec((1, D_MODEL), lambda i, j: (0, 0)),
                  pl.BlockSpec((D_MODEL, tn), lambda i, j: (0, j))],
        out_specs=pl.BlockSpec((tm, tn), lambda i, j: (i, j)),
        out_shape=jax.ShapeDtypeStruct((m, PROJ_TOTAL), F32),
        scratch_shapes=[pltpu.VMEM((tm, D_MODEL), BF16)],
        compiler_params=_params("arbitrary", "arbitrary"),
        name="inproj",
    )(x2d, norm_w.reshape(1, D_MODEL), w_bf16)


def _head_norm(x, w, lo):
    x2 = x * x
    s_lo = jnp.sum(jnp.where(lo, x2, 0.0), axis=-1, keepdims=True)
    s_hi = jnp.sum(jnp.where(lo, 0.0, x2), axis=-1, keepdims=True)
    ms = jnp.where(lo, s_lo, s_hi) * (1.0 / HEAD_DIM)
    return x * lax.rsqrt(ms + EPS) * w


def _attn_body(q_ref, k_ref, v_ref, km_ref, vm_ref, qw_ref, kw_ref, bias_ref, mb_ref, o_ref,
               qs, ks, vs, pa, pb, pma, pmb, *, rows):
    lo = lax.broadcasted_iota(jnp.int32, (1, LANES), 1) < HEAD_DIM
    scale = HEAD_DIM ** -0.5
    chunk = 256
    seq = rows * GRID_W

    def prep(i, _):
        sl = pl.ds(pl.multiple_of(i * chunk, chunk), chunk)
        qs[sl, :] = (_head_norm(q_ref[sl, :], qw_ref[...], lo) * scale).astype(BF16)
        ks[sl, :] = _head_norm(k_ref[sl, :], kw_ref[...], lo).astype(BF16)
        vs[sl, :] = v_ref[sl, :].astype(BF16)
        return 0

    lax.fori_loop(0, seq // chunk, prep, 0)
    kmb = _head_norm(km_ref[...], kw_ref[...], lo).astype(BF16)
    vmb = vm_ref[...].astype(BF16)
    wr = min(WIN_ROWS, rows)
    nk = wr * GRID_W
    contract_last = (((1,), (1,)), ((), ()))

    def window_start(r):
        return jnp.clip(r - wr // 2, 0, rows - wr)

    def row_slice(r, n):
        return pl.ds(pl.multiple_of(r * GRID_W, GRID_W), n)

    def scores(r):
        rs = window_start(r)
        si = rs - r + (WIN_ROWS - 1)
        q_r = qs[row_slice(r, GRID_W), :]
        kwin = ks[row_slice(rs, nk), :]
        out = []
        for h in range(2):
            mask = lo if h == 0 else jnp.logical_not(lo)
            qh = jnp.where(mask, q_r, jnp.zeros_like(q_r))
            s = lax.dot_general(qh, kwin, contract_last, preferred_element_type=F32) + bias_ref[h, si]
            sm = lax.dot_general(qh, kmb, contract_last, preferred_element_type=F32) + mb_ref[h]
            out.append((s, sm))
        return out

    def store_softmax(row_scores, p_ref, pm_ref):
        for h, (s, sm) in enumerate(row_scores):
            m = jnp.maximum(jnp.max(s, axis=-1, keepdims=True), jnp.max(sm, axis=-1, keepdims=True))
            p = jnp.exp(s - m)
            pm = jnp.exp(sm - m)
            inv = 1.0 / (jnp.sum(p, axis=-1, keepdims=True) + jnp.sum(pm, axis=-1, keepdims=True))
            p_ref[h] = (p * inv).astype(BF16)
            pm_ref[h] = (pm * inv).astype(BF16)

    def weighted_values(r, p_ref, pm_ref):
        vwin = vs[row_slice(window_start(r), nk), :]
        outs = [jnp.dot(p_ref[h], vwin, preferred_element_type=F32)
                + jnp.dot(pm_ref[h], vmb, preferred_element_type=F32) for h in range(2)]
        o_ref[row_slice(r, GRID_W), :] = jnp.where(lo, outs[0], outs[1])

    pairs = rows // 2
    bufs = ((pa, pma), (pb, pmb))

    first = [scores(k) for k in range(2)]
    for k in range(2):
        store_softmax(first[k], *bufs[k])

    def pair_body(j, _):
        current = [scores(2 * j + k) for k in range(2)]
        for k in range(2):
            weighted_values(2 * j - 2 + k, *bufs[k])
        for k in range(2):
            store_softmax(current[k], *bufs[k])
        return 0

    lax.fori_loop(1, pairs, pair_body, 0)
    for k in range(2):
        weighted_values(rows - 2 + k, *bufs[k])


def _bias_table(rel_bias, rows):
    wr = min(WIN_ROWS, rows)
    c = jnp.arange(GRID_W)
    col_start = jnp.clip(c - WIN_COLS // 2, 0, GRID_W - WIN_COLS)
    col_mask = (c[None, :] >= col_start[:, None]) & (c[None, :] < col_start[:, None] + WIN_COLS)
    dc = jnp.clip(c[None, :] - c[:, None], -(WIN_COLS - 1), WIN_COLS - 1) + (WIN_COLS - 1)
    onehot = (dc[None] == jnp.arange(2 * WIN_COLS - 1)[:, None, None]).astype(F32)
    toep = jnp.einsum('hdc,cqk->hdqk', rel_bias.astype(F32), onehot, precision=lax.Precision.HIGHEST)
    toep = jnp.where(col_mask[None, None], toep, -jnp.inf)
    tbl = jnp.stack([toep[:, si:si + wr] for si in range(WIN_ROWS)], axis=1)
    return tbl.transpose(0, 1, 3, 2, 4).reshape(N_HEADS, WIN_ROWS, GRID_W, wr * GRID_W)


def _attention(proj, proj_meta, q_norm_w, k_norm_w, rel_bias, meta_bias, bsz, seq):
    rows = seq // GRID_W
    nk = min(WIN_ROWS, rows) * GRID_W
    npairs = N_HEADS // 2
    qw = jnp.tile(q_norm_w.astype(F32), 2).reshape(1, LANES)
    kw = jnp.tile(k_norm_w.astype(F32), 2).reshape(1, LANES)
    bias = _bias_table(rel_bias, rows)
    mb = meta_bias.astype(F32).reshape(N_HEADS, 1, N_META)
    return pl.pallas_call(
        functools.partial(_attn_body, rows=rows),
        grid=(bsz, npairs),
        in_specs=[pl.BlockSpec((seq, LANES), lambda b, p: (b, p)),
                  pl.BlockSpec((seq, LANES), lambda b, p: (b, npairs + p)),
                  pl.BlockSpec((seq, LANES), lambda b, p: (b, 2 * npairs + p)),
                  pl.BlockSpec((N_META, LANES), lambda b, p: (0, npairs + p)),
                  pl.BlockSpec((N_META, LANES), lambda b, p: (0, 2 * npairs + p)),
                  pl.BlockSpec((1, LANES), lambda b, p: (0, 0)),
                  pl.BlockSpec((1, LANES), lambda b, p: (0, 0)),
                  pl.BlockSpec((2, WIN_ROWS, GRID_W, nk), lambda b, p: (p, 0, 0, 0)),
                  pl.BlockSpec((2, 1, N_META), lambda b, p: (p, 0, 0))],
        out_specs=pl.BlockSpec((seq, LANES), lambda b, p: (b, p)),
        out_shape=jax.ShapeDtypeStruct((bsz * seq, ATTN_WIDTH), F32),
        scratch_shapes=([pltpu.VMEM((seq, LANES), BF16)] * 3
                        + [pltpu.VMEM((2, GRID_W, nk), BF16)] * 2
                        + [pltpu.VMEM((2, GRID_W, N_META), BF16)] * 2),
        compiler_params=_params("arbitrary", "arbitrary"),
        name="attention",
    )(proj, proj, proj, proj_meta, proj_meta, qw, kw, bias, mb)


def _conv_body(gb_ref, gc_ref, hc_ref, gcm_ref, hcm_ref, w_ref, o_ref):
    seq = gb_ref.shape[0]
    u = gc_ref[...] * hc_ref[...]
    u_meta_last = gcm_ref[N_META - 1:N_META, :] * hcm_ref[N_META - 1:N_META, :]
    row = lax.broadcasted_iota(jnp.int32, (seq, 1), 0)
    u_prev = jnp.where(row == 0, u_meta_last, pltpu.roll(u, 1, 0))
    u_next = jnp.where(row == seq - 1, 0.0, pltpu.roll(u, seq - 1, 0))
    w = w_ref[...]
    y = u_prev * w[0:1] + u * w[1:2] + u_next * w[2:3]
    o_ref[...] = gb_ref[...] * y


def _short_conv(proj, proj_meta, conv_w, bsz, seq):
    nct = CONV_WIDTH // LANES
    base = 3 * ATTN_WIDTH // LANES
    return pl.pallas_call(
        _conv_body,
        grid=(bsz, nct),
        in_specs=[pl.BlockSpec((seq, LANES), lambda b, c: (b, base + c)),
                  pl.BlockSpec((seq, LANES), lambda b, c: (b, base + nct + c)),
                  pl.BlockSpec((seq, LANES), lambda b, c: (b, base + 2 * nct + c)),
                  pl.BlockSpec((N_META, LANES), lambda b, c: (0, base + nct + c)),
                  pl.BlockSpec((N_META, LANES), lambda b, c: (0, base + 2 * nct + c)),
                  pl.BlockSpec((3, LANES), lambda b, c: (0, c))],
        out_specs=pl.BlockSpec((seq, LANES), lambda b, c: (b, c)),
        out_shape=jax.ShapeDtypeStruct((bsz * seq, CONV_WIDTH), F32),
        compiler_params=_params("arbitrary", "arbitrary"),
        name="short_conv",
    )(proj, proj, proj, proj_meta, proj_meta, conv_w.astype(F32))


def _rms(x, w):
    ms = jnp.mean(x * x, axis=-1, keepdims=True)
    return x * lax.rsqrt(ms + EPS) * w


def _pack_bf16_pairs(x):
    w = x.shape[1] // 2
    lo = lax.bitcast_convert_type(x[:, :w].astype(BF16).astype(F32), jnp.uint32)
    hi = lax.bitcast_convert_type(x[:, w:].astype(BF16).astype(F32), jnp.uint32)
    return (hi & jnp.uint32(0xFFFF0000)) | (lo >> 16)


def _unpack_bf16_pairs(p):
    lo = lax.bitcast_convert_type(p << 16, F32).astype(BF16)
    hi = lax.bitcast_convert_type(p & jnp.uint32(0xFFFF0000), F32).astype(BF16)
    return lo, hi


def _outproj_body(a_ref, c_ref, x_ref, aw_ref, cw_ref, wo_ref, fw_ref, wr_ref, br_ref,
                  h1_ref, hn_ref, lg_ref):
    an = _rms(a_ref[...], aw_ref[...]).astype(BF16)
    cn = _rms(c_ref[...], cw_ref[...]).astype(BF16)
    mixed = (jnp.dot(an, wo_ref[0:ATTN_WIDTH, :], preferred_element_type=F32)
             + jnp.dot(cn, wo_ref[ATTN_WIDTH:D_MODEL, :], preferred_element_type=F32))
    h1 = x_ref[...] + mixed
    h1_ref[...] = h1
    hn = _rms(h1, fw_ref[...])
    hn_ref[...] = _pack_bf16_pairs(hn)
    lg_ref[...] = jnp.dot(hn, wr_ref[...], preferred_element_type=F32,
                          precision=lax.Precision.HIGHEST) + br_ref[...]


def _outproj(a, c, x2d, aw, cw, wo_bf16, fw, w_router, b_router):
    t = x2d.shape[0]
    tm = OUTPROJ_TM
    row = lambda i: (i, 0)
    fixed = lambda i: (0, 0)
    return pl.pallas_call(
        _outproj_body,
        grid=(t // tm,),
        in_specs=[pl.BlockSpec((tm, ATTN_WIDTH), row),
                  pl.BlockSpec((tm, CONV_WIDTH), row),
                  pl.BlockSpec((tm, D_MODEL), row),
                  pl.BlockSpec((1, ATTN_WIDTH), fixed),
                  pl.BlockSpec((1, CONV_WIDTH), fixed),
                  pl.BlockSpec((D_MODEL, D_MODEL), fixed),
                  pl.BlockSpec((1, D_MODEL), fixed),
                  pl.BlockSpec((D_MODEL, LANES), fixed),
                  pl.BlockSpec((1, LANES), fixed)],
        out_specs=[pl.BlockSpec((tm, D_MODEL), row),
                   pl.BlockSpec((tm, D_MODEL // 2), row),
                   pl.BlockSpec((tm, LANES), row)],
        out_shape=[jax.ShapeDtypeStruct((t, D_MODEL), F32),
                   jax.ShapeDtypeStruct((t, D_MODEL // 2), jnp.uint32),
                   jax.ShapeDtypeStruct((t, LANES), F32)],
        compiler_params=_params("arbitrary"),
        name="outproj",
    )(a, c, x2d, aw.reshape(1, -1), cw.reshape(1, -1), wo_bf16, fw.reshape(1, -1), w_router, b_router)


def _route_body(lg_ref, idx_ref, wt_ref, cnt_ref, run_ref):
    @pl.when(pl.program_id(0) == 0)
    def _():
        run_ref[...] = jnp.zeros_like(run_ref)

    logits = lg_ref[...]
    tm = logits.shape[0]
    lane = lax.broadcasted_iota(jnp.int32, (tm, LANES), 1)
    neg = -jnp.inf

    def first_argmax(v):
        m = jnp.max(v, axis=-1, keepdims=True)
        first = jnp.min(jnp.where(v == m, lane.astype(F32), float(LANES)), axis=-1, keepdims=True)
        return m, first.astype(jnp.int32)

    gl = jnp.where(lane < N_GROUPS, logits, neg)
    gmax, gidx = first_argmax(gl)
    g_w = 1.0 / jnp.sum(jnp.exp(gl - gmax), axis=-1, keepdims=True)
    first = N_GROUPS + gidx * EXPERTS_PER_GROUP
    el = jnp.where((lane >= first) & (lane < first + EXPERTS_PER_GROUP), logits, neg)
    m0, j0 = first_argmax(el)
    m1, j1 = first_argmax(jnp.where(lane == j0, neg, el))
    p1 = jnp.exp(m1 - m0)
    w0 = g_w / (1.0 + p1)
    w1 = g_w * p1 / (1.0 + p1)
    e0 = j0 - N_GROUPS
    e1 = j1 - N_GROUPS

    onehot = ((lane == e0) | (lane == e1)).astype(BF16)
    tri = (lax.broadcasted_iota(jnp.int32, (tm, tm), 0)
           > lax.broadcasted_iota(jnp.int32, (tm, tm), 1)).astype(BF16)
    before = jnp.dot(tri, onehot, preferred_element_type=F32) + run_ref[0:1, :]
    r0 = jnp.sum(jnp.where(lane == e0, before, 0.0), axis=-1, keepdims=True).astype(jnp.int32)
    r1 = jnp.sum(jnp.where(lane == e1, before, 0.0), axis=-1, keepdims=True).astype(jnp.int32)
    run = run_ref[0:1, :] + jnp.sum(onehot.astype(F32), axis=0, keepdims=True)
    run_ref[...] = jnp.broadcast_to(run, run_ref.shape)
    cnt_ref[...] = jnp.broadcast_to(run, cnt_ref.shape)

    zero = jnp.zeros_like(lane)
    idx_ref[...] = jnp.where(lane == 0, e0, jnp.where(lane == 1, e1,
                             jnp.where(lane == 2, r0, jnp.where(lane == 3, r1, zero))))
    wt_ref[...] = jnp.where(lane == 0, w0, jnp.where(lane == 1, w1, 0.0))


def _route(logits):
    t = logits.shape[0]
    tm = ROUTE_TM
    return pl.pallas_call(
        _route_body,
        grid=(t // tm,),
        in_specs=[pl.BlockSpec((tm, LANES), lambda i: (i, 0))],
        out_specs=[pl.BlockSpec((tm, LANES), lambda i: (i, 0)),
                   pl.BlockSpec((tm, LANES), lambda i: (i, 0)),
                   pl.BlockSpec((8, LANES), lambda i: (0, 0))],
        out_shape=[jax.ShapeDtypeStruct((t, LANES), jnp.int32),
                   jax.ShapeDtypeStruct((t, LANES), F32),
                   jax.ShapeDtypeStruct((8, LANES), F32)],
        scratch_shapes=[pltpu.VMEM((8, LANES), F32)],
        compiler_params=_params("arbitrary"),
        name="route",
    )(logits)


def _expert_body(sbe, sbs, sbn, sbr, tok, tail, hn_hbm, wg_ref, wu_ref, wd_ref, y_hbm,
                 x32, xb, acc, wgb, wub, wdb, gsem, osem):
    s = pl.program_id(0)
    c = pl.program_id(1)
    nsb = pl.num_programs(0)
    nch = pl.num_programs(1)
    rb = EXPERT_ROW_BLOCK
    half = D_MODEL // 2
    n = sbn[s]
    start = sbs[s]
    nblk = n // rb
    slot = s % 2

    def for_each(count, fn):
        def body(i, _):
            fn(i)
            return 0
        lax.fori_loop(0, count, body, 0)

    def block_rows(j):
        return pl.ds(pl.multiple_of(j * rb, rb), rb)

    def start_gather(sb, buf):
        first = sbs[sb]

        def group(g):
            for k in range(GATHER_UNROLL):
                i = g * GATHER_UNROLL + k
                src = hn_hbm.at[pl.ds(tok[first + i], 1)]
                pltpu.make_async_copy(src, x32.at[buf, pl.ds(i, 1)], gsem.at[buf]).start()
        for_each(sbr[sb] // GATHER_UNROLL, group)

    def wait_gather(sb, buf):
        rows = pl.ds(0, pl.multiple_of(sbr[sb], GATHER_UNROLL))
        pltpu.make_async_copy(hn_hbm.at[rows], x32.at[buf, rows], gsem.at[buf]).wait()

    def out_copy(first_row, j):
        dst = pl.ds(pl.multiple_of(first_row + j * rb, rb), rb)
        return pltpu.make_async_copy(acc.at[block_rows(j)], y_hbm.at[dst], osem)

    @pl.when((s == 0) & (c == 0))
    def _():
        def clear(j):
            for buf in range(2):
                x32[buf, block_rows(j), :] = jnp.zeros((rb, half), jnp.uint32)
            acc[block_rows(j), :] = jnp.zeros((rb, D_MODEL), F32)
        for_each(EXPERT_CAP // rb, clear)
        start_gather(0, 0)

    prev = jnp.maximum(s - 1, 0)
    prev_pending = (c == 0) & (s > 0) & (sbn[prev] > 0)

    def wait_prev_output():
        for_each(sbn[prev] // rb, lambda j: out_copy(sbs[prev], j).wait())

    @pl.when(prev_pending & (n == 0))
    def _():
        wait_prev_output()

    @pl.when((s == nsb - 1) & (c == 0))
    def _():
        first = tail[0]
        nfill = (y_hbm.shape[0] - first) // rb
        acc[0:rb, :] = jnp.zeros((rb, D_MODEL), F32)

        def fill_copy(j):
            dst = pl.ds(pl.multiple_of(first + j * rb, rb), rb)
            return pltpu.make_async_copy(acc.at[0:rb], y_hbm.at[dst], osem)

        for_each(nfill, lambda j: fill_copy(j).start())
        for_each(nfill, lambda j: fill_copy(j).wait())

    @pl.when(n > 0)
    def _():
        @pl.when(c == 0)
        def _():
            wait_gather(s, slot)

            def unpack(j):
                rows = block_rows(j)
                lo, hi = _unpack_bf16_pairs(x32[slot, rows, :])
                xb[rows, 0:half] = lo
                xb[rows, half:D_MODEL] = hi
            for_each(nblk, unpack)

            nxt = jnp.minimum(s + 1, nsb - 1)

            @pl.when((s + 1 < nsb) & (sbn[nxt] > 0))
            def _():
                start_gather(nxt, 1 - slot)

            @pl.when(prev_pending)
            def _():
                wait_prev_output()

        wgb[...] = wg_ref[0].astype(BF16)
        wub[...] = wu_ref[0].astype(BF16)
        wdb[...] = wd_ref[0].astype(BF16)

        def mlp(first_row, m):
            rows = pl.ds(pl.multiple_of(first_row, rb), m)
            x = xb[rows, :]
            g = jnp.dot(x, wgb[...], preferred_element_type=F32)
            u = jnp.dot(x, wub[...], preferred_element_type=F32)
            h = (jax.nn.silu(g) * u).astype(BF16)
            y = jnp.dot(h, wdb[...], preferred_element_type=F32)
            acc[rows, :] = jnp.where(c == 0, y, acc[rows, :] + y)

        wide = n // (2 * rb)
        for_each(wide, lambda j: mlp(j * 2 * rb, 2 * rb))

        @pl.when(n % (2 * rb) != 0)
        def _():
            mlp(wide * 2 * rb, rb)

        @pl.when(c == nch - 1)
        def _():
            for_each(nblk, lambda j: out_copy(start, j).start())


def _experts(hn_packed, w_gate, w_up, w_down, sb_expert, sb_start, sb_rows, sb_real, tok_buf, tail, n_rows):
    n_sb = sb_expert.shape[0]
    nch = EXPERT_FF // EXPERT_FF_CHUNK
    fc = EXPERT_FF_CHUNK

    def chunk(s, c, sbn):
        return jnp.where(sbn[s] > 0, c, nch - 1)

    def up_map(s, c, sbe, sbs, sbn, *_):
        return (sbe[s], 0, chunk(s, c, sbn))

    def down_map(s, c, sbe, sbs, sbn, *_):
        return (sbe[s], chunk(s, c, sbn), 0)

    grid_spec = pltpu.PrefetchScalarGridSpec(
        num_scalar_prefetch=6,
        grid=(n_sb, nch),
        in_specs=[pl.BlockSpec(memory_space=pl.ANY),
                  pl.BlockSpec((1, D_MODEL, fc), up_map),
                  pl.BlockSpec((1, D_MODEL, fc), up_map),
                  pl.BlockSpec((1, fc, D_MODEL), down_map)],
        out_specs=pl.BlockSpec(memory_space=pl.ANY),
        scratch_shapes=[pltpu.VMEM((2, EXPERT_CAP, D_MODEL // 2), jnp.uint32),
                        pltpu.VMEM((EXPERT_CAP, D_MODEL), BF16),
                        pltpu.VMEM((EXPERT_CAP, D_MODEL), F32),
                        pltpu.VMEM((D_MODEL, fc), BF16),
                        pltpu.VMEM((D_MODEL, fc), BF16),
                        pltpu.VMEM((fc, D_MODEL), BF16),
                        pltpu.SemaphoreType.DMA((2,)),
                        pltpu.SemaphoreType.DMA(())],
    )
    return pl.pallas_call(
        _expert_body,
        grid_spec=grid_spec,
        out_shape=jax.ShapeDtypeStruct((n_rows, D_MODEL), F32),
        compiler_params=_params("arbitrary", "arbitrary"),
        name="experts",
    )(sb_expert, sb_start, sb_rows, sb_real, tok_buf, tail, hn_packed, w_gate, w_up, w_down)


def _combine_body(dest, h1_ref, wt_ref, y_hbm, o_ref, g, sem):
    i = pl.program_id(0)
    nsteps = pl.num_programs(0)
    tm = h1_ref.shape[0]

    def issue(step, slot):
        def f(q, _):
            for u in range(GATHER_UNROLL):
                r = q * GATHER_UNROLL + u
                for k in range(TOP_K):
                    src = y_hbm.at[pl.ds(dest[(step * tm + r) * TOP_K + k], 1)]
                    pltpu.make_async_copy(src, g.at[slot, k, pl.ds(r, 1)], sem.at[slot]).start()
            return 0
        lax.fori_loop(0, tm // GATHER_UNROLL, f, 0)

    @pl.when(i == 0)
    def _():
        issue(0, 0)

    @pl.when(i + 1 < nsteps)
    def _():
        issue(i + 1, (i + 1) % 2)

    slot = i % 2
    for k in range(TOP_K):
        pltpu.make_async_copy(y_hbm.at[pl.ds(0, tm)], g.at[slot, k], sem.at[slot]).wait()

    w = wt_ref[...]
    o_ref[...] = h1_ref[...] + (w[:, 0:1] * g[slot, 0] + w[:, 1:2] * g[slot, 1])


def _combine(h1, wts, y_buf, dest_flat):
    t = h1.shape[0]
    tm = COMBINE_TM
    grid_spec = pltpu.PrefetchScalarGridSpec(
        num_scalar_prefetch=1,
        grid=(t // tm,),
        in_specs=[pl.BlockSpec((tm, D_MODEL), lambda i, d: (i, 0)),
                  pl.BlockSpec((tm, LANES), lambda i, d: (i, 0)),
                  pl.BlockSpec(memory_space=pl.ANY)],
        out_specs=pl.BlockSpec((tm, D_MODEL), lambda i, d: (i, 0)),
        scratch_shapes=[pltpu.VMEM((2, TOP_K, tm, D_MODEL), F32),
                        pltpu.SemaphoreType.DMA((2,))],
    )
    return pl.pallas_call(
        _combine_body,
        grid_spec=grid_spec,
        out_shape=jax.ShapeDtypeStruct((t, D_MODEL), F32),
        compiler_params=_params("arbitrary"),
        name="combine",
    )(dest_flat, h1, wts, y_buf)


def _dispatch_tables(idx, cnt, t):
    rb, cap = EXPERT_ROW_BLOCK, EXPERT_CAP
    n_assign = t * TOP_K
    n_rows = -(-(n_assign + N_EXPERTS * (rb - 1)) // rb) * rb
    n_sb = (n_rows + N_EXPERTS * (cap - rb)) // cap
    experts = idx[:, 0:TOP_K]
    ranks = idx[:, TOP_K:2 * TOP_K]
    counts = cnt[0, :N_EXPERTS].astype(jnp.int32)
    padded = (counts + rb - 1) // rb * rb
    seg_end = jnp.cumsum(padded)
    seg_start = seg_end - padded
    dest = (seg_start[experts] + ranks).reshape(-1)
    tok = jnp.repeat(jnp.arange(t, dtype=jnp.int32), TOP_K)
    tok_buf = jnp.zeros((n_rows,), jnp.int32).at[dest].set(tok)

    sb_per_expert = (padded + cap - 1) // cap
    sb_end = jnp.cumsum(sb_per_expert)
    total = sb_end[-1]
    s = jnp.arange(n_sb, dtype=jnp.int32)
    s_eff = jnp.minimum(s, total - 1)
    e = jnp.minimum(jnp.searchsorted(sb_end, s_eff, side='right'), N_EXPERTS - 1).astype(jnp.int32)
    local = s_eff - (sb_end[e] - sb_per_expert[e])
    sb_start = (seg_start[e] + local * cap).astype(jnp.int32)
    sb_rows = jnp.where(s < total, jnp.clip(padded[e] - local * cap, 0, cap), 0).astype(jnp.int32)
    gathered = (counts + GATHER_UNROLL - 1) // GATHER_UNROLL * GATHER_UNROLL
    sb_real = jnp.where(s < total, jnp.clip(gathered[e] - local * cap, 0, cap), 0).astype(jnp.int32)
    tail = seg_end[-1:].astype(jnp.int32)
    return dest.astype(jnp.int32), tok_buf, e, sb_start, sb_rows, sb_real, tail, n_rows


def kernel(x, meta_tokens, mix_norm_w, w_in, q_norm_w, k_norm_w, rel_bias, meta_bias, conv_w,
           attn_out_norm_w, conv_out_norm_w, w_out, ffn_norm_w, w_router_group, b_router_group,
           w_router_expert, b_router_expert, w_gate, w_up, w_down):
    bsz, seq, d = x.shape
    depth = mix_norm_w.shape[0]
    assert depth == 1 and d == D_MODEL and seq % GRID_W == 0
    t = bsz * seq
    x2d = x.reshape(t, d)
    l = 0

    w_in_b = w_in[l].astype(BF16)
    proj = _inproj(x2d, mix_norm_w[l], w_in_b, INPROJ_TM)
    proj_meta = _inproj(meta_tokens.astype(x.dtype), mix_norm_w[l], w_in_b, N_META)

    a = _attention(proj, proj_meta, q_norm_w[l], k_norm_w[l], rel_bias[l], meta_bias[l], bsz, seq)
    c = _short_conv(proj, proj_meta, conv_w[l], bsz, seq)

    w_router = jnp.zeros((d, LANES), F32)
    w_router = w_router.at[:, :N_GROUPS].set(w_router_group[l].astype(F32))
    w_router = w_router.at[:, N_GROUPS:N_GROUPS + N_EXPERTS].set(w_router_expert[l].astype(F32))
    b_router = jnp.zeros((1, LANES), F32)
    b_router = b_router.at[0, :N_GROUPS].set(b_router_group[l].astype(F32))
    b_router = b_router.at[0, N_GROUPS:N_GROUPS + N_EXPERTS].set(b_router_expert[l].astype(F32))
    h1, hn, logits = _outproj(a, c, x2d, attn_out_norm_w[l], conv_out_norm_w[l], w_out[l].astype(BF16),
                              ffn_norm_w[l], w_router, b_router)

    idx, wts, cnt = _route(logits)
    dest, tok_buf, sb_expert, sb_start, sb_rows, sb_real, tail, n_rows = _dispatch_tables(idx, cnt, t)
    y_buf = _experts(hn, w_gate.reshape(N_EXPERTS, d, EXPERT_FF), w_up.reshape(N_EXPERTS, d, EXPERT_FF),
                     w_down.reshape(N_EXPERTS, EXPERT_FF, d), sb_expert, sb_start, sb_rows, sb_real,
                     tok_buf, tail, n_rows)
    out = _combine(h1, wts, y_buf, dest)
    return out.reshape(bsz, seq, d)
```

```python
import functools

import jax
import jax.numpy as jnp
from jax import lax
from jax.experimental import pallas as pl
from jax.experimental.pallas import tpu as pltpu

F32 = jnp.float32
BF16 = jnp.bfloat16

D_MODEL = 2048
N_META = 16
GRID_W = 64
N_HEADS = 16
HEAD_DIM = 64
ATTN_WIDTH = N_HEADS * HEAD_DIM
CONV_WIDTH = D_MODEL - ATTN_WIDTH
PROJ_TOTAL = 3 * ATTN_WIDTH + 3 * CONV_WIDTH
WIN_ROWS = 8
WIN_COLS = 16
N_GROUPS = 4
EXPERTS_PER_GROUP = 8
N_EXPERTS = N_GROUPS * EXPERTS_PER_GROUP
TOP_K = 2
EXPERT_FF = 1024
EPS = 1e-6

LANES = 128
VMEM_LIMIT = 52 * 1024 * 1024

INPROJ_TM = 1024
INPROJ_TN = 512
OUTPROJ_TM = 512
OUTPROJ_SUB = 256
ROUTE_TM = 512
EXPERT_ROW_BLOCK = 128
EXPERT_CAP = 1024
EXPERT_FF_CHUNK = 256
GATHER_UNROLL = 8
COMBINE_TM = 256


def _params(*sem):
    return pltpu.CompilerParams(dimension_semantics=sem, vmem_limit_bytes=VMEM_LIMIT)


def _inproj_body(x_ref, nw_ref, w_ref, o_ref, xn_ref):
    @pl.when(pl.program_id(1) == 0)
    def _():
        x = x_ref[...]
        ms = jnp.mean(x * x, axis=-1, keepdims=True)
        xn_ref[...] = (x * lax.rsqrt(ms + EPS) * nw_ref[...]).astype(BF16)

    o_ref[...] = jnp.dot(xn_ref[...], w_ref[...], preferred_element_type=F32)


def _inproj(x2d, norm_w, w_bf16, tm):
    m = x2d.shape[0]
    tn = INPROJ_TN
    return pl.pallas_call(
        _inproj_body,
        grid=(m // tm, PROJ_TOTAL // tn),
        in_specs=[pl.BlockSpec((tm, D_MODEL), lambda i, j: (i, 0)),
                  pl.BlockSpec((1, D_MODEL), lambda i, j: (0, 0)),
                  pl.BlockSpec((D_MODEL, tn), lambda i, j: (0, j))],
        out_specs=pl.BlockSpec((tm, tn), lambda i, j: (i, j)),
        out_shape=jax.ShapeDtypeStruct((m, PROJ_TOTAL), F32),
        scratch_shapes=[pltpu.VMEM((tm, D_MODEL), BF16)],
        compiler_params=_params("arbitrary", "arbitrary"),
        name="inproj",
    )(x2d, norm_w.reshape(1, D_MODEL), w_bf16)


def _head_norm(x, w, lo):
    x2 = x * x
    s_lo = jnp.sum(jnp.where(lo, x2, 0.0), axis=-1, keepdims=True)
    s_hi = jnp.sum(jnp.where(lo, 0.0, x2), axis=-1, keepdims=True)
    ms = jnp.where(lo, s_lo, s_hi) * (1.0 / HEAD_DIM)
    return x * lax.rsqrt(ms + EPS) * w


def _attn_body(q_ref, k_ref, v_ref, km_ref, vm_ref, qw_ref, kw_ref, bias_ref, mb_ref, o_ref,
               qs, ks, vs, pa, pb, pma, pmb, *, rows):
    lo = lax.broadcasted_iota(jnp.int32, (1, LANES), 1) < HEAD_DIM
    scale = HEAD_DIM ** -0.5
    chunk = 256
    seq = rows * GRID_W

    def prep(i, _):
        sl = pl.ds(pl.multiple_of(i * chunk, chunk), chunk)
        qs[sl, :] = (_head_norm(q_ref[sl, :], qw_ref[...], lo) * scale).astype(BF16)
        ks[sl, :] = _head_norm(k_ref[sl, :], kw_ref[...], lo).astype(BF16)
        vs[sl, :] = v_ref[sl, :].astype(BF16)
        return 0

    lax.fori_loop(0, seq // chunk, prep, 0)
    kmb = _head_norm(km_ref[...], kw_ref[...], lo).astype(BF16)
    vmb = vm_ref[...].astype(BF16)
    wr = min(WIN_ROWS, rows)
    nk = wr * GRID_W
    contract_last = (((1,), (1,)), ((), ()))

    def window_start(r):
        return jnp.clip(r - wr // 2, 0, rows - wr)

    def row_slice(r, n):
        return pl.ds(pl.multiple_of(r * GRID_W, GRID_W), n)

    def scores(r):
        rs = window_start(r)
        si = rs - r + (WIN_ROWS - 1)
        q_r = qs[row_slice(r, GRID_W), :]
        kwin = ks[row_slice(rs, nk), :]
        out = []
        for h in range(2):
            mask = lo if h == 0 else jnp.logical_not(lo)
            qh = jnp.where(mask, q_r, jnp.zeros_like(q_r))
            s = lax.dot_general(qh, kwin, contract_last, preferred_element_type=F32) + bias_ref[h, si]
            sm = lax.dot_general(qh, kmb, contract_last, preferred_element_type=F32) + mb_ref[h]
            out.append((s, sm))
        return out

    def store_softmax(row_scores, p_ref, pm_ref):
        for h, (s, sm) in enumerate(row_scores):
            m = jnp.maximum(jnp.max(s, axis=-1, keepdims=True), jnp.max(sm, axis=-1, keepdims=True))
            p = jnp.exp(s - m)
            pm = jnp.exp(sm - m)
            inv = 1.0 / (jnp.sum(p, axis=-1, keepdims=True) + jnp.sum(pm, axis=-1, keepdims=True))
            p_ref[h] = (p * inv).astype(BF16)
            pm_ref[h] = (pm * inv).astype(BF16)

    def weighted_values(r, p_ref, pm_ref):
        vwin = vs[row_slice(window_start(r), nk), :]
        outs = [jnp.dot(p_ref[h], vwin, preferred_element_type=F32)
                + jnp.dot(pm_ref[h], vmb, preferred_element_type=F32) for h in range(2)]
        o_ref[row_slice(r, GRID_W), :] = jnp.where(lo, outs[0], outs[1])

    pairs = rows // 2
    bufs = ((pa, pma), (pb, pmb))

    first = [scores(k) for k in range(2)]
    for k in range(2):
        store_softmax(first[k], *bufs[k])

    def pair_body(j, _):
        current = [scores(2 * j + k) for k in range(2)]
        for k in range(2):
            weighted_values(2 * j - 2 + k, *bufs[k])
        for k in range(2):
            store_softmax(current[k], *bufs[k])
        return 0

    lax.fori_loop(1, pairs, pair_body, 0)
    for k in range(2):
        weighted_values(rows - 2 + k, *bufs[k])


def _bias_table(rel_bias, rows):
    wr = min(WIN_ROWS, rows)
    c = jnp.arange(GRID_W)
    col_start = jnp.clip(c - WIN_COLS // 2, 0, GRID_W - WIN_COLS)
    col_mask = (c[None, :] >= col_start[:, None]) & (c[None, :] < col_start[:, None] + WIN_COLS)
    dc = jnp.clip(c[None, :] - c[:, None], -(WIN_COLS - 1), WIN_COLS - 1) + (WIN_COLS - 1)
    onehot = (dc[None] == jnp.arange(2 * WIN_COLS - 1)[:, None, None]).astype(F32)
    toep = jnp.einsum('hdc,cqk->hdqk', rel_bias.astype(F32), onehot, precision=lax.Precision.HIGHEST)
    toep = jnp.where(col_mask[None, None], toep, -jnp.inf)
    tbl = jnp.stack([toep[:, si:si + wr] for si in range(WIN_ROWS)], axis=1)
    return tbl.transpose(0, 1, 3, 2, 4).reshape(N_HEADS, WIN_ROWS, GRID_W, wr * GRID_W)


def _attention(proj, proj_meta, q_norm_w, k_norm_w, rel_bias, meta_bias, bsz, seq):
    rows = seq // GRID_W
    nk = min(WIN_ROWS, rows) * GRID_W
    npairs = N_HEADS // 2
    qw = jnp.tile(q_norm_w.astype(F32), 2).reshape(1, LANES)
    kw = jnp.tile(k_norm_w.astype(F32), 2).reshape(1, LANES)
    bias = _bias_table(rel_bias, rows)
    mb = meta_bias.astype(F32).reshape(N_HEADS, 1, N_META)
    return pl.pallas_call(
        functools.partial(_attn_body, rows=rows),
        grid=(bsz, npairs),
        in_specs=[pl.BlockSpec((seq, LANES), lambda b, p: (b, p)),
                  pl.BlockSpec((seq, LANES), lambda b, p: (b, npairs + p)),
                  pl.BlockSpec((seq, LANES), lambda b, p: (b, 2 * npairs + p)),
                  pl.BlockSpec((N_META, LANES), lambda b, p: (0, npairs + p)),
                  pl.BlockSpec((N_META, LANES), lambda b, p: (0, 2 * npairs + p)),
                  pl.BlockSpec((1, LANES), lambda b, p: (0, 0)),
                  pl.BlockSpec((1, LANES), lambda b, p: (0, 0)),
                  pl.BlockSpec((2, WIN_ROWS, GRID_W, nk), lambda b, p: (p, 0, 0, 0)),
                  pl.BlockSpec((2, 1, N_META), lambda b, p: (p, 0, 0))],
        out_specs=pl.BlockSpec((seq, LANES), lambda b, p: (b, p)),
        out_shape=jax.ShapeDtypeStruct((bsz * seq, ATTN_WIDTH), F32),
        scratch_shapes=([pltpu.VMEM((seq, LANES), BF16)] * 3
                        + [pltpu.VMEM((2, GRID_W, nk), BF16)] * 2
                        + [pltpu.VMEM((2, GRID_W, N_META), BF16)] * 2),
        compiler_params=_params("arbitrary", "arbitrary"),
        name="attention",
    )(proj, proj, proj, proj_meta, proj_meta, qw, kw, bias, mb)


def _conv_body(gb_ref, gc_ref, hc_ref, gcm_ref, hcm_ref, w_ref, o_ref):
    seq = gb_ref.shape[0]
    u = gc_ref[...] * hc_ref[...]
    u_meta_last = gcm_ref[N_META - 1:N_META, :] * hcm_ref[N_META - 1:N_META, :]
    row = lax.broadcasted_iota(jnp.int32, (seq, 1), 0)
    u_prev = jnp.where(row == 0, u_meta_last, pltpu.roll(u, 1, 0))
    u_next = jnp.where(row == seq - 1, 0.0, pltpu.roll(u, seq - 1, 0))
    w = w_ref[...]
    y = u_prev * w[0:1] + u * w[1:2] + u_next * w[2:3]
    o_ref[...] = gb_ref[...] * y


def _short_conv(proj, proj_meta, conv_w, bsz, seq):
    nct = CONV_WIDTH // LANES
    base = 3 * ATTN_WIDTH // LANES
    return pl.pallas_call(
        _conv_body,
        grid=(bsz, nct),
        in_specs=[pl.BlockSpec((seq, LANES), lambda b, c: (b, base + c)),
                  pl.BlockSpec((seq, LANES), lambda b, c: (b, base + nct + c)),
                  pl.BlockSpec((seq, LANES), lambda b, c: (b, base + 2 * nct + c)),
                  pl.BlockSpec((N_META, LANES), lambda b, c: (0, base + nct + c)),
                  pl.BlockSpec((N_META, LANES), lambda b, c: (0, base + 2 * nct + c)),
                  pl.BlockSpec((3, LANES), lambda b, c: (0, c))],
        out_specs=pl.BlockSpec((seq, LANES), lambda b, c: (b, c)),
        out_shape=jax.ShapeDtypeStruct((bsz * seq, CONV_WIDTH), F32),
        compiler_params=_params("arbitrary", "arbitrary"),
        name="short_conv",
    )(proj, proj, proj, proj_meta, proj_meta, conv_w.astype(F32))


def _rms(x, w):
    ms = jnp.mean(x * x, axis=-1, keepdims=True)
    return x * lax.rsqrt(ms + EPS) * w


def _pack_bf16_pairs(x):
    w = x.shape[1] // 2
    lo = lax.bitcast_convert_type(x[:, :w].astype(BF16).astype(F32), jnp.uint32)
    hi = lax.bitcast_convert_type(x[:, w:].astype(BF16).astype(F32), jnp.uint32)
    return (hi & jnp.uint32(0xFFFF0000)) | (lo >> 16)


def _unpack_bf16_pairs(p):
    lo = lax.bitcast_convert_type(p << 16, F32).astype(BF16)
    hi = lax.bitcast_convert_type(p & jnp.uint32(0xFFFF0000), F32).astype(BF16)
    return lo, hi


def _outproj_body(a_ref, c_ref, x_ref, aw_ref, cw_ref, wo_ref, fw_ref, wr_ref, br_ref,
                  h1_ref, hn_ref, lg_ref):
    sub = OUTPROJ_SUB
    blocks = [pl.ds(k * sub, sub) for k in range(a_ref.shape[0] // sub)]

    def mix(rows):
        an = _rms(a_ref[rows, :], aw_ref[...]).astype(BF16)
        cn = _rms(c_ref[rows, :], cw_ref[...]).astype(BF16)
        return (jnp.dot(an, wo_ref[0:ATTN_WIDTH, :], preferred_element_type=F32)
                + jnp.dot(cn, wo_ref[ATTN_WIDTH:D_MODEL, :], preferred_element_type=F32))

    def finish(rows, mixed):
        h1 = x_ref[rows, :] + mixed
        h1_ref[rows, :] = h1
        hn = _rms(h1, fw_ref[...])
        hn_ref[rows, :] = _pack_bf16_pairs(hn)
        lg_ref[rows, :] = jnp.dot(hn.astype(BF16), wr_ref[...], preferred_element_type=F32) + br_ref[...]

    mixed = mix(blocks[0])
    for k, rows in enumerate(blocks):
        following = mix(blocks[k + 1]) if k + 1 < len(blocks) else None
        finish(rows, mixed)
        mixed = following


def _outproj(a, c, x2d, aw, cw, wo_bf16, fw, w_router, b_router):
    t = x2d.shape[0]
    tm = OUTPROJ_TM
    row = lambda i: (i, 0)
    fixed = lambda i: (0, 0)
    return pl.pallas_call(
        _outproj_body,
        grid=(t // tm,),
        in_specs=[pl.BlockSpec((tm, ATTN_WIDTH), row),
                  pl.BlockSpec((tm, CONV_WIDTH), row),
                  pl.BlockSpec((tm, D_MODEL), row),
                  pl.BlockSpec((1, ATTN_WIDTH), fixed),
                  pl.BlockSpec((1, CONV_WIDTH), fixed),
                  pl.BlockSpec((D_MODEL, D_MODEL), fixed, pipeline_mode=pl.Buffered(1)),
                  pl.BlockSpec((1, D_MODEL), fixed),
                  pl.BlockSpec((D_MODEL, LANES), fixed),
                  pl.BlockSpec((1, LANES), fixed)],
        out_specs=[pl.BlockSpec((tm, D_MODEL), row),
                   pl.BlockSpec((tm, D_MODEL // 2), row),
                   pl.BlockSpec((tm, LANES), row)],
        out_shape=[jax.ShapeDtypeStruct((t, D_MODEL), F32),
                   jax.ShapeDtypeStruct((t, D_MODEL // 2), jnp.uint32),
                   jax.ShapeDtypeStruct((t, LANES), F32)],
        compiler_params=_params("arbitrary"),
        name="outproj",
    )(a, c, x2d, aw.reshape(1, -1), cw.reshape(1, -1), wo_bf16, fw.reshape(1, -1), w_router, b_router)


def _route_body(lg_ref, idx_ref, wt_ref, cnt_ref, run_ref):
    @pl.when(pl.program_id(0) == 0)
    def _():
        run_ref[...] = jnp.zeros_like(run_ref)

    logits = lg_ref[...]
    tm = logits.shape[0]
    lane = lax.broadcasted_iota(jnp.int32, (tm, LANES), 1)
    neg = -jnp.inf

    def first_argmax(v):
        m = jnp.max(v, axis=-1, keepdims=True)
        first = jnp.min(jnp.where(v == m, lane.astype(F32), float(LANES)), axis=-1, keepdims=True)
        return m, first.astype(jnp.int32)

    gl = jnp.where(lane < N_GROUPS, logits, neg)
    gmax, gidx = first_argmax(gl)
    g_w = 1.0 / jnp.sum(jnp.exp(gl - gmax), axis=-1, keepdims=True)
    first = N_GROUPS + gidx * EXPERTS_PER_GROUP
    el = jnp.where((lane >= first) & (lane < first + EXPERTS_PER_GROUP), logits, neg)
    m0, j0 = first_argmax(el)
    m1, j1 = first_argmax(jnp.where(lane == j0, neg, el))
    p1 = jnp.exp(m1 - m0)
    w0 = g_w / (1.0 + p1)
    w1 = g_w * p1 / (1.0 + p1)
    e0 = j0 - N_GROUPS
    e1 = j1 - N_GROUPS

    onehot = ((lane == e0) | (lane == e1)).astype(BF16)
    tri = (lax.broadcasted_iota(jnp.int32, (tm, tm), 0)
           > lax.broadcasted_iota(jnp.int32, (tm, tm), 1)).astype(BF16)
    before = jnp.dot(tri, onehot, preferred_element_type=F32) + run_ref[0:1, :]
    r0 = jnp.sum(jnp.where(lane == e0, before, 0.0), axis=-1, keepdims=True).astype(jnp.int32)
    r1 = jnp.sum(jnp.where(lane == e1, before, 0.0), axis=-1, keepdims=True).astype(jnp.int32)
    run = run_ref[0:1, :] + jnp.sum(onehot.astype(F32), axis=0, keepdims=True)
    run_ref[...] = jnp.broadcast_to(run, run_ref.shape)
    cnt_ref[...] = jnp.broadcast_to(run, cnt_ref.shape)

    zero = jnp.zeros_like(lane)
    idx_ref[...] = jnp.where(lane == 0, e0, jnp.where(lane == 1, e1,
                             jnp.where(lane == 2, r0, jnp.where(lane == 3, r1, zero))))
    wt_ref[...] = jnp.where(lane == 0, w0, jnp.where(lane == 1, w1, 0.0))


def _route(logits):
    t = logits.shape[0]
    tm = ROUTE_TM
    return pl.pallas_call(
        _route_body,
        grid=(t // tm,),
        in_specs=[pl.BlockSpec((tm, LANES), lambda i: (i, 0))],
        out_specs=[pl.BlockSpec((tm, LANES), lambda i: (i, 0)),
                   pl.BlockSpec((tm, LANES), lambda i: (i, 0)),
                   pl.BlockSpec((8, LANES), lambda i: (0, 0))],
        out_shape=[jax.ShapeDtypeStruct((t, LANES), jnp.int32),
                   jax.ShapeDtypeStruct((t, LANES), F32),
                   jax.ShapeDtypeStruct((8, LANES), F32)],
        scratch_shapes=[pltpu.VMEM((8, LANES), F32)],
        compiler_params=_params("arbitrary"),
        name="route",
    )(logits)


def _expert_body(sbe, sbs, sbn, sbr, tok, tail, hn_hbm, wg_ref, wu_ref, wd_ref, y_hbm,
                 x32, xb, acc, wgb, wub, wdb, gsem, osem):
    s = pl.program_id(0)
    c = pl.program_id(1)
    nsb = pl.num_programs(0)
    nch = pl.num_programs(1)
    rb = EXPERT_ROW_BLOCK
    half = D_MODEL // 2
    n = sbn[s]
    start = sbs[s]
    nblk = n // rb
    slot = s % 2

    def for_each(count, fn):
        def body(i, _):
            fn(i)
            return 0
        lax.fori_loop(0, count, body, 0)

    def block_rows(j):
        return pl.ds(pl.multiple_of(j * rb, rb), rb)

    def start_gather(sb, buf):
        first = sbs[sb]

        def group(g):
            for k in range(GATHER_UNROLL):
                i = g * GATHER_UNROLL + k
                src = hn_hbm.at[pl.ds(tok[first + i], 1)]
                pltpu.make_async_copy(src, x32.at[buf, pl.ds(i, 1)], gsem.at[buf]).start()
        for_each(sbr[sb] // GATHER_UNROLL, group)

    def wait_gather(sb, buf):
        rows = pl.ds(0, pl.multiple_of(sbr[sb], GATHER_UNROLL))
        pltpu.make_async_copy(hn_hbm.at[rows], x32.at[buf, rows], gsem.at[buf]).wait()

    def out_copy(first_row, j):
        dst = pl.ds(pl.multiple_of(first_row + j * rb, rb), rb)
        return pltpu.make_async_copy(acc.at[block_rows(j)], y_hbm.at[dst], osem)

    @pl.when((s == 0) & (c == 0))
    def _():
        def clear(j):
            for buf in range(2):
                x32[buf, block_rows(j), :] = jnp.zeros((rb, half), jnp.uint32)
            acc[block_rows(j), :] = jnp.zeros((rb, D_MODEL), F32)
        for_each(EXPERT_CAP // rb, clear)
        start_gather(0, 0)

    prev = jnp.maximum(s - 1, 0)
    prev_pending = (c == 0) & (s > 0) & (sbn[prev] > 0)

    def wait_prev_output():
        for_each(sbn[prev] // rb, lambda j: out_copy(sbs[prev], j).wait())

    @pl.when(prev_pending & (n == 0))
    def _():
        wait_prev_output()

    @pl.when((s == nsb - 1) & (c == 0))
    def _():
        first = tail[0]
        nfill = (y_hbm.shape[0] - first) // rb
        acc[0:rb, :] = jnp.zeros((rb, D_MODEL), F32)

        def fill_copy(j):
            dst = pl.ds(pl.multiple_of(first + j * rb, rb), rb)
            return pltpu.make_async_copy(acc.at[0:rb], y_hbm.at[dst], osem)

        for_each(nfill, lambda j: fill_copy(j).start())
        for_each(nfill, lambda j: fill_copy(j).wait())

    @pl.when(n > 0)
    def _():
        @pl.when(c == 0)
        def _():
            wait_gather(s, slot)

            def unpack(j):
                rows = block_rows(j)
                lo, hi = _unpack_bf16_pairs(x32[slot, rows, :])
                xb[rows, 0:half] = lo
                xb[rows, half:D_MODEL] = hi
            for_each(nblk, unpack)

            nxt = jnp.minimum(s + 1, nsb - 1)

            @pl.when((s + 1 < nsb) & (sbn[nxt] > 0))
            def _():
                start_gather(nxt, 1 - slot)

            @pl.when(prev_pending)
            def _():
                wait_prev_output()

        wgb[...] = wg_ref[0].astype(BF16)
        wub[...] = wu_ref[0].astype(BF16)
        wdb[...] = wd_ref[0].astype(BF16)

        def mlp(first_row, m):
            rows = pl.ds(pl.multiple_of(first_row, rb), m)
            x = xb[rows, :]
            g = jnp.dot(x, wgb[...], preferred_element_type=F32)
            u = jnp.dot(x, wub[...], preferred_element_type=F32)
            h = (jax.nn.silu(g) * u).astype(BF16)
            y = jnp.dot(h, wdb[...], preferred_element_type=F32)
            acc[rows, :] = jnp.where(c == 0, y, acc[rows, :] + y)

        wide = n // (2 * rb)
        for_each(wide, lambda j: mlp(j * 2 * rb, 2 * rb))

        @pl.when(n % (2 * rb) != 0)
        def _():
            mlp(wide * 2 * rb, rb)

        @pl.when(c == nch - 1)
        def _():
            for_each(nblk, lambda j: out_copy(start, j).start())


def _experts(hn_packed, w_gate, w_up, w_down, sb_expert, sb_start, sb_rows, sb_real, tok_buf, tail, n_rows):
    n_sb = sb_expert.shape[0]
    nch = EXPERT_FF // EXPERT_FF_CHUNK
    fc = EXPERT_FF_CHUNK

    def chunk(s, c, sbn):
        return jnp.where(sbn[s] > 0, c, nch - 1)

    def up_map(s, c, sbe, sbs, sbn, *_):
        return (sbe[s], 0, chunk(s, c, sbn))

    def down_map(s, c, sbe, sbs, sbn, *_):
        return (sbe[s], chunk(s, c, sbn), 0)

    grid_spec = pltpu.PrefetchScalarGridSpec(
        num_scalar_prefetch=6,
        grid=(n_sb, nch),
        in_specs=[pl.BlockSpec(memory_space=pl.ANY),
                  pl.BlockSpec((1, D_MODEL, fc), up_map),
                  pl.BlockSpec((1, D_MODEL, fc), up_map),
                  pl.BlockSpec((1, fc, D_MODEL), down_map)],
        out_specs=pl.BlockSpec(memory_space=pl.ANY),
        scratch_shapes=[pltpu.VMEM((2, EXPERT_CAP, D_MODEL // 2), jnp.uint32),
                        pltpu.VMEM((EXPERT_CAP, D_MODEL), BF16),
                        pltpu.VMEM((EXPERT_CAP, D_MODEL), F32),
                        pltpu.VMEM((D_MODEL, fc), BF16),
                        pltpu.VMEM((D_MODEL, fc), BF16),
                        pltpu.VMEM((fc, D_MODEL), BF16),
                        pltpu.SemaphoreType.DMA((2,)),
                        pltpu.SemaphoreType.DMA(())],
    )
    return pl.pallas_call(
        _expert_body,
        grid_spec=grid_spec,
        out_shape=jax.ShapeDtypeStruct((n_rows, D_MODEL), F32),
        compiler_params=_params("arbitrary", "arbitrary"),
        name="experts",
    )(sb_expert, sb_start, sb_rows, sb_real, tok_buf, tail, hn_packed, w_gate, w_up, w_down)


def _combine_body(dest, h1_ref, wt_ref, y_hbm, o_ref, g, sem):
    i = pl.program_id(0)
    nsteps = pl.num_programs(0)
    tm = h1_ref.shape[0]

    def issue(step, slot):
        def f(q, _):
            for u in range(GATHER_UNROLL):
                r = q * GATHER_UNROLL + u
                for k in range(TOP_K):
                    src = y_hbm.at[pl.ds(dest[(step * tm + r) * TOP_K + k], 1)]
                    pltpu.make_async_copy(src, g.at[slot, k, pl.ds(r, 1)], sem.at[slot]).start()
            return 0
        lax.fori_loop(0, tm // GATHER_UNROLL, f, 0)

    @pl.when(i == 0)
    def _():
        issue(0, 0)

    @pl.when(i + 1 < nsteps)
    def _():
        issue(i + 1, (i + 1) % 2)

    slot = i % 2
    for k in range(TOP_K):
        pltpu.make_async_copy(y_hbm.at[pl.ds(0, tm)], g.at[slot, k], sem.at[slot]).wait()

    w = wt_ref[...]
    o_ref[...] = h1_ref[...] + (w[:, 0:1] * g[slot, 0] + w[:, 1:2] * g[slot, 1])


def _combine(h1, wts, y_buf, dest_flat):
    t = h1.shape[0]
    tm = COMBINE_TM
    grid_spec = pltpu.PrefetchScalarGridSpec(
        num_scalar_prefetch=1,
        grid=(t // tm,),
        in_specs=[pl.BlockSpec((tm, D_MODEL), lambda i, d: (i, 0)),
                  pl.BlockSpec((tm, LANES), lambda i, d: (i, 0)),
                  pl.BlockSpec(memory_space=pl.ANY)],
        out_specs=pl.BlockSpec((tm, D_MODEL), lambda i, d: (i, 0)),
        scratch_shapes=[pltpu.VMEM((2, TOP_K, tm, D_MODEL), F32),
                        pltpu.SemaphoreType.DMA((2,))],
    )
    return pl.pallas_call(
        _combine_body,
        grid_spec=grid_spec,
        out_shape=jax.ShapeDtypeStruct((t, D_MODEL), F32),
        compiler_params=_params("arbitrary"),
        name="combine",
    )(dest_flat, h1, wts, y_buf)


def _dispatch_tables(idx, cnt, t):
    rb, cap = EXPERT_ROW_BLOCK, EXPERT_CAP
    n_assign = t * TOP_K
    n_rows = -(-(n_assign + N_EXPERTS * (rb - 1)) // rb) * rb
    n_sb = (n_rows + N_EXPERTS * (cap - rb)) // cap
    experts = idx[:, 0:TOP_K]
    ranks = idx[:, TOP_K:2 * TOP_K]
    counts = cnt[0, :N_EXPERTS].astype(jnp.int32)
    padded = (counts + rb - 1) // rb * rb
    seg_end = jnp.cumsum(padded)
    seg_start = seg_end - padded
    dest = (seg_start[experts] + ranks).reshape(-1)
    tok = jnp.repeat(jnp.arange(t, dtype=jnp.int32), TOP_K)
    tok_buf = jnp.zeros((n_rows,), jnp.int32).at[dest].set(tok)

    sb_per_expert = (padded + cap - 1) // cap
    sb_end = jnp.cumsum(sb_per_expert)
    total = sb_end[-1]
    s = jnp.arange(n_sb, dtype=jnp.int32)
    s_eff = jnp.minimum(s, total - 1)
    e = jnp.minimum(jnp.searchsorted(sb_end, s_eff, side='right'), N_EXPERTS - 1).astype(jnp.int32)
    local = s_eff - (sb_end[e] - sb_per_expert[e])
    sb_start = (seg_start[e] + local * cap).astype(jnp.int32)
    sb_rows = jnp.where(s < total, jnp.clip(padded[e] - local * cap, 0, cap), 0).astype(jnp.int32)
    gathered = (counts + GATHER_UNROLL - 1) // GATHER_UNROLL * GATHER_UNROLL
    sb_real = jnp.where(s < total, jnp.clip(gathered[e] - local * cap, 0, cap), 0).astype(jnp.int32)
    tail = seg_end[-1:].astype(jnp.int32)
    return dest.astype(jnp.int32), tok_buf, e, sb_start, sb_rows, sb_real, tail, n_rows


def kernel(x, meta_tokens, mix_norm_w, w_in, q_norm_w, k_norm_w, rel_bias, meta_bias, conv_w,
           attn_out_norm_w, conv_out_norm_w, w_out, ffn_norm_w, w_router_group, b_router_group,
           w_router_expert, b_router_expert, w_gate, w_up, w_down):
    bsz, seq, d = x.shape
    depth = mix_norm_w.shape[0]
    assert depth == 1 and d == D_MODEL and seq % GRID_W == 0
    t = bsz * seq
    x2d = x.reshape(t, d)
    l = 0

    w_in_b = w_in[l].astype(BF16)
    proj = _inproj(x2d, mix_norm_w[l], w_in_b, INPROJ_TM)
    proj_meta = _inproj(meta_tokens.astype(x.dtype), mix_norm_w[l], w_in_b, N_META)

    a = _attention(proj, proj_meta, q_norm_w[l], k_norm_w[l], rel_bias[l], meta_bias[l], bsz, seq)
    c = _short_conv(proj, proj_meta, conv_w[l], bsz, seq)

    w_router = jnp.zeros((d, LANES), F32)
    w_router = w_router.at[:, :N_GROUPS].set(w_router_group[l].astype(F32))
    w_router = w_router.at[:, N_GROUPS:N_GROUPS + N_EXPERTS].set(w_router_expert[l].astype(F32))
    b_router = jnp.zeros((1, LANES), F32)
    b_router = b_router.at[0, :N_GROUPS].set(b_router_group[l].astype(F32))
    b_router = b_router.at[0, N_GROUPS:N_GROUPS + N_EXPERTS].set(b_router_expert[l].astype(F32))
    h1, hn, logits = _outproj(a, c, x2d, attn_out_norm_w[l], conv_out_norm_w[l], w_out[l].astype(BF16),
                              ffn_norm_w[l], w_router.astype(BF16), b_router)

    idx, wts, cnt = _route(logits)
    dest, tok_buf, sb_expert, sb_start, sb_rows, sb_real, tail, n_rows = _dispatch_tables(idx, cnt, t)
    y_buf = _experts(hn, w_gate.reshape(N_EXPERTS, d, EXPERT_FF), w_up.reshape(N_EXPERTS, d, EXPERT_FF),
                     w_down.reshape(N_EXPERTS, EXPERT_FF, d), sb_expert, sb_start, sb_rows, sb_real,
                     tok_buf, tail, n_rows)
    out = _combine(h1, wts, y_buf, dest)
    return out.reshape(bsz, seq, d)
```

```python
import functools

import jax
import jax.numpy as jnp
from jax import lax
from jax.experimental import pallas as pl
from jax.experimental.pallas import tpu as pltpu

F32 = jnp.float32
BF16 = jnp.bfloat16

D_MODEL = 2048
N_META = 16
GRID_W = 64
N_HEADS = 16
HEAD_DIM = 64
ATTN_WIDTH = N_HEADS * HEAD_DIM
CONV_WIDTH = D_MODEL - ATTN_WIDTH
PROJ_TOTAL = 3 * ATTN_WIDTH + 3 * CONV_WIDTH
WIN_ROWS = 8
WIN_COLS = 16
N_GROUPS = 4
EXPERTS_PER_GROUP = 8
N_EXPERTS = N_GROUPS * EXPERTS_PER_GROUP
TOP_K = 2
EXPERT_FF = 1024
EPS = 1e-6

LANES = 128
VMEM_LIMIT = 52 * 1024 * 1024

ATTN_ROWS_PER_STEP = 4
INPROJ_TM = 1024
INPROJ_TN = 512
OUTPROJ_TM = 512
OUTPROJ_SUB = 256
ROUTE_TM = 512
EXPERT_ROW_BLOCK = 128
EXPERT_CAP = 1024
EXPERT_FF_CHUNK = 256
GATHER_UNROLL = 8
COMBINE_TM = 256


def _params(*sem):
    return pltpu.CompilerParams(dimension_semantics=sem, vmem_limit_bytes=VMEM_LIMIT)


def _inproj_body(x_ref, nw_ref, w_ref, o_ref, xn_ref):
    @pl.when(pl.program_id(1) == 0)
    def _():
        x = x_ref[...]
        ms = jnp.mean(x * x, axis=-1, keepdims=True)
        xn_ref[...] = (x * lax.rsqrt(ms + EPS) * nw_ref[...]).astype(BF16)

    o_ref[...] = jnp.dot(xn_ref[...], w_ref[...], preferred_element_type=F32)


def _inproj(x2d, norm_w, w_bf16, tm):
    m = x2d.shape[0]
    tn = INPROJ_TN
    return pl.pallas_call(
        _inproj_body,
        grid=(m // tm, PROJ_TOTAL // tn),
        in_specs=[pl.BlockSpec((tm, D_MODEL), lambda i, j: (i, 0)),
                  pl.BlockSpec((1, D_MODEL), lambda i, j: (0, 0)),
                  pl.BlockSpec((D_MODEL, tn), lambda i, j: (0, j))],
        out_specs=pl.BlockSpec((tm, tn), lambda i, j: (i, j)),
        out_shape=jax.ShapeDtypeStruct((m, PROJ_TOTAL), F32),
        scratch_shapes=[pltpu.VMEM((tm, D_MODEL), BF16)],
        compiler_params=_params("arbitrary", "arbitrary"),
        name="inproj",
    )(x2d, norm_w.reshape(1, D_MODEL), w_bf16)


def _head_norm(x, w, lo):
    x2 = x * x
    s_lo = jnp.sum(jnp.where(lo, x2, 0.0), axis=-1, keepdims=True)
    s_hi = jnp.sum(jnp.where(lo, 0.0, x2), axis=-1, keepdims=True)
    ms = jnp.where(lo, s_lo, s_hi) * (1.0 / HEAD_DIM)
    return x * lax.rsqrt(ms + EPS) * w


def _attn_body(q_ref, k_ref, v_ref, km_ref, vm_ref, qw_ref, kw_ref, bias_ref, mb_ref, o_ref,
               qs, ks, vs, sc, smc, pr, pmr, *, rows):
    lo = lax.broadcasted_iota(jnp.int32, (1, LANES), 1) < HEAD_DIM
    scale = HEAD_DIM ** -0.5
    chunk = 256
    seq = rows * GRID_W

    def prep(i, _):
        sl = pl.ds(pl.multiple_of(i * chunk, chunk), chunk)
        qs[sl, :] = (_head_norm(q_ref[sl, :], qw_ref[...], lo) * scale).astype(BF16)
        ks[sl, :] = _head_norm(k_ref[sl, :], kw_ref[...], lo).astype(BF16)
        vs[sl, :] = v_ref[sl, :].astype(BF16)
        return 0

    lax.fori_loop(0, seq // chunk, prep, 0, unroll=2)
    kmb = _head_norm(km_ref[...], kw_ref[...], lo).astype(BF16)
    vmb = vm_ref[...].astype(BF16)
    wr = min(WIN_ROWS, rows)
    nk = wr * GRID_W
    contract_last = (((1,), (1,)), ((), ()))

    def window_start(r):
        return jnp.clip(r - wr // 2, 0, rows - wr)

    def row_slice(r, n):
        return pl.ds(pl.multiple_of(r * GRID_W, GRID_W), n)

    def store_scores(r, s_ref, sm_ref):
        rs = window_start(r)
        si = rs - r + (WIN_ROWS - 1)
        q_r = qs[row_slice(r, GRID_W), :]
        kwin = ks[row_slice(rs, nk), :]
        for h in range(2):
            mask = lo if h == 0 else jnp.logical_not(lo)
            qh = jnp.where(mask, q_r, jnp.zeros_like(q_r))
            bias = jnp.concatenate([bias_ref[h, si + w] for w in range(0, wr, 2)], axis=-1)
            s_ref[h] = lax.dot_general(qh, kwin, contract_last, preferred_element_type=F32) + bias
            sm_ref[h] = lax.dot_general(qh, kmb, contract_last, preferred_element_type=F32) + mb_ref[h]

    def load_scores(s_ref, sm_ref):
        return [(s_ref[h], sm_ref[h]) for h in range(2)]

    def store_softmax(row_scores, p_ref, pm_ref):
        for h, (s, sm) in enumerate(row_scores):
            m = jnp.maximum(jnp.max(s, axis=-1, keepdims=True), jnp.max(sm, axis=-1, keepdims=True))
            p = jnp.exp(s - m)
            pm = jnp.exp(sm - m)
            inv = 1.0 / (jnp.sum(p, axis=-1, keepdims=True) + jnp.sum(pm, axis=-1, keepdims=True))
            p_ref[h] = (p * inv).astype(BF16)
            pm_ref[h] = (pm * inv).astype(BF16)

    def weighted_values(r, p_ref, pm_ref):
        vwin = vs[row_slice(window_start(r), nk), :]
        outs = [jnp.dot(p_ref[h], vwin, preferred_element_type=F32)
                + jnp.dot(pm_ref[h], vmb, preferred_element_type=F32) for h in range(2)]
        o_ref[row_slice(r, GRID_W), :] = jnp.where(lo, outs[0], outs[1])

    per = ATTN_ROWS_PER_STEP
    groups = rows // per

    def step(j, parity, do_scores, do_softmax, do_values):
        if do_scores:
            for k in range(per):
                store_scores(per * j + k, sc.at[parity, k], smc.at[parity, k])
        if do_values:
            for k in range(per):
                weighted_values(per * (j - 2) + k, pr.at[1 - parity, k], pmr.at[1 - parity, k])
        if do_softmax:
            for k in range(per):
                store_softmax(load_scores(sc.at[1 - parity, k], smc.at[1 - parity, k]),
                              pr.at[parity, k], pmr.at[parity, k])

    step(0, 0, True, False, False)
    step(1, 1, True, True, False)

    def two_steps(i, _):
        step(2 * i, 0, True, True, True)
        step(2 * i + 1, 1, True, True, True)
        return 0

    lax.fori_loop(1, groups // 2, two_steps, 0)
    step(groups, 0, False, True, True)
    step(groups + 1, 1, False, False, True)


def _bias_table(rel_bias):
    c = jnp.arange(GRID_W)
    col_start = jnp.clip(c - WIN_COLS // 2, 0, GRID_W - WIN_COLS)
    col_mask = (c[None, :] >= col_start[:, None]) & (c[None, :] < col_start[:, None] + WIN_COLS)
    dc = jnp.clip(c[None, :] - c[:, None], -(WIN_COLS - 1), WIN_COLS - 1) + (WIN_COLS - 1)
    onehot = (dc[None] == jnp.arange(2 * WIN_COLS - 1)[:, None, None]).astype(F32)
    toep = jnp.einsum('hdc,cqk->hdqk', rel_bias.astype(F32), onehot, precision=lax.Precision.HIGHEST)
    toep = jnp.where(col_mask[None, None], toep, -jnp.inf)
    return jnp.concatenate([toep[:, :-1], toep[:, 1:]], axis=-1)


def _attention(proj, proj_meta, q_norm_w, k_norm_w, rel_bias, meta_bias, bsz, seq):
    rows = seq // GRID_W
    assert rows >= WIN_ROWS and rows % (2 * ATTN_ROWS_PER_STEP) == 0
    nk = WIN_ROWS * GRID_W
    npairs = N_HEADS // 2
    qw = jnp.tile(q_norm_w.astype(F32), 2).reshape(1, LANES)
    kw = jnp.tile(k_norm_w.astype(F32), 2).reshape(1, LANES)
    bias = _bias_table(rel_bias)
    mb = meta_bias.astype(F32).reshape(N_HEADS, 1, N_META)
    return pl.pallas_call(
        functools.partial(_attn_body, rows=rows),
        grid=(bsz, npairs),
        in_specs=[pl.BlockSpec((seq, LANES), lambda b, p: (b, p)),
                  pl.BlockSpec((seq, LANES), lambda b, p: (b, npairs + p)),
                  pl.BlockSpec((seq, LANES), lambda b, p: (b, 2 * npairs + p)),
                  pl.BlockSpec((N_META, LANES), lambda b, p: (0, npairs + p)),
                  pl.BlockSpec((N_META, LANES), lambda b, p: (0, 2 * npairs + p)),
                  pl.BlockSpec((1, LANES), lambda b, p: (0, 0)),
                  pl.BlockSpec((1, LANES), lambda b, p: (0, 0)),
                  pl.BlockSpec((2, 2 * WIN_ROWS - 2, GRID_W, LANES), lambda b, p: (p, 0, 0, 0)),
                  pl.BlockSpec((2, 1, N_META), lambda b, p: (p, 0, 0))],
        out_specs=pl.BlockSpec((seq, LANES), lambda b, p: (b, p)),
        out_shape=jax.ShapeDtypeStruct((bsz * seq, ATTN_WIDTH), F32),
        scratch_shapes=([pltpu.VMEM((seq, LANES), BF16)] * 3
                        + [pltpu.VMEM((2, ATTN_ROWS_PER_STEP, 2, GRID_W, nk), F32),
                           pltpu.VMEM((2, ATTN_ROWS_PER_STEP, 2, GRID_W, N_META), F32),
                           pltpu.VMEM((2, ATTN_ROWS_PER_STEP, 2, GRID_W, nk), BF16),
                           pltpu.VMEM((2, ATTN_ROWS_PER_STEP, 2, GRID_W, N_META), BF16)]),
        compiler_params=_params("arbitrary", "arbitrary"),
        name="attention",
    )(proj, proj, proj, proj_meta, proj_meta, qw, kw, bias, mb)


def _conv_body(gb_ref, gc_ref, hc_ref, gcm_ref, hcm_ref, w_ref, o_ref):
    seq = gb_ref.shape[0]
    u = gc_ref[...] * hc_ref[...]
    u_meta_last = gcm_ref[N_META - 1:N_META, :] * hcm_ref[N_META - 1:N_META, :]
    row = lax.broadcasted_iota(jnp.int32, (seq, 1), 0)
    u_prev = jnp.where(row == 0, u_meta_last, pltpu.roll(u, 1, 0))
    u_next = jnp.where(row == seq - 1, 0.0, pltpu.roll(u, seq - 1, 0))
    w = w_ref[...]
    y = u_prev * w[0:1] + u * w[1:2] + u_next * w[2:3]
    o_ref[...] = gb_ref[...] * y


def _short_conv(proj, proj_meta, conv_w, bsz, seq):
    nct = CONV_WIDTH // LANES
    base = 3 * ATTN_WIDTH // LANES
    return pl.pallas_call(
        _conv_body,
        grid=(bsz, nct),
        in_specs=[pl.BlockSpec((seq, LANES), lambda b, c: (b, base + c)),
                  pl.BlockSpec((seq, LANES), lambda b, c: (b, base + nct + c)),
                  pl.BlockSpec((seq, LANES), lambda b, c: (b, base + 2 * nct + c)),
                  pl.BlockSpec((N_META, LANES), lambda b, c: (0, base + nct + c)),
                  pl.BlockSpec((N_META, LANES), lambda b, c: (0, base + 2 * nct + c)),
                  pl.BlockSpec((3, LANES), lambda b, c: (0, c))],
        out_specs=pl.BlockSpec((seq, LANES), lambda b, c: (b, c)),
        out_shape=jax.ShapeDtypeStruct((bsz * seq, CONV_WIDTH), F32),
        compiler_params=_params("arbitrary", "arbitrary"),
        name="short_conv",
    )(proj, proj, proj, proj_meta, proj_meta, conv_w.astype(F32))


def _rms(x, w):
    ms = jnp.mean(x * x, axis=-1, keepdims=True)
    return x * lax.rsqrt(ms + EPS) * w


def _pack_bf16_pairs(x):
    w = x.shape[1] // 2
    lo = lax.bitcast_convert_type(x[:, :w].astype(BF16).astype(F32), jnp.uint32)
    hi = lax.bitcast_convert_type(x[:, w:].astype(BF16).astype(F32), jnp.uint32)
    return (hi & jnp.uint32(0xFFFF0000)) | (lo >> 16)


def _unpack_bf16_pairs(p):
    lo = lax.bitcast_convert_type(p << 16, F32).astype(BF16)
    hi = lax.bitcast_convert_type(p & jnp.uint32(0xFFFF0000), F32).astype(BF16)
    return lo, hi


def _outproj_body(a_ref, c_ref, x_ref, aw_ref, cw_ref, wo_ref, fw_ref, wr_ref, br_ref,
                  h1_ref, hn_ref, lg_ref):
    sub = OUTPROJ_SUB
    blocks = [pl.ds(k * sub, sub) for k in range(a_ref.shape[0] // sub)]

    def mix(rows):
        an = _rms(a_ref[rows, :], aw_ref[...]).astype(BF16)
        cn = _rms(c_ref[rows, :], cw_ref[...]).astype(BF16)
        return (jnp.dot(an, wo_ref[0:ATTN_WIDTH, :], preferred_element_type=F32)
                + jnp.dot(cn, wo_ref[ATTN_WIDTH:D_MODEL, :], preferred_element_type=F32))

    def finish(rows, mixed):
        h1 = x_ref[rows, :] + mixed
        h1_ref[rows, :] = h1
        hn = _rms(h1, fw_ref[...])
        hn_ref[rows, :] = _pack_bf16_pairs(hn)
        lg_ref[rows, :] = jnp.dot(hn.astype(BF16), wr_ref[...], preferred_element_type=F32) + br_ref[...]

    mixed = mix(blocks[0])
    for k, rows in enumerate(blocks):
        following = mix(blocks[k + 1]) if k + 1 < len(blocks) else None
        finish(rows, mixed)
        mixed = following


def _outproj(a, c, x2d, aw, cw, wo_bf16, fw, w_router, b_router):
    t = x2d.shape[0]
    tm = OUTPROJ_TM
    row = lambda i: (i, 0)
    fixed = lambda i: (0, 0)
    return pl.pallas_call(
        _outproj_body,
        grid=(t // tm,),
        in_specs=[pl.BlockSpec((tm, ATTN_WIDTH), row),
                  pl.BlockSpec((tm, CONV_WIDTH), row),
                  pl.BlockSpec((tm, D_MODEL), row),
                  pl.BlockSpec((1, ATTN_WIDTH), fixed),
                  pl.BlockSpec((1, CONV_WIDTH), fixed),
                  pl.BlockSpec((D_MODEL, D_MODEL), fixed, pipeline_mode=pl.Buffered(1)),
                  pl.BlockSpec((1, D_MODEL), fixed),
                  pl.BlockSpec((D_MODEL, LANES), fixed),
                  pl.BlockSpec((1, LANES), fixed)],
        out_specs=[pl.BlockSpec((tm, D_MODEL), row),
                   pl.BlockSpec((tm, D_MODEL // 2), row),
                   pl.BlockSpec((tm, LANES), row)],
        out_shape=[jax.ShapeDtypeStruct((t, D_MODEL), F32),
                   jax.ShapeDtypeStruct((t, D_MODEL // 2), jnp.uint32),
                   jax.ShapeDtypeStruct((t, LANES), F32)],
        compiler_params=_params("arbitrary"),
        name="outproj",
    )(a, c, x2d, aw.reshape(1, -1), cw.reshape(1, -1), wo_bf16, fw.reshape(1, -1), w_router, b_router)


def _route_body(lg_ref, idx_ref, wt_ref, cnt_ref, run_ref):
    @pl.when(pl.program_id(0) == 0)
    def _():
        run_ref[...] = jnp.zeros_like(run_ref)

    logits = lg_ref[...]
    tm = logits.shape[0]
    lane = lax.broadcasted_iota(jnp.int32, (tm, LANES), 1)
    neg = -jnp.inf

    def first_argmax(v):
        m = jnp.max(v, axis=-1, keepdims=True)
        first = jnp.min(jnp.where(v == m, lane.astype(F32), float(LANES)), axis=-1, keepdims=True)
        return m, first.astype(jnp.int32)

    gl = jnp.where(lane < N_GROUPS, logits, neg)
    gmax, gidx = first_argmax(gl)
    g_w = 1.0 / jnp.sum(jnp.exp(gl - gmax), axis=-1, keepdims=True)
    first = N_GROUPS + gidx * EXPERTS_PER_GROUP
    el = jnp.where((lane >= first) & (lane < first + EXPERTS_PER_GROUP), logits, neg)
    m0, j0 = first_argmax(el)
    m1, j1 = first_argmax(jnp.where(lane == j0, neg, el))
    p1 = jnp.exp(m1 - m0)
    w0 = g_w / (1.0 + p1)
    w1 = g_w * p1 / (1.0 + p1)
    e0 = j0 - N_GROUPS
    e1 = j1 - N_GROUPS

    onehot = ((lane == e0) | (lane == e1)).astype(BF16)
    tri = (lax.broadcasted_iota(jnp.int32, (tm, tm), 0)
           > lax.broadcasted_iota(jnp.int32, (tm, tm), 1)).astype(BF16)
    before = jnp.dot(tri, onehot, preferred_element_type=F32) + run_ref[0:1, :]
    r0 = jnp.sum(jnp.where(lane == e0, before, 0.0), axis=-1, keepdims=True).astype(jnp.int32)
    r1 = jnp.sum(jnp.where(lane == e1, before, 0.0), axis=-1, keepdims=True).astype(jnp.int32)
    run = run_ref[0:1, :] + jnp.sum(onehot.astype(F32), axis=0, keepdims=True)
    run_ref[...] = jnp.broadcast_to(run, run_ref.shape)
    cnt_ref[...] = jnp.broadcast_to(run, cnt_ref.shape)

    zero = jnp.zeros_like(lane)
    idx_ref[...] = jnp.where(lane == 0, e0, jnp.where(lane == 1, e1,
                             jnp.where(lane == 2, r0, jnp.where(lane == 3, r1, zero))))
    wt_ref[...] = jnp.where(lane == 0, w0, jnp.where(lane == 1, w1, 0.0))


def _route(logits):
    t = logits.shape[0]
    tm = ROUTE_TM
    return pl.pallas_call(
        _route_body,
        grid=(t // tm,),
        in_specs=[pl.BlockSpec((tm, LANES), lambda i: (i, 0))],
        out_specs=[pl.BlockSpec((tm, LANES), lambda i: (i, 0)),
                   pl.BlockSpec((tm, LANES), lambda i: (i, 0)),
                   pl.BlockSpec((8, LANES), lambda i: (0, 0))],
        out_shape=[jax.ShapeDtypeStruct((t, LANES), jnp.int32),
                   jax.ShapeDtypeStruct((t, LANES), F32),
                   jax.ShapeDtypeStruct((8, LANES), F32)],
        scratch_shapes=[pltpu.VMEM((8, LANES), F32)],
        compiler_params=_params("arbitrary"),
        name="route",
    )(logits)


def _expert_body(sbe, sbs, sbn, sbr, tok, tail, hn_hbm, wg_ref, wu_ref, wd_ref, y_hbm,
                 x32, xb, acc, wgb, wub, wdb, gsem, osem):
    s = pl.program_id(0)
    c = pl.program_id(1)
    nsb = pl.num_programs(0)
    nch = pl.num_programs(1)
    rb = EXPERT_ROW_BLOCK
    half = D_MODEL // 2
    n = sbn[s]
    start = sbs[s]
    nblk = n // rb
    slot = s % 2

    def for_each(count, fn):
        def body(i, _):
            fn(i)
            return 0
        lax.fori_loop(0, count, body, 0)

    def block_rows(j):
        return pl.ds(pl.multiple_of(j * rb, rb), rb)

    def start_gather(sb, buf):
        first = sbs[sb]

        def group(g):
            for k in range(GATHER_UNROLL):
                i = g * GATHER_UNROLL + k
                src = hn_hbm.at[pl.ds(tok[first + i], 1)]
                pltpu.make_async_copy(src, x32.at[buf, pl.ds(i, 1)], gsem.at[buf]).start()
        for_each(sbr[sb] // GATHER_UNROLL, group)

    def wait_gather(sb, buf):
        rows = pl.ds(0, pl.multiple_of(sbr[sb], GATHER_UNROLL))
        pltpu.make_async_copy(hn_hbm.at[rows], x32.at[buf, rows], gsem.at[buf]).wait()

    def out_copy(first_row, j):
        dst = pl.ds(pl.multiple_of(first_row + j * rb, rb), rb)
        return pltpu.make_async_copy(acc.at[block_rows(j)], y_hbm.at[dst], osem)

    @pl.when((s == 0) & (c == 0))
    def _():
        def clear(j):
            for buf in range(2):
                x32[buf, block_rows(j), :] = jnp.zeros((rb, half), jnp.uint32)
            acc[block_rows(j), :] = jnp.zeros((rb, D_MODEL), F32)
        for_each(EXPERT_CAP // rb, clear)
        start_gather(0, 0)

    prev = jnp.maximum(s - 1, 0)
    prev_pending = (c == 0) & (s > 0) & (sbn[prev] > 0)

    def wait_prev_output():
        for_each(sbn[prev] // rb, lambda j: out_copy(sbs[prev], j).wait())

    @pl.when(prev_pending & (n == 0))
    def _():
        wait_prev_output()

    @pl.when((s == nsb - 1) & (c == 0))
    def _():
        first = tail[0]
        nfill = (y_hbm.shape[0] - first) // rb
        acc[0:rb, :] = jnp.zeros((rb, D_MODEL), F32)

        def fill_copy(j):
            dst = pl.ds(pl.multiple_of(first + j * rb, rb), rb)
            return pltpu.make_async_copy(acc.at[0:rb], y_hbm.at[dst], osem)

        for_each(nfill, lambda j: fill_copy(j).start())
        for_each(nfill, lambda j: fill_copy(j).wait())

    @pl.when(n > 0)
    def _():
        @pl.when(c == 0)
        def _():
            wait_gather(s, slot)

            def unpack(j):
                rows = block_rows(j)
                lo, hi = _unpack_bf16_pairs(x32[slot, rows, :])
                xb[rows, 0:half] = lo
                xb[rows, half:D_MODEL] = hi
            for_each(nblk, unpack)

            nxt = jnp.minimum(s + 1, nsb - 1)

            @pl.when((s + 1 < nsb) & (sbn[nxt] > 0))
            def _():
                start_gather(nxt, 1 - slot)

            @pl.when(prev_pending)
            def _():
                wait_prev_output()

        wgb[...] = wg_ref[0].astype(BF16)
        wub[...] = wu_ref[0].astype(BF16)
        wdb[...] = wd_ref[0].astype(BF16)

        def mlp(first_row, m):
            rows = pl.ds(pl.multiple_of(first_row, rb), m)
            x = xb[rows, :]
            g = jnp.dot(x, wgb[...], preferred_element_type=F32)
            u = jnp.dot(x, wub[...], preferred_element_type=F32)
            h = (jax.nn.silu(g) * u).astype(BF16)
            y = jnp.dot(h, wdb[...], preferred_element_type=F32)
            acc[rows, :] = jnp.where(c == 0, y, acc[rows, :] + y)

        wide = n // (2 * rb)
        for_each(wide, lambda j: mlp(j * 2 * rb, 2 * rb))

        @pl.when(n % (2 * rb) != 0)
        def _():
            mlp(wide * 2 * rb, rb)

        @pl.when(c == nch - 1)
        def _():
            for_each(nblk, lambda j: out_copy(start, j).start())


def _experts(hn_packed, w_gate, w_up, w_down, sb_expert, sb_start, sb_rows, sb_real, tok_buf, tail, n_rows):
    n_sb = sb_expert.shape[0]
    nch = EXPERT_FF // EXPERT_FF_CHUNK
    fc = EXPERT_FF_CHUNK

    def chunk(s, c, sbn):
        return jnp.where(sbn[s] > 0, c, nch - 1)

    def up_map(s, c, sbe, sbs, sbn, *_):
        return (sbe[s], 0, chunk(s, c, sbn))

    def down_map(s, c, sbe, sbs, sbn, *_):
        return (sbe[s], chunk(s, c, sbn), 0)

    grid_spec = pltpu.PrefetchScalarGridSpec(
        num_scalar_prefetch=6,
        grid=(n_sb, nch),
        in_specs=[pl.BlockSpec(memory_space=pl.ANY),
                  pl.BlockSpec((1, D_MODEL, fc), up_map),
                  pl.BlockSpec((1, D_MODEL, fc), up_map),
                  pl.BlockSpec((1, fc, D_MODEL), down_map)],
        out_specs=pl.BlockSpec(memory_space=pl.ANY),
        scratch_shapes=[pltpu.VMEM((2, EXPERT_CAP, D_MODEL // 2), jnp.uint32),
                        pltpu.VMEM((EXPERT_CAP, D_MODEL), BF16),
                        pltpu.VMEM((EXPERT_CAP, D_MODEL), F32),
                        pltpu.VMEM((D_MODEL, fc), BF16),
                        pltpu.VMEM((D_MODEL, fc), BF16),
                        pltpu.VMEM((fc, D_MODEL), BF16),
                        pltpu.SemaphoreType.DMA((2,)),
                        pltpu.SemaphoreType.DMA(())],
    )
    return pl.pallas_call(
        _expert_body,
        grid_spec=grid_spec,
        out_shape=jax.ShapeDtypeStruct((n_rows, D_MODEL), F32),
        compiler_params=_params("arbitrary", "arbitrary"),
        name="experts",
    )(sb_expert, sb_start, sb_rows, sb_real, tok_buf, tail, hn_packed, w_gate, w_up, w_down)


def _combine_body(dest, h1_ref, wt_ref, y_hbm, o_ref, g, sem):
    i = pl.program_id(0)
    nsteps = pl.num_programs(0)
    tm = h1_ref.shape[0]

    def issue(step, slot):
        def f(q, _):
            for u in range(GATHER_UNROLL):
                r = q * GATHER_UNROLL + u
                for k in range(TOP_K):
                    src = y_hbm.at[pl.ds(dest[(step * tm + r) * TOP_K + k], 1)]
                    pltpu.make_async_copy(src, g.at[slot, k, pl.ds(r, 1)], sem.at[slot]).start()
            return 0
        lax.fori_loop(0, tm // GATHER_UNROLL, f, 0)

    @pl.when(i == 0)
    def _():
        issue(0, 0)

    @pl.when(i + 1 < nsteps)
    def _():
        issue(i + 1, (i + 1) % 2)

    slot = i % 2
    for k in range(TOP_K):
        pltpu.make_async_copy(y_hbm.at[pl.ds(0, tm)], g.at[slot, k], sem.at[slot]).wait()

    w = wt_ref[...]
    o_ref[...] = h1_ref[...] + (w[:, 0:1] * g[slot, 0] + w[:, 1:2] * g[slot, 1])


def _combine(h1, wts, y_buf, dest_flat):
    t = h1.shape[0]
    tm = COMBINE_TM
    grid_spec = pltpu.PrefetchScalarGridSpec(
        num_scalar_prefetch=1,
        grid=(t // tm,),
        in_specs=[pl.BlockSpec((tm, D_MODEL), lambda i, d: (i, 0)),
                  pl.BlockSpec((tm, LANES), lambda i, d: (i, 0)),
                  pl.BlockSpec(memory_space=pl.ANY)],
        out_specs=pl.BlockSpec((tm, D_MODEL), lambda i, d: (i, 0)),
        scratch_shapes=[pltpu.VMEM((2, TOP_K, tm, D_MODEL), F32),
                        pltpu.SemaphoreType.DMA((2,))],
    )
    return pl.pallas_call(
        _combine_body,
        grid_spec=grid_spec,
        out_shape=jax.ShapeDtypeStruct((t, D_MODEL), F32),
        compiler_params=_params("arbitrary"),
        name="combine",
    )(dest_flat, h1, wts, y_buf)


def _dispatch_tables(idx, cnt, t):
    rb, cap = EXPERT_ROW_BLOCK, EXPERT_CAP
    n_assign = t * TOP_K
    n_rows = -(-(n_assign + N_EXPERTS * (rb - 1)) // rb) * rb
    n_sb = (n_rows + N_EXPERTS * (cap - rb)) // cap
    experts = idx[:, 0:TOP_K]
    ranks = idx[:, TOP_K:2 * TOP_K]
    counts = cnt[0, :N_EXPERTS].astype(jnp.int32)
    padded = (counts + rb - 1) // rb * rb
    seg_end = jnp.cumsum(padded)
    seg_start = seg_end - padded
    dest = (seg_start[experts] + ranks).reshape(-1)
    tok = jnp.repeat(jnp.arange(t, dtype=jnp.int32), TOP_K)
    tok_buf = jnp.zeros((n_rows,), jnp.int32).at[dest].set(tok)

    sb_per_expert = (padded + cap - 1) // cap
    sb_end = jnp.cumsum(sb_per_expert)
    total = sb_end[-1]
    s = jnp.arange(n_sb, dtype=jnp.int32)
    s_eff = jnp.minimum(s, total - 1)
    e = jnp.minimum(jnp.searchsorted(sb_end, s_eff, side='right'), N_EXPERTS - 1).astype(jnp.int32)
    local = s_eff - (sb_end[e] - sb_per_expert[e])
    sb_start = (seg_start[e] + local * cap).astype(jnp.int32)
    sb_rows = jnp.where(s < total, jnp.clip(padded[e] - local * cap, 0, cap), 0).astype(jnp.int32)
    gathered = (counts + GATHER_UNROLL - 1) // GATHER_UNROLL * GATHER_UNROLL
    sb_real = jnp.where(s < total, jnp.clip(gathered[e] - local * cap, 0, cap), 0).astype(jnp.int32)
    tail = seg_end[-1:].astype(jnp.int32)
    return dest.astype(jnp.int32), tok_buf, e, sb_start, sb_rows, sb_real, tail, n_rows


def kernel(x, meta_tokens, mix_norm_w, w_in, q_norm_w, k_norm_w, rel_bias, meta_bias, conv_w,
           attn_out_norm_w, conv_out_norm_w, w_out, ffn_norm_w, w_router_group, b_router_group,
           w_router_expert, b_router_expert, w_gate, w_up, w_down):
    bsz, seq, d = x.shape
    depth = mix_norm_w.shape[0]
    assert depth == 1 and d == D_MODEL and seq % GRID_W == 0
    t = bsz * seq
    x2d = x.reshape(t, d)
    l = 0

    w_in_b = w_in[l].astype(BF16)
    proj = _inproj(x2d, mix_norm_w[l], w_in_b, INPROJ_TM)
    proj_meta = _inproj(meta_tokens.astype(x.dtype), mix_norm_w[l], w_in_b, N_META)

    a = _attention(proj, proj_meta, q_norm_w[l], k_norm_w[l], rel_bias[l], meta_bias[l], bsz, seq)
    c = _short_conv(proj, proj_meta, conv_w[l], bsz, seq)

    w_router = jnp.zeros((d, LANES), F32)
    w_router = w_router.at[:, :N_GROUPS].set(w_router_group[l].astype(F32))
    w_router = w_router.at[:, N_GROUPS:N_GROUPS + N_EXPERTS].set(w_router_expert[l].astype(F32))
    b_router = jnp.zeros((1, LANES), F32)
    b_router = b_router.at[0, :N_GROUPS].set(b_router_group[l].astype(F32))
    b_router = b_router.at[0, N_GROUPS:N_GROUPS + N_EXPERTS].set(b_router_expert[l].astype(F32))
    h1, hn, logits = _outproj(a, c, x2d, attn_out_norm_w[l], conv_out_norm_w[l], w_out[l].astype(BF16),
                              ffn_norm_w[l], w_router.astype(BF16), b_router)

    idx, wts, cnt = _route(logits)
    dest, tok_buf, sb_expert, sb_start, sb_rows, sb_real, tail, n_rows = _dispatch_tables(idx, cnt, t)
    y_buf = _experts(hn, w_gate.reshape(N_EXPERTS, d, EXPERT_FF), w_up.reshape(N_EXPERTS, d, EXPERT_FF),
                     w_down.reshape(N_EXPERTS, EXPERT_FF, d), sb_expert, sb_start, sb_rows, sb_real,
                     tok_buf, tail, n_rows)
    out = _combine(h1, wts, y_buf, dest)
    return out.reshape(bsz, seq, d)
```

```python
import functools

import jax
import jax.numpy as jnp
from jax import lax
from jax.experimental import pallas as pl
from jax.experimental.pallas import tpu as pltpu

F32 = jnp.float32
BF16 = jnp.bfloat16

D_MODEL = 2048
N_META = 16
GRID_W = 64
N_HEADS = 16
HEAD_DIM = 64
ATTN_WIDTH = N_HEADS * HEAD_DIM
CONV_WIDTH = D_MODEL - ATTN_WIDTH
PROJ_TOTAL = 3 * ATTN_WIDTH + 3 * CONV_WIDTH
WIN_ROWS = 8
WIN_COLS = 16
N_GROUPS = 4
EXPERTS_PER_GROUP = 8
N_EXPERTS = N_GROUPS * EXPERTS_PER_GROUP
TOP_K = 2
EXPERT_FF = 1024
EPS = 1e-6

LANES = 128
VMEM_LIMIT = 52 * 1024 * 1024

ATTN_ROWS_PER_STEP = 4
INPROJ_TM = 1024
INPROJ_TN = 512
OUTPROJ_TM = 512
OUTPROJ_SUB = 256
ROUTE_TM = 512
EXPERT_ROW_BLOCK = 128
EXPERT_CAP = 1024
EXPERT_FF_CHUNK = 256
GATHER_UNROLL = 8
COMBINE_TM = 256


def _params(*sem):
    return pltpu.CompilerParams(dimension_semantics=sem, vmem_limit_bytes=VMEM_LIMIT)


def _inproj_body(x_ref, nw_ref, w_ref, o_ref, xn_ref):
    @pl.when(pl.program_id(1) == 0)
    def _():
        x = x_ref[...]
        ms = jnp.mean(x * x, axis=-1, keepdims=True)
        xn_ref[...] = (x * lax.rsqrt(ms + EPS) * nw_ref[...]).astype(BF16)

    o_ref[...] = jnp.dot(xn_ref[...], w_ref[...], preferred_element_type=F32)


def _inproj(x2d, norm_w, w_bf16, tm):
    m = x2d.shape[0]
    tn = INPROJ_TN
    return pl.pallas_call(
        _inproj_body,
        grid=(m // tm, PROJ_TOTAL // tn),
        in_specs=[pl.BlockSpec((tm, D_MODEL), lambda i, j: (i, 0)),
                  pl.BlockSpec((1, D_MODEL), lambda i, j: (0, 0)),
                  pl.BlockSpec((D_MODEL, tn), lambda i, j: (0, j))],
        out_specs=pl.BlockSpec((tm, tn), lambda i, j: (i, j)),
        out_shape=jax.ShapeDtypeStruct((m, PROJ_TOTAL), F32),
        scratch_shapes=[pltpu.VMEM((tm, D_MODEL), BF16)],
        compiler_params=_params("arbitrary", "arbitrary"),
        name="inproj",
    )(x2d, norm_w.reshape(1, D_MODEL), w_bf16)


def _head_norm(x, w, lo):
    x2 = x * x
    s_lo = jnp.sum(jnp.where(lo, x2, 0.0), axis=-1, keepdims=True)
    s_hi = jnp.sum(jnp.where(lo, 0.0, x2), axis=-1, keepdims=True)
    ms = jnp.where(lo, s_lo, s_hi) * (1.0 / HEAD_DIM)
    return x * lax.rsqrt(ms + EPS) * w


def _attn_body(q_ref, k_ref, v_ref, km_ref, vm_ref, qw_ref, kw_ref, bias_ref, mb_ref, o_ref,
               qs, ks, vs, sc, smc, pr, pmr, *, rows):
    lo = lax.broadcasted_iota(jnp.int32, (1, LANES), 1) < HEAD_DIM
    scale = HEAD_DIM ** -0.5
    chunk = 256
    seq = rows * GRID_W

    def prep(i, _):
        sl = pl.ds(pl.multiple_of(i * chunk, chunk), chunk)
        qs[sl, :] = (_head_norm(q_ref[sl, :], qw_ref[...], lo) * scale).astype(BF16)
        ks[sl, :] = _head_norm(k_ref[sl, :], kw_ref[...], lo).astype(BF16)
        vs[sl, :] = v_ref[sl, :].astype(BF16)
        return 0

    lax.fori_loop(0, seq // chunk, prep, 0, unroll=2)
    kmb = _head_norm(km_ref[...], kw_ref[...], lo).astype(BF16)
    vmb = vm_ref[...].astype(BF16)
    wr = min(WIN_ROWS, rows)
    nk = wr * GRID_W
    contract_last = (((1,), (1,)), ((), ()))

    def window_start(r):
        return jnp.clip(r - wr // 2, 0, rows - wr)

    def row_slice(r, n):
        return pl.ds(pl.multiple_of(r * GRID_W, GRID_W), n)

    def store_scores(r, s_ref, sm_ref):
        rs = window_start(r)
        si = rs - r + (WIN_ROWS - 1)
        q_r = qs[row_slice(r, GRID_W), :]
        kwin = ks[row_slice(rs, nk), :]
        for h in range(2):
            mask = lo if h == 0 else jnp.logical_not(lo)
            qh = jnp.where(mask, q_r, jnp.zeros_like(q_r))
            bias = jnp.concatenate([bias_ref[h, si + w] for w in range(0, wr, 2)], axis=-1)
            s_ref[h] = lax.dot_general(qh, kwin, contract_last, preferred_element_type=F32) + bias
            sm_ref[h] = lax.dot_general(qh, kmb, contract_last, preferred_element_type=F32) + mb_ref[h]

    def load_scores(s_ref, sm_ref):
        return [(s_ref[h], sm_ref[h]) for h in range(2)]

    def store_softmax(row_scores, p_ref, pm_ref):
        for h, (s, sm) in enumerate(row_scores):
            m = jnp.maximum(jnp.max(s, axis=-1, keepdims=True), jnp.max(sm, axis=-1, keepdims=True))
            p = jnp.exp(s - m)
            pm = jnp.exp(sm - m)
            inv = 1.0 / (jnp.sum(p, axis=-1, keepdims=True) + jnp.sum(pm, axis=-1, keepdims=True))
            p_ref[h] = (p * inv).astype(BF16)
            pm_ref[h] = (pm * inv).astype(BF16)

    def weighted_values(r, p_ref, pm_ref):
        vwin = vs[row_slice(window_start(r), nk), :]
        outs = [jnp.dot(p_ref[h], vwin, preferred_element_type=F32)
                + jnp.dot(pm_ref[h], vmb, preferred_element_type=F32) for h in range(2)]
        o_ref[row_slice(r, GRID_W), :] = jnp.where(lo, outs[0], outs[1])

    per = ATTN_ROWS_PER_STEP
    groups = rows // per

    def step(j, parity, do_scores, do_softmax, do_values):
        if do_scores:
            for k in range(per):
                store_scores(per * j + k, sc.at[parity, k], smc.at[parity, k])
        if do_values:
            for k in range(per):
                weighted_values(per * (j - 2) + k, pr.at[1 - parity, k], pmr.at[1 - parity, k])
        if do_softmax:
            for k in range(per):
                store_softmax(load_scores(sc.at[1 - parity, k], smc.at[1 - parity, k]),
                              pr.at[parity, k], pmr.at[parity, k])

    step(0, 0, True, False, False)
    step(1, 1, True, True, False)

    def two_steps(i, _):
        step(2 * i, 0, True, True, True)
        step(2 * i + 1, 1, True, True, True)
        return 0

    lax.fori_loop(1, groups // 2, two_steps, 0)
    step(groups, 0, False, True, True)
    step(groups + 1, 1, False, False, True)


def _bias_table(rel_bias):
    c = jnp.arange(GRID_W)
    col_start = jnp.clip(c - WIN_COLS // 2, 0, GRID_W - WIN_COLS)
    col_mask = (c[None, :] >= col_start[:, None]) & (c[None, :] < col_start[:, None] + WIN_COLS)
    dc = jnp.clip(c[None, :] - c[:, None], -(WIN_COLS - 1), WIN_COLS - 1) + (WIN_COLS - 1)
    onehot = (dc[None] == jnp.arange(2 * WIN_COLS - 1)[:, None, None]).astype(F32)
    toep = jnp.einsum('hdc,cqk->hdqk', rel_bias.astype(F32), onehot, precision=lax.Precision.HIGHEST)
    toep = jnp.where(col_mask[None, None], toep, -jnp.inf)
    return jnp.concatenate([toep[:, :-1], toep[:, 1:]], axis=-1)


def _attention(proj, proj_meta, q_norm_w, k_norm_w, rel_bias, meta_bias, bsz, seq):
    rows = seq // GRID_W
    assert rows >= WIN_ROWS and rows % (2 * ATTN_ROWS_PER_STEP) == 0
    nk = WIN_ROWS * GRID_W
    npairs = N_HEADS // 2
    qw = jnp.tile(q_norm_w.astype(F32), 2).reshape(1, LANES)
    kw = jnp.tile(k_norm_w.astype(F32), 2).reshape(1, LANES)
    bias = _bias_table(rel_bias)
    mb = meta_bias.astype(F32).reshape(N_HEADS, 1, N_META)
    return pl.pallas_call(
        functools.partial(_attn_body, rows=rows),
        grid=(bsz, npairs),
        in_specs=[pl.BlockSpec((seq, LANES), lambda b, p: (b, p)),
                  pl.BlockSpec((seq, LANES), lambda b, p: (b, npairs + p)),
                  pl.BlockSpec((seq, LANES), lambda b, p: (b, 2 * npairs + p)),
                  pl.BlockSpec((N_META, LANES), lambda b, p: (0, npairs + p)),
                  pl.BlockSpec((N_META, LANES), lambda b, p: (0, 2 * npairs + p)),
                  pl.BlockSpec((1, LANES), lambda b, p: (0, 0)),
                  pl.BlockSpec((1, LANES), lambda b, p: (0, 0)),
                  pl.BlockSpec((2, 2 * WIN_ROWS - 2, GRID_W, LANES), lambda b, p: (p, 0, 0, 0)),
                  pl.BlockSpec((2, 1, N_META), lambda b, p: (p, 0, 0))],
        out_specs=pl.BlockSpec((seq, LANES), lambda b, p: (b, p)),
        out_shape=jax.ShapeDtypeStruct((bsz * seq, ATTN_WIDTH), F32),
        scratch_shapes=([pltpu.VMEM((seq, LANES), BF16)] * 3
                        + [pltpu.VMEM((2, ATTN_ROWS_PER_STEP, 2, GRID_W, nk), F32),
                           pltpu.VMEM((2, ATTN_ROWS_PER_STEP, 2, GRID_W, N_META), F32),
                           pltpu.VMEM((2, ATTN_ROWS_PER_STEP, 2, GRID_W, nk), BF16),
                           pltpu.VMEM((2, ATTN_ROWS_PER_STEP, 2, GRID_W, N_META), BF16)]),
        compiler_params=_params("arbitrary", "arbitrary"),
        name="attention",
    )(proj, proj, proj, proj_meta, proj_meta, qw, kw, bias, mb)


def _conv_body(gb_ref, gc_ref, hc_ref, gcm_ref, hcm_ref, w_ref, o_ref):
    seq = gb_ref.shape[0]
    u = gc_ref[...] * hc_ref[...]
    u_meta_last = gcm_ref[N_META - 1:N_META, :] * hcm_ref[N_META - 1:N_META, :]
    row = lax.broadcasted_iota(jnp.int32, (seq, 1), 0)
    u_prev = jnp.where(row == 0, u_meta_last, pltpu.roll(u, 1, 0))
    u_next = jnp.where(row == seq - 1, 0.0, pltpu.roll(u, seq - 1, 0))
    w = w_ref[...]
    y = u_prev * w[0:1] + u * w[1:2] + u_next * w[2:3]
    o_ref[...] = gb_ref[...] * y


def _short_conv(proj, proj_meta, conv_w, bsz, seq):
    nct = CONV_WIDTH // LANES
    base = 3 * ATTN_WIDTH // LANES
    return pl.pallas_call(
        _conv_body,
        grid=(bsz, nct),
        in_specs=[pl.BlockSpec((seq, LANES), lambda b, c: (b, base + c)),
                  pl.BlockSpec((seq, LANES), lambda b, c: (b, base + nct + c)),
                  pl.BlockSpec((seq, LANES), lambda b, c: (b, base + 2 * nct + c)),
                  pl.BlockSpec((N_META, LANES), lambda b, c: (0, base + nct + c)),
                  pl.BlockSpec((N_META, LANES), lambda b, c: (0, base + 2 * nct + c)),
                  pl.BlockSpec((3, LANES), lambda b, c: (0, c))],
        out_specs=pl.BlockSpec((seq, LANES), lambda b, c: (b, c)),
        out_shape=jax.ShapeDtypeStruct((bsz * seq, CONV_WIDTH), F32),
        compiler_params=_params("arbitrary", "arbitrary"),
        name="short_conv",
    )(proj, proj, proj, proj_meta, proj_meta, conv_w.astype(F32))


def _rms(x, w):
    ms = jnp.mean(x * x, axis=-1, keepdims=True)
    return x * lax.rsqrt(ms + EPS) * w


def _pack_bf16_pairs(x):
    w = x.shape[1] // 2
    lo = lax.bitcast_convert_type(x[:, :w].astype(BF16).astype(F32), jnp.uint32)
    hi = lax.bitcast_convert_type(x[:, w:].astype(BF16).astype(F32), jnp.uint32)
    return (hi & jnp.uint32(0xFFFF0000)) | (lo >> 16)


def _unpack_bf16_pairs(p):
    lo = lax.bitcast_convert_type(p << 16, F32).astype(BF16)
    hi = lax.bitcast_convert_type(p & jnp.uint32(0xFFFF0000), F32).astype(BF16)
    return lo, hi


def _outproj_body(a_ref, c_ref, x_ref, aw_ref, cw_ref, wo_ref, fw_ref, wr_ref, br_ref,
                  h1_ref, hn_ref, lg_ref):
    sub = OUTPROJ_SUB
    blocks = [pl.ds(k * sub, sub) for k in range(a_ref.shape[0] // sub)]

    def mix(rows):
        an = _rms(a_ref[rows, :], aw_ref[...]).astype(BF16)
        cn = _rms(c_ref[rows, :], cw_ref[...]).astype(BF16)
        return (jnp.dot(an, wo_ref[0:ATTN_WIDTH, :], preferred_element_type=F32)
                + jnp.dot(cn, wo_ref[ATTN_WIDTH:D_MODEL, :], preferred_element_type=F32))

    def finish(rows, mixed):
        h1 = x_ref[rows, :] + mixed
        h1_ref[rows, :] = h1
        hn = _rms(h1, fw_ref[...])
        hn_ref[rows, :] = _pack_bf16_pairs(hn)
        lg_ref[rows, :] = jnp.dot(hn.astype(BF16), wr_ref[...], preferred_element_type=F32) + br_ref[...]

    mixed = mix(blocks[0])
    for k, rows in enumerate(blocks):
        following = mix(blocks[k + 1]) if k + 1 < len(blocks) else None
        finish(rows, mixed)
        mixed = following


def _outproj(a, c, x2d, aw, cw, wo_bf16, fw, w_router, b_router):
    t = x2d.shape[0]
    tm = OUTPROJ_TM
    row = lambda i: (i, 0)
    fixed = lambda i: (0, 0)
    return pl.pallas_call(
        _outproj_body,
        grid=(t // tm,),
        in_specs=[pl.BlockSpec((tm, ATTN_WIDTH), row),
                  pl.BlockSpec((tm, CONV_WIDTH), row),
                  pl.BlockSpec((tm, D_MODEL), row),
                  pl.BlockSpec((1, ATTN_WIDTH), fixed),
                  pl.BlockSpec((1, CONV_WIDTH), fixed),
                  pl.BlockSpec((D_MODEL, D_MODEL), fixed, pipeline_mode=pl.Buffered(1)),
                  pl.BlockSpec((1, D_MODEL), fixed),
                  pl.BlockSpec((D_MODEL, LANES), fixed),
                  pl.BlockSpec((1, LANES), fixed)],
        out_specs=[pl.BlockSpec((tm, D_MODEL), row),
                   pl.BlockSpec((tm, D_MODEL // 2), row),
                   pl.BlockSpec((tm, LANES), row)],
        out_shape=[jax.ShapeDtypeStruct((t, D_MODEL), F32),
                   jax.ShapeDtypeStruct((t, D_MODEL // 2), jnp.uint32),
                   jax.ShapeDtypeStruct((t, LANES), F32)],
        compiler_params=_params("arbitrary"),
        name="outproj",
    )(a, c, x2d, aw.reshape(1, -1), cw.reshape(1, -1), wo_bf16, fw.reshape(1, -1), w_router, b_router)


def _route_body(lg_ref, idx_ref, wt_ref, cnt_ref, run_ref):
    @pl.when(pl.program_id(0) == 0)
    def _():
        run_ref[...] = jnp.zeros_like(run_ref)

    logits = lg_ref[...]
    tm = logits.shape[0]
    lane = lax.broadcasted_iota(jnp.int32, (tm, LANES), 1)
    neg = -jnp.inf

    def first_argmax(v):
        m = jnp.max(v, axis=-1, keepdims=True)
        first = jnp.min(jnp.where(v == m, lane.astype(F32), float(LANES)), axis=-1, keepdims=True)
        return m, first.astype(jnp.int32)

    gl = jnp.where(lane < N_GROUPS, logits, neg)
    gmax, gidx = first_argmax(gl)
    g_w = 1.0 / jnp.sum(jnp.exp(gl - gmax), axis=-1, keepdims=True)
    first = N_GROUPS + gidx * EXPERTS_PER_GROUP
    el = jnp.where((lane >= first) & (lane < first + EXPERTS_PER_GROUP), logits, neg)
    m0, j0 = first_argmax(el)
    m1, j1 = first_argmax(jnp.where(lane == j0, neg, el))
    p1 = jnp.exp(m1 - m0)
    w0 = g_w / (1.0 + p1)
    w1 = g_w * p1 / (1.0 + p1)
    e0 = j0 - N_GROUPS
    e1 = j1 - N_GROUPS

    onehot = ((lane == e0) | (lane == e1)).astype(BF16)
    tri = (lax.broadcasted_iota(jnp.int32, (tm, tm), 0)
           > lax.broadcasted_iota(jnp.int32, (tm, tm), 1)).astype(BF16)
    before = jnp.dot(tri, onehot, preferred_element_type=F32) + run_ref[0:1, :]
    r0 = jnp.sum(jnp.where(lane == e0, before, 0.0), axis=-1, keepdims=True).astype(jnp.int32)
    r1 = jnp.sum(jnp.where(lane == e1, before, 0.0), axis=-1, keepdims=True).astype(jnp.int32)
    run = run_ref[0:1, :] + jnp.sum(onehot.astype(F32), axis=0, keepdims=True)
    run_ref[...] = jnp.broadcast_to(run, run_ref.shape)
    cnt_ref[...] = jnp.broadcast_to(run, cnt_ref.shape)

    zero = jnp.zeros_like(lane)
    idx_ref[...] = jnp.where(lane == 0, e0, jnp.where(lane == 1, e1,
                             jnp.where(lane == 2, r0, jnp.where(lane == 3, r1, zero))))
    wt_ref[...] = jnp.where(lane == 0, w0, jnp.where(lane == 1, w1, 0.0))


def _route(logits):
    t = logits.shape[0]
    tm = ROUTE_TM
    return pl.pallas_call(
        _route_body,
        grid=(t // tm,),
        in_specs=[pl.BlockSpec((tm, LANES), lambda i: (i, 0))],
        out_specs=[pl.BlockSpec((tm, LANES), lambda i: (i, 0)),
                   pl.BlockSpec((tm, LANES), lambda i: (i, 0)),
                   pl.BlockSpec((8, LANES), lambda i: (0, 0))],
        out_shape=[jax.ShapeDtypeStruct((t, LANES), jnp.int32),
                   jax.ShapeDtypeStruct((t, LANES), F32),
                   jax.ShapeDtypeStruct((8, LANES), F32)],
        scratch_shapes=[pltpu.VMEM((8, LANES), F32)],
        compiler_params=_params("arbitrary"),
        name="route",
    )(logits)


def _expert_body(sbe, sbs, sbn, sbr, tok, tail, hn_hbm, wg_ref, wu_ref, wd_ref, y_hbm,
                 x32, xb, acc, gsem, osem):
    s = pl.program_id(0)
    c = pl.program_id(1)
    nsb = pl.num_programs(0)
    nch = EXPERT_FF // EXPERT_FF_CHUNK
    rb = EXPERT_ROW_BLOCK
    half = D_MODEL // 2
    n = sbn[s]
    start = sbs[s]
    nblk = n // rb
    slot = s % 2

    def for_each(count, fn):
        def body(i, _):
            fn(i)
            return 0
        lax.fori_loop(0, count, body, 0)

    def block_rows(j):
        return pl.ds(pl.multiple_of(j * rb, rb), rb)

    def gather_row(sb_first, buf, i):
        src = hn_hbm.at[pl.ds(tok[sb_first + i], 1)]
        pltpu.make_async_copy(src, x32.at[buf, pl.ds(i, 1)], gsem.at[buf]).start()

    def gather_rows(sb_first, buf, first_row, count):
        def group(g):
            for k in range(GATHER_UNROLL):
                gather_row(sb_first, buf, first_row + g * GATHER_UNROLL + k)
        for_each(count // GATHER_UNROLL, group)

    def wait_gathered(buf, count):
        rows = pl.ds(0, pl.multiple_of(count, GATHER_UNROLL))
        pltpu.make_async_copy(hn_hbm.at[rows], x32.at[buf, rows], gsem.at[buf]).wait()

    def out_copy(first_row, j):
        dst = pl.ds(pl.multiple_of(first_row + j * rb, rb), rb)
        return pltpu.make_async_copy(acc.at[block_rows(j)], y_hbm.at[dst], osem)

    @pl.when((s == 0) & (c == 0))
    def _():
        def clear(j):
            for buf in range(2):
                x32[buf, block_rows(j), :] = jnp.zeros((rb, half), jnp.uint32)
            acc[block_rows(j), :] = jnp.zeros((rb, D_MODEL), F32)
        for_each(EXPERT_CAP // rb, clear)
        gather_rows(sbs[0], 0, 0, sbr[0])

    prev = jnp.maximum(s - 1, 0)
    nxt = jnp.minimum(s + 1, nsb - 1)
    prev_rows = jnp.where(s > 0, sbn[prev], 0)
    prev_pending = (c == 0) & (prev_rows > 0)

    @pl.when((c == 0) & (sbr[s] > 0))
    def _():
        wait_gathered(slot, sbr[s])

    def wait_prev_output():
        for_each(prev_rows // rb, lambda j: out_copy(sbs[prev], j).wait())

    @pl.when(prev_pending & (n == 0))
    def _():
        wait_prev_output()

    @pl.when((s == nsb - 1) & (c == 0))
    def _():
        first = tail[0]
        nfill = (y_hbm.shape[0] - first) // rb
        acc[0:rb, :] = jnp.zeros((rb, D_MODEL), F32)

        def fill_copy(j):
            dst = pl.ds(pl.multiple_of(first + j * rb, rb), rb)
            return pltpu.make_async_copy(acc.at[0:rb], y_hbm.at[dst], osem)

        for_each(nfill, lambda j: fill_copy(j).start())
        for_each(nfill, lambda j: fill_copy(j).wait())

    @pl.when(n > 0)
    def _():
        @pl.when(c == 0)
        def _():
            def unpack(j):
                rows = block_rows(j)
                lo, hi = _unpack_bf16_pairs(x32[slot, rows, :])
                xb[rows, 0:half] = lo
                xb[rows, half:D_MODEL] = hi
            for_each(nblk, unpack)

            @pl.when((s + 1 < nsb) & (sbr[nxt] > 0))
            def _():
                gather_rows(sbs[nxt], 1 - slot, 0, sbr[nxt])

            @pl.when(prev_pending)
            def _():
                wait_prev_output()

        def mlp(first_row, m):
            rows = pl.ds(pl.multiple_of(first_row, rb), m)
            x = xb[rows, :]
            g = jnp.dot(x, wg_ref[0].astype(BF16), preferred_element_type=F32)
            u = jnp.dot(x, wu_ref[0].astype(BF16), preferred_element_type=F32)
            h = (jax.nn.silu(g) * u).astype(BF16)
            y = jnp.dot(h, wd_ref[0].astype(BF16), preferred_element_type=F32)
            acc[rows, :] = jnp.where(c == 0, y, acc[rows, :] + y)

        wide = n // (2 * rb)
        for_each(wide, lambda j: mlp(j * 2 * rb, 2 * rb))

        @pl.when(n % (2 * rb) != 0)
        def _():
            mlp(wide * 2 * rb, rb)

        @pl.when(c == nch - 1)
        def _():
            for_each(nblk, lambda j: out_copy(start, j).start())


def _experts(hn_packed, w_gate, w_up, w_down, sb_expert, sb_start, sb_rows, sb_real, tok_buf, tail, n_rows):
    n_sb = sb_expert.shape[0]
    nch = EXPERT_FF // EXPERT_FF_CHUNK
    fc = EXPERT_FF_CHUNK

    def chunk(s, c, sbn):
        return jnp.where(sbn[s] > 0, c, nch - 1)

    def up_map(s, c, sbe, sbs, sbn, *_):
        return (sbe[s], 0, chunk(s, c, sbn))

    def down_map(s, c, sbe, sbs, sbn, *_):
        return (sbe[s], chunk(s, c, sbn), 0)

    grid_spec = pltpu.PrefetchScalarGridSpec(
        num_scalar_prefetch=6,
        grid=(n_sb, nch),
        in_specs=[pl.BlockSpec(memory_space=pl.ANY),
                  pl.BlockSpec((1, D_MODEL, fc), up_map),
                  pl.BlockSpec((1, D_MODEL, fc), up_map),
                  pl.BlockSpec((1, fc, D_MODEL), down_map)],
        out_specs=pl.BlockSpec(memory_space=pl.ANY),
        scratch_shapes=[pltpu.VMEM((2, EXPERT_CAP, D_MODEL // 2), jnp.uint32),
                        pltpu.VMEM((EXPERT_CAP, D_MODEL), BF16),
                        pltpu.VMEM((EXPERT_CAP, D_MODEL), F32),
                        pltpu.SemaphoreType.DMA((2,)),
                        pltpu.SemaphoreType.DMA(())],
    )
    return pl.pallas_call(
        _expert_body,
        grid_spec=grid_spec,
        out_shape=jax.ShapeDtypeStruct((n_rows, D_MODEL), F32),
        compiler_params=_params("arbitrary", "arbitrary"),
        name="experts",
    )(sb_expert, sb_start, sb_rows, sb_real, tok_buf, tail, hn_packed, w_gate, w_up, w_down)


def _combine_body(dest, h1_ref, wt_ref, y_hbm, o_ref, g, sem):
    i = pl.program_id(0)
    nsteps = pl.num_programs(0)
    tm = h1_ref.shape[0]

    def issue(step, slot):
        def f(q, _):
            for u in range(GATHER_UNROLL):
                r = q * GATHER_UNROLL + u
                for k in range(TOP_K):
                    src = y_hbm.at[pl.ds(dest[(step * tm + r) * TOP_K + k], 1)]
                    pltpu.make_async_copy(src, g.at[slot, k, pl.ds(r, 1)], sem.at[slot]).start()
            return 0
        lax.fori_loop(0, tm // GATHER_UNROLL, f, 0)

    @pl.when(i == 0)
    def _():
        issue(0, 0)

    @pl.when(i + 1 < nsteps)
    def _():
        issue(i + 1, (i + 1) % 2)

    slot = i % 2
    for k in range(TOP_K):
        pltpu.make_async_copy(y_hbm.at[pl.ds(0, tm)], g.at[slot, k], sem.at[slot]).wait()

    w = wt_ref[...]
    o_ref[...] = h1_ref[...] + (w[:, 0:1] * g[slot, 0] + w[:, 1:2] * g[slot, 1])


def _combine(h1, wts, y_buf, dest_flat):
    t = h1.shape[0]
    tm = COMBINE_TM
    grid_spec = pltpu.PrefetchScalarGridSpec(
        num_scalar_prefetch=1,
        grid=(t // tm,),
        in_specs=[pl.BlockSpec((tm, D_MODEL), lambda i, d: (i, 0)),
                  pl.BlockSpec((tm, LANES), lambda i, d: (i, 0)),
                  pl.BlockSpec(memory_space=pl.ANY)],
        out_specs=pl.BlockSpec((tm, D_MODEL), lambda i, d: (i, 0)),
        scratch_shapes=[pltpu.VMEM((2, TOP_K, tm, D_MODEL), F32),
                        pltpu.SemaphoreType.DMA((2,))],
    )
    return pl.pallas_call(
        _combine_body,
        grid_spec=grid_spec,
        out_shape=jax.ShapeDtypeStruct((t, D_MODEL), F32),
        compiler_params=_params("arbitrary"),
        name="combine",
    )(dest_flat, h1, wts, y_buf)


def _dispatch_tables(idx, cnt, t):
    rb, cap = EXPERT_ROW_BLOCK, EXPERT_CAP
    n_assign = t * TOP_K
    n_rows = -(-(n_assign + N_EXPERTS * (rb - 1)) // rb) * rb
    n_sb = (n_rows + N_EXPERTS * (cap - rb)) // cap
    experts = idx[:, 0:TOP_K]
    ranks = idx[:, TOP_K:2 * TOP_K]
    counts = cnt[0, :N_EXPERTS].astype(jnp.int32)
    padded = (counts + rb - 1) // rb * rb
    seg_end = jnp.cumsum(padded)
    seg_start = seg_end - padded
    dest = (seg_start[experts] + ranks).reshape(-1)
    tok = jnp.repeat(jnp.arange(t, dtype=jnp.int32), TOP_K)
    tok_buf = jnp.zeros((n_rows,), jnp.int32).at[dest].set(tok)

    sb_per_expert = (padded + cap - 1) // cap
    sb_end = jnp.cumsum(sb_per_expert)
    total = sb_end[-1]
    s = jnp.arange(n_sb, dtype=jnp.int32)
    s_eff = jnp.minimum(s, total - 1)
    e = jnp.minimum(jnp.searchsorted(sb_end, s_eff, side='right'), N_EXPERTS - 1).astype(jnp.int32)
    local = s_eff - (sb_end[e] - sb_per_expert[e])
    sb_start = (seg_start[e] + local * cap).astype(jnp.int32)
    sb_rows = jnp.where(s < total, jnp.clip(padded[e] - local * cap, 0, cap), 0).astype(jnp.int32)
    gathered = (counts + GATHER_UNROLL - 1) // GATHER_UNROLL * GATHER_UNROLL
    sb_real = jnp.where(s < total, jnp.clip(gathered[e] - local * cap, 0, cap), 0).astype(jnp.int32)
    tail = seg_end[-1:].astype(jnp.int32)
    return dest.astype(jnp.int32), tok_buf, e, sb_start, sb_rows, sb_real, tail, n_rows


def kernel(x, meta_tokens, mix_norm_w, w_in, q_norm_w, k_norm_w, rel_bias, meta_bias, conv_w,
           attn_out_norm_w, conv_out_norm_w, w_out, ffn_norm_w, w_router_group, b_router_group,
           w_router_expert, b_router_expert, w_gate, w_up, w_down):
    bsz, seq, d = x.shape
    depth = mix_norm_w.shape[0]
    assert depth == 1 and d == D_MODEL and seq % GRID_W == 0
    t = bsz * seq
    x2d = x.reshape(t, d)
    l = 0

    w_in_b = w_in[l].astype(BF16)
    proj = _inproj(x2d, mix_norm_w[l], w_in_b, INPROJ_TM)
    proj_meta = _inproj(meta_tokens.astype(x.dtype), mix_norm_w[l], w_in_b, N_META)

    a = _attention(proj, proj_meta, q_norm_w[l], k_norm_w[l], rel_bias[l], meta_bias[l], bsz, seq)
    c = _short_conv(proj, proj_meta, conv_w[l], bsz, seq)

    w_router = jnp.zeros((d, LANES), F32)
    w_router = w_router.at[:, :N_GROUPS].set(w_router_group[l].astype(F32))
    w_router = w_router.at[:, N_GROUPS:N_GROUPS + N_EXPERTS].set(w_router_expert[l].astype(F32))
    b_router = jnp.zeros((1, LANES), F32)
    b_router = b_router.at[0, :N_GROUPS].set(b_router_group[l].astype(F32))
    b_router = b_router.at[0, N_GROUPS:N_GROUPS + N_EXPERTS].set(b_router_expert[l].astype(F32))
    h1, hn, logits = _outproj(a, c, x2d, attn_out_norm_w[l], conv_out_norm_w[l], w_out[l].astype(BF16),
                              ffn_norm_w[l], w_router.astype(BF16), b_router)

    idx, wts, cnt = _route(logits)
    dest, tok_buf, sb_expert, sb_start, sb_rows, sb_real, tail, n_rows = _dispatch_tables(idx, cnt, t)
    y_buf = _experts(hn, w_gate.reshape(N_EXPERTS, d, EXPERT_FF), w_up.reshape(N_EXPERTS, d, EXPERT_FF),
                     w_down.reshape(N_EXPERTS, EXPERT_FF, d), sb_expert, sb_start, sb_rows, sb_real,
                     tok_buf, tail, n_rows)
    out = _combine(h1, wts, y_buf, dest)
    return out.reshape(bsz, seq, d)
```

```python
import functools

import jax
import jax.numpy as jnp
from jax import lax
from jax.experimental import pallas as pl
from jax.experimental.pallas import tpu as pltpu

F32 = jnp.float32
BF16 = jnp.bfloat16

D_MODEL = 2048
N_META = 16
GRID_W = 64
N_HEADS = 16
HEAD_DIM = 64
ATTN_WIDTH = N_HEADS * HEAD_DIM
CONV_WIDTH = D_MODEL - ATTN_WIDTH
PROJ_TOTAL = 3 * ATTN_WIDTH + 3 * CONV_WIDTH
WIN_ROWS = 8
WIN_COLS = 16
N_GROUPS = 4
EXPERTS_PER_GROUP = 8
N_EXPERTS = N_GROUPS * EXPERTS_PER_GROUP
TOP_K = 2
EXPERT_FF = 1024
EPS = 1e-6

LANES = 128
VMEM_LIMIT = 52 * 1024 * 1024

ATTN_ROWS_PER_STEP = 4
INPROJ_TM = 1024
INPROJ_TN = 1024
OUTPROJ_TM = 512
OUTPROJ_SUB = 256
ROUTE_TM = 512
EXPERT_ROW_BLOCK = 128
EXPERT_CAP = 1024
EXPERT_FF_CHUNK = 256
GATHER_UNROLL = 8
RANK_BITS = 15
RANK_SPAN = 1 << RANK_BITS
COMBINE_TM = 256


def _params(*sem):
    return pltpu.CompilerParams(dimension_semantics=sem, vmem_limit_bytes=VMEM_LIMIT)


def _inproj_body(x_ref, nw_ref, w_ref, o_ref, xn_ref):
    @pl.when(pl.program_id(1) == 0)
    def _():
        x = x_ref[...]
        ms = jnp.mean(x * x, axis=-1, keepdims=True)
        xn_ref[...] = (x * lax.rsqrt(ms + EPS) * nw_ref[...]).astype(BF16)

    o_ref[...] = jnp.dot(xn_ref[...], w_ref[...], preferred_element_type=F32)


def _inproj(x2d, norm_w, w_bf16, tm):
    m = x2d.shape[0]
    tn = INPROJ_TN
    return pl.pallas_call(
        _inproj_body,
        grid=(m // tm, PROJ_TOTAL // tn),
        in_specs=[pl.BlockSpec((tm, D_MODEL), lambda i, j: (i, 0)),
                  pl.BlockSpec((1, D_MODEL), lambda i, j: (0, 0)),
                  pl.BlockSpec((D_MODEL, tn), lambda i, j: (0, j))],
        out_specs=pl.BlockSpec((tm, tn), lambda i, j: (i, j)),
        out_shape=jax.ShapeDtypeStruct((m, PROJ_TOTAL), F32),
        scratch_shapes=[pltpu.VMEM((tm, D_MODEL), BF16)],
        compiler_params=_params("arbitrary", "arbitrary"),
        name="inproj",
    )(x2d, norm_w.reshape(1, D_MODEL), w_bf16)


def _head_norm(x, w, lo):
    x2 = x * x
    s_lo = jnp.sum(jnp.where(lo, x2, 0.0), axis=-1, keepdims=True)
    s_hi = jnp.sum(jnp.where(lo, 0.0, x2), axis=-1, keepdims=True)
    ms = jnp.where(lo, s_lo, s_hi) * (1.0 / HEAD_DIM)
    return x * lax.rsqrt(ms + EPS) * w


def _attn_body(q_ref, k_ref, v_ref, km_ref, vm_ref, qw_ref, kw_ref, bias_ref, mb_ref, o_ref,
               qs, ks, vs, sc, smc, pr, pmr, *, rows):
    lo = lax.broadcasted_iota(jnp.int32, (1, LANES), 1) < HEAD_DIM
    scale = HEAD_DIM ** -0.5
    chunk = 256
    seq = rows * GRID_W

    def prep(i, _):
        sl = pl.ds(pl.multiple_of(i * chunk, chunk), chunk)
        qs[sl, :] = (_head_norm(q_ref[sl, :], qw_ref[...], lo) * scale).astype(BF16)
        ks[sl, :] = _head_norm(k_ref[sl, :], kw_ref[...], lo).astype(BF16)
        vs[sl, :] = v_ref[sl, :].astype(BF16)
        return 0

    lax.fori_loop(0, seq // chunk, prep, 0, unroll=2)
    kmb = _head_norm(km_ref[...], kw_ref[...], lo).astype(BF16)
    vmb = vm_ref[...].astype(BF16)
    wr = min(WIN_ROWS, rows)
    nk = wr * GRID_W
    contract_last = (((1,), (1,)), ((), ()))

    def window_start(r):
        return jnp.clip(r - wr // 2, 0, rows - wr)

    def row_slice(r, n):
        return pl.ds(pl.multiple_of(r * GRID_W, GRID_W), n)

    def store_scores(r, s_ref, sm_ref):
        rs = window_start(r)
        si = rs - r + (WIN_ROWS - 1)
        q_r = qs[row_slice(r, GRID_W), :]
        kwin = ks[row_slice(rs, nk), :]
        for h in range(2):
            mask = lo if h == 0 else jnp.logical_not(lo)
            qh = jnp.where(mask, q_r, jnp.zeros_like(q_r))
            bias = jnp.concatenate([bias_ref[h, si + w] for w in range(0, wr, 2)], axis=-1)
            s_ref[h] = lax.dot_general(qh, kwin, contract_last, preferred_element_type=F32) + bias
            sm_ref[h] = lax.dot_general(qh, kmb, contract_last, preferred_element_type=F32) + mb_ref[h]

    def load_scores(s_ref, sm_ref):
        return [(s_ref[h], sm_ref[h]) for h in range(2)]

    def store_softmax(row_scores, p_ref, pm_ref):
        for h, (s, sm) in enumerate(row_scores):
            m = jnp.maximum(jnp.max(s, axis=-1, keepdims=True), jnp.max(sm, axis=-1, keepdims=True))
            p = jnp.exp(s - m)
            pm = jnp.exp(sm - m)
            inv = 1.0 / (jnp.sum(p, axis=-1, keepdims=True) + jnp.sum(pm, axis=-1, keepdims=True))
            p_ref[h] = (p * inv).astype(BF16)
            pm_ref[h] = (pm * inv).astype(BF16)

    def weighted_values(r, p_ref, pm_ref):
        vwin = vs[row_slice(window_start(r), nk), :]
        outs = [jnp.dot(p_ref[h], vwin, preferred_element_type=F32)
                + jnp.dot(pm_ref[h], vmb, preferred_element_type=F32) for h in range(2)]
        o_ref[row_slice(r, GRID_W), :] = jnp.where(lo, outs[0], outs[1])

    per = ATTN_ROWS_PER_STEP
    groups = rows // per

    def step(j, parity, do_scores, do_softmax, do_values):
        if do_scores:
            for k in range(per):
                store_scores(per * j + k, sc.at[parity, k], smc.at[parity, k])
        if do_values:
            for k in range(per):
                weighted_values(per * (j - 2) + k, pr.at[1 - parity, k], pmr.at[1 - parity, k])
        if do_softmax:
            for k in range(per):
                store_softmax(load_scores(sc.at[1 - parity, k], smc.at[1 - parity, k]),
                              pr.at[parity, k], pmr.at[parity, k])

    step(0, 0, True, False, False)
    step(1, 1, True, True, False)

    def two_steps(i, _):
        step(2 * i, 0, True, True, True)
        step(2 * i + 1, 1, True, True, True)
        return 0

    lax.fori_loop(1, groups // 2, two_steps, 0)
    step(groups, 0, False, True, True)
    step(groups + 1, 1, False, False, True)


def _bias_table(rel_bias):
    c = jnp.arange(GRID_W)
    col_start = jnp.clip(c - WIN_COLS // 2, 0, GRID_W - WIN_COLS)
    col_mask = (c[None, :] >= col_start[:, None]) & (c[None, :] < col_start[:, None] + WIN_COLS)
    dc = jnp.clip(c[None, :] - c[:, None], -(WIN_COLS - 1), WIN_COLS - 1) + (WIN_COLS - 1)
    onehot = (dc[None] == jnp.arange(2 * WIN_COLS - 1)[:, None, None]).astype(F32)
    toep = jnp.einsum('hdc,cqk->hdqk', rel_bias.astype(F32), onehot, precision=lax.Precision.HIGHEST)
    toep = jnp.where(col_mask[None, None], toep, -jnp.inf)
    return jnp.concatenate([toep[:, :-1], toep[:, 1:]], axis=-1)


def _attention(proj, proj_meta, q_norm_w, k_norm_w, rel_bias, meta_bias, bsz, seq):
    rows = seq // GRID_W
    assert rows >= WIN_ROWS and rows % (2 * ATTN_ROWS_PER_STEP) == 0
    nk = WIN_ROWS * GRID_W
    npairs = N_HEADS // 2
    qw = jnp.tile(q_norm_w.astype(F32), 2).reshape(1, LANES)
    kw = jnp.tile(k_norm_w.astype(F32), 2).reshape(1, LANES)
    bias = _bias_table(rel_bias)
    mb = meta_bias.astype(F32).reshape(N_HEADS, 1, N_META)
    return pl.pallas_call(
        functools.partial(_attn_body, rows=rows),
        grid=(bsz, npairs),
        in_specs=[pl.BlockSpec((seq, LANES), lambda b, p: (b, p)),
                  pl.BlockSpec((seq, LANES), lambda b, p: (b, npairs + p)),
                  pl.BlockSpec((seq, LANES), lambda b, p: (b, 2 * npairs + p)),
                  pl.BlockSpec((N_META, LANES), lambda b, p: (0, npairs + p)),
                  pl.BlockSpec((N_META, LANES), lambda b, p: (0, 2 * npairs + p)),
                  pl.BlockSpec((1, LANES), lambda b, p: (0, 0)),
                  pl.BlockSpec((1, LANES), lambda b, p: (0, 0)),
                  pl.BlockSpec((2, 2 * WIN_ROWS - 2, GRID_W, LANES), lambda b, p: (p, 0, 0, 0)),
                  pl.BlockSpec((2, 1, N_META), lambda b, p: (p, 0, 0))],
        out_specs=pl.BlockSpec((seq, LANES), lambda b, p: (b, p)),
        out_shape=jax.ShapeDtypeStruct((bsz * seq, ATTN_WIDTH), F32),
        scratch_shapes=([pltpu.VMEM((seq, LANES), BF16)] * 3
                        + [pltpu.VMEM((2, ATTN_ROWS_PER_STEP, 2, GRID_W, nk), F32),
                           pltpu.VMEM((2, ATTN_ROWS_PER_STEP, 2, GRID_W, N_META), F32),
                           pltpu.VMEM((2, ATTN_ROWS_PER_STEP, 2, GRID_W, nk), BF16),
                           pltpu.VMEM((2, ATTN_ROWS_PER_STEP, 2, GRID_W, N_META), BF16)]),
        compiler_params=_params("arbitrary", "arbitrary"),
        name="attention",
    )(proj, proj, proj, proj_meta, proj_meta, qw, kw, bias, mb)


def _conv_body(gb_ref, gc_ref, hc_ref, gcm_ref, hcm_ref, w_ref, o_ref):
    seq = gb_ref.shape[0]
    u = gc_ref[...] * hc_ref[...]
    u_meta_last = gcm_ref[N_META - 1:N_META, :] * hcm_ref[N_META - 1:N_META, :]
    row = lax.broadcasted_iota(jnp.int32, (seq, 1), 0)
    u_prev = jnp.where(row == 0, u_meta_last, pltpu.roll(u, 1, 0))
    u_next = jnp.where(row == seq - 1, 0.0, pltpu.roll(u, seq - 1, 0))
    w = w_ref[...]
    y = u_prev * w[0:1] + u * w[1:2] + u_next * w[2:3]
    o_ref[...] = gb_ref[...] * y


def _short_conv(proj, proj_meta, conv_w, bsz, seq):
    nct = CONV_WIDTH // LANES
    base = 3 * ATTN_WIDTH // LANES
    return pl.pallas_call(
        _conv_body,
        grid=(bsz, nct),
        in_specs=[pl.BlockSpec((seq, LANES), lambda b, c: (b, base + c)),
                  pl.BlockSpec((seq, LANES), lambda b, c: (b, base + nct + c)),
                  pl.BlockSpec((seq, LANES), lambda b, c: (b, base + 2 * nct + c)),
                  pl.BlockSpec((N_META, LANES), lambda b, c: (0, base + nct + c)),
                  pl.BlockSpec((N_META, LANES), lambda b, c: (0, base + 2 * nct + c)),
                  pl.BlockSpec((3, LANES), lambda b, c: (0, c))],
        out_specs=pl.BlockSpec((seq, LANES), lambda b, c: (b, c)),
        out_shape=jax.ShapeDtypeStruct((bsz * seq, CONV_WIDTH), F32),
        compiler_params=_params("arbitrary", "arbitrary"),
        name="short_conv",
    )(proj, proj, proj, proj_meta, proj_meta, conv_w.astype(F32))


def _rms(x, w):
    ms = jnp.mean(x * x, axis=-1, keepdims=True)
    return x * lax.rsqrt(ms + EPS) * w


def _pack_bf16_pairs(x):
    w = x.shape[1] // 2
    lo = lax.bitcast_convert_type(x[:, :w].astype(BF16).astype(F32), jnp.uint32)
    hi = lax.bitcast_convert_type(x[:, w:].astype(BF16).astype(F32), jnp.uint32)
    return (hi & jnp.uint32(0xFFFF0000)) | (lo >> 16)


def _unpack_bf16_pairs(p):
    lo = lax.bitcast_convert_type(p << 16, F32).astype(BF16)
    hi = lax.bitcast_convert_type(p & jnp.uint32(0xFFFF0000), F32).astype(BF16)
    return lo, hi


def _outproj_body(a_ref, c_ref, x_ref, aw_ref, cw_ref, wo_ref, fw_ref, wr_ref, br_ref,
                  h1_ref, hn_ref, lg_ref):
    sub = OUTPROJ_SUB
    blocks = [pl.ds(k * sub, sub) for k in range(a_ref.shape[0] // sub)]

    def mix(rows):
        an = _rms(a_ref[rows, :], aw_ref[...]).astype(BF16)
        cn = _rms(c_ref[rows, :], cw_ref[...]).astype(BF16)
        return (jnp.dot(an, wo_ref[0:ATTN_WIDTH, :], preferred_element_type=F32)
                + jnp.dot(cn, wo_ref[ATTN_WIDTH:D_MODEL, :], preferred_element_type=F32))

    def finish(rows, mixed):
        h1 = x_ref[rows, :] + mixed
        h1_ref[rows, :] = h1
        hn = _rms(h1, fw_ref[...])
        hn_ref[rows, :] = _pack_bf16_pairs(hn)
        lg_ref[rows, :] = jnp.dot(hn.astype(BF16), wr_ref[...], preferred_element_type=F32) + br_ref[...]

    mixed = mix(blocks[0])
    for k, rows in enumerate(blocks):
        following = mix(blocks[k + 1]) if k + 1 < len(blocks) else None
        finish(rows, mixed)
        mixed = following


def _outproj(a, c, x2d, aw, cw, wo_bf16, fw, w_router, b_router):
    t = x2d.shape[0]
    tm = OUTPROJ_TM
    row = lambda i: (i, 0)
    fixed = lambda i: (0, 0)
    return pl.pallas_call(
        _outproj_body,
        grid=(t // tm,),
        in_specs=[pl.BlockSpec((tm, ATTN_WIDTH), row),
                  pl.BlockSpec((tm, CONV_WIDTH), row),
                  pl.BlockSpec((tm, D_MODEL), row),
                  pl.BlockSpec((1, ATTN_WIDTH), fixed),
                  pl.BlockSpec((1, CONV_WIDTH), fixed),
                  pl.BlockSpec((D_MODEL, D_MODEL), fixed, pipeline_mode=pl.Buffered(1)),
                  pl.BlockSpec((1, D_MODEL), fixed),
                  pl.BlockSpec((D_MODEL, LANES), fixed),
                  pl.BlockSpec((1, LANES), fixed)],
        out_specs=[pl.BlockSpec((tm, D_MODEL), row),
                   pl.BlockSpec((tm, D_MODEL // 2), row),
                   pl.BlockSpec((tm, LANES), row)],
        out_shape=[jax.ShapeDtypeStruct((t, D_MODEL), F32),
                   jax.ShapeDtypeStruct((t, D_MODEL // 2), jnp.uint32),
                   jax.ShapeDtypeStruct((t, LANES), F32)],
        compiler_params=_params("arbitrary"),
        name="outproj",
    )(a, c, x2d, aw.reshape(1, -1), cw.reshape(1, -1), wo_bf16, fw.reshape(1, -1), w_router, b_router)


def _route_body(lg_ref, idx_ref, wt_ref, cnt_ref, run_ref):
    @pl.when(pl.program_id(0) == 0)
    def _():
        run_ref[...] = jnp.zeros_like(run_ref)

    logits = lg_ref[...]
    tm = logits.shape[0]
    lane = lax.broadcasted_iota(jnp.int32, (tm, LANES), 1)
    neg = -jnp.inf

    def first_argmax(v):
        m = jnp.max(v, axis=-1, keepdims=True)
        first = jnp.min(jnp.where(v == m, lane.astype(F32), float(LANES)), axis=-1, keepdims=True)
        return m, first.astype(jnp.int32)

    gl = jnp.where(lane < N_GROUPS, logits, neg)
    gmax, gidx = first_argmax(gl)
    g_w = 1.0 / jnp.sum(jnp.exp(gl - gmax), axis=-1, keepdims=True)
    first = N_GROUPS + gidx * EXPERTS_PER_GROUP
    el = jnp.where((lane >= first) & (lane < first + EXPERTS_PER_GROUP), logits, neg)
    m0, j0 = first_argmax(el)
    m1, j1 = first_argmax(jnp.where(lane == j0, neg, el))
    p1 = jnp.exp(m1 - m0)
    w0 = g_w / (1.0 + p1)
    w1 = g_w * p1 / (1.0 + p1)
    e0 = j0 - N_GROUPS
    e1 = j1 - N_GROUPS

    onehot = ((lane == e0) | (lane == e1)).astype(BF16)
    tri = (lax.broadcasted_iota(jnp.int32, (tm, tm), 0)
           > lax.broadcasted_iota(jnp.int32, (tm, tm), 1)).astype(BF16)
    before = jnp.dot(tri, onehot, preferred_element_type=F32) + run_ref[0:1, :]
    r0 = jnp.sum(jnp.where(lane == e0, before, 0.0), axis=-1, keepdims=True).astype(jnp.int32)
    r1 = jnp.sum(jnp.where(lane == e1, before, 0.0), axis=-1, keepdims=True).astype(jnp.int32)
    run = run_ref[0:1, :] + jnp.sum(onehot.astype(F32), axis=0, keepdims=True)
    run_ref[...] = jnp.broadcast_to(run, run_ref.shape)
    cnt_ref[...] = jnp.broadcast_to(run, cnt_ref.shape)

    zero = jnp.zeros_like(lane)
    idx_ref[...] = jnp.where(lane == 0, e0 * RANK_SPAN + r0, jnp.where(lane == 1, e1 * RANK_SPAN + r1, zero))
    wt_ref[...] = jnp.where(lane == 0, w0, jnp.where(lane == 1, w1, 0.0))


def _route(logits):
    t = logits.shape[0]
    tm = ROUTE_TM
    return pl.pallas_call(
        _route_body,
        grid=(t // tm,),
        in_specs=[pl.BlockSpec((tm, LANES), lambda i: (i, 0))],
        out_specs=[pl.BlockSpec((tm, LANES), lambda i: (i, 0)),
                   pl.BlockSpec((tm, LANES), lambda i: (i, 0)),
                   pl.BlockSpec((8, LANES), lambda i: (0, 0))],
        out_shape=[jax.ShapeDtypeStruct((t, LANES), jnp.int32),
                   jax.ShapeDtypeStruct((t, LANES), F32),
                   jax.ShapeDtypeStruct((8, LANES), F32)],
        scratch_shapes=[pltpu.VMEM((8, LANES), F32)],
        compiler_params=_params("arbitrary"),
        name="route",
    )(logits)


def _expert_body(sbe, sbs, sbn, sbr, codes, seg_start, seg_fill, tail, hn_hbm, wg_ref, wu_ref, wd_ref,
                 y_hbm, dest, x32, xb, acc, tok, gsem, osem):
    s = pl.program_id(0)
    c = pl.program_id(1)
    nsb = pl.num_programs(0)
    nch = EXPERT_FF // EXPERT_FF_CHUNK
    rb = EXPERT_ROW_BLOCK
    half = D_MODEL // 2
    n = sbn[s]
    start = sbs[s]
    nblk = n // rb
    slot = s % 2

    def for_each(count, fn):
        def body(i, _):
            fn(i)
            return 0
        lax.fori_loop(0, count, body, 0)

    def block_rows(j):
        return pl.ds(pl.multiple_of(j * rb, rb), rb)

    def gather_row(sb_first, buf, i):
        src = hn_hbm.at[pl.ds(tok[sb_first + i], 1)]
        pltpu.make_async_copy(src, x32.at[buf, pl.ds(i, 1)], gsem.at[buf]).start()

    def gather_rows(sb_first, buf, first_row, count):
        def group(g):
            for k in range(GATHER_UNROLL):
                gather_row(sb_first, buf, first_row + g * GATHER_UNROLL + k)
        for_each(count // GATHER_UNROLL, group)

    def wait_gathered(buf, count):
        rows = pl.ds(0, pl.multiple_of(count, GATHER_UNROLL))
        pltpu.make_async_copy(hn_hbm.at[rows], x32.at[buf, rows], gsem.at[buf]).wait()

    def out_copy(first_row, j):
        dst = pl.ds(pl.multiple_of(first_row + j * rb, rb), rb)
        return pltpu.make_async_copy(acc.at[block_rows(j)], y_hbm.at[dst], osem)

    @pl.when((s == 0) & (c == 0))
    def _():
        def clear(j):
            for buf in range(2):
                x32[buf, block_rows(j), :] = jnp.zeros((rb, half), jnp.uint32)
            acc[block_rows(j), :] = jnp.zeros((rb, D_MODEL), F32)
        for_each(EXPERT_CAP // rb, clear)

        def pad_rows(e):
            for k in range(GATHER_UNROLL - 1):
                tok[jnp.minimum(seg_fill[e] + k, tok.shape[0] - 1)] = 0
        for_each(N_EXPERTS, pad_rows)

        def sort_rows(g):
            for k in range(GATHER_UNROLL):
                a = g * GATHER_UNROLL + k
                code = codes[a]
                row = seg_start[code >> RANK_BITS] + (code & (RANK_SPAN - 1))
                dest[a] = row
                tok[row] = g * (GATHER_UNROLL // TOP_K) + k // TOP_K
        for_each(codes.shape[0] // GATHER_UNROLL, sort_rows)

        gather_rows(sbs[0], 0, 0, sbr[0])

    prev = jnp.maximum(s - 1, 0)
    nxt = jnp.minimum(s + 1, nsb - 1)
    prev_rows = jnp.where(s > 0, sbn[prev], 0)
    prev_pending = (c == 0) & (prev_rows > 0)

    @pl.when((c == 0) & (sbr[s] > 0))
    def _():
        wait_gathered(slot, sbr[s])

    def wait_prev_output():
        for_each(prev_rows // rb, lambda j: out_copy(sbs[prev], j).wait())

    @pl.when(prev_pending & (n == 0))
    def _():
        wait_prev_output()

    @pl.when((s == nsb - 1) & (c == 0))
    def _():
        first = tail[0]
        nfill = (y_hbm.shape[0] - first) // rb
        acc[0:rb, :] = jnp.zeros((rb, D_MODEL), F32)

        def fill_copy(j):
            dst = pl.ds(pl.multiple_of(first + j * rb, rb), rb)
            return pltpu.make_async_copy(acc.at[0:rb], y_hbm.at[dst], osem)

        for_each(nfill, lambda j: fill_copy(j).start())
        for_each(nfill, lambda j: fill_copy(j).wait())

    @pl.when(n > 0)
    def _():
        @pl.when(c == 0)
        def _():
            def unpack(j):
                rows = block_rows(j)
                lo, hi = _unpack_bf16_pairs(x32[slot, rows, :])
                xb[rows, 0:half] = lo
                xb[rows, half:D_MODEL] = hi
            for_each(nblk, unpack)

            @pl.when((s + 1 < nsb) & (sbr[nxt] > 0))
            def _():
                gather_rows(sbs[nxt], 1 - slot, 0, sbr[nxt])

            @pl.when(prev_pending)
            def _():
                wait_prev_output()

        def mlp(first_row, m):
            rows = pl.ds(pl.multiple_of(first_row, rb), m)
            x = xb[rows, :]
            g = jnp.dot(x, wg_ref[0].astype(BF16), preferred_element_type=F32)
            u = jnp.dot(x, wu_ref[0].astype(BF16), preferred_element_type=F32)
            h = (jax.nn.silu(g) * u).astype(BF16)
            y = jnp.dot(h, wd_ref[0].astype(BF16), preferred_element_type=F32)
            acc[rows, :] = jnp.where(c == 0, y, acc[rows, :] + y)

        wide = n // (2 * rb)
        for_each(wide, lambda j: mlp(j * 2 * rb, 2 * rb))

        @pl.when(n % (2 * rb) != 0)
        def _():
            mlp(wide * 2 * rb, rb)

        @pl.when(c == nch - 1)
        def _():
            for_each(nblk, lambda j: out_copy(start, j).start())


def _experts(hn_packed, w_gate, w_up, w_down, sb_expert, sb_start, sb_rows, sb_real, codes, seg_start,
             seg_fill, tail, n_rows):
    n_sb = sb_expert.shape[0]
    nch = EXPERT_FF // EXPERT_FF_CHUNK
    fc = EXPERT_FF_CHUNK

    def chunk(s, c, sbn):
        return jnp.where(sbn[s] > 0, c, nch - 1)

    def up_map(s, c, sbe, sbs, sbn, *_):
        return (sbe[s], 0, chunk(s, c, sbn))

    def down_map(s, c, sbe, sbs, sbn, *_):
        return (sbe[s], chunk(s, c, sbn), 0)

    grid_spec = pltpu.PrefetchScalarGridSpec(
        num_scalar_prefetch=8,
        grid=(n_sb, nch),
        in_specs=[pl.BlockSpec(memory_space=pl.ANY),
                  pl.BlockSpec((1, D_MODEL, fc), up_map),
                  pl.BlockSpec((1, D_MODEL, fc), up_map),
                  pl.BlockSpec((1, fc, D_MODEL), down_map)],
        out_specs=[pl.BlockSpec(memory_space=pl.ANY),
                   pl.BlockSpec(memory_space=pltpu.SMEM)],
        scratch_shapes=[pltpu.VMEM((2, EXPERT_CAP, D_MODEL // 2), jnp.uint32),
                        pltpu.VMEM((EXPERT_CAP, D_MODEL), BF16),
                        pltpu.VMEM((EXPERT_CAP, D_MODEL), F32),
                        pltpu.SMEM((n_rows,), jnp.int32),
                        pltpu.SemaphoreType.DMA((2,)),
                        pltpu.SemaphoreType.DMA(())],
    )
    return pl.pallas_call(
        _expert_body,
        grid_spec=grid_spec,
        out_shape=[jax.ShapeDtypeStruct((n_rows, D_MODEL), F32),
                   jax.ShapeDtypeStruct(codes.shape, jnp.int32)],
        compiler_params=_params("arbitrary", "arbitrary"),
        name="experts",
    )(sb_expert, sb_start, sb_rows, sb_real, codes, seg_start, seg_fill, tail, hn_packed, w_gate, w_up, w_down)


def _combine_body(dest, h1_ref, wt_ref, y_hbm, o_ref, g, sem):
    i = pl.program_id(0)
    nsteps = pl.num_programs(0)
    tm = h1_ref.shape[0]

    def issue(step, slot):
        def f(q, _):
            for u in range(GATHER_UNROLL):
                r = q * GATHER_UNROLL + u
                for k in range(TOP_K):
                    src = y_hbm.at[pl.ds(dest[(step * tm + r) * TOP_K + k], 1)]
                    pltpu.make_async_copy(src, g.at[slot, k, pl.ds(r, 1)], sem.at[slot]).start()
            return 0
        lax.fori_loop(0, tm // GATHER_UNROLL, f, 0)

    @pl.when(i == 0)
    def _():
        issue(0, 0)

    @pl.when(i + 1 < nsteps)
    def _():
        issue(i + 1, (i + 1) % 2)

    slot = i % 2
    for k in range(TOP_K):
        pltpu.make_async_copy(y_hbm.at[pl.ds(0, tm)], g.at[slot, k], sem.at[slot]).wait()

    w = wt_ref[...]
    o_ref[...] = h1_ref[...] + (w[:, 0:1] * g[slot, 0] + w[:, 1:2] * g[slot, 1])


def _combine(h1, wts, y_buf, dest_flat):
    t = h1.shape[0]
    tm = COMBINE_TM
    grid_spec = pltpu.PrefetchScalarGridSpec(
        num_scalar_prefetch=1,
        grid=(t // tm,),
        in_specs=[pl.BlockSpec((tm, D_MODEL), lambda i, d: (i, 0)),
                  pl.BlockSpec((tm, LANES), lambda i, d: (i, 0)),
                  pl.BlockSpec(memory_space=pl.ANY)],
        out_specs=pl.BlockSpec((tm, D_MODEL), lambda i, d: (i, 0)),
        scratch_shapes=[pltpu.VMEM((2, TOP_K, tm, D_MODEL), F32),
                        pltpu.SemaphoreType.DMA((2,))],
    )
    return pl.pallas_call(
        _combine_body,
        grid_spec=grid_spec,
        out_shape=jax.ShapeDtypeStruct((t, D_MODEL), F32),
        compiler_params=_params("arbitrary"),
        name="combine",
    )(dest_flat, h1, wts, y_buf)


def _dispatch_tables(idx, cnt, t):
    rb, cap = EXPERT_ROW_BLOCK, EXPERT_CAP
    n_assign = t * TOP_K
    n_rows = -(-(n_assign + N_EXPERTS * (rb - 1)) // rb) * rb
    n_sb = (n_rows + N_EXPERTS * (cap - rb)) // cap
    assert n_assign <= RANK_SPAN
    codes = idx[:, 0:TOP_K].reshape(-1)
    counts = cnt[0, :N_EXPERTS].astype(jnp.int32)
    padded = (counts + rb - 1) // rb * rb
    seg_end = jnp.cumsum(padded)
    seg_start = (seg_end - padded).astype(jnp.int32)
    seg_fill = (seg_start + counts).astype(jnp.int32)

    sb_per_expert = (padded + cap - 1) // cap
    sb_end = jnp.cumsum(sb_per_expert)
    total = sb_end[-1]
    s = jnp.arange(n_sb, dtype=jnp.int32)
    s_eff = jnp.minimum(s, total - 1)
    e = jnp.minimum(jnp.searchsorted(sb_end, s_eff, side='right'), N_EXPERTS - 1).astype(jnp.int32)
    local = s_eff - (sb_end[e] - sb_per_expert[e])
    sb_start = (seg_start[e] + local * cap).astype(jnp.int32)
    sb_rows = jnp.where(s < total, jnp.clip(padded[e] - local * cap, 0, cap), 0).astype(jnp.int32)
    gathered = (counts + GATHER_UNROLL - 1) // GATHER_UNROLL * GATHER_UNROLL
    sb_real = jnp.where(s < total, jnp.clip(gathered[e] - local * cap, 0, cap), 0).astype(jnp.int32)
    tail = seg_end[-1:].astype(jnp.int32)
    return codes, seg_start, seg_fill, e, sb_start, sb_rows, sb_real, tail, n_rows


def kernel(x, meta_tokens, mix_norm_w, w_in, q_norm_w, k_norm_w, rel_bias, meta_bias, conv_w,
           attn_out_norm_w, conv_out_norm_w, w_out, ffn_norm_w, w_router_group, b_router_group,
           w_router_expert, b_router_expert, w_gate, w_up, w_down):
    bsz, seq, d = x.shape
    depth = mix_norm_w.shape[0]
    assert depth == 1 and d == D_MODEL and seq % GRID_W == 0
    t = bsz * seq
    x2d = x.reshape(t, d)
    l = 0

    w_in_b = w_in[l].astype(BF16)
    proj = _inproj(x2d, mix_norm_w[l], w_in_b, INPROJ_TM)
    proj_meta = _inproj(meta_tokens.astype(x.dtype), mix_norm_w[l], w_in_b, N_META)

    a = _attention(proj, proj_meta, q_norm_w[l], k_norm_w[l], rel_bias[l], meta_bias[l], bsz, seq)
    c = _short_conv(proj, proj_meta, conv_w[l], bsz, seq)

    w_router = jnp.zeros((d, LANES), F32)
    w_router = w_router.at[:, :N_GROUPS].set(w_router_group[l].astype(F32))
    w_router = w_router.at[:, N_GROUPS:N_GROUPS + N_EXPERTS].set(w_router_expert[l].astype(F32))
    b_router = jnp.zeros((1, LANES), F32)
    b_router = b_router.at[0, :N_GROUPS].set(b_router_group[l].astype(F32))
    b_router = b_router.at[0, N_GROUPS:N_GROUPS + N_EXPERTS].set(b_router_expert[l].astype(F32))
    h1, hn, logits = _outproj(a, c, x2d, attn_out_norm_w[l], conv_out_norm_w[l], w_out[l].astype(BF16),
                              ffn_norm_w[l], w_router.astype(BF16), b_router)

    idx, wts, cnt = _route(logits)
    (codes, seg_start, seg_fill, sb_expert, sb_start, sb_rows, sb_real, tail,
     n_rows) = _dispatch_tables(idx, cnt, t)
    y_buf, dest = _experts(hn, w_gate.reshape(N_EXPERTS, d, EXPERT_FF), w_up.reshape(N_EXPERTS, d, EXPERT_FF),
                           w_down.reshape(N_EXPERTS, EXPERT_FF, d), sb_expert, sb_start, sb_rows, sb_real,
                           codes, seg_start, seg_fill, tail, n_rows)
    out = _combine(h1, wts, y_buf, dest)
    return out.reshape(bsz, seq, d)
```

```python
import functools

import jax
import jax.numpy as jnp
from jax import lax
from jax.experimental import pallas as pl
from jax.experimental.pallas import tpu as pltpu

F32 = jnp.float32
BF16 = jnp.bfloat16

D_MODEL = 2048
N_META = 16
GRID_W = 64
N_HEADS = 16
HEAD_DIM = 64
ATTN_WIDTH = N_HEADS * HEAD_DIM
CONV_WIDTH = D_MODEL - ATTN_WIDTH
PROJ_TOTAL = 3 * ATTN_WIDTH + 3 * CONV_WIDTH
WIN_ROWS = 8
WIN_COLS = 16
N_GROUPS = 4
EXPERTS_PER_GROUP = 8
N_EXPERTS = N_GROUPS * EXPERTS_PER_GROUP
TOP_K = 2
EXPERT_FF = 1024
EPS = 1e-6

LANES = 128
VMEM_LIMIT = 52 * 1024 * 1024

ATTN_ROWS_PER_STEP = 4
INPROJ_TM = 1024
INPROJ_TN = 1024
OUTPROJ_TM = 512
OUTPROJ_SUB = 256
ROUTE_TM = 512
EXPERT_ROW_BLOCK = 128
EXPERT_BLOCKS = (512, 256, 128)
EXPERT_CAP = 1024
EXPERT_FF_CHUNK = 256
GATHER_UNROLL = 8
RANK_BITS = 15
RANK_SPAN = 1 << RANK_BITS
COMBINE_TM = 256


def _params(*sem):
    return pltpu.CompilerParams(dimension_semantics=sem, vmem_limit_bytes=VMEM_LIMIT)


def _inproj_body(x_ref, nw_ref, w_ref, o_ref, xn_ref):
    @pl.when(pl.program_id(1) == 0)
    def _():
        x = x_ref[...]
        ms = jnp.mean(x * x, axis=-1, keepdims=True)
        xn_ref[...] = (x * lax.rsqrt(ms + EPS) * nw_ref[...]).astype(BF16)

    o_ref[...] = jnp.dot(xn_ref[...], w_ref[...].astype(BF16), preferred_element_type=F32)


def _inproj(x2d, norm_w, w_bf16, tm):
    m = x2d.shape[0]
    tn = INPROJ_TN
    return pl.pallas_call(
        _inproj_body,
        grid=(m // tm, PROJ_TOTAL // tn),
        in_specs=[pl.BlockSpec((tm, D_MODEL), lambda i, j: (i, 0)),
                  pl.BlockSpec((1, D_MODEL), lambda i, j: (0, 0)),
                  pl.BlockSpec((D_MODEL, tn), lambda i, j: (0, j))],
        out_specs=pl.BlockSpec((tm, tn), lambda i, j: (i, j)),
        out_shape=jax.ShapeDtypeStruct((m, PROJ_TOTAL), F32),
        scratch_shapes=[pltpu.VMEM((tm, D_MODEL), BF16)],
        compiler_params=_params("arbitrary", "arbitrary"),
        name="inproj",
    )(x2d, norm_w.reshape(1, D_MODEL), w_bf16)


def _head_norm(x, w, lo):
    x2 = x * x
    s_lo = jnp.sum(jnp.where(lo, x2, 0.0), axis=-1, keepdims=True)
    s_hi = jnp.sum(jnp.where(lo, 0.0, x2), axis=-1, keepdims=True)
    ms = jnp.where(lo, s_lo, s_hi) * (1.0 / HEAD_DIM)
    return x * lax.rsqrt(ms + EPS) * w


def _attn_body(q_ref, k_ref, v_ref, km_ref, vm_ref, qw_ref, kw_ref, bias_ref, mb_ref, o_ref,
               qs, ks, vs, sc, smc, pr, pmr, *, rows):
    lo = lax.broadcasted_iota(jnp.int32, (1, LANES), 1) < HEAD_DIM
    scale = HEAD_DIM ** -0.5
    chunk = 256
    seq = rows * GRID_W

    def prep(i, _):
        sl = pl.ds(pl.multiple_of(i * chunk, chunk), chunk)
        qs[sl, :] = (_head_norm(q_ref[sl, :], qw_ref[...], lo) * scale).astype(BF16)
        ks[sl, :] = _head_norm(k_ref[sl, :], kw_ref[...], lo).astype(BF16)
        vs[sl, :] = v_ref[sl, :].astype(BF16)
        return 0

    lax.fori_loop(0, seq // chunk, prep, 0, unroll=2)
    kmb = _head_norm(km_ref[...], kw_ref[...], lo).astype(BF16)
    vmb = vm_ref[...].astype(BF16)
    wr = min(WIN_ROWS, rows)
    nk = wr * GRID_W
    contract_last = (((1,), (1,)), ((), ()))

    def window_start(r):
        return jnp.clip(r - wr // 2, 0, rows - wr)

    def row_slice(r, n):
        return pl.ds(pl.multiple_of(r * GRID_W, GRID_W), n)

    def store_scores(r, s_ref, sm_ref):
        rs = window_start(r)
        si = rs - r + (WIN_ROWS - 1)
        q_r = qs[row_slice(r, GRID_W), :]
        kwin = ks[row_slice(rs, nk), :]
        for h in range(2):
            mask = lo if h == 0 else jnp.logical_not(lo)
            qh = jnp.where(mask, q_r, jnp.zeros_like(q_r))
            bias = jnp.concatenate([bias_ref[h, si + w] for w in range(0, wr, 2)], axis=-1)
            s_ref[h] = lax.dot_general(qh, kwin, contract_last, preferred_element_type=F32) + bias
            sm_ref[h] = lax.dot_general(qh, kmb, contract_last, preferred_element_type=F32) + mb_ref[h]

    def load_scores(s_ref, sm_ref):
        return [(s_ref[h], sm_ref[h]) for h in range(2)]

    def store_softmax(row_scores, p_ref, pm_ref):
        for h, (s, sm) in enumerate(row_scores):
            m = jnp.maximum(jnp.max(s, axis=-1, keepdims=True), jnp.max(sm, axis=-1, keepdims=True))
            p = jnp.exp(s - m)
            pm = jnp.exp(sm - m)
            inv = 1.0 / (jnp.sum(p, axis=-1, keepdims=True) + jnp.sum(pm, axis=-1, keepdims=True))
            p_ref[h] = (p * inv).astype(BF16)
            pm_ref[h] = (pm * inv).astype(BF16)

    def weighted_values(r, p_ref, pm_ref):
        vwin = vs[row_slice(window_start(r), nk), :]
        outs = [jnp.dot(p_ref[h], vwin, preferred_element_type=F32)
                + jnp.dot(pm_ref[h], vmb, preferred_element_type=F32) for h in range(2)]
        o_ref[row_slice(r, GRID_W), :] = jnp.where(lo, outs[0], outs[1])

    per = ATTN_ROWS_PER_STEP
    groups = rows // per

    def step(j, parity, do_scores, do_softmax, do_values):
        if do_scores:
            for k in range(per):
                store_scores(per * j + k, sc.at[parity, k], smc.at[parity, k])
        if do_values:
            for k in range(per):
                weighted_values(per * (j - 2) + k, pr.at[1 - parity, k], pmr.at[1 - parity, k])
        if do_softmax:
            for k in range(per):
                store_softmax(load_scores(sc.at[1 - parity, k], smc.at[1 - parity, k]),
                              pr.at[parity, k], pmr.at[parity, k])

    step(0, 0, True, False, False)
    step(1, 1, True, True, False)

    def two_steps(i, _):
        step(2 * i, 0, True, True, True)
        step(2 * i + 1, 1, True, True, True)
        return 0

    lax.fori_loop(1, groups // 2, two_steps, 0)
    step(groups, 0, False, True, True)
    step(groups + 1, 1, False, False, True)


def _bias_table(rel_bias):
    c = jnp.arange(GRID_W)
    col_start = jnp.clip(c - WIN_COLS // 2, 0, GRID_W - WIN_COLS)
    col_mask = (c[None, :] >= col_start[:, None]) & (c[None, :] < col_start[:, None] + WIN_COLS)
    dc = jnp.clip(c[None, :] - c[:, None], -(WIN_COLS - 1), WIN_COLS - 1) + (WIN_COLS - 1)
    onehot = (dc[None] == jnp.arange(2 * WIN_COLS - 1)[:, None, None]).astype(F32)
    toep = jnp.einsum('hdc,cqk->hdqk', rel_bias.astype(F32), onehot, precision=lax.Precision.HIGHEST)
    toep = jnp.where(col_mask[None, None], toep, -jnp.inf)
    return jnp.concatenate([toep[:, :-1], toep[:, 1:]], axis=-1)


def _attention(proj, proj_meta, q_norm_w, k_norm_w, rel_bias, meta_bias, bsz, seq):
    rows = seq // GRID_W
    assert rows >= WIN_ROWS and rows % (2 * ATTN_ROWS_PER_STEP) == 0
    nk = WIN_ROWS * GRID_W
    npairs = N_HEADS // 2
    qw = jnp.tile(q_norm_w.astype(F32), 2).reshape(1, LANES)
    kw = jnp.tile(k_norm_w.astype(F32), 2).reshape(1, LANES)
    bias = _bias_table(rel_bias)
    mb = meta_bias.astype(F32).reshape(N_HEADS, 1, N_META)
    return pl.pallas_call(
        functools.partial(_attn_body, rows=rows),
        grid=(bsz, npairs),
        in_specs=[pl.BlockSpec((seq, LANES), lambda b, p: (b, p)),
                  pl.BlockSpec((seq, LANES), lambda b, p: (b, npairs + p)),
                  pl.BlockSpec((seq, LANES), lambda b, p: (b, 2 * npairs + p)),
                  pl.BlockSpec((N_META, LANES), lambda b, p: (0, npairs + p)),
                  pl.BlockSpec((N_META, LANES), lambda b, p: (0, 2 * npairs + p)),
                  pl.BlockSpec((1, LANES), lambda b, p: (0, 0)),
                  pl.BlockSpec((1, LANES), lambda b, p: (0, 0)),
                  pl.BlockSpec((2, 2 * WIN_ROWS - 2, GRID_W, LANES), lambda b, p: (p, 0, 0, 0)),
                  pl.BlockSpec((2, 1, N_META), lambda b, p: (p, 0, 0))],
        out_specs=pl.BlockSpec((seq, LANES), lambda b, p: (b, p)),
        out_shape=jax.ShapeDtypeStruct((bsz * seq, ATTN_WIDTH), F32),
        scratch_shapes=([pltpu.VMEM((seq, LANES), BF16)] * 3
                        + [pltpu.VMEM((2, ATTN_ROWS_PER_STEP, 2, GRID_W, nk), F32),
                           pltpu.VMEM((2, ATTN_ROWS_PER_STEP, 2, GRID_W, N_META), F32),
                           pltpu.VMEM((2, ATTN_ROWS_PER_STEP, 2, GRID_W, nk), BF16),
                           pltpu.VMEM((2, ATTN_ROWS_PER_STEP, 2, GRID_W, N_META), BF16)]),
        compiler_params=_params("arbitrary", "arbitrary"),
        name="attention",
    )(proj, proj, proj, proj_meta, proj_meta, qw, kw, bias, mb)


def _conv_body(gb_ref, gc_ref, hc_ref, gcm_ref, hcm_ref, w_ref, o_ref):
    seq = gb_ref.shape[0]
    u = gc_ref[...] * hc_ref[...]
    u_meta_last = gcm_ref[N_META - 1:N_META, :] * hcm_ref[N_META - 1:N_META, :]
    row = lax.broadcasted_iota(jnp.int32, (seq, 1), 0)
    u_prev = jnp.where(row == 0, u_meta_last, pltpu.roll(u, 1, 0))
    u_next = jnp.where(row == seq - 1, 0.0, pltpu.roll(u, seq - 1, 0))
    w = w_ref[...]
    y = u_prev * w[0:1] + u * w[1:2] + u_next * w[2:3]
    o_ref[...] = gb_ref[...] * y


def _short_conv(proj, proj_meta, conv_w, bsz, seq):
    nct = CONV_WIDTH // LANES
    base = 3 * ATTN_WIDTH // LANES
    return pl.pallas_call(
        _conv_body,
        grid=(bsz, nct),
        in_specs=[pl.BlockSpec((seq, LANES), lambda b, c: (b, base + c)),
                  pl.BlockSpec((seq, LANES), lambda b, c: (b, base + nct + c)),
                  pl.BlockSpec((seq, LANES), lambda b, c: (b, base + 2 * nct + c)),
                  pl.BlockSpec((N_META, LANES), lambda b, c: (0, base + nct + c)),
                  pl.BlockSpec((N_META, LANES), lambda b, c: (0, base + 2 * nct + c)),
                  pl.BlockSpec((3, LANES), lambda b, c: (0, c))],
        out_specs=pl.BlockSpec((seq, LANES), lambda b, c: (b, c)),
        out_shape=jax.ShapeDtypeStruct((bsz * seq, CONV_WIDTH), F32),
        compiler_params=_params("arbitrary", "arbitrary"),
        name="short_conv",
    )(proj, proj, proj, proj_meta, proj_meta, conv_w.astype(F32))


def _rms(x, w):
    ms = jnp.mean(x * x, axis=-1, keepdims=True)
    return x * lax.rsqrt(ms + EPS) * w


def _pack_bf16_pairs(x):
    w = x.shape[1] // 2
    lo = lax.bitcast_convert_type(x[:, :w].astype(BF16).astype(F32), jnp.uint32)
    hi = lax.bitcast_convert_type(x[:, w:].astype(BF16).astype(F32), jnp.uint32)
    return (hi & jnp.uint32(0xFFFF0000)) | (lo >> 16)


def _unpack_bf16_pairs(p):
    lo = lax.bitcast_convert_type(p << 16, F32).astype(BF16)
    hi = lax.bitcast_convert_type(p & jnp.uint32(0xFFFF0000), F32).astype(BF16)
    return lo, hi


def _outproj_body(a_ref, c_ref, x_ref, aw_ref, cw_ref, wo_ref, fw_ref, wr_ref, br_ref,
                  h1_ref, hn_ref, lg_ref):
    sub = OUTPROJ_SUB
    blocks = [pl.ds(k * sub, sub) for k in range(a_ref.shape[0] // sub)]

    def mix(rows):
        an = _rms(a_ref[rows, :], aw_ref[...]).astype(BF16)
        cn = _rms(c_ref[rows, :], cw_ref[...]).astype(BF16)
        return (jnp.dot(an, wo_ref[0:ATTN_WIDTH, :], preferred_element_type=F32)
                + jnp.dot(cn, wo_ref[ATTN_WIDTH:D_MODEL, :], preferred_element_type=F32))

    def finish(rows, mixed):
        h1 = x_ref[rows, :] + mixed
        h1_ref[rows, :] = h1
        hn = _rms(h1, fw_ref[...])
        hn_ref[rows, :] = _pack_bf16_pairs(hn)
        lg_ref[rows, :] = jnp.dot(hn.astype(BF16), wr_ref[...], preferred_element_type=F32) + br_ref[...]

    mixed = mix(blocks[0])
    for k, rows in enumerate(blocks):
        following = mix(blocks[k + 1]) if k + 1 < len(blocks) else None
        finish(rows, mixed)
        mixed = following


def _outproj(a, c, x2d, aw, cw, wo_bf16, fw, w_router, b_router):
    t = x2d.shape[0]
    tm = OUTPROJ_TM
    row = lambda i: (i, 0)
    fixed = lambda i: (0, 0)
    return pl.pallas_call(
        _outproj_body,
        grid=(t // tm,),
        in_specs=[pl.BlockSpec((tm, ATTN_WIDTH), row),
                  pl.BlockSpec((tm, CONV_WIDTH), row),
                  pl.BlockSpec((tm, D_MODEL), row),
                  pl.BlockSpec((1, ATTN_WIDTH), fixed),
                  pl.BlockSpec((1, CONV_WIDTH), fixed),
                  pl.BlockSpec((D_MODEL, D_MODEL), fixed, pipeline_mode=pl.Buffered(1)),
                  pl.BlockSpec((1, D_MODEL), fixed),
                  pl.BlockSpec((D_MODEL, LANES), fixed),
                  pl.BlockSpec((1, LANES), fixed)],
        out_specs=[pl.BlockSpec((tm, D_MODEL), row),
                   pl.BlockSpec((tm, D_MODEL // 2), row),
                   pl.BlockSpec((tm, LANES), row)],
        out_shape=[jax.ShapeDtypeStruct((t, D_MODEL), F32),
                   jax.ShapeDtypeStruct((t, D_MODEL // 2), jnp.uint32),
                   jax.ShapeDtypeStruct((t, LANES), F32)],
        compiler_params=_params("arbitrary"),
        name="outproj",
    )(a, c, x2d, aw.reshape(1, -1), cw.reshape(1, -1), wo_bf16, fw.reshape(1, -1), w_router, b_router)


def _route_body(lg_ref, idx_ref, wt_ref, cnt_ref, run_ref):
    @pl.when(pl.program_id(0) == 0)
    def _():
        run_ref[...] = jnp.zeros_like(run_ref)

    logits = lg_ref[...]
    tm = logits.shape[0]
    lane = lax.broadcasted_iota(jnp.int32, (tm, LANES), 1)
    neg = -jnp.inf

    def first_argmax(v):
        m = jnp.max(v, axis=-1, keepdims=True)
        first = jnp.min(jnp.where(v == m, lane.astype(F32), float(LANES)), axis=-1, keepdims=True)
        return m, first.astype(jnp.int32)

    gl = jnp.where(lane < N_GROUPS, logits, neg)
    gmax, gidx = first_argmax(gl)
    g_w = 1.0 / jnp.sum(jnp.exp(gl - gmax), axis=-1, keepdims=True)
    first = N_GROUPS + gidx * EXPERTS_PER_GROUP
    el = jnp.where((lane >= first) & (lane < first + EXPERTS_PER_GROUP), logits, neg)
    m0, j0 = first_argmax(el)
    m1, j1 = first_argmax(jnp.where(lane == j0, neg, el))
    p1 = jnp.exp(m1 - m0)
    w0 = g_w / (1.0 + p1)
    w1 = g_w * p1 / (1.0 + p1)
    e0 = j0 - N_GROUPS
    e1 = j1 - N_GROUPS

    onehot = ((lane == e0) | (lane == e1)).astype(BF16)
    tri = (lax.broadcasted_iota(jnp.int32, (tm, tm), 0)
           > lax.broadcasted_iota(jnp.int32, (tm, tm), 1)).astype(BF16)
    before = jnp.dot(tri, onehot, preferred_element_type=F32) + run_ref[0:1, :]
    r0 = jnp.sum(jnp.where(lane == e0, before, 0.0), axis=-1, keepdims=True).astype(jnp.int32)
    r1 = jnp.sum(jnp.where(lane == e1, before, 0.0), axis=-1, keepdims=True).astype(jnp.int32)
    run = run_ref[0:1, :] + jnp.sum(onehot.astype(F32), axis=0, keepdims=True)
    run_ref[...] = jnp.broadcast_to(run, run_ref.shape)
    cnt_ref[...] = jnp.broadcast_to(run, cnt_ref.shape)

    zero = jnp.zeros_like(lane)
    idx_ref[...] = jnp.where(lane == 0, e0 * RANK_SPAN + r0, jnp.where(lane == 1, e1 * RANK_SPAN + r1, zero))
    wt_ref[...] = jnp.where(lane == 0, w0, jnp.where(lane == 1, w1, 0.0))


def _route(logits):
    t = logits.shape[0]
    tm = ROUTE_TM
    return pl.pallas_call(
        _route_body,
        grid=(t // tm,),
        in_specs=[pl.BlockSpec((tm, LANES), lambda i: (i, 0))],
        out_specs=[pl.BlockSpec((tm, LANES), lambda i: (i, 0)),
                   pl.BlockSpec((tm, LANES), lambda i: (i, 0)),
                   pl.BlockSpec((8, LANES), lambda i: (0, 0))],
        out_shape=[jax.ShapeDtypeStruct((t, LANES), jnp.int32),
                   jax.ShapeDtypeStruct((t, LANES), F32),
                   jax.ShapeDtypeStruct((8, LANES), F32)],
        scratch_shapes=[pltpu.VMEM((8, LANES), F32)],
        compiler_params=_params("arbitrary"),
        name="route",
    )(logits)


def _expert_body(sbe, sbs, sbn, sbr, codes, seg_start, seg_fill, tail, hn_hbm, wg_ref, wu_ref, wd_ref,
                 y_hbm, dest, x32, xb, acc, tok, gsem, osem):
    s = pl.program_id(0)
    c = pl.program_id(1)
    nsb = pl.num_programs(0)
    nch = EXPERT_FF // EXPERT_FF_CHUNK
    rb = EXPERT_ROW_BLOCK
    half = D_MODEL // 2
    n = sbn[s]
    start = sbs[s]
    nblk = n // rb
    slot = s % 2

    def for_each(count, fn):
        def body(i, _):
            fn(i)
            return 0
        lax.fori_loop(0, count, body, 0)

    def block_rows(j):
        return pl.ds(pl.multiple_of(j * rb, rb), rb)

    def gather_row(sb_first, buf, i):
        src = hn_hbm.at[pl.ds(tok[sb_first + i], 1)]
        pltpu.make_async_copy(src, x32.at[buf, pl.ds(i, 1)], gsem.at[buf]).start()

    def gather_rows(sb_first, buf, first_row, count):
        def group(g):
            for k in range(GATHER_UNROLL):
                gather_row(sb_first, buf, first_row + g * GATHER_UNROLL + k)
        for_each(count // GATHER_UNROLL, group)

    def wait_gathered(buf, count):
        rows = pl.ds(0, pl.multiple_of(count, GATHER_UNROLL))
        pltpu.make_async_copy(hn_hbm.at[rows], x32.at[buf, rows], gsem.at[buf]).wait()

    def out_copy(first_row, j):
        dst = pl.ds(pl.multiple_of(first_row + j * rb, rb), rb)
        return pltpu.make_async_copy(acc.at[block_rows(j)], y_hbm.at[dst], osem)

    @pl.when((s == 0) & (c == 0))
    def _():
        def clear(j):
            for buf in range(2):
                x32[buf, block_rows(j), :] = jnp.zeros((rb, half), jnp.uint32)
            acc[block_rows(j), :] = jnp.zeros((rb, D_MODEL), F32)
        for_each(EXPERT_CAP // rb, clear)

        def pad_rows(e):
            for k in range(GATHER_UNROLL - 1):
                tok[jnp.minimum(seg_fill[e] + k, tok.shape[0] - 1)] = 0
        for_each(N_EXPERTS, pad_rows)

        def sort_rows(g):
            for k in range(GATHER_UNROLL):
                a = g * GATHER_UNROLL + k
                code = codes[a]
                row = seg_start[code >> RANK_BITS] + (code & (RANK_SPAN - 1))
                dest[a] = row
                tok[row] = g * (GATHER_UNROLL // TOP_K) + k // TOP_K
        for_each(codes.shape[0] // GATHER_UNROLL, sort_rows)

        gather_rows(sbs[0], 0, 0, sbr[0])

    prev = jnp.maximum(s - 1, 0)
    nxt = jnp.minimum(s + 1, nsb - 1)
    prev_rows = jnp.where(s > 0, sbn[prev], 0)
    prev_pending = (c == 0) & (prev_rows > 0)

    @pl.when((c == 0) & (sbr[s] > 0))
    def _():
        wait_gathered(slot, sbr[s])

    def wait_prev_output():
        for_each(prev_rows // rb, lambda j: out_copy(sbs[prev], j).wait())

    @pl.when(prev_pending & (n == 0))
    def _():
        wait_prev_output()

    @pl.when((s == nsb - 1) & (c == 0))
    def _():
        first = tail[0]
        nfill = (y_hbm.shape[0] - first) // rb
        acc[0:rb, :] = jnp.zeros((rb, D_MODEL), F32)

        def fill_copy(j):
            dst = pl.ds(pl.multiple_of(first + j * rb, rb), rb)
            return pltpu.make_async_copy(acc.at[0:rb], y_hbm.at[dst], osem)

        for_each(nfill, lambda j: fill_copy(j).start())
        for_each(nfill, lambda j: fill_copy(j).wait())

    @pl.when(n > 0)
    def _():
        @pl.when(c == 0)
        def _():
            def unpack(j):
                rows = block_rows(j)
                lo, hi = _unpack_bf16_pairs(x32[slot, rows, :])
                xb[rows, 0:half] = lo
                xb[rows, half:D_MODEL] = hi
            for_each(nblk, unpack)

            @pl.when((s + 1 < nsb) & (sbr[nxt] > 0))
            def _():
                gather_rows(sbs[nxt], 1 - slot, 0, sbr[nxt])

            @pl.when(prev_pending)
            def _():
                wait_prev_output()

        def mlp(first_row, m):
            rows = pl.ds(pl.multiple_of(first_row, rb), m)
            x = xb[rows, :]
            g = jnp.dot(x, wg_ref[0].astype(BF16), preferred_element_type=F32)
            u = jnp.dot(x, wu_ref[0].astype(BF16), preferred_element_type=F32)
            h = (jax.nn.silu(g) * u).astype(BF16)
            y = jnp.dot(h, wd_ref[0].astype(BF16), preferred_element_type=F32)
            acc[rows, :] = jnp.where(c == 0, y, acc[rows, :] + y)

        done = 0
        for m in EXPERT_BLOCKS[:-1]:
            count = (n - done) // m
            for_each(count, lambda j, done=done, m=m: mlp(done + j * m, m))
            done = done + count * m

        @pl.when(done < n)
        def _():
            mlp(done, rb)

        @pl.when(c == nch - 1)
        def _():
            for_each(nblk, lambda j: out_copy(start, j).start())


def _experts(hn_packed, w_gate, w_up, w_down, sb_expert, sb_start, sb_rows, sb_real, codes, seg_start,
             seg_fill, tail, n_rows):
    n_sb = sb_expert.shape[0]
    nch = EXPERT_FF // EXPERT_FF_CHUNK
    fc = EXPERT_FF_CHUNK

    def chunk(s, c, sbn):
        return jnp.where(sbn[s] > 0, c, nch - 1)

    def up_map(s, c, sbe, sbs, sbn, *_):
        return (sbe[s], 0, chunk(s, c, sbn))

    def down_map(s, c, sbe, sbs, sbn, *_):
        return (sbe[s], chunk(s, c, sbn), 0)

    grid_spec = pltpu.PrefetchScalarGridSpec(
        num_scalar_prefetch=8,
        grid=(n_sb, nch),
        in_specs=[pl.BlockSpec(memory_space=pl.ANY),
                  pl.BlockSpec((1, D_MODEL, fc), up_map),
                  pl.BlockSpec((1, D_MODEL, fc), up_map),
                  pl.BlockSpec((1, fc, D_MODEL), down_map)],
        out_specs=[pl.BlockSpec(memory_space=pl.ANY),
                   pl.BlockSpec(memory_space=pltpu.SMEM)],
        scratch_shapes=[pltpu.VMEM((2, EXPERT_CAP, D_MODEL // 2), jnp.uint32),
                        pltpu.VMEM((EXPERT_CAP, D_MODEL), BF16),
                        pltpu.VMEM((EXPERT_CAP, D_MODEL), F32),
                        pltpu.SMEM((n_rows,), jnp.int32),
                        pltpu.SemaphoreType.DMA((2,)),
                        pltpu.SemaphoreType.DMA(())],
    )
    return pl.pallas_call(
        _expert_body,
        grid_spec=grid_spec,
        out_shape=[jax.ShapeDtypeStruct((n_rows, D_MODEL), F32),
                   jax.ShapeDtypeStruct(codes.shape, jnp.int32)],
        compiler_params=_params("arbitrary", "arbitrary"),
        name="experts",
    )(sb_expert, sb_start, sb_rows, sb_real, codes, seg_start, seg_fill, tail, hn_packed, w_gate, w_up, w_down)


def _combine_body(dest, h1_ref, wt_ref, y_hbm, o_ref, g, sem):
    i = pl.program_id(0)
    nsteps = pl.num_programs(0)
    tm = h1_ref.shape[0]

    def issue(step, slot):
        def f(q, _):
            for u in range(GATHER_UNROLL):
                r = q * GATHER_UNROLL + u
                for k in range(TOP_K):
                    src = y_hbm.at[pl.ds(dest[(step * tm + r) * TOP_K + k], 1)]
                    pltpu.make_async_copy(src, g.at[slot, k, pl.ds(r, 1)], sem.at[slot]).start()
            return 0
        lax.fori_loop(0, tm // GATHER_UNROLL, f, 0)

    @pl.when(i == 0)
    def _():
        issue(0, 0)

    @pl.when(i + 1 < nsteps)
    def _():
        issue(i + 1, (i + 1) % 2)

    slot = i % 2
    for k in range(TOP_K):
        pltpu.make_async_copy(y_hbm.at[pl.ds(0, tm)], g.at[slot, k], sem.at[slot]).wait()

    w = wt_ref[...]
    o_ref[...] = h1_ref[...] + (w[:, 0:1] * g[slot, 0] + w[:, 1:2] * g[slot, 1])


def _combine(h1, wts, y_buf, dest_flat):
    t = h1.shape[0]
    tm = COMBINE_TM
    grid_spec = pltpu.PrefetchScalarGridSpec(
        num_scalar_prefetch=1,
        grid=(t // tm,),
        in_specs=[pl.BlockSpec((tm, D_MODEL), lambda i, d: (i, 0)),
                  pl.BlockSpec((tm, LANES), lambda i, d: (i, 0)),
                  pl.BlockSpec(memory_space=pl.ANY)],
        out_specs=pl.BlockSpec((tm, D_MODEL), lambda i, d: (i, 0)),
        scratch_shapes=[pltpu.VMEM((2, TOP_K, tm, D_MODEL), F32),
                        pltpu.SemaphoreType.DMA((2,))],
    )
    return pl.pallas_call(
        _combine_body,
        grid_spec=grid_spec,
        out_shape=jax.ShapeDtypeStruct((t, D_MODEL), F32),
        compiler_params=_params("arbitrary"),
        name="combine",
    )(dest_flat, h1, wts, y_buf)


def _dispatch_tables(idx, cnt, t):
    rb, cap = EXPERT_ROW_BLOCK, EXPERT_CAP
    n_assign = t * TOP_K
    n_rows = -(-(n_assign + N_EXPERTS * (rb - 1)) // rb) * rb
    n_sb = (n_rows + N_EXPERTS * (cap - rb)) // cap
    assert n_assign <= RANK_SPAN
    codes = idx[:, 0:TOP_K].reshape(-1)
    counts = cnt[0, :N_EXPERTS].astype(jnp.int32)
    padded = (counts + rb - 1) // rb * rb
    seg_end = jnp.cumsum(padded)
    seg_start = (seg_end - padded).astype(jnp.int32)
    seg_fill = (seg_start + counts).astype(jnp.int32)

    sb_per_expert = (padded + cap - 1) // cap
    sb_end = jnp.cumsum(sb_per_expert)
    total = sb_end[-1]
    s = jnp.arange(n_sb, dtype=jnp.int32)
    s_eff = jnp.minimum(s, total - 1)
    e = jnp.minimum(jnp.searchsorted(sb_end, s_eff, side='right'), N_EXPERTS - 1).astype(jnp.int32)
    local = s_eff - (sb_end[e] - sb_per_expert[e])
    sb_start = (seg_start[e] + local * cap).astype(jnp.int32)
    sb_rows = jnp.where(s < total, jnp.clip(padded[e] - local * cap, 0, cap), 0).astype(jnp.int32)
    gathered = (counts + GATHER_UNROLL - 1) // GATHER_UNROLL * GATHER_UNROLL
    sb_real = jnp.where(s < total, jnp.clip(gathered[e] - local * cap, 0, cap), 0).astype(jnp.int32)
    tail = seg_end[-1:].astype(jnp.int32)
    return codes, seg_start, seg_fill, e, sb_start, sb_rows, sb_real, tail, n_rows


def kernel(x, meta_tokens, mix_norm_w, w_in, q_norm_w, k_norm_w, rel_bias, meta_bias, conv_w,
           attn_out_norm_w, conv_out_norm_w, w_out, ffn_norm_w, w_router_group, b_router_group,
           w_router_expert, b_router_expert, w_gate, w_up, w_down):
    bsz, seq, d = x.shape
    depth = mix_norm_w.shape[0]
    assert depth == 1 and d == D_MODEL and seq % GRID_W == 0
    t = bsz * seq
    x2d = x.reshape(t, d)
    l = 0

    w_in_l = w_in.reshape(d, PROJ_TOTAL)
    proj = _inproj(x2d, mix_norm_w[l], w_in_l, INPROJ_TM)
    proj_meta = _inproj(meta_tokens.astype(x.dtype), mix_norm_w[l], w_in_l, N_META)

    a = _attention(proj, proj_meta, q_norm_w[l], k_norm_w[l], rel_bias[l], meta_bias[l], bsz, seq)
    c = _short_conv(proj, proj_meta, conv_w[l], bsz, seq)

    w_router = jnp.zeros((d, LANES), F32)
    w_router = w_router.at[:, :N_GROUPS].set(w_router_group[l].astype(F32))
    w_router = w_router.at[:, N_GROUPS:N_GROUPS + N_EXPERTS].set(w_router_expert[l].astype(F32))
    b_router = jnp.zeros((1, LANES), F32)
    b_router = b_router.at[0, :N_GROUPS].set(b_router_group[l].astype(F32))
    b_router = b_router.at[0, N_GROUPS:N_GROUPS + N_EXPERTS].set(b_router_expert[l].astype(F32))
    h1, hn, logits = _outproj(a, c, x2d, attn_out_norm_w[l], conv_out_norm_w[l], w_out[l].astype(BF16),
                              ffn_norm_w[l], w_router.astype(BF16), b_router)

    idx, wts, cnt = _route(logits)
    (codes, seg_start, seg_fill, sb_expert, sb_start, sb_rows, sb_real, tail,
     n_rows) = _dispatch_tables(idx, cnt, t)
    y_buf, dest = _experts(hn, w_gate.reshape(N_EXPERTS, d, EXPERT_FF), w_up.reshape(N_EXPERTS, d, EXPERT_FF),
                           w_down.reshape(N_EXPERTS, EXPERT_FF, d), sb_expert, sb_start, sb_rows, sb_real,
                           codes, seg_start, seg_fill, tail, n_rows)
    out = _combine(h1, wts, y_buf, dest)
    return out.reshape(bsz, seq, d)
```

```python
import functools

import jax
import jax.numpy as jnp
from jax import lax
from jax.experimental import pallas as pl
from jax.experimental.pallas import tpu as pltpu

F32 = jnp.float32
BF16 = jnp.bfloat16

D_MODEL = 2048
N_META = 16
GRID_W = 64
N_HEADS = 16
HEAD_DIM = 64
ATTN_WIDTH = N_HEADS * HEAD_DIM
CONV_WIDTH = D_MODEL - ATTN_WIDTH
PROJ_TOTAL = 3 * ATTN_WIDTH + 3 * CONV_WIDTH
WIN_ROWS = 8
WIN_COLS = 16
N_GROUPS = 4
EXPERTS_PER_GROUP = 8
N_EXPERTS = N_GROUPS * EXPERTS_PER_GROUP
TOP_K = 2
EXPERT_FF = 1024
EPS = 1e-6

LANES = 128
VMEM_LIMIT = 52 * 1024 * 1024

ATTN_ROWS_PER_STEP = 4
INPROJ_TM = 1024
INPROJ_TN = 1024
OUTPROJ_TM = 512
OUTPROJ_SUB = 256
ROUTE_TM = 512
EXPERT_ROW_BLOCK = 128
EXPERT_BLOCKS = (512, 256, 128)
EXPERT_CAP = 1024
EXPERT_FF_CHUNK = 256
GATHER_UNROLL = 8
RANK_BITS = 15
RANK_SPAN = 1 << RANK_BITS
COMBINE_TM = 256


def _params(*sem):
    return pltpu.CompilerParams(dimension_semantics=sem, vmem_limit_bytes=VMEM_LIMIT)


def _inproj_body(x_ref, nw_ref, w_ref, o_ref, xn_ref):
    @pl.when(pl.program_id(1) == 0)
    def _():
        x = x_ref[...]
        ms = jnp.mean(x * x, axis=-1, keepdims=True)
        xn_ref[...] = (x * lax.rsqrt(ms + EPS) * nw_ref[...]).astype(BF16)

    o_ref[...] = jnp.dot(xn_ref[...], w_ref[...], preferred_element_type=F32)


def _inproj(x2d, norm_w, w_bf16, tm):
    m = x2d.shape[0]
    tn = INPROJ_TN
    return pl.pallas_call(
        _inproj_body,
        grid=(m // tm, PROJ_TOTAL // tn),
        in_specs=[pl.BlockSpec((tm, D_MODEL), lambda i, j: (i, 0)),
                  pl.BlockSpec((1, D_MODEL), lambda i, j: (0, 0)),
                  pl.BlockSpec((D_MODEL, tn), lambda i, j: (0, j))],
        out_specs=pl.BlockSpec((tm, tn), lambda i, j: (i, j)),
        out_shape=jax.ShapeDtypeStruct((m, PROJ_TOTAL), F32),
        scratch_shapes=[pltpu.VMEM((tm, D_MODEL), BF16)],
        compiler_params=_params("arbitrary", "arbitrary"),
        name="inproj",
    )(x2d, norm_w.reshape(1, D_MODEL), w_bf16)


def _head_norm(x, w, lo):
    x2 = x * x
    s_lo = jnp.sum(jnp.where(lo, x2, 0.0), axis=-1, keepdims=True)
    s_hi = jnp.sum(jnp.where(lo, 0.0, x2), axis=-1, keepdims=True)
    ms = jnp.where(lo, s_lo, s_hi) * (1.0 / HEAD_DIM)
    return x * lax.rsqrt(ms + EPS) * w


def _attn_body(q_ref, k_ref, v_ref, km_ref, vm_ref, qw_ref, kw_ref, bias_ref, mb_ref, o_ref,
               qs, ks, vs, sc, smc, pr, pmr, *, rows):
    lo = lax.broadcasted_iota(jnp.int32, (1, LANES), 1) < HEAD_DIM
    scale = HEAD_DIM ** -0.5
    chunk = 256
    seq = rows * GRID_W

    kmb = _head_norm(km_ref[...], kw_ref[...], lo).astype(BF16)
    vmb = vm_ref[...].astype(BF16)
    contract_last = (((1,), (1,)), ((), ()))
    head_masks = (lo, jnp.logical_not(lo))

    def one_head(x, h):
        return jnp.where(head_masks[h], x, jnp.zeros_like(x))

    def prep(i, _):
        sl = pl.ds(pl.multiple_of(i * chunk, chunk), chunk)
        qs[sl, :] = (_head_norm(q_ref[sl, :], qw_ref[...], lo) * scale).astype(BF16)
        ks[sl, :] = _head_norm(k_ref[sl, :], kw_ref[...], lo).astype(BF16)
        vs[sl, :] = v_ref[sl, :].astype(BF16)
        return 0

    lax.fori_loop(0, seq // chunk, prep, 0, unroll=2)
    wr = min(WIN_ROWS, rows)
    nk = wr * GRID_W

    def window_start(r):
        return jnp.clip(r - wr // 2, 0, rows - wr)

    def row_slice(r, n):
        return pl.ds(pl.multiple_of(r * GRID_W, GRID_W), n)

    def store_scores(r, s_ref, sm_ref):
        rs = window_start(r)
        si = rs - r + (WIN_ROWS - 1)
        q_r = qs[row_slice(r, GRID_W), :]
        kwin = ks[row_slice(rs, nk), :]
        for h in range(2):
            qh = one_head(q_r, h)
            bias = jnp.concatenate([bias_ref[h, si + w] for w in range(0, wr, 2)], axis=-1)
            s_ref[h] = lax.dot_general(qh, kwin, contract_last, preferred_element_type=F32) + bias
            sm_ref[h] = lax.dot_general(qh, kmb, contract_last, preferred_element_type=F32) + mb_ref[h]

    def store_softmax(s_ref, sm_ref, p_ref, pm_ref):
        for h in range(2):
            s = s_ref[h]
            sm = sm_ref[h]
            m = jnp.maximum(jnp.max(s, axis=-1, keepdims=True), jnp.max(sm, axis=-1, keepdims=True))
            p = jnp.exp(s - m)
            pm = jnp.exp(sm - m)
            inv = 1.0 / (jnp.sum(p, axis=-1, keepdims=True) + jnp.sum(pm, axis=-1, keepdims=True))
            p_ref[h] = (p * inv).astype(BF16)
            pm_ref[h] = (pm * inv).astype(BF16)

    def weighted_values(r, p_ref, pm_ref):
        vwin = vs[row_slice(window_start(r), nk), :]
        outs = [jnp.dot(p_ref[h], vwin, preferred_element_type=F32)
                + jnp.dot(pm_ref[h], vmb, preferred_element_type=F32) for h in range(2)]
        o_ref[row_slice(r, GRID_W), :] = jnp.where(lo, outs[0], outs[1])

    per = ATTN_ROWS_PER_STEP
    groups = rows // per

    def step(j, parity, do_scores, do_softmax, do_values):
        if do_scores:
            for k in range(per):
                store_scores(per * j + k, sc.at[parity, k], smc.at[parity, k])
        if do_values:
            for k in range(per):
                weighted_values(per * (j - 2) + k, pr.at[1 - parity, k], pmr.at[1 - parity, k])
        if do_softmax:
            for k in range(per):
                store_softmax(sc.at[1 - parity, k], smc.at[1 - parity, k], pr.at[parity, k], pmr.at[parity, k])

    step(0, 0, True, False, False)
    step(1, 1, True, True, False)

    def two_steps(i, _):
        step(2 * i, 0, True, True, True)
        step(2 * i + 1, 1, True, True, True)
        return 0

    lax.fori_loop(1, groups // 2, two_steps, 0)
    step(groups, 0, False, True, True)
    step(groups + 1, 1, False, False, True)


def _bias_table(rel_bias):
    c = jnp.arange(GRID_W)
    col_start = jnp.clip(c - WIN_COLS // 2, 0, GRID_W - WIN_COLS)
    col_mask = (c[None, :] >= col_start[:, None]) & (c[None, :] < col_start[:, None] + WIN_COLS)
    dc = jnp.clip(c[None, :] - c[:, None], -(WIN_COLS - 1), WIN_COLS - 1) + (WIN_COLS - 1)
    onehot = (dc[None] == jnp.arange(2 * WIN_COLS - 1)[:, None, None]).astype(F32)
    toep = jnp.einsum('hdc,cqk->hdqk', rel_bias.astype(F32), onehot, precision=lax.Precision.HIGHEST)
    toep = jnp.where(col_mask[None, None], toep, -jnp.inf)
    return jnp.concatenate([toep[:, :-1], toep[:, 1:]], axis=-1)


def _attention(proj, proj_meta, q_norm_w, k_norm_w, rel_bias, meta_bias, bsz, seq):
    rows = seq // GRID_W
    assert rows >= WIN_ROWS and rows % (2 * ATTN_ROWS_PER_STEP) == 0
    nk = WIN_ROWS * GRID_W
    npairs = N_HEADS // 2
    qw = jnp.tile(q_norm_w.astype(F32), 2).reshape(1, LANES)
    kw = jnp.tile(k_norm_w.astype(F32), 2).reshape(1, LANES)
    bias = _bias_table(rel_bias)
    mb = meta_bias.astype(F32).reshape(N_HEADS, 1, N_META)
    return pl.pallas_call(
        functools.partial(_attn_body, rows=rows),
        grid=(bsz, npairs),
        in_specs=[pl.BlockSpec((seq, LANES), lambda b, p: (b, p)),
                  pl.BlockSpec((seq, LANES), lambda b, p: (b, npairs + p)),
                  pl.BlockSpec((seq, LANES), lambda b, p: (b, 2 * npairs + p)),
                  pl.BlockSpec((N_META, LANES), lambda b, p: (0, npairs + p)),
                  pl.BlockSpec((N_META, LANES), lambda b, p: (0, 2 * npairs + p)),
                  pl.BlockSpec((1, LANES), lambda b, p: (0, 0)),
                  pl.BlockSpec((1, LANES), lambda b, p: (0, 0)),
                  pl.BlockSpec((2, 2 * WIN_ROWS - 2, GRID_W, LANES), lambda b, p: (p, 0, 0, 0)),
                  pl.BlockSpec((2, 1, N_META), lambda b, p: (p, 0, 0))],
        out_specs=pl.BlockSpec((seq, LANES), lambda b, p: (b, p)),
        out_shape=jax.ShapeDtypeStruct((bsz * seq, ATTN_WIDTH), F32),
        scratch_shapes=([pltpu.VMEM((seq, LANES), BF16)] * 3
                        + [pltpu.VMEM((2, ATTN_ROWS_PER_STEP, 2, GRID_W, nk), F32),
                           pltpu.VMEM((2, ATTN_ROWS_PER_STEP, 2, GRID_W, N_META), F32),
                           pltpu.VMEM((2, ATTN_ROWS_PER_STEP, 2, GRID_W, nk), BF16),
                           pltpu.VMEM((2, ATTN_ROWS_PER_STEP, 2, GRID_W, N_META), BF16)]),
        compiler_params=_params("arbitrary", "arbitrary"),
        name="attention",
    )(proj, proj, proj, proj_meta, proj_meta, qw, kw, bias, mb)


def _conv_body(gb_ref, gc_ref, hc_ref, gcm_ref, hcm_ref, w_ref, o_ref):
    seq = gb_ref.shape[0]
    u = gc_ref[...] * hc_ref[...]
    u_meta_last = gcm_ref[N_META - 1:N_META, :] * hcm_ref[N_META - 1:N_META, :]
    row = lax.broadcasted_iota(jnp.int32, (seq, 1), 0)
    u_prev = jnp.where(row == 0, u_meta_last, pltpu.roll(u, 1, 0))
    u_next = jnp.where(row == seq - 1, 0.0, pltpu.roll(u, seq - 1, 0))
    w = w_ref[...]
    y = u_prev * w[0:1] + u * w[1:2] + u_next * w[2:3]
    o_ref[...] = gb_ref[...] * y


def _short_conv(proj, proj_meta, conv_w, bsz, seq):
    nct = CONV_WIDTH // LANES
    base = 3 * ATTN_WIDTH // LANES
    return pl.pallas_call(
        _conv_body,
        grid=(bsz, nct),
        in_specs=[pl.BlockSpec((seq, LANES), lambda b, c: (b, base + c)),
                  pl.BlockSpec((seq, LANES), lambda b, c: (b, base + nct + c)),
                  pl.BlockSpec((seq, LANES), lambda b, c: (b, base + 2 * nct + c)),
                  pl.BlockSpec((N_META, LANES), lambda b, c: (0, base + nct + c)),
                  pl.BlockSpec((N_META, LANES), lambda b, c: (0, base + 2 * nct + c)),
                  pl.BlockSpec((3, LANES), lambda b, c: (0, c))],
        out_specs=pl.BlockSpec((seq, LANES), lambda b, c: (b, c)),
        out_shape=jax.ShapeDtypeStruct((bsz * seq, CONV_WIDTH), F32),
        compiler_params=_params("arbitrary", "arbitrary"),
        name="short_conv",
    )(proj, proj, proj, proj_meta, proj_meta, conv_w.astype(F32))


def _rms(x, w):
    ms = jnp.mean(x * x, axis=-1, keepdims=True)
    return x * lax.rsqrt(ms + EPS) * w


def _pack_bf16_pairs(x):
    w = x.shape[1] // 2
    lo = lax.bitcast_convert_type(x[:, :w].astype(BF16).astype(F32), jnp.uint32)
    hi = lax.bitcast_convert_type(x[:, w:].astype(BF16).astype(F32), jnp.uint32)
    return (hi & jnp.uint32(0xFFFF0000)) | (lo >> 16)


def _unpack_bf16_pairs(p):
    lo = lax.bitcast_convert_type(p << 16, F32).astype(BF16)
    hi = lax.bitcast_convert_type(p & jnp.uint32(0xFFFF0000), F32).astype(BF16)
    return lo, hi


def _outproj_body(a_ref, c_ref, x_ref, aw_ref, cw_ref, wo_ref, fw_ref, wr_ref, br_ref,
                  h1_ref, hn_ref, lg_ref):
    sub = OUTPROJ_SUB
    blocks = [pl.ds(k * sub, sub) for k in range(a_ref.shape[0] // sub)]

    def mix(rows):
        an = _rms(a_ref[rows, :], aw_ref[...]).astype(BF16)
        cn = _rms(c_ref[rows, :], cw_ref[...]).astype(BF16)
        return (jnp.dot(an, wo_ref[0:ATTN_WIDTH, :], preferred_element_type=F32)
                + jnp.dot(cn, wo_ref[ATTN_WIDTH:D_MODEL, :], preferred_element_type=F32))

    def finish(rows, mixed):
        h1 = x_ref[rows, :] + mixed
        h1_ref[rows, :] = h1
        hn = _rms(h1, fw_ref[...])
        hn_ref[rows, :] = _pack_bf16_pairs(hn)
        lg_ref[rows, :] = jnp.dot(hn.astype(BF16), wr_ref[...], preferred_element_type=F32) + br_ref[...]

    mixed = mix(blocks[0])
    for k, rows in enumerate(blocks):
        following = mix(blocks[k + 1]) if k + 1 < len(blocks) else None
        finish(rows, mixed)
        mixed = following


def _outproj(a, c, x2d, aw, cw, wo_bf16, fw, w_router, b_router):
    t = x2d.shape[0]
    tm = OUTPROJ_TM
    row = lambda i: (i, 0)
    fixed = lambda i: (0, 0)
    return pl.pallas_call(
        _outproj_body,
        grid=(t // tm,),
        in_specs=[pl.BlockSpec((tm, ATTN_WIDTH), row),
                  pl.BlockSpec((tm, CONV_WIDTH), row),
                  pl.BlockSpec((tm, D_MODEL), row),
                  pl.BlockSpec((1, ATTN_WIDTH), fixed),
                  pl.BlockSpec((1, CONV_WIDTH), fixed),
                  pl.BlockSpec((D_MODEL, D_MODEL), fixed, pipeline_mode=pl.Buffered(1)),
                  pl.BlockSpec((1, D_MODEL), fixed),
                  pl.BlockSpec((D_MODEL, LANES), fixed),
                  pl.BlockSpec((1, LANES), fixed)],
        out_specs=[pl.BlockSpec((tm, D_MODEL), row),
                   pl.BlockSpec((tm, D_MODEL // 2), row),
                   pl.BlockSpec((tm, LANES), row)],
        out_shape=[jax.ShapeDtypeStruct((t, D_MODEL), F32),
                   jax.ShapeDtypeStruct((t, D_MODEL // 2), jnp.uint32),
                   jax.ShapeDtypeStruct((t, LANES), F32)],
        compiler_params=_params("arbitrary"),
        name="outproj",
    )(a, c, x2d, aw.reshape(1, -1), cw.reshape(1, -1), wo_bf16, fw.reshape(1, -1), w_router, b_router)


def _route_body(lg_ref, idx_ref, wt_ref, cnt_ref, run_ref):
    @pl.when(pl.program_id(0) == 0)
    def _():
        run_ref[...] = jnp.zeros_like(run_ref)

    logits = lg_ref[...]
    tm = logits.shape[0]
    lane = lax.broadcasted_iota(jnp.int32, (tm, LANES), 1)
    neg = -jnp.inf

    def first_argmax(v):
        m = jnp.max(v, axis=-1, keepdims=True)
        first = jnp.min(jnp.where(v == m, lane.astype(F32), float(LANES)), axis=-1, keepdims=True)
        return m, first.astype(jnp.int32)

    gl = jnp.where(lane < N_GROUPS, logits, neg)
    gmax, gidx = first_argmax(gl)
    g_w = 1.0 / jnp.sum(jnp.exp(gl - gmax), axis=-1, keepdims=True)
    first = N_GROUPS + gidx * EXPERTS_PER_GROUP
    el = jnp.where((lane >= first) & (lane < first + EXPERTS_PER_GROUP), logits, neg)
    m0, j0 = first_argmax(el)
    m1, j1 = first_argmax(jnp.where(lane == j0, neg, el))
    p1 = jnp.exp(m1 - m0)
    w0 = g_w / (1.0 + p1)
    w1 = g_w * p1 / (1.0 + p1)
    e0 = j0 - N_GROUPS
    e1 = j1 - N_GROUPS

    onehot = ((lane == e0) | (lane == e1)).astype(BF16)
    tri = (lax.broadcasted_iota(jnp.int32, (tm, tm), 0)
           > lax.broadcasted_iota(jnp.int32, (tm, tm), 1)).astype(BF16)
    before = jnp.dot(tri, onehot, preferred_element_type=F32) + run_ref[0:1, :]
    r0 = jnp.sum(jnp.where(lane == e0, before, 0.0), axis=-1, keepdims=True).astype(jnp.int32)
    r1 = jnp.sum(jnp.where(lane == e1, before, 0.0), axis=-1, keepdims=True).astype(jnp.int32)
    run = run_ref[0:1, :] + jnp.sum(onehot.astype(F32), axis=0, keepdims=True)
    run_ref[...] = jnp.broadcast_to(run, run_ref.shape)
    cnt_ref[...] = jnp.broadcast_to(run, cnt_ref.shape)

    zero = jnp.zeros_like(lane)
    idx_ref[...] = jnp.where(lane == 0, e0 * RANK_SPAN + r0, jnp.where(lane == 1, e1 * RANK_SPAN + r1, zero))
    wt_ref[...] = jnp.where(lane == 0, w0, jnp.where(lane == 1, w1, 0.0))


def _route(logits):
    t = logits.shape[0]
    tm = ROUTE_TM
    return pl.pallas_call(
        _route_body,
        grid=(t // tm,),
        in_specs=[pl.BlockSpec((tm, LANES), lambda i: (i, 0))],
        out_specs=[pl.BlockSpec((tm, LANES), lambda i: (i, 0)),
                   pl.BlockSpec((tm, LANES), lambda i: (i, 0)),
                   pl.BlockSpec((8, LANES), lambda i: (0, 0))],
        out_shape=[jax.ShapeDtypeStruct((t, LANES), jnp.int32),
                   jax.ShapeDtypeStruct((t, LANES), F32),
                   jax.ShapeDtypeStruct((8, LANES), F32)],
        scratch_shapes=[pltpu.VMEM((8, LANES), F32)],
        compiler_params=_params("arbitrary"),
        name="route",
    )(logits)


def _expert_body(sbe, sbs, sbn, sbr, codes, seg_start, seg_fill, tail, hn_hbm, wg_ref, wu_ref, wd_ref,
                 y_hbm, dest, x32, xb, acc, tok, gsem, osem):
    s = pl.program_id(0)
    c = pl.program_id(1)
    nsb = pl.num_programs(0)
    nch = EXPERT_FF // EXPERT_FF_CHUNK
    rb = EXPERT_ROW_BLOCK
    half = D_MODEL // 2
    n = sbn[s]
    start = sbs[s]
    nblk = n // rb
    slot = s % 2

    def for_each(count, fn):
        def body(i, _):
            fn(i)
            return 0
        lax.fori_loop(0, count, body, 0)

    def block_rows(j):
        return pl.ds(pl.multiple_of(j * rb, rb), rb)

    def gather_row(sb_first, buf, i):
        src = hn_hbm.at[pl.ds(tok[sb_first + i], 1)]
        pltpu.make_async_copy(src, x32.at[buf, pl.ds(i, 1)], gsem.at[buf]).start()

    def gather_rows(sb_first, buf, first_row, count):
        def group(g):
            for k in range(GATHER_UNROLL):
                gather_row(sb_first, buf, first_row + g * GATHER_UNROLL + k)
        for_each(count // GATHER_UNROLL, group)

    def wait_gathered(buf, count):
        rows = pl.ds(0, pl.multiple_of(count, GATHER_UNROLL))
        pltpu.make_async_copy(hn_hbm.at[rows], x32.at[buf, rows], gsem.at[buf]).wait()

    def out_copy(first_row, j):
        dst = pl.ds(pl.multiple_of(first_row + j * rb, rb), rb)
        return pltpu.make_async_copy(acc.at[block_rows(j)], y_hbm.at[dst], osem)

    @pl.when((s == 0) & (c == 0))
    def _():
        def clear(j):
            for buf in range(2):
                x32[buf, block_rows(j), :] = jnp.zeros((rb, half), jnp.uint32)
            acc[block_rows(j), :] = jnp.zeros((rb, D_MODEL), F32)
        for_each(EXPERT_CAP // rb, clear)

        def pad_rows(e):
            for k in range(GATHER_UNROLL - 1):
                tok[jnp.minimum(seg_fill[e] + k, tok.shape[0] - 1)] = 0
        for_each(N_EXPERTS, pad_rows)

        def sort_rows(g):
            for k in range(GATHER_UNROLL):
                a = g * GATHER_UNROLL + k
                code = codes[a]
                row = seg_start[code >> RANK_BITS] + (code & (RANK_SPAN - 1))
                dest[a] = row
                tok[row] = g * (GATHER_UNROLL // TOP_K) + k // TOP_K
        for_each(codes.shape[0] // GATHER_UNROLL, sort_rows)

        gather_rows(sbs[0], 0, 0, sbr[0])

    prev = jnp.maximum(s - 1, 0)
    nxt = jnp.minimum(s + 1, nsb - 1)
    prev_rows = jnp.where(s > 0, sbn[prev], 0)
    prev_pending = (c == 0) & (prev_rows > 0)

    @pl.when((c == 0) & (sbr[s] > 0))
    def _():
        wait_gathered(slot, sbr[s])

    def wait_prev_output():
        for_each(prev_rows // rb, lambda j: out_copy(sbs[prev], j).wait())

    @pl.when(prev_pending & (n == 0))
    def _():
        wait_prev_output()

    @pl.when((s == nsb - 1) & (c == 0))
    def _():
        first = tail[0]
        nfill = (y_hbm.shape[0] - first) // rb
        acc[0:rb, :] = jnp.zeros((rb, D_MODEL), F32)

        def fill_copy(j):
            dst = pl.ds(pl.multiple_of(first + j * rb, rb), rb)
            return pltpu.make_async_copy(acc.at[0:rb], y_hbm.at[dst], osem)

        for_each(nfill, lambda j: fill_copy(j).start())
        for_each(nfill, lambda j: fill_copy(j).wait())

    @pl.when(n > 0)
    def _():
        @pl.when(c == 0)
        def _():
            def unpack(j):
                rows = block_rows(j)
                lo, hi = _unpack_bf16_pairs(x32[slot, rows, :])
                xb[rows, 0:half] = lo
                xb[rows, half:D_MODEL] = hi
            for_each(nblk, unpack)

            @pl.when((s + 1 < nsb) & (sbr[nxt] > 0))
            def _():
                gather_rows(sbs[nxt], 1 - slot, 0, sbr[nxt])

            @pl.when(prev_pending)
            def _():
                wait_prev_output()

        def mlp(first_row, m):
            rows = pl.ds(pl.multiple_of(first_row, rb), m)
            x = xb[rows, :]
            g = jnp.dot(x, wg_ref[0].astype(BF16), preferred_element_type=F32)
            u = jnp.dot(x, wu_ref[0].astype(BF16), preferred_element_type=F32)
            h = (jax.nn.silu(g) * u).astype(BF16)
            y = jnp.dot(h, wd_ref[0].astype(BF16), preferred_element_type=F32)
            acc[rows, :] = jnp.where(c == 0, y, acc[rows, :] + y)

        done = 0
        for m in EXPERT_BLOCKS[:-1]:
            count = (n - done) // m
            for_each(count, lambda j, done=done, m=m: mlp(done + j * m, m))
            done = done + count * m

        @pl.when(done < n)
        def _():
            mlp(done, rb)

        @pl.when(c == nch - 1)
        def _():
            for_each(nblk, lambda j: out_copy(start, j).start())


def _experts(hn_packed, w_gate, w_up, w_down, sb_expert, sb_start, sb_rows, sb_real, codes, seg_start,
             seg_fill, tail, n_rows):
    n_sb = sb_expert.shape[0]
    nch = EXPERT_FF // EXPERT_FF_CHUNK
    fc = EXPERT_FF_CHUNK

    def chunk(s, c, sbn):
        return jnp.where(sbn[s] > 0, c, nch - 1)

    def up_map(s, c, sbe, sbs, sbn, *_):
        return (sbe[s], 0, chunk(s, c, sbn))

    def down_map(s, c, sbe, sbs, sbn, *_):
        return (sbe[s], chunk(s, c, sbn), 0)

    grid_spec = pltpu.PrefetchScalarGridSpec(
        num_scalar_prefetch=8,
        grid=(n_sb, nch),
        in_specs=[pl.BlockSpec(memory_space=pl.ANY),
                  pl.BlockSpec((1, D_MODEL, fc), up_map),
                  pl.BlockSpec((1, D_MODEL, fc), up_map),
                  pl.BlockSpec((1, fc, D_MODEL), down_map)],
        out_specs=[pl.BlockSpec(memory_space=pl.ANY),
                   pl.BlockSpec(memory_space=pltpu.SMEM)],
        scratch_shapes=[pltpu.VMEM((2, EXPERT_CAP, D_MODEL // 2), jnp.uint32),
                        pltpu.VMEM((EXPERT_CAP, D_MODEL), BF16),
                        pltpu.VMEM((EXPERT_CAP, D_MODEL), F32),
                        pltpu.SMEM((n_rows,), jnp.int32),
                        pltpu.SemaphoreType.DMA((2,)),
                        pltpu.SemaphoreType.DMA(())],
    )
    return pl.pallas_call(
        _expert_body,
        grid_spec=grid_spec,
        out_shape=[jax.ShapeDtypeStruct((n_rows, D_MODEL), F32),
                   jax.ShapeDtypeStruct(codes.shape, jnp.int32)],
        compiler_params=_params("arbitrary", "arbitrary"),
        name="experts",
    )(sb_expert, sb_start, sb_rows, sb_real, codes, seg_start, seg_fill, tail, hn_packed, w_gate, w_up, w_down)


def _combine_body(dest, h1_ref, wt_ref, y_hbm, o_ref, g, sem):
    i = pl.program_id(0)
    nsteps = pl.num_programs(0)
    tm = h1_ref.shape[0]

    def issue(step, slot):
        def f(q, _):
            for u in range(GATHER_UNROLL):
                r = q * GATHER_UNROLL + u
                for k in range(TOP_K):
                    src = y_hbm.at[pl.ds(dest[(step * tm + r) * TOP_K + k], 1)]
                    pltpu.make_async_copy(src, g.at[slot, k, pl.ds(r, 1)], sem.at[slot]).start()
            return 0
        lax.fori_loop(0, tm // GATHER_UNROLL, f, 0)

    @pl.when(i == 0)
    def _():
        issue(0, 0)

    @pl.when(i + 1 < nsteps)
    def _():
        issue(i + 1, (i + 1) % 2)

    slot = i % 2
    for k in range(TOP_K):
        pltpu.make_async_copy(y_hbm.at[pl.ds(0, tm)], g.at[slot, k], sem.at[slot]).wait()

    w = wt_ref[...]
    o_ref[...] = h1_ref[...] + (w[:, 0:1] * g[slot, 0] + w[:, 1:2] * g[slot, 1])


def _combine(h1, wts, y_buf, dest_flat):
    t = h1.shape[0]
    tm = COMBINE_TM
    grid_spec = pltpu.PrefetchScalarGridSpec(
        num_scalar_prefetch=1,
        grid=(t // tm,),
        in_specs=[pl.BlockSpec((tm, D_MODEL), lambda i, d: (i, 0)),
                  pl.BlockSpec((tm, LANES), lambda i, d: (i, 0)),
                  pl.BlockSpec(memory_space=pl.ANY)],
        out_specs=pl.BlockSpec((tm, D_MODEL), lambda i, d: (i, 0)),
        scratch_shapes=[pltpu.VMEM((2, TOP_K, tm, D_MODEL), F32),
                        pltpu.SemaphoreType.DMA((2,))],
    )
    return pl.pallas_call(
        _combine_body,
        grid_spec=grid_spec,
        out_shape=jax.ShapeDtypeStruct((t, D_MODEL), F32),
        compiler_params=_params("arbitrary"),
        name="combine",
    )(dest_flat, h1, wts, y_buf)


def _dispatch_tables(idx, cnt, t):
    rb, cap = EXPERT_ROW_BLOCK, EXPERT_CAP
    n_assign = t * TOP_K
    n_rows = -(-(n_assign + N_EXPERTS * (rb - 1)) // rb) * rb
    n_sb = (n_rows + N_EXPERTS * (cap - rb)) // cap
    assert n_assign <= RANK_SPAN
    codes = idx[:, 0:TOP_K].reshape(-1)
    counts = cnt[0, :N_EXPERTS].astype(jnp.int32)
    padded = (counts + rb - 1) // rb * rb
    seg_end = jnp.cumsum(padded)
    seg_start = (seg_end - padded).astype(jnp.int32)
    seg_fill = (seg_start + counts).astype(jnp.int32)

    sb_per_expert = (padded + cap - 1) // cap
    sb_end = jnp.cumsum(sb_per_expert)
    total = sb_end[-1]
    s = jnp.arange(n_sb, dtype=jnp.int32)
    s_eff = jnp.minimum(s, total - 1)
    e = jnp.minimum(jnp.sum(sb_end[None, :] <= s_eff[:, None], axis=1), N_EXPERTS - 1).astype(jnp.int32)
    local = s_eff - (sb_end[e] - sb_per_expert[e])
    sb_start = (seg_start[e] + local * cap).astype(jnp.int32)
    sb_rows = jnp.where(s < total, jnp.clip(padded[e] - local * cap, 0, cap), 0).astype(jnp.int32)
    gathered = (counts + GATHER_UNROLL - 1) // GATHER_UNROLL * GATHER_UNROLL
    sb_real = jnp.where(s < total, jnp.clip(gathered[e] - local * cap, 0, cap), 0).astype(jnp.int32)
    tail = seg_end[-1:].astype(jnp.int32)
    return codes, seg_start, seg_fill, e, sb_start, sb_rows, sb_real, tail, n_rows


def kernel(x, meta_tokens, mix_norm_w, w_in, q_norm_w, k_norm_w, rel_bias, meta_bias, conv_w,
           attn_out_norm_w, conv_out_norm_w, w_out, ffn_norm_w, w_router_group, b_router_group,
           w_router_expert, b_router_expert, w_gate, w_up, w_down):
    bsz, seq, d = x.shape
    depth = mix_norm_w.shape[0]
    assert depth == 1 and d == D_MODEL and seq % GRID_W == 0
    t = bsz * seq
    x2d = x.reshape(t, d)
    l = 0

    w_in_b = w_in[l].astype(BF16)
    proj = _inproj(x2d, mix_norm_w[l], w_in_b, INPROJ_TM)
    proj_meta = _inproj(meta_tokens.astype(x.dtype), mix_norm_w[l], w_in_b, N_META)

    a = _attention(proj, proj_meta, q_norm_w[l], k_norm_w[l], rel_bias[l], meta_bias[l], bsz, seq)
    c = _short_conv(proj, proj_meta, conv_w[l], bsz, seq)

    spare = LANES - N_GROUPS - N_EXPERTS
    w_router = jnp.concatenate([w_router_group[l].astype(F32), w_router_expert[l].astype(F32),
                                jnp.zeros((d, spare), F32)], axis=1)
    b_router = jnp.concatenate([b_router_group[l].astype(F32), b_router_expert[l].astype(F32),
                                jnp.zeros((spare,), F32)]).reshape(1, LANES)
    h1, hn, logits = _outproj(a, c, x2d, attn_out_norm_w[l], conv_out_norm_w[l], w_out[l].astype(BF16),
                              ffn_norm_w[l], w_router.astype(BF16), b_router)

    idx, wts, cnt = _route(logits)
    (codes, seg_start, seg_fill, sb_expert, sb_start, sb_rows, sb_real, tail,
     n_rows) = _dispatch_tables(idx, cnt, t)
    y_buf, dest = _experts(hn, w_gate.reshape(N_EXPERTS, d, EXPERT_FF), w_up.reshape(N_EXPERTS, d, EXPERT_FF),
                           w_down.reshape(N_EXPERTS, EXPERT_FF, d), sb_expert, sb_start, sb_rows, sb_real,
                           codes, seg_start, seg_fill, tail, n_rows)
    out = _combine(h1, wts, y_buf, dest)
    return out.reshape(bsz, seq, d)
```

```python
import functools

import jax
import jax.numpy as jnp
from jax import lax
from jax.experimental import pallas as pl
from jax.experimental.pallas import tpu as pltpu

F32 = jnp.float32
BF16 = jnp.bfloat16

D_MODEL = 2048
N_META = 16
GRID_W = 64
N_HEADS = 16
HEAD_DIM = 64
ATTN_WIDTH = N_HEADS * HEAD_DIM
CONV_WIDTH = D_MODEL - ATTN_WIDTH
PROJ_TOTAL = 3 * ATTN_WIDTH + 3 * CONV_WIDTH
WIN_ROWS = 8
WIN_COLS = 16
N_GROUPS = 4
EXPERTS_PER_GROUP = 8
N_EXPERTS = N_GROUPS * EXPERTS_PER_GROUP
TOP_K = 2
EXPERT_FF = 1024
EPS = 1e-6

LANES = 128
VMEM_LIMIT = 52 * 1024 * 1024

ATTN_ROWS_PER_STEP = 4
INPROJ_TM = 1024
INPROJ_TN = 1024
OUTPROJ_TM = 512
OUTPROJ_SUB = 256
ROUTE_TM = 512
EXPERT_ROW_BLOCK = 128
EXPERT_BLOCKS = (512, 256, 128)
EXPERT_CAP = 1024
EXPERT_FF_CHUNK = 256
GATHER_UNROLL = 8
COMBINE_TM = 256


def _params(*sem):
    return pltpu.CompilerParams(dimension_semantics=sem, vmem_limit_bytes=VMEM_LIMIT)


def _inproj_body(x_ref, nw_ref, w_ref, o_ref, xn_ref):
    @pl.when(pl.program_id(1) == 0)
    def _():
        x = x_ref[...]
        ms = jnp.mean(x * x, axis=-1, keepdims=True)
        xn_ref[...] = (x * lax.rsqrt(ms + EPS) * nw_ref[...]).astype(BF16)

    o_ref[...] = jnp.dot(xn_ref[...], w_ref[...], preferred_element_type=F32)


def _inproj(x2d, norm_w, w_bf16, tm):
    m = x2d.shape[0]
    tn = INPROJ_TN
    return pl.pallas_call(
        _inproj_body,
        grid=(m // tm, PROJ_TOTAL // tn),
        in_specs=[pl.BlockSpec((tm, D_MODEL), lambda i, j: (i, 0)),
                  pl.BlockSpec((1, D_MODEL), lambda i, j: (0, 0)),
                  pl.BlockSpec((D_MODEL, tn), lambda i, j: (0, j))],
        out_specs=pl.BlockSpec((tm, tn), lambda i, j: (i, j)),
        out_shape=jax.ShapeDtypeStruct((m, PROJ_TOTAL), F32),
        scratch_shapes=[pltpu.VMEM((tm, D_MODEL), BF16)],
        compiler_params=_params("arbitrary", "arbitrary"),
        name="inproj",
    )(x2d, norm_w.reshape(1, D_MODEL), w_bf16)


def _head_norm(x, w, lo):
    x2 = x * x
    s_lo = jnp.sum(jnp.where(lo, x2, 0.0), axis=-1, keepdims=True)
    s_hi = jnp.sum(jnp.where(lo, 0.0, x2), axis=-1, keepdims=True)
    ms = jnp.where(lo, s_lo, s_hi) * (1.0 / HEAD_DIM)
    return x * lax.rsqrt(ms + EPS) * w


def _attn_body(q_ref, k_ref, v_ref, km_ref, vm_ref, qw_ref, kw_ref, bias_ref, mb_ref, o_ref,
               qs, ks, vs, sc, smc, pr, pmr, *, rows):
    lo = lax.broadcasted_iota(jnp.int32, (1, LANES), 1) < HEAD_DIM
    scale = HEAD_DIM ** -0.5
    chunk = 256
    seq = rows * GRID_W

    kmb = _head_norm(km_ref[...], kw_ref[...], lo).astype(BF16)
    vmb = vm_ref[...].astype(BF16)
    contract_last = (((1,), (1,)), ((), ()))
    head_masks = (lo, jnp.logical_not(lo))

    def one_head(x, h):
        return jnp.where(head_masks[h], x, jnp.zeros_like(x))

    def prep(i, _):
        sl = pl.ds(pl.multiple_of(i * chunk, chunk), chunk)
        qs[sl, :] = (_head_norm(q_ref[sl, :], qw_ref[...], lo) * scale).astype(BF16)
        ks[sl, :] = _head_norm(k_ref[sl, :], kw_ref[...], lo).astype(BF16)
        vs[sl, :] = v_ref[sl, :].astype(BF16)
        return 0

    lax.fori_loop(0, seq // chunk, prep, 0, unroll=2)
    wr = min(WIN_ROWS, rows)
    nk = wr * GRID_W

    def window_start(r):
        return jnp.clip(r - wr // 2, 0, rows - wr)

    def row_slice(r, n):
        return pl.ds(pl.multiple_of(r * GRID_W, GRID_W), n)

    def store_scores(r, s_ref, sm_ref):
        rs = window_start(r)
        si = rs - r + (WIN_ROWS - 1)
        q_r = qs[row_slice(r, GRID_W), :]
        kwin = ks[row_slice(rs, nk), :]
        for h in range(2):
            qh = one_head(q_r, h)
            bias = jnp.concatenate([bias_ref[h, si + w] for w in range(0, wr, 2)], axis=-1)
            s_ref[h] = lax.dot_general(qh, kwin, contract_last, preferred_element_type=F32) + bias
            sm_ref[h] = lax.dot_general(qh, kmb, contract_last, preferred_element_type=F32) + mb_ref[h]

    def store_softmax(s_ref, sm_ref, p_ref, pm_ref):
        for h in range(2):
            s = s_ref[h]
            sm = sm_ref[h]
            m = jnp.maximum(jnp.max(s, axis=-1, keepdims=True), jnp.max(sm, axis=-1, keepdims=True))
            p = jnp.exp(s - m)
            pm = jnp.exp(sm - m)
            inv = 1.0 / (jnp.sum(p, axis=-1, keepdims=True) + jnp.sum(pm, axis=-1, keepdims=True))
            p_ref[h] = (p * inv).astype(BF16)
            pm_ref[h] = (pm * inv).astype(BF16)

    def weighted_values(r, p_ref, pm_ref):
        vwin = vs[row_slice(window_start(r), nk), :]
        outs = [jnp.dot(p_ref[h], vwin, preferred_element_type=F32)
                + jnp.dot(pm_ref[h], vmb, preferred_element_type=F32) for h in range(2)]
        o_ref[row_slice(r, GRID_W), :] = jnp.where(lo, outs[0], outs[1])

    per = ATTN_ROWS_PER_STEP
    groups = rows // per

    def step(j, parity, do_scores, do_softmax, do_values):
        if do_scores:
            for k in range(per):
                store_scores(per * j + k, sc.at[parity, k], smc.at[parity, k])
        if do_values:
            for k in range(per):
                weighted_values(per * (j - 2) + k, pr.at[1 - parity, k], pmr.at[1 - parity, k])
        if do_softmax:
            for k in range(per):
                store_softmax(sc.at[1 - parity, k], smc.at[1 - parity, k], pr.at[parity, k], pmr.at[parity, k])

    step(0, 0, True, False, False)
    step(1, 1, True, True, False)

    def two_steps(i, _):
        step(2 * i, 0, True, True, True)
        step(2 * i + 1, 1, True, True, True)
        return 0

    lax.fori_loop(1, groups // 2, two_steps, 0)
    step(groups, 0, False, True, True)
    step(groups + 1, 1, False, False, True)


def _bias_table(rel_bias):
    c = jnp.arange(GRID_W)
    col_start = jnp.clip(c - WIN_COLS // 2, 0, GRID_W - WIN_COLS)
    col_mask = (c[None, :] >= col_start[:, None]) & (c[None, :] < col_start[:, None] + WIN_COLS)
    dc = jnp.clip(c[None, :] - c[:, None], -(WIN_COLS - 1), WIN_COLS - 1) + (WIN_COLS - 1)
    ncol = 2 * WIN_COLS - 1
    pair_bias = jnp.concatenate([rel_bias[:, :-1], rel_bias[:, 1:]], axis=-1).astype(F32)
    onehot = (dc[None] == jnp.arange(ncol)[:, None, None]).astype(F32)
    zeros = jnp.zeros_like(onehot)
    pair_onehot = jnp.concatenate([jnp.concatenate([onehot, zeros], axis=-1),
                                   jnp.concatenate([zeros, onehot], axis=-1)], axis=0)
    table = jnp.einsum('hdm,mqn->hdqn', pair_bias, pair_onehot, precision=lax.Precision.HIGHEST)
    pair_mask = jnp.concatenate([col_mask, col_mask], axis=-1)
    return jnp.where(pair_mask[None, None], table, -jnp.inf)


def _attention(proj, proj_meta, q_norm_w, k_norm_w, rel_bias, meta_bias, bsz, seq):
    rows = seq // GRID_W
    assert rows >= WIN_ROWS and rows % (2 * ATTN_ROWS_PER_STEP) == 0
    nk = WIN_ROWS * GRID_W
    npairs = N_HEADS // 2
    qw = jnp.tile(q_norm_w.astype(F32), 2).reshape(1, LANES)
    kw = jnp.tile(k_norm_w.astype(F32), 2).reshape(1, LANES)
    bias = _bias_table(rel_bias)
    mb = meta_bias.astype(F32).reshape(N_HEADS, 1, N_META)
    return pl.pallas_call(
        functools.partial(_attn_body, rows=rows),
        grid=(bsz, npairs),
        in_specs=[pl.BlockSpec((seq, LANES), lambda b, p: (b, p)),
                  pl.BlockSpec((seq, LANES), lambda b, p: (b, npairs + p)),
                  pl.BlockSpec((seq, LANES), lambda b, p: (b, 2 * npairs + p)),
                  pl.BlockSpec((N_META, LANES), lambda b, p: (0, npairs + p)),
                  pl.BlockSpec((N_META, LANES), lambda b, p: (0, 2 * npairs + p)),
                  pl.BlockSpec((1, LANES), lambda b, p: (0, 0)),
                  pl.BlockSpec((1, LANES), lambda b, p: (0, 0)),
                  pl.BlockSpec((2, 2 * WIN_ROWS - 2, GRID_W, LANES), lambda b, p: (p, 0, 0, 0)),
                  pl.BlockSpec((2, 1, N_META), lambda b, p: (p, 0, 0))],
        out_specs=pl.BlockSpec((seq, LANES), lambda b, p: (b, p)),
        out_shape=jax.ShapeDtypeStruct((bsz * seq, ATTN_WIDTH), F32),
        scratch_shapes=([pltpu.VMEM((seq, LANES), BF16)] * 3
                        + [pltpu.VMEM((2, ATTN_ROWS_PER_STEP, 2, GRID_W, nk), F32),
                           pltpu.VMEM((2, ATTN_ROWS_PER_STEP, 2, GRID_W, N_META), F32),
                           pltpu.VMEM((2, ATTN_ROWS_PER_STEP, 2, GRID_W, nk), BF16),
                           pltpu.VMEM((2, ATTN_ROWS_PER_STEP, 2, GRID_W, N_META), BF16)]),
        compiler_params=_params("arbitrary", "arbitrary"),
        name="attention",
    )(proj, proj, proj, proj_meta, proj_meta, qw, kw, bias, mb)


def _conv_body(gb_ref, gc_ref, hc_ref, gcm_ref, hcm_ref, w_ref, o_ref):
    seq = gb_ref.shape[0]
    u = gc_ref[...] * hc_ref[...]
    u_meta_last = gcm_ref[N_META - 1:N_META, :] * hcm_ref[N_META - 1:N_META, :]
    row = lax.broadcasted_iota(jnp.int32, (seq, 1), 0)
    u_prev = jnp.where(row == 0, u_meta_last, pltpu.roll(u, 1, 0))
    u_next = jnp.where(row == seq - 1, 0.0, pltpu.roll(u, seq - 1, 0))
    w = w_ref[...]
    y = u_prev * w[0:1] + u * w[1:2] + u_next * w[2:3]
    o_ref[...] = gb_ref[...] * y


def _short_conv(proj, proj_meta, conv_w, bsz, seq):
    nct = CONV_WIDTH // LANES
    base = 3 * ATTN_WIDTH // LANES
    return pl.pallas_call(
        _conv_body,
        grid=(bsz, nct),
        in_specs=[pl.BlockSpec((seq, LANES), lambda b, c: (b, base + c)),
                  pl.BlockSpec((seq, LANES), lambda b, c: (b, base + nct + c)),
                  pl.BlockSpec((seq, LANES), lambda b, c: (b, base + 2 * nct + c)),
                  pl.BlockSpec((N_META, LANES), lambda b, c: (0, base + nct + c)),
                  pl.BlockSpec((N_META, LANES), lambda b, c: (0, base + 2 * nct + c)),
                  pl.BlockSpec((3, LANES), lambda b, c: (0, c))],
        out_specs=pl.BlockSpec((seq, LANES), lambda b, c: (b, c)),
        out_shape=jax.ShapeDtypeStruct((bsz * seq, CONV_WIDTH), F32),
        compiler_params=_params("arbitrary", "arbitrary"),
        name="short_conv",
    )(proj, proj, proj, proj_meta, proj_meta, conv_w.astype(F32))


def _rms(x, w):
    ms = jnp.mean(x * x, axis=-1, keepdims=True)
    return x * lax.rsqrt(ms + EPS) * w


def _pack_bf16_pairs(x):
    w = x.shape[1] // 2
    lo = lax.bitcast_convert_type(x[:, :w].astype(BF16).astype(F32), jnp.uint32)
    hi = lax.bitcast_convert_type(x[:, w:].astype(BF16).astype(F32), jnp.uint32)
    return (hi & jnp.uint32(0xFFFF0000)) | (lo >> 16)


def _unpack_bf16_pairs(p):
    lo = lax.bitcast_convert_type(p << 16, F32).astype(BF16)
    hi = lax.bitcast_convert_type(p & jnp.uint32(0xFFFF0000), F32).astype(BF16)
    return lo, hi


def _outproj_body(a_ref, c_ref, x_ref, aw_ref, cw_ref, wo_ref, fw_ref, wr_ref, br_ref,
                  h1_ref, hn_ref, lg_ref):
    sub = OUTPROJ_SUB
    blocks = [pl.ds(k * sub, sub) for k in range(a_ref.shape[0] // sub)]

    def mix(rows):
        an = _rms(a_ref[rows, :], aw_ref[...]).astype(BF16)
        cn = _rms(c_ref[rows, :], cw_ref[...]).astype(BF16)
        return (jnp.dot(an, wo_ref[0:ATTN_WIDTH, :], preferred_element_type=F32)
                + jnp.dot(cn, wo_ref[ATTN_WIDTH:D_MODEL, :], preferred_element_type=F32))

    def finish(rows, mixed):
        h1 = x_ref[rows, :] + mixed
        h1_ref[rows, :] = h1
        hn = _rms(h1, fw_ref[...])
        hn_ref[rows, :] = _pack_bf16_pairs(hn)
        lg_ref[rows, :] = jnp.dot(hn.astype(BF16), wr_ref[...], preferred_element_type=F32) + br_ref[...]

    mixed = mix(blocks[0])
    for k, rows in enumerate(blocks):
        following = mix(blocks[k + 1]) if k + 1 < len(blocks) else None
        finish(rows, mixed)
        mixed = following


def _outproj(a, c, x2d, aw, cw, wo_bf16, fw, w_router, b_router):
    t = x2d.shape[0]
    tm = OUTPROJ_TM
    row = lambda i: (i, 0)
    fixed = lambda i: (0, 0)
    return pl.pallas_call(
        _outproj_body,
        grid=(t // tm,),
        in_specs=[pl.BlockSpec((tm, ATTN_WIDTH), row),
                  pl.BlockSpec((tm, CONV_WIDTH), row),
                  pl.BlockSpec((tm, D_MODEL), row),
                  pl.BlockSpec((1, ATTN_WIDTH), fixed),
                  pl.BlockSpec((1, CONV_WIDTH), fixed),
                  pl.BlockSpec((D_MODEL, D_MODEL), fixed, pipeline_mode=pl.Buffered(1)),
                  pl.BlockSpec((1, D_MODEL), fixed),
                  pl.BlockSpec((D_MODEL, LANES), fixed),
                  pl.BlockSpec((1, LANES), fixed)],
        out_specs=[pl.BlockSpec((tm, D_MODEL), row),
                   pl.BlockSpec((tm, D_MODEL // 2), row),
                   pl.BlockSpec((tm, LANES), row)],
        out_shape=[jax.ShapeDtypeStruct((t, D_MODEL), F32),
                   jax.ShapeDtypeStruct((t, D_MODEL // 2), jnp.uint32),
                   jax.ShapeDtypeStruct((t, LANES), F32)],
        compiler_params=_params("arbitrary"),
        name="outproj",
    )(a, c, x2d, aw.reshape(1, -1), cw.reshape(1, -1), wo_bf16, fw.reshape(1, -1), w_router, b_router)


def _route_body(lg_ref, idx_ref, wt_ref, cnt_ref, run_ref, seg_ref):
    final_pass = pl.program_id(0) == 1
    first_tile = pl.program_id(1) == 0

    @pl.when(first_tile & jnp.logical_not(final_pass))
    def _():
        run_ref[...] = jnp.zeros_like(run_ref)
        seg_ref[...] = jnp.zeros_like(seg_ref)

    @pl.when(first_tile & final_pass)
    def _():
        counts = run_ref[...]
        rb = float(EXPERT_ROW_BLOCK)
        padded = jnp.ceil(counts * (1.0 / rb)) * rb
        lane8 = lax.broadcasted_iota(jnp.int32, counts.shape, 1)
        ends = padded
        shift = 1
        while shift < LANES:
            ends = ends + jnp.where(lane8 >= shift, pltpu.roll(ends, shift, 1), 0.0)
            shift *= 2
        seg_ref[...] = ends - padded
        run_ref[...] = jnp.zeros_like(run_ref)

    logits = lg_ref[...]
    tm = logits.shape[0]
    lane = lax.broadcasted_iota(jnp.int32, (tm, LANES), 1)
    neg = -jnp.inf

    def first_argmax(v):
        m = jnp.max(v, axis=-1, keepdims=True)
        first = jnp.min(jnp.where(v == m, lane.astype(F32), float(LANES)), axis=-1, keepdims=True)
        return m, first.astype(jnp.int32)

    gl = jnp.where(lane < N_GROUPS, logits, neg)
    gmax, gidx = first_argmax(gl)
    g_w = 1.0 / jnp.sum(jnp.exp(gl - gmax), axis=-1, keepdims=True)
    first = N_GROUPS + gidx * EXPERTS_PER_GROUP
    el = jnp.where((lane >= first) & (lane < first + EXPERTS_PER_GROUP), logits, neg)
    m0, j0 = first_argmax(el)
    m1, j1 = first_argmax(jnp.where(lane == j0, neg, el))
    p1 = jnp.exp(m1 - m0)
    w0 = g_w / (1.0 + p1)
    w1 = g_w * p1 / (1.0 + p1)
    e0 = j0 - N_GROUPS
    e1 = j1 - N_GROUPS

    onehot = ((lane == e0) | (lane == e1)).astype(BF16)
    seen = run_ref[0:1, :]
    run = seen + jnp.sum(onehot.astype(F32), axis=0, keepdims=True)
    run_ref[...] = jnp.broadcast_to(run, run_ref.shape)

    @pl.when(final_pass)
    def _():
        tri = (lax.broadcasted_iota(jnp.int32, (tm, tm), 0)
               > lax.broadcasted_iota(jnp.int32, (tm, tm), 1)).astype(BF16)
        place = jnp.dot(tri, onehot, preferred_element_type=F32) + seen + seg_ref[0:1, :]
        d0 = jnp.sum(jnp.where(lane == e0, place, 0.0), axis=-1, keepdims=True).astype(jnp.int32)
        d1 = jnp.sum(jnp.where(lane == e1, place, 0.0), axis=-1, keepdims=True).astype(jnp.int32)
        idx_ref[...] = jnp.where(lane == 0, d0, jnp.where(lane == 1, d1, jnp.zeros_like(lane)))
        wt_ref[...] = jnp.where(lane == 0, w0, jnp.where(lane == 1, w1, 0.0))
        cnt_ref[...] = jnp.broadcast_to(run, cnt_ref.shape)


def _route(logits):
    t = logits.shape[0]
    tm = ROUTE_TM
    return pl.pallas_call(
        _route_body,
        grid=(2, t // tm),
        in_specs=[pl.BlockSpec((tm, LANES), lambda p, i: (i, 0))],
        out_specs=[pl.BlockSpec((tm, LANES), lambda p, i: (i * p, 0)),
                   pl.BlockSpec((tm, LANES), lambda p, i: (i * p, 0)),
                   pl.BlockSpec((8, LANES), lambda p, i: (0, 0))],
        out_shape=[jax.ShapeDtypeStruct((t, LANES), jnp.int32),
                   jax.ShapeDtypeStruct((t, LANES), F32),
                   jax.ShapeDtypeStruct((8, LANES), F32)],
        scratch_shapes=[pltpu.VMEM((8, LANES), F32)] * 2,
        compiler_params=_params("arbitrary", "arbitrary"),
        name="route",
    )(logits)


def _expert_body(sbe, sbs, sbn, sbr, dest, seg_fill, tail, hn_hbm, wg_ref, wu_ref, wd_ref,
                 y_hbm, x32, xb, acc, tok, gsem, osem):
    s = pl.program_id(0)
    c = pl.program_id(1)
    nsb = pl.num_programs(0)
    nch = EXPERT_FF // EXPERT_FF_CHUNK
    rb = EXPERT_ROW_BLOCK
    half = D_MODEL // 2
    n = sbn[s]
    start = sbs[s]
    nblk = n // rb
    slot = s % 2

    def for_each(count, fn):
        def body(i, _):
            fn(i)
            return 0
        lax.fori_loop(0, count, body, 0)

    def block_rows(j):
        return pl.ds(pl.multiple_of(j * rb, rb), rb)

    def gather_row(sb_first, buf, i):
        src = hn_hbm.at[pl.ds(tok[sb_first + i], 1)]
        pltpu.make_async_copy(src, x32.at[buf, pl.ds(i, 1)], gsem.at[buf]).start()

    def gather_rows(sb_first, buf, first_row, count):
        def group(g):
            for k in range(GATHER_UNROLL):
                gather_row(sb_first, buf, first_row + g * GATHER_UNROLL + k)
        for_each(count // GATHER_UNROLL, group)

    def wait_gathered(buf, count):
        rows = pl.ds(0, pl.multiple_of(count, GATHER_UNROLL))
        pltpu.make_async_copy(hn_hbm.at[rows], x32.at[buf, rows], gsem.at[buf]).wait()

    def out_copy(first_row, j):
        dst = pl.ds(pl.multiple_of(first_row + j * rb, rb), rb)
        return pltpu.make_async_copy(acc.at[block_rows(j)], y_hbm.at[dst], osem)

    @pl.when((s == 0) & (c == 0))
    def _():
        def clear(j):
            for buf in range(2):
                x32[buf, block_rows(j), :] = jnp.zeros((rb, half), jnp.uint32)
            acc[block_rows(j), :] = jnp.zeros((rb, D_MODEL), F32)
        for_each(EXPERT_CAP // rb, clear)

        def pad_rows(e):
            for k in range(GATHER_UNROLL - 1):
                tok[jnp.minimum(seg_fill[e] + k, tok.shape[0] - 1)] = 0
        for_each(N_EXPERTS, pad_rows)

        def invert(g):
            for k in range(GATHER_UNROLL):
                tok[dest[g * GATHER_UNROLL + k]] = g * (GATHER_UNROLL // TOP_K) + k // TOP_K
        for_each(dest.shape[0] // GATHER_UNROLL, invert)

        gather_rows(sbs[0], 0, 0, sbr[0])

    prev = jnp.maximum(s - 1, 0)
    nxt = jnp.minimum(s + 1, nsb - 1)
    prev_rows = jnp.where(s > 0, sbn[prev], 0)
    prev_pending = (c == 0) & (prev_rows > 0)

    @pl.when((c == 0) & (sbr[s] > 0))
    def _():
        wait_gathered(slot, sbr[s])

    def wait_prev_output():
        for_each(prev_rows // rb, lambda j: out_copy(sbs[prev], j).wait())

    @pl.when(prev_pending & (n == 0))
    def _():
        wait_prev_output()

    @pl.when((s == nsb - 1) & (c == 0))
    def _():
        first = tail[0]
        nfill = (y_hbm.shape[0] - first) // rb
        acc[0:rb, :] = jnp.zeros((rb, D_MODEL), F32)

        def fill_copy(j):
            dst = pl.ds(pl.multiple_of(first + j * rb, rb), rb)
            return pltpu.make_async_copy(acc.at[0:rb], y_hbm.at[dst], osem)

        for_each(nfill, lambda j: fill_copy(j).start())
        for_each(nfill, lambda j: fill_copy(j).wait())

    @pl.when(n > 0)
    def _():
        @pl.when(c == 0)
        def _():
            def unpack(j):
                rows = block_rows(j)
                lo, hi = _unpack_bf16_pairs(x32[slot, rows, :])
                xb[rows, 0:half] = lo
                xb[rows, half:D_MODEL] = hi
            for_each(nblk, unpack)

            @pl.when((s + 1 < nsb) & (sbr[nxt] > 0))
            def _():
                gather_rows(sbs[nxt], 1 - slot, 0, sbr[nxt])

            @pl.when(prev_pending)
            def _():
                wait_prev_output()

        def mlp(first_row, m):
            rows = pl.ds(pl.multiple_of(first_row, rb), m)
            x = xb[rows, :]
            g = jnp.dot(x, wg_ref[0].astype(BF16), preferred_element_type=F32)
            u = jnp.dot(x, wu_ref[0].astype(BF16), preferred_element_type=F32)
            h = (jax.nn.silu(g) * u).astype(BF16)
            y = jnp.dot(h, wd_ref[0].astype(BF16), preferred_element_type=F32)
            acc[rows, :] = jnp.where(c == 0, y, acc[rows, :] + y)

        done = 0
        for m in EXPERT_BLOCKS[:-1]:
            count = (n - done) // m
            for_each(count, lambda j, done=done, m=m: mlp(done + j * m, m))
            done = done + count * m

        @pl.when(done < n)
        def _():
            mlp(done, rb)

        @pl.when(c == nch - 1)
        def _():
            for_each(nblk, lambda j: out_copy(start, j).start())


def _experts(hn_packed, w_gate, w_up, w_down, sb_expert, sb_start, sb_rows, sb_real, dest, seg_fill, tail,
             n_rows):
    n_sb = sb_expert.shape[0]
    nch = EXPERT_FF // EXPERT_FF_CHUNK
    fc = EXPERT_FF_CHUNK

    def chunk(s, c, sbn):
        return jnp.where(sbn[s] > 0, c, nch - 1)

    def up_map(s, c, sbe, sbs, sbn, *_):
        return (sbe[s], 0, chunk(s, c, sbn))

    def down_map(s, c, sbe, sbs, sbn, *_):
        return (sbe[s], chunk(s, c, sbn), 0)

    grid_spec = pltpu.PrefetchScalarGridSpec(
        num_scalar_prefetch=7,
        grid=(n_sb, nch),
        in_specs=[pl.BlockSpec(memory_space=pl.ANY),
                  pl.BlockSpec((1, D_MODEL, fc), up_map),
                  pl.BlockSpec((1, D_MODEL, fc), up_map),
                  pl.BlockSpec((1, fc, D_MODEL), down_map)],
        out_specs=pl.BlockSpec(memory_space=pl.ANY),
        scratch_shapes=[pltpu.VMEM((2, EXPERT_CAP, D_MODEL // 2), jnp.uint32),
                        pltpu.VMEM((EXPERT_CAP, D_MODEL), BF16),
                        pltpu.VMEM((EXPERT_CAP, D_MODEL), F32),
                        pltpu.SMEM((n_rows,), jnp.int32),
                        pltpu.SemaphoreType.DMA((2,)),
                        pltpu.SemaphoreType.DMA(())],
    )
    return pl.pallas_call(
        _expert_body,
        grid_spec=grid_spec,
        out_shape=jax.ShapeDtypeStruct((n_rows, D_MODEL), F32),
        compiler_params=_params("arbitrary", "arbitrary"),
        name="experts",
    )(sb_expert, sb_start, sb_rows, sb_real, dest, seg_fill, tail, hn_packed, w_gate, w_up, w_down)


def _combine_body(dest, h1_ref, wt_ref, y_hbm, o_ref, g, sem):
    i = pl.program_id(0)
    nsteps = pl.num_programs(0)
    tm = h1_ref.shape[0]

    def issue(step, slot):
        def f(q, _):
            for u in range(GATHER_UNROLL):
                r = q * GATHER_UNROLL + u
                for k in range(TOP_K):
                    src = y_hbm.at[pl.ds(dest[(step * tm + r) * TOP_K + k], 1)]
                    pltpu.make_async_copy(src, g.at[slot, k, pl.ds(r, 1)], sem.at[slot]).start()
            return 0
        lax.fori_loop(0, tm // GATHER_UNROLL, f, 0)

    @pl.when(i == 0)
    def _():
        issue(0, 0)

    @pl.when(i + 1 < nsteps)
    def _():
        issue(i + 1, (i + 1) % 2)

    slot = i % 2
    for k in range(TOP_K):
        pltpu.make_async_copy(y_hbm.at[pl.ds(0, tm)], g.at[slot, k], sem.at[slot]).wait()

    w = wt_ref[...]
    o_ref[...] = h1_ref[...] + (w[:, 0:1] * g[slot, 0] + w[:, 1:2] * g[slot, 1])


def _combine(h1, wts, y_buf, dest_flat):
    t = h1.shape[0]
    tm = COMBINE_TM
    grid_spec = pltpu.PrefetchScalarGridSpec(
        num_scalar_prefetch=1,
        grid=(t // tm,),
        in_specs=[pl.BlockSpec((tm, D_MODEL), lambda i, d: (i, 0)),
                  pl.BlockSpec((tm, LANES), lambda i, d: (i, 0)),
                  pl.BlockSpec(memory_space=pl.ANY)],
        out_specs=pl.BlockSpec((tm, D_MODEL), lambda i, d: (i, 0)),
        scratch_shapes=[pltpu.VMEM((2, TOP_K, tm, D_MODEL), F32),
                        pltpu.SemaphoreType.DMA((2,))],
    )
    return pl.pallas_call(
        _combine_body,
        grid_spec=grid_spec,
        out_shape=jax.ShapeDtypeStruct((t, D_MODEL), F32),
        compiler_params=_params("arbitrary"),
        name="combine",
    )(dest_flat, h1, wts, y_buf)


def _dispatch_tables(idx, cnt, t):
    rb, cap = EXPERT_ROW_BLOCK, EXPERT_CAP
    n_assign = t * TOP_K
    n_rows = -(-(n_assign + N_EXPERTS * (rb - 1)) // rb) * rb
    n_sb = (n_rows + N_EXPERTS * (cap - rb)) // cap
    dest = idx[:, 0:TOP_K].reshape(-1)
    counts = cnt[0, :N_EXPERTS].astype(jnp.int32)
    padded = (counts + rb - 1) // rb * rb
    seg_end = jnp.cumsum(padded)
    seg_start = (seg_end - padded).astype(jnp.int32)
    seg_fill = (seg_start + counts).astype(jnp.int32)

    sb_per_expert = (padded + cap - 1) // cap
    sb_end = jnp.cumsum(sb_per_expert)
    total = sb_end[-1]
    s = jnp.arange(n_sb, dtype=jnp.int32)
    s_eff = jnp.minimum(s, total - 1)
    e = jnp.minimum(jnp.sum(sb_end[None, :] <= s_eff[:, None], axis=1), N_EXPERTS - 1).astype(jnp.int32)
    local = s_eff - (sb_end[e] - sb_per_expert[e])
    sb_start = (seg_start[e] + local * cap).astype(jnp.int32)
    sb_rows = jnp.where(s < total, jnp.clip(padded[e] - local * cap, 0, cap), 0).astype(jnp.int32)
    gathered = (counts + GATHER_UNROLL - 1) // GATHER_UNROLL * GATHER_UNROLL
    sb_real = jnp.where(s < total, jnp.clip(gathered[e] - local * cap, 0, cap), 0).astype(jnp.int32)
    tail = seg_end[-1:].astype(jnp.int32)
    return dest, seg_fill, e, sb_start, sb_rows, sb_real, tail, n_rows


def kernel(x, meta_tokens, mix_norm_w, w_in, q_norm_w, k_norm_w, rel_bias, meta_bias, conv_w,
           attn_out_norm_w, conv_out_norm_w, w_out, ffn_norm_w, w_router_group, b_router_group,
           w_router_expert, b_router_expert, w_gate, w_up, w_down):
    bsz, seq, d = x.shape
    depth = mix_norm_w.shape[0]
    assert depth == 1 and d == D_MODEL and seq % GRID_W == 0
    t = bsz * seq
    x2d = x.reshape(t, d)
    l = 0

    w_in_b = w_in[l].astype(BF16)
    proj = _inproj(x2d, mix_norm_w[l], w_in_b, INPROJ_TM)
    proj_meta = _inproj(meta_tokens.astype(x.dtype), mix_norm_w[l], w_in_b, N_META)

    a = _attention(proj, proj_meta, q_norm_w[l], k_norm_w[l], rel_bias[l], meta_bias[l], bsz, seq)
    c = _short_conv(proj, proj_meta, conv_w[l], bsz, seq)

    spare = LANES - N_GROUPS - N_EXPERTS
    w_router = jnp.concatenate([w_router_group[l].astype(F32), w_router_expert[l].astype(F32),
                                jnp.zeros((d, spare), F32)], axis=1)
    b_router = jnp.concatenate([b_router_group[l].astype(F32), b_router_expert[l].astype(F32),
                                jnp.zeros((spare,), F32)]).reshape(1, LANES)
    h1, hn, logits = _outproj(a, c, x2d, attn_out_norm_w[l], conv_out_norm_w[l], w_out[l].astype(BF16),
                              ffn_norm_w[l], w_router.astype(BF16), b_router)

    idx, wts, cnt = _route(logits)
    dest, seg_fill, sb_expert, sb_start, sb_rows, sb_real, tail, n_rows = _dispatch_tables(idx, cnt, t)
    y_buf = _experts(hn, w_gate.reshape(N_EXPERTS, d, EXPERT_FF), w_up.reshape(N_EXPERTS, d, EXPERT_FF),
                     w_down.reshape(N_EXPERTS, EXPERT_FF, d), sb_expert, sb_start, sb_rows, sb_real,
                     dest, seg_fill, tail, n_rows)
    out = _combine(h1, wts, y_buf, dest)
    return out.reshape(bsz, seq, d)
```

```python
import functools

import jax
import jax.numpy as jnp
from jax import lax
from jax.experimental import pallas as pl
from jax.experimental.pallas import tpu as pltpu

F32 = jnp.float32
BF16 = jnp.bfloat16

D_MODEL = 2048
N_META = 16
GRID_W = 64
N_HEADS = 16
HEAD_DIM = 64
ATTN_WIDTH = N_HEADS * HEAD_DIM
CONV_WIDTH = D_MODEL - ATTN_WIDTH
PROJ_TOTAL = 3 * ATTN_WIDTH + 3 * CONV_WIDTH
WIN_ROWS = 8
WIN_COLS = 16
N_GROUPS = 4
EXPERTS_PER_GROUP = 8
N_EXPERTS = N_GROUPS * EXPERTS_PER_GROUP
TOP_K = 2
EXPERT_FF = 1024
EPS = 1e-6

LANES = 128
VMEM_LIMIT = 52 * 1024 * 1024

ATTN_ROWS_PER_STEP = 4
INPROJ_TM = 1024
INPROJ_TN = 1024
OUTPROJ_TM = 512
OUTPROJ_SUB = 256
ROUTE_TM = 512
EXPERT_ROW_BLOCK = 128
EXPERT_BLOCKS = (512, 256, 128)
EXPERT_CAP = 1024
EXPERT_FF_CHUNK = 256
GATHER_UNROLL = 8
COMBINE_TM = 256


def _params(*sem):
    return pltpu.CompilerParams(dimension_semantics=sem, vmem_limit_bytes=VMEM_LIMIT)


def _inproj_body(x_ref, nw_ref, w_ref, o_ref, xn_ref):
    @pl.when(pl.program_id(1) == 0)
    def _():
        x = x_ref[...]
        ms = jnp.mean(x * x, axis=-1, keepdims=True)
        xn_ref[...] = (x * lax.rsqrt(ms + EPS) * nw_ref[...]).astype(BF16)

    o_ref[...] = jnp.dot(xn_ref[...], w_ref[...], preferred_element_type=F32)


def _inproj(x2d, norm_w, w_bf16, tm):
    m = x2d.shape[0]
    tn = INPROJ_TN
    return pl.pallas_call(
        _inproj_body,
        grid=(m // tm, PROJ_TOTAL // tn),
        in_specs=[pl.BlockSpec((tm, D_MODEL), lambda i, j: (i, 0)),
                  pl.BlockSpec((1, D_MODEL), lambda i, j: (0, 0)),
                  pl.BlockSpec((D_MODEL, tn), lambda i, j: (0, j))],
        out_specs=pl.BlockSpec((tm, tn), lambda i, j: (i, j)),
        out_shape=jax.ShapeDtypeStruct((m, PROJ_TOTAL), F32),
        scratch_shapes=[pltpu.VMEM((tm, D_MODEL), BF16)],
        compiler_params=_params("arbitrary", "arbitrary"),
        name="inproj",
    )(x2d, norm_w.reshape(1, D_MODEL), w_bf16)


def _head_norm(x, w, lo):
    x2 = x * x
    s_lo = jnp.sum(jnp.where(lo, x2, 0.0), axis=-1, keepdims=True)
    s_hi = jnp.sum(jnp.where(lo, 0.0, x2), axis=-1, keepdims=True)
    ms = jnp.where(lo, s_lo, s_hi) * (1.0 / HEAD_DIM)
    return x * lax.rsqrt(ms + EPS) * w


def _attn_body(q_ref, k_ref, v_ref, km_ref, vm_ref, qw_ref, kw_ref, bias_ref, mb_ref,
               gb_ref, gc_ref, hc_ref, gcm_ref, hcm_ref, cw_ref, o_ref, c_ref,
               qs, ks, vs, sc, smc, pr, pmr, *, rows):
    _conv_body(gb_ref, gc_ref, hc_ref, gcm_ref, hcm_ref, cw_ref, c_ref)
    lo = lax.broadcasted_iota(jnp.int32, (1, LANES), 1) < HEAD_DIM
    scale = HEAD_DIM ** -0.5
    chunk = 256
    seq = rows * GRID_W

    kmb = _head_norm(km_ref[...], kw_ref[...], lo).astype(BF16)
    vmb = vm_ref[...].astype(BF16)
    contract_last = (((1,), (1,)), ((), ()))
    head_masks = (lo, jnp.logical_not(lo))

    def one_head(x, h):
        return jnp.where(head_masks[h], x, jnp.zeros_like(x))

    def prep(i, _):
        sl = pl.ds(pl.multiple_of(i * chunk, chunk), chunk)
        qs[sl, :] = (_head_norm(q_ref[sl, :], qw_ref[...], lo) * scale).astype(BF16)
        ks[sl, :] = _head_norm(k_ref[sl, :], kw_ref[...], lo).astype(BF16)
        vs[sl, :] = v_ref[sl, :].astype(BF16)
        return 0

    lax.fori_loop(0, seq // chunk, prep, 0, unroll=2)
    wr = min(WIN_ROWS, rows)
    nk = wr * GRID_W

    def window_start(r):
        return jnp.clip(r - wr // 2, 0, rows - wr)

    def row_slice(r, n):
        return pl.ds(pl.multiple_of(r * GRID_W, GRID_W), n)

    def store_scores(r, s_ref, sm_ref):
        rs = window_start(r)
        si = rs - r + (WIN_ROWS - 1)
        q_r = qs[row_slice(r, GRID_W), :]
        kwin = ks[row_slice(rs, nk), :]
        for h in range(2):
            qh = one_head(q_r, h)
            bias = jnp.concatenate([bias_ref[h, si + w] for w in range(0, wr, 2)], axis=-1)
            s_ref[h] = lax.dot_general(qh, kwin, contract_last, preferred_element_type=F32) + bias
            sm_ref[h] = lax.dot_general(qh, kmb, contract_last, preferred_element_type=F32) + mb_ref[h]

    def store_softmax(s_ref, sm_ref, p_ref, pm_ref):
        for h in range(2):
            s = s_ref[h]
            sm = sm_ref[h]
            m = jnp.maximum(jnp.max(s, axis=-1, keepdims=True), jnp.max(sm, axis=-1, keepdims=True))
            p = jnp.exp(s - m)
            pm = jnp.exp(sm - m)
            inv = 1.0 / (jnp.sum(p, axis=-1, keepdims=True) + jnp.sum(pm, axis=-1, keepdims=True))
            p_ref[h] = (p * inv).astype(BF16)
            pm_ref[h] = (pm * inv).astype(BF16)

    def weighted_values(r, p_ref, pm_ref):
        vwin = vs[row_slice(window_start(r), nk), :]
        outs = [jnp.dot(p_ref[h], vwin, preferred_element_type=F32)
                + jnp.dot(pm_ref[h], vmb, preferred_element_type=F32) for h in range(2)]
        o_ref[row_slice(r, GRID_W), :] = jnp.where(lo, outs[0], outs[1])

    per = ATTN_ROWS_PER_STEP
    groups = rows // per

    def step(j, parity, do_scores, do_softmax, do_values):
        if do_scores:
            for k in range(per):
                store_scores(per * j + k, sc.at[parity, k], smc.at[parity, k])
        if do_values:
            for k in range(per):
                weighted_values(per * (j - 2) + k, pr.at[1 - parity, k], pmr.at[1 - parity, k])
        if do_softmax:
            for k in range(per):
                store_softmax(sc.at[1 - parity, k], smc.at[1 - parity, k], pr.at[parity, k], pmr.at[parity, k])

    step(0, 0, True, False, False)
    step(1, 1, True, True, False)

    def two_steps(i, _):
        step(2 * i, 0, True, True, True)
        step(2 * i + 1, 1, True, True, True)
        return 0

    lax.fori_loop(1, groups // 2, two_steps, 0)
    step(groups, 0, False, True, True)
    step(groups + 1, 1, False, False, True)


def _bias_table(rel_bias):
    c = jnp.arange(GRID_W)
    col_start = jnp.clip(c - WIN_COLS // 2, 0, GRID_W - WIN_COLS)
    col_mask = (c[None, :] >= col_start[:, None]) & (c[None, :] < col_start[:, None] + WIN_COLS)
    dc = jnp.clip(c[None, :] - c[:, None], -(WIN_COLS - 1), WIN_COLS - 1) + (WIN_COLS - 1)
    ncol = 2 * WIN_COLS - 1
    pair_bias = jnp.concatenate([rel_bias[:, :-1], rel_bias[:, 1:]], axis=-1).astype(F32)
    onehot = (dc[None] == jnp.arange(ncol)[:, None, None]).astype(F32)
    zeros = jnp.zeros_like(onehot)
    pair_onehot = jnp.concatenate([jnp.concatenate([onehot, zeros], axis=-1),
                                   jnp.concatenate([zeros, onehot], axis=-1)], axis=0)
    table = jnp.einsum('hdm,mqn->hdqn', pair_bias, pair_onehot, precision=lax.Precision.HIGHEST)
    pair_mask = jnp.concatenate([col_mask, col_mask], axis=-1)
    return jnp.where(pair_mask[None, None], table, -jnp.inf)


def _mixers(proj, proj_meta, q_norm_w, k_norm_w, rel_bias, meta_bias, conv_w, bsz, seq):
    rows = seq // GRID_W
    assert rows >= WIN_ROWS and rows % (2 * ATTN_ROWS_PER_STEP) == 0
    nk = WIN_ROWS * GRID_W
    npairs = N_HEADS // 2
    assert CONV_WIDTH // LANES == npairs
    conv_base = 3 * ATTN_WIDTH // LANES
    qw = jnp.tile(q_norm_w.astype(F32), 2).reshape(1, LANES)
    kw = jnp.tile(k_norm_w.astype(F32), 2).reshape(1, LANES)
    bias = _bias_table(rel_bias)
    mb = meta_bias.astype(F32).reshape(N_HEADS, 1, N_META)
    return pl.pallas_call(
        functools.partial(_attn_body, rows=rows),
        grid=(bsz, npairs),
        in_specs=[pl.BlockSpec((seq, LANES), lambda b, p: (b, p)),
                  pl.BlockSpec((seq, LANES), lambda b, p: (b, npairs + p)),
                  pl.BlockSpec((seq, LANES), lambda b, p: (b, 2 * npairs + p)),
                  pl.BlockSpec((N_META, LANES), lambda b, p: (0, npairs + p)),
                  pl.BlockSpec((N_META, LANES), lambda b, p: (0, 2 * npairs + p)),
                  pl.BlockSpec((1, LANES), lambda b, p: (0, 0)),
                  pl.BlockSpec((1, LANES), lambda b, p: (0, 0)),
                  pl.BlockSpec((2, 2 * WIN_ROWS - 2, GRID_W, LANES), lambda b, p: (p, 0, 0, 0)),
                  pl.BlockSpec((2, 1, N_META), lambda b, p: (p, 0, 0)),
                  pl.BlockSpec((seq, LANES), lambda b, p: (b, conv_base + p)),
                  pl.BlockSpec((seq, LANES), lambda b, p: (b, conv_base + npairs + p)),
                  pl.BlockSpec((seq, LANES), lambda b, p: (b, conv_base + 2 * npairs + p)),
                  pl.BlockSpec((N_META, LANES), lambda b, p: (0, conv_base + npairs + p)),
                  pl.BlockSpec((N_META, LANES), lambda b, p: (0, conv_base + 2 * npairs + p)),
                  pl.BlockSpec((3, LANES), lambda b, p: (0, p))],
        out_specs=[pl.BlockSpec((seq, LANES), lambda b, p: (b, p)),
                   pl.BlockSpec((seq, LANES), lambda b, p: (b, p))],
        out_shape=[jax.ShapeDtypeStruct((bsz * seq, ATTN_WIDTH), F32),
                   jax.ShapeDtypeStruct((bsz * seq, CONV_WIDTH), F32)],
        scratch_shapes=([pltpu.VMEM((seq, LANES), BF16)] * 3
                        + [pltpu.VMEM((2, ATTN_ROWS_PER_STEP, 2, GRID_W, nk), F32),
                           pltpu.VMEM((2, ATTN_ROWS_PER_STEP, 2, GRID_W, N_META), F32),
                           pltpu.VMEM((2, ATTN_ROWS_PER_STEP, 2, GRID_W, nk), BF16),
                           pltpu.VMEM((2, ATTN_ROWS_PER_STEP, 2, GRID_W, N_META), BF16)]),
        compiler_params=_params("arbitrary", "arbitrary"),
        name="mixers",
    )(proj, proj, proj, proj_meta, proj_meta, qw, kw, bias, mb,
      proj, proj, proj, proj_meta, proj_meta, conv_w.astype(F32))


def _conv_body(gb_ref, gc_ref, hc_ref, gcm_ref, hcm_ref, w_ref, o_ref):
    seq = gb_ref.shape[0]
    u = gc_ref[...] * hc_ref[...]
    u_meta_last = gcm_ref[N_META - 1:N_META, :] * hcm_ref[N_META - 1:N_META, :]
    row = lax.broadcasted_iota(jnp.int32, (seq, 1), 0)
    u_prev = jnp.where(row == 0, u_meta_last, pltpu.roll(u, 1, 0))
    u_next = jnp.where(row == seq - 1, 0.0, pltpu.roll(u, seq - 1, 0))
    w = w_ref[...]
    y = u_prev * w[0:1] + u * w[1:2] + u_next * w[2:3]
    o_ref[...] = gb_ref[...] * y


def _rms(x, w):
    ms = jnp.mean(x * x, axis=-1, keepdims=True)
    return x * lax.rsqrt(ms + EPS) * w


def _pack_bf16_pairs(x):
    w = x.shape[1] // 2
    lo = lax.bitcast_convert_type(x[:, :w].astype(BF16).astype(F32), jnp.uint32)
    hi = lax.bitcast_convert_type(x[:, w:].astype(BF16).astype(F32), jnp.uint32)
    return (hi & jnp.uint32(0xFFFF0000)) | (lo >> 16)


def _unpack_bf16_pairs(p):
    lo = lax.bitcast_convert_type(p << 16, F32).astype(BF16)
    hi = lax.bitcast_convert_type(p & jnp.uint32(0xFFFF0000), F32).astype(BF16)
    return lo, hi


def _outproj_body(a_ref, c_ref, x_ref, aw_ref, cw_ref, wo_ref, fw_ref, wr_ref, br_ref,
                  h1_ref, hn_ref, lg_ref):
    sub = OUTPROJ_SUB
    blocks = [pl.ds(k * sub, sub) for k in range(a_ref.shape[0] // sub)]

    def mix(rows):
        an = _rms(a_ref[rows, :], aw_ref[...]).astype(BF16)
        cn = _rms(c_ref[rows, :], cw_ref[...]).astype(BF16)
        return (jnp.dot(an, wo_ref[0:ATTN_WIDTH, :], preferred_element_type=F32)
                + jnp.dot(cn, wo_ref[ATTN_WIDTH:D_MODEL, :], preferred_element_type=F32))

    def finish(rows, mixed):
        h1 = x_ref[rows, :] + mixed
        h1_ref[rows, :] = h1
        hn = _rms(h1, fw_ref[...])
        hn_ref[rows, :] = _pack_bf16_pairs(hn)
        lg_ref[rows, :] = jnp.dot(hn.astype(BF16), wr_ref[...], preferred_element_type=F32) + br_ref[...]

    mixed = mix(blocks[0])
    for k, rows in enumerate(blocks):
        following = mix(blocks[k + 1]) if k + 1 < len(blocks) else None
        finish(rows, mixed)
        mixed = following


def _outproj(a, c, x2d, aw, cw, wo_bf16, fw, w_router, b_router):
    t = x2d.shape[0]
    tm = OUTPROJ_TM
    row = lambda i: (i, 0)
    fixed = lambda i: (0, 0)
    return pl.pallas_call(
        _outproj_body,
        grid=(t // tm,),
        in_specs=[pl.BlockSpec((tm, ATTN_WIDTH), row),
                  pl.BlockSpec((tm, CONV_WIDTH), row),
                  pl.BlockSpec((tm, D_MODEL), row),
                  pl.BlockSpec((1, ATTN_WIDTH), fixed),
                  pl.BlockSpec((1, CONV_WIDTH), fixed),
                  pl.BlockSpec((D_MODEL, D_MODEL), fixed, pipeline_mode=pl.Buffered(1)),
                  pl.BlockSpec((1, D_MODEL), fixed),
                  pl.BlockSpec((D_MODEL, LANES), fixed),
                  pl.BlockSpec((1, LANES), fixed)],
        out_specs=[pl.BlockSpec((tm, D_MODEL), row),
                   pl.BlockSpec((tm, D_MODEL // 2), row),
                   pl.BlockSpec((tm, LANES), row)],
        out_shape=[jax.ShapeDtypeStruct((t, D_MODEL), F32),
                   jax.ShapeDtypeStruct((t, D_MODEL // 2), jnp.uint32),
                   jax.ShapeDtypeStruct((t, LANES), F32)],
        compiler_params=_params("arbitrary"),
        name="outproj",
    )(a, c, x2d, aw.reshape(1, -1), cw.reshape(1, -1), wo_bf16, fw.reshape(1, -1), w_router, b_router)


def _route_body(lg_ref, idx_ref, wt_ref, cnt_ref, run_ref, seg_ref):
    final_pass = pl.program_id(0) == 1
    first_tile = pl.program_id(1) == 0

    @pl.when(first_tile & jnp.logical_not(final_pass))
    def _():
        run_ref[...] = jnp.zeros_like(run_ref)
        seg_ref[...] = jnp.zeros_like(seg_ref)

    @pl.when(first_tile & final_pass)
    def _():
        counts = run_ref[...]
        rb = float(EXPERT_ROW_BLOCK)
        padded = jnp.ceil(counts * (1.0 / rb)) * rb
        lane8 = lax.broadcasted_iota(jnp.int32, counts.shape, 1)
        ends = padded
        shift = 1
        while shift < LANES:
            ends = ends + jnp.where(lane8 >= shift, pltpu.roll(ends, shift, 1), 0.0)
            shift *= 2
        seg_ref[...] = ends - padded
        run_ref[...] = jnp.zeros_like(run_ref)

    logits = lg_ref[...]
    tm = logits.shape[0]
    lane = lax.broadcasted_iota(jnp.int32, (tm, LANES), 1)
    neg = -jnp.inf

    def first_argmax(v):
        m = jnp.max(v, axis=-1, keepdims=True)
        first = jnp.min(jnp.where(v == m, lane.astype(F32), float(LANES)), axis=-1, keepdims=True)
        return m, first.astype(jnp.int32)

    gl = jnp.where(lane < N_GROUPS, logits, neg)
    gmax, gidx = first_argmax(gl)
    g_w = 1.0 / jnp.sum(jnp.exp(gl - gmax), axis=-1, keepdims=True)
    first = N_GROUPS + gidx * EXPERTS_PER_GROUP
    el = jnp.where((lane >= first) & (lane < first + EXPERTS_PER_GROUP), logits, neg)
    m0, j0 = first_argmax(el)
    m1, j1 = first_argmax(jnp.where(lane == j0, neg, el))
    p1 = jnp.exp(m1 - m0)
    w0 = g_w / (1.0 + p1)
    w1 = g_w * p1 / (1.0 + p1)
    e0 = j0 - N_GROUPS
    e1 = j1 - N_GROUPS

    onehot = ((lane == e0) | (lane == e1)).astype(BF16)
    seen = run_ref[0:1, :]
    run = seen + jnp.sum(onehot.astype(F32), axis=0, keepdims=True)
    run_ref[...] = jnp.broadcast_to(run, run_ref.shape)

    @pl.when(final_pass)
    def _():
        tri = (lax.broadcasted_iota(jnp.int32, (tm, tm), 0)
               > lax.broadcasted_iota(jnp.int32, (tm, tm), 1)).astype(BF16)
        place = jnp.dot(tri, onehot, preferred_element_type=F32) + seen + seg_ref[0:1, :]
        d0 = jnp.sum(jnp.where(lane == e0, place, 0.0), axis=-1, keepdims=True).astype(jnp.int32)
        d1 = jnp.sum(jnp.where(lane == e1, place, 0.0), axis=-1, keepdims=True).astype(jnp.int32)
        idx_ref[...] = jnp.where(lane == 0, d0, jnp.where(lane == 1, d1, jnp.zeros_like(lane)))
        wt_ref[...] = jnp.where(lane == 0, w0, jnp.where(lane == 1, w1, 0.0))
        cnt_ref[...] = jnp.broadcast_to(run, cnt_ref.shape)


def _route(logits):
    t = logits.shape[0]
    tm = ROUTE_TM
    return pl.pallas_call(
        _route_body,
        grid=(2, t // tm),
        in_specs=[pl.BlockSpec((tm, LANES), lambda p, i: (i, 0))],
        out_specs=[pl.BlockSpec((tm, LANES), lambda p, i: (i * p, 0)),
                   pl.BlockSpec((tm, LANES), lambda p, i: (i * p, 0)),
                   pl.BlockSpec((8, LANES), lambda p, i: (0, 0))],
        out_shape=[jax.ShapeDtypeStruct((t, LANES), jnp.int32),
                   jax.ShapeDtypeStruct((t, LANES), F32),
                   jax.ShapeDtypeStruct((8, LANES), F32)],
        scratch_shapes=[pltpu.VMEM((8, LANES), F32)] * 2,
        compiler_params=_params("arbitrary", "arbitrary"),
        name="route",
    )(logits)


def _expert_body(sbe, sbs, sbn, sbr, dest, seg_fill, tail, hn_hbm, wg_ref, wu_ref, wd_ref,
                 y_hbm, x32, xb, acc, tok, gsem, osem):
    s = pl.program_id(0)
    c = pl.program_id(1)
    nsb = pl.num_programs(0)
    nch = EXPERT_FF // EXPERT_FF_CHUNK
    rb = EXPERT_ROW_BLOCK
    half = D_MODEL // 2
    n = sbn[s]
    start = sbs[s]
    nblk = n // rb
    slot = s % 2

    def for_each(count, fn):
        def body(i, _):
            fn(i)
            return 0
        lax.fori_loop(0, count, body, 0)

    def block_rows(j):
        return pl.ds(pl.multiple_of(j * rb, rb), rb)

    def gather_row(sb_first, buf, i):
        src = hn_hbm.at[pl.ds(tok[sb_first + i], 1)]
        pltpu.make_async_copy(src, x32.at[buf, pl.ds(i, 1)], gsem.at[buf]).start()

    def gather_rows(sb_first, buf, first_row, count):
        def group(g):
            for k in range(GATHER_UNROLL):
                gather_row(sb_first, buf, first_row + g * GATHER_UNROLL + k)
        for_each(count // GATHER_UNROLL, group)

    def wait_gathered(buf, count):
        rows = pl.ds(0, pl.multiple_of(count, GATHER_UNROLL))
        pltpu.make_async_copy(hn_hbm.at[rows], x32.at[buf, rows], gsem.at[buf]).wait()

    def out_copy(first_row, j):
        dst = pl.ds(pl.multiple_of(first_row + j * rb, rb), rb)
        return pltpu.make_async_copy(acc.at[block_rows(j)], y_hbm.at[dst], osem)

    @pl.when((s == 0) & (c == 0))
    def _():
        def clear(j):
            for buf in range(2):
                x32[buf, block_rows(j), :] = jnp.zeros((rb, half), jnp.uint32)
            acc[block_rows(j), :] = jnp.zeros((rb, D_MODEL), F32)
        for_each(EXPERT_CAP // rb, clear)

        def pad_rows(e):
            for k in range(GATHER_UNROLL - 1):
                tok[jnp.minimum(seg_fill[e] + k, tok.shape[0] - 1)] = 0
        for_each(N_EXPERTS, pad_rows)

        def invert(g):
            for k in range(GATHER_UNROLL):
                tok[dest[g * GATHER_UNROLL + k]] = g * (GATHER_UNROLL // TOP_K) + k // TOP_K
        for_each(dest.shape[0] // GATHER_UNROLL, invert)

        gather_rows(sbs[0], 0, 0, sbr[0])

    prev = jnp.maximum(s - 1, 0)
    nxt = jnp.minimum(s + 1, nsb - 1)
    prev_rows = jnp.where(s > 0, sbn[prev], 0)
    prev_pending = (c == 0) & (prev_rows > 0)

    @pl.when((c == 0) & (sbr[s] > 0))
    def _():
        wait_gathered(slot, sbr[s])

    def wait_prev_output():
        for_each(prev_rows // rb, lambda j: out_copy(sbs[prev], j).wait())

    @pl.when(prev_pending & (n == 0))
    def _():
        wait_prev_output()

    @pl.when((s == nsb - 1) & (c == 0))
    def _():
        first = tail[0]
        nfill = (y_hbm.shape[0] - first) // rb
        acc[0:rb, :] = jnp.zeros((rb, D_MODEL), F32)

        def fill_copy(j):
            dst = pl.ds(pl.multiple_of(first + j * rb, rb), rb)
            return pltpu.make_async_copy(acc.at[0:rb], y_hbm.at[dst], osem)

        for_each(nfill, lambda j: fill_copy(j).start())
        for_each(nfill, lambda j: fill_copy(j).wait())

    @pl.when(n > 0)
    def _():
        @pl.when(c == 0)
        def _():
            def unpack(j):
                rows = block_rows(j)
                lo, hi = _unpack_bf16_pairs(x32[slot, rows, :])
                xb[rows, 0:half] = lo
                xb[rows, half:D_MODEL] = hi
            for_each(nblk, unpack)

            @pl.when((s + 1 < nsb) & (sbr[nxt] > 0))
            def _():
                gather_rows(sbs[nxt], 1 - slot, 0, sbr[nxt])

            @pl.when(prev_pending)
            def _():
                wait_prev_output()

        def mlp(first_row, m):
            rows = pl.ds(pl.multiple_of(first_row, rb), m)
            x = xb[rows, :]
            g = jnp.dot(x, wg_ref[0].astype(BF16), preferred_element_type=F32)
            u = jnp.dot(x, wu_ref[0].astype(BF16), preferred_element_type=F32)
            h = (jax.nn.silu(g) * u).astype(BF16)
            y = jnp.dot(h, wd_ref[0].astype(BF16), preferred_element_type=F32)
            acc[rows, :] = jnp.where(c == 0, y, acc[rows, :] + y)

        done = 0
        for m in EXPERT_BLOCKS[:-1]:
            count = (n - done) // m
            for_each(count, lambda j, done=done, m=m: mlp(done + j * m, m))
            done = done + count * m

        @pl.when(done < n)
        def _():
            mlp(done, rb)

        @pl.when(c == nch - 1)
        def _():
            for_each(nblk, lambda j: out_copy(start, j).start())


def _experts(hn_packed, w_gate, w_up, w_down, sb_expert, sb_start, sb_rows, sb_real, dest, seg_fill, tail,
             n_rows):
    n_sb = sb_expert.shape[0]
    nch = EXPERT_FF // EXPERT_FF_CHUNK
    fc = EXPERT_FF_CHUNK

    def chunk(s, c, sbn):
        return jnp.where(sbn[s] > 0, c, nch - 1)

    def up_map(s, c, sbe, sbs, sbn, *_):
        return (sbe[s], 0, chunk(s, c, sbn))

    def down_map(s, c, sbe, sbs, sbn, *_):
        return (sbe[s], chunk(s, c, sbn), 0)

    grid_spec = pltpu.PrefetchScalarGridSpec(
        num_scalar_prefetch=7,
        grid=(n_sb, nch),
        in_specs=[pl.BlockSpec(memory_space=pl.ANY),
                  pl.BlockSpec((1, D_MODEL, fc), up_map),
                  pl.BlockSpec((1, D_MODEL, fc), up_map),
                  pl.BlockSpec((1, fc, D_MODEL), down_map)],
        out_specs=pl.BlockSpec(memory_space=pl.ANY),
        scratch_shapes=[pltpu.VMEM((2, EXPERT_CAP, D_MODEL // 2), jnp.uint32),
                        pltpu.VMEM((EXPERT_CAP, D_MODEL), BF16),
                        pltpu.VMEM((EXPERT_CAP, D_MODEL), F32),
                        pltpu.SMEM((n_rows,), jnp.int32),
                        pltpu.SemaphoreType.DMA((2,)),
                        pltpu.SemaphoreType.DMA(())],
    )
    return pl.pallas_call(
        _expert_body,
        grid_spec=grid_spec,
        out_shape=jax.ShapeDtypeStruct((n_rows, D_MODEL), F32),
        compiler_params=_params("arbitrary", "arbitrary"),
        name="experts",
    )(sb_expert, sb_start, sb_rows, sb_real, dest, seg_fill, tail, hn_packed, w_gate, w_up, w_down)


def _combine_body(dest, h1_ref, wt_ref, y_hbm, o_ref, g, sem):
    i = pl.program_id(0)
    nsteps = pl.num_programs(0)
    tm = h1_ref.shape[0]

    def issue(step, slot):
        def f(q, _):
            for u in range(GATHER_UNROLL):
                r = q * GATHER_UNROLL + u
                for k in range(TOP_K):
                    src = y_hbm.at[pl.ds(dest[(step * tm + r) * TOP_K + k], 1)]
                    pltpu.make_async_copy(src, g.at[slot, k, pl.ds(r, 1)], sem.at[slot]).start()
            return 0
        lax.fori_loop(0, tm // GATHER_UNROLL, f, 0)

    @pl.when(i == 0)
    def _():
        issue(0, 0)

    @pl.when(i + 1 < nsteps)
    def _():
        issue(i + 1, (i + 1) % 2)

    slot = i % 2
    for k in range(TOP_K):
        pltpu.make_async_copy(y_hbm.at[pl.ds(0, tm)], g.at[slot, k], sem.at[slot]).wait()

    w = wt_ref[...]
    o_ref[...] = h1_ref[...] + (w[:, 0:1] * g[slot, 0] + w[:, 1:2] * g[slot, 1])


def _combine(h1, wts, y_buf, dest_flat):
    t = h1.shape[0]
    tm = COMBINE_TM
    grid_spec = pltpu.PrefetchScalarGridSpec(
        num_scalar_prefetch=1,
        grid=(t // tm,),
        in_specs=[pl.BlockSpec((tm, D_MODEL), lambda i, d: (i, 0)),
                  pl.BlockSpec((tm, LANES), lambda i, d: (i, 0)),
                  pl.BlockSpec(memory_space=pl.ANY)],
        out_specs=pl.BlockSpec((tm, D_MODEL), lambda i, d: (i, 0)),
        scratch_shapes=[pltpu.VMEM((2, TOP_K, tm, D_MODEL), F32),
                        pltpu.SemaphoreType.DMA((2,))],
    )
    return pl.pallas_call(
        _combine_body,
        grid_spec=grid_spec,
        out_shape=jax.ShapeDtypeStruct((t, D_MODEL), F32),
        compiler_params=_params("arbitrary"),
        name="combine",
    )(dest_flat, h1, wts, y_buf)


def _dispatch_tables(idx, cnt, t):
    rb, cap = EXPERT_ROW_BLOCK, EXPERT_CAP
    n_assign = t * TOP_K
    n_rows = -(-(n_assign + N_EXPERTS * (rb - 1)) // rb) * rb
    n_sb = (n_rows + N_EXPERTS * (cap - rb)) // cap
    dest = idx[:, 0:TOP_K].reshape(-1)
    counts = cnt[0, :N_EXPERTS].astype(jnp.int32)
    padded = (counts + rb - 1) // rb * rb
    seg_end = jnp.cumsum(padded)
    seg_start = (seg_end - padded).astype(jnp.int32)
    seg_fill = (seg_start + counts).astype(jnp.int32)

    sb_per_expert = (padded + cap - 1) // cap
    sb_end = jnp.cumsum(sb_per_expert)
    total = sb_end[-1]
    s = jnp.arange(n_sb, dtype=jnp.int32)
    s_eff = jnp.minimum(s, total - 1)
    e = jnp.minimum(jnp.sum(sb_end[None, :] <= s_eff[:, None], axis=1), N_EXPERTS - 1).astype(jnp.int32)
    local = s_eff - (sb_end[e] - sb_per_expert[e])
    sb_start = (seg_start[e] + local * cap).astype(jnp.int32)
    sb_rows = jnp.where(s < total, jnp.clip(padded[e] - local * cap, 0, cap), 0).astype(jnp.int32)
    gathered = (counts + GATHER_UNROLL - 1) // GATHER_UNROLL * GATHER_UNROLL
    sb_real = jnp.where(s < total, jnp.clip(gathered[e] - local * cap, 0, cap), 0).astype(jnp.int32)
    tail = seg_end[-1:].astype(jnp.int32)
    return dest, seg_fill, e, sb_start, sb_rows, sb_real, tail, n_rows


def kernel(x, meta_tokens, mix_norm_w, w_in, q_norm_w, k_norm_w, rel_bias, meta_bias, conv_w,
           attn_out_norm_w, conv_out_norm_w, w_out, ffn_norm_w, w_router_group, b_router_group,
           w_router_expert, b_router_expert, w_gate, w_up, w_down):
    bsz, seq, d = x.shape
    depth = mix_norm_w.shape[0]
    assert depth == 1 and d == D_MODEL and seq % GRID_W == 0
    t = bsz * seq
    x2d = x.reshape(t, d)
    l = 0

    w_in_b = w_in[l].astype(BF16)
    proj = _inproj(x2d, mix_norm_w[l], w_in_b, INPROJ_TM)
    proj_meta = _inproj(meta_tokens.astype(x.dtype), mix_norm_w[l], w_in_b, N_META)

    a, c = _mixers(proj, proj_meta, q_norm_w[l], k_norm_w[l], rel_bias[l], meta_bias[l], conv_w[l], bsz, seq)

    spare = LANES - N_GROUPS - N_EXPERTS
    w_router = jnp.concatenate([w_router_group[l].astype(F32), w_router_expert[l].astype(F32),
                                jnp.zeros((d, spare), F32)], axis=1)
    b_router = jnp.concatenate([b_router_group[l].astype(F32), b_router_expert[l].astype(F32),
                                jnp.zeros((spare,), F32)]).reshape(1, LANES)
    h1, hn, logits = _outproj(a, c, x2d, attn_out_norm_w[l], conv_out_norm_w[l], w_out[l].astype(BF16),
                              ffn_norm_w[l], w_router.astype(BF16), b_router)

    idx, wts, cnt = _route(logits)
    dest, seg_fill, sb_expert, sb_start, sb_rows, sb_real, tail, n_rows = _dispatch_tables(idx, cnt, t)
    y_buf = _experts(hn, w_gate.reshape(N_EXPERTS, d, EXPERT_FF), w_up.reshape(N_EXPERTS, d, EXPERT_FF),
                     w_down.reshape(N_EXPERTS, EXPERT_FF, d), sb_expert, sb_start, sb_rows, sb_real,
                     dest, seg_fill, tail, n_rows)
    out = _combine(h1, wts, y_buf, dest)
    return out.reshape(bsz, seq, d)
```

```python
import functools

import jax
import jax.numpy as jnp
from jax import lax
from jax.experimental import pallas as pl
from jax.experimental.pallas import tpu as pltpu

F32 = jnp.float32
BF16 = jnp.bfloat16

D_MODEL = 2048
N_META = 16
GRID_W = 64
N_HEADS = 16
HEAD_DIM = 64
ATTN_WIDTH = N_HEADS * HEAD_DIM
CONV_WIDTH = D_MODEL - ATTN_WIDTH
PROJ_TOTAL = 3 * ATTN_WIDTH + 3 * CONV_WIDTH
WIN_ROWS = 8
WIN_COLS = 16
N_GROUPS = 4
EXPERTS_PER_GROUP = 8
N_EXPERTS = N_GROUPS * EXPERTS_PER_GROUP
TOP_K = 2
EXPERT_FF = 1024
EPS = 1e-6

LANES = 128
VMEM_LIMIT = 52 * 1024 * 1024

ATTN_ROWS_PER_STEP = 4
INPROJ_TM = 1024
INPROJ_TN = 1024
OUTPROJ_TM = 512
OUTPROJ_SUB = 256
ROUTE_TM = 512
EXPERT_ROW_BLOCK = 128
EXPERT_BLOCKS = (512, 256, 128)
EXPERT_CAP = 1024
EXPERT_FF_CHUNK = 512
GATHER_UNROLL = 8
COMBINE_TM = 256


def _params(*sem):
    return pltpu.CompilerParams(dimension_semantics=sem, vmem_limit_bytes=VMEM_LIMIT)


def _inproj_body(x_ref, nw_ref, w_ref, o_ref, xn_ref):
    @pl.when(pl.program_id(1) == 0)
    def _():
        x = x_ref[...]
        ms = jnp.mean(x * x, axis=-1, keepdims=True)
        xn_ref[...] = (x * lax.rsqrt(ms + EPS) * nw_ref[...]).astype(BF16)

    o_ref[...] = jnp.dot(xn_ref[...], w_ref[...], preferred_element_type=F32)


def _inproj(x2d, norm_w, w_bf16, tm):
    m = x2d.shape[0]
    tn = INPROJ_TN
    return pl.pallas_call(
        _inproj_body,
        grid=(m // tm, PROJ_TOTAL // tn),
        in_specs=[pl.BlockSpec((tm, D_MODEL), lambda i, j: (i, 0)),
                  pl.BlockSpec((1, D_MODEL), lambda i, j: (0, 0)),
                  pl.BlockSpec((D_MODEL, tn), lambda i, j: (0, j))],
        out_specs=pl.BlockSpec((tm, tn), lambda i, j: (i, j)),
        out_shape=jax.ShapeDtypeStruct((m, PROJ_TOTAL), F32),
        scratch_shapes=[pltpu.VMEM((tm, D_MODEL), BF16)],
        compiler_params=_params("arbitrary", "arbitrary"),
        name="inproj",
    )(x2d, norm_w.reshape(1, D_MODEL), w_bf16)


def _head_norm(x, w, lo):
    x2 = x * x
    s_lo = jnp.sum(jnp.where(lo, x2, 0.0), axis=-1, keepdims=True)
    s_hi = jnp.sum(jnp.where(lo, 0.0, x2), axis=-1, keepdims=True)
    ms = jnp.where(lo, s_lo, s_hi) * (1.0 / HEAD_DIM)
    return x * lax.rsqrt(ms + EPS) * w


def _attn_body(q_ref, k_ref, v_ref, km_ref, vm_ref, qw_ref, kw_ref, bias_ref, mb_ref,
               gb_ref, gc_ref, hc_ref, gcm_ref, hcm_ref, cw_ref, o_ref, c_ref,
               qs, ks, vs, sc, smc, pr, pmr, *, rows):
    _conv_body(gb_ref, gc_ref, hc_ref, gcm_ref, hcm_ref, cw_ref, c_ref)
    lo = lax.broadcasted_iota(jnp.int32, (1, LANES), 1) < HEAD_DIM
    scale = HEAD_DIM ** -0.5
    chunk = 256
    seq = rows * GRID_W

    kmb = _head_norm(km_ref[...], kw_ref[...], lo).astype(BF16)
    vmb = vm_ref[...].astype(BF16)
    contract_last = (((1,), (1,)), ((), ()))
    head_masks = (lo, jnp.logical_not(lo))

    def one_head(x, h):
        return jnp.where(head_masks[h], x, jnp.zeros_like(x))

    def prep(i, _):
        sl = pl.ds(pl.multiple_of(i * chunk, chunk), chunk)
        qs[sl, :] = (_head_norm(q_ref[sl, :], qw_ref[...], lo) * scale).astype(BF16)
        ks[sl, :] = _head_norm(k_ref[sl, :], kw_ref[...], lo).astype(BF16)
        vs[sl, :] = v_ref[sl, :].astype(BF16)
        return 0

    lax.fori_loop(0, seq // chunk, prep, 0, unroll=2)
    wr = min(WIN_ROWS, rows)
    nk = wr * GRID_W

    def window_start(r):
        return jnp.clip(r - wr // 2, 0, rows - wr)

    def row_slice(r, n):
        return pl.ds(pl.multiple_of(r * GRID_W, GRID_W), n)

    def store_scores(r, s_ref, sm_ref):
        rs = window_start(r)
        si = rs - r + (WIN_ROWS - 1)
        q_r = qs[row_slice(r, GRID_W), :]
        kwin = ks[row_slice(rs, nk), :]
        for h in range(2):
            qh = one_head(q_r, h)
            bias = jnp.concatenate([bias_ref[h, si + w] for w in range(0, wr, 2)], axis=-1)
            s_ref[h] = lax.dot_general(qh, kwin, contract_last, preferred_element_type=F32) + bias
            sm_ref[h] = lax.dot_general(qh, kmb, contract_last, preferred_element_type=F32) + mb_ref[h]

    def store_softmax(s_ref, sm_ref, p_ref, pm_ref):
        for h in range(2):
            s = s_ref[h]
            sm = sm_ref[h]
            m = jnp.maximum(jnp.max(s, axis=-1, keepdims=True), jnp.max(sm, axis=-1, keepdims=True))
            p = jnp.exp(s - m)
            pm = jnp.exp(sm - m)
            inv = 1.0 / (jnp.sum(p, axis=-1, keepdims=True) + jnp.sum(pm, axis=-1, keepdims=True))
            p_ref[h] = (p * inv).astype(BF16)
            pm_ref[h] = (pm * inv).astype(BF16)

    def weighted_values(r, p_ref, pm_ref):
        vwin = vs[row_slice(window_start(r), nk), :]
        outs = [jnp.dot(p_ref[h], vwin, preferred_element_type=F32)
                + jnp.dot(pm_ref[h], vmb, preferred_element_type=F32) for h in range(2)]
        o_ref[row_slice(r, GRID_W), :] = jnp.where(lo, outs[0], outs[1])

    per = ATTN_ROWS_PER_STEP
    groups = rows // per

    def step(j, parity, do_scores, do_softmax, do_values):
        if do_scores:
            for k in range(per):
                store_scores(per * j + k, sc.at[parity, k], smc.at[parity, k])
        if do_values:
            for k in range(per):
                weighted_values(per * (j - 2) + k, pr.at[1 - parity, k], pmr.at[1 - parity, k])
        if do_softmax:
            for k in range(per):
                store_softmax(sc.at[1 - parity, k], smc.at[1 - parity, k], pr.at[parity, k], pmr.at[parity, k])

    step(0, 0, True, False, False)
    step(1, 1, True, True, False)

    def two_steps(i, _):
        step(2 * i, 0, True, True, True)
        step(2 * i + 1, 1, True, True, True)
        return 0

    lax.fori_loop(1, groups // 2, two_steps, 0)
    step(groups, 0, False, True, True)
    step(groups + 1, 1, False, False, True)


def _bias_table(rel_bias):
    c = jnp.arange(GRID_W)
    col_start = jnp.clip(c - WIN_COLS // 2, 0, GRID_W - WIN_COLS)
    col_mask = (c[None, :] >= col_start[:, None]) & (c[None, :] < col_start[:, None] + WIN_COLS)
    dc = jnp.clip(c[None, :] - c[:, None], -(WIN_COLS - 1), WIN_COLS - 1) + (WIN_COLS - 1)
    ncol = 2 * WIN_COLS - 1
    pair_bias = jnp.concatenate([rel_bias[:, :-1], rel_bias[:, 1:]], axis=-1).astype(F32)
    onehot = (dc[None] == jnp.arange(ncol)[:, None, None]).astype(F32)
    zeros = jnp.zeros_like(onehot)
    pair_onehot = jnp.concatenate([jnp.concatenate([onehot, zeros], axis=-1),
                                   jnp.concatenate([zeros, onehot], axis=-1)], axis=0)
    table = jnp.einsum('hdm,mqn->hdqn', pair_bias, pair_onehot, precision=lax.Precision.HIGHEST)
    pair_mask = jnp.concatenate([col_mask, col_mask], axis=-1)
    return jnp.where(pair_mask[None, None], table, -jnp.inf)


def _mixers(proj, proj_meta, q_norm_w, k_norm_w, rel_bias, meta_bias, conv_w, bsz, seq):
    rows = seq // GRID_W
    assert rows >= WIN_ROWS and rows % (2 * ATTN_ROWS_PER_STEP) == 0
    nk = WIN_ROWS * GRID_W
    npairs = N_HEADS // 2
    assert CONV_WIDTH // LANES == npairs
    conv_base = 3 * ATTN_WIDTH // LANES
    qw = jnp.tile(q_norm_w.astype(F32), 2).reshape(1, LANES)
    kw = jnp.tile(k_norm_w.astype(F32), 2).reshape(1, LANES)
    bias = _bias_table(rel_bias)
    mb = meta_bias.astype(F32).reshape(N_HEADS, 1, N_META)
    return pl.pallas_call(
        functools.partial(_attn_body, rows=rows),
        grid=(bsz, npairs),
        in_specs=[pl.BlockSpec((seq, LANES), lambda b, p: (b, p)),
                  pl.BlockSpec((seq, LANES), lambda b, p: (b, npairs + p)),
                  pl.BlockSpec((seq, LANES), lambda b, p: (b, 2 * npairs + p)),
                  pl.BlockSpec((N_META, LANES), lambda b, p: (0, npairs + p)),
                  pl.BlockSpec((N_META, LANES), lambda b, p: (0, 2 * npairs + p)),
                  pl.BlockSpec((1, LANES), lambda b, p: (0, 0)),
                  pl.BlockSpec((1, LANES), lambda b, p: (0, 0)),
                  pl.BlockSpec((2, 2 * WIN_ROWS - 2, GRID_W, LANES), lambda b, p: (p, 0, 0, 0)),
                  pl.BlockSpec((2, 1, N_META), lambda b, p: (p, 0, 0)),
                  pl.BlockSpec((seq, LANES), lambda b, p: (b, conv_base + p)),
                  pl.BlockSpec((seq, LANES), lambda b, p: (b, conv_base + npairs + p)),
                  pl.BlockSpec((seq, LANES), lambda b, p: (b, conv_base + 2 * npairs + p)),
                  pl.BlockSpec((N_META, LANES), lambda b, p: (0, conv_base + npairs + p)),
                  pl.BlockSpec((N_META, LANES), lambda b, p: (0, conv_base + 2 * npairs + p)),
                  pl.BlockSpec((3, LANES), lambda b, p: (0, p))],
        out_specs=[pl.BlockSpec((seq, LANES), lambda b, p: (b, p)),
                   pl.BlockSpec((seq, LANES), lambda b, p: (b, p))],
        out_shape=[jax.ShapeDtypeStruct((bsz * seq, ATTN_WIDTH), F32),
                   jax.ShapeDtypeStruct((bsz * seq, CONV_WIDTH), F32)],
        scratch_shapes=([pltpu.VMEM((seq, LANES), BF16)] * 3
                        + [pltpu.VMEM((2, ATTN_ROWS_PER_STEP, 2, GRID_W, nk), F32),
                           pltpu.VMEM((2, ATTN_ROWS_PER_STEP, 2, GRID_W, N_META), F32),
                           pltpu.VMEM((2, ATTN_ROWS_PER_STEP, 2, GRID_W, nk), BF16),
                           pltpu.VMEM((2, ATTN_ROWS_PER_STEP, 2, GRID_W, N_META), BF16)]),
        compiler_params=_params("arbitrary", "arbitrary"),
        name="mixers",
    )(proj, proj, proj, proj_meta, proj_meta, qw, kw, bias, mb,
      proj, proj, proj, proj_meta, proj_meta, conv_w.astype(F32))


def _conv_body(gb_ref, gc_ref, hc_ref, gcm_ref, hcm_ref, w_ref, o_ref):
    seq = gb_ref.shape[0]
    u = gc_ref[...] * hc_ref[...]
    u_meta_last = gcm_ref[N_META - 1:N_META, :] * hcm_ref[N_META - 1:N_META, :]
    row = lax.broadcasted_iota(jnp.int32, (seq, 1), 0)
    u_prev = jnp.where(row == 0, u_meta_last, pltpu.roll(u, 1, 0))
    u_next = jnp.where(row == seq - 1, 0.0, pltpu.roll(u, seq - 1, 0))
    w = w_ref[...]
    y = u_prev * w[0:1] + u * w[1:2] + u_next * w[2:3]
    o_ref[...] = gb_ref[...] * y


def _rms(x, w):
    ms = jnp.mean(x * x, axis=-1, keepdims=True)
    return x * lax.rsqrt(ms + EPS) * w


def _pack_bf16_pairs(x):
    w = x.shape[1] // 2
    lo = lax.bitcast_convert_type(x[:, :w].astype(BF16).astype(F32), jnp.uint32)
    hi = lax.bitcast_convert_type(x[:, w:].astype(BF16).astype(F32), jnp.uint32)
    return (hi & jnp.uint32(0xFFFF0000)) | (lo >> 16)


def _unpack_bf16_pairs(p):
    lo = lax.bitcast_convert_type(p << 16, F32).astype(BF16)
    hi = lax.bitcast_convert_type(p & jnp.uint32(0xFFFF0000), F32).astype(BF16)
    return lo, hi


def _outproj_body(a_ref, c_ref, x_ref, aw_ref, cw_ref, wo_ref, fw_ref, wr_ref, br_ref,
                  h1_ref, hn_ref, lg_ref):
    sub = OUTPROJ_SUB
    blocks = [pl.ds(k * sub, sub) for k in range(a_ref.shape[0] // sub)]

    def mix(rows):
        an = _rms(a_ref[rows, :], aw_ref[...]).astype(BF16)
        cn = _rms(c_ref[rows, :], cw_ref[...]).astype(BF16)
        return (jnp.dot(an, wo_ref[0:ATTN_WIDTH, :], preferred_element_type=F32)
                + jnp.dot(cn, wo_ref[ATTN_WIDTH:D_MODEL, :], preferred_element_type=F32))

    def finish(rows, mixed):
        h1 = x_ref[rows, :] + mixed
        h1_ref[rows, :] = h1
        hn = _rms(h1, fw_ref[...])
        hn_ref[rows, :] = _pack_bf16_pairs(hn)
        lg_ref[rows, :] = jnp.dot(hn.astype(BF16), wr_ref[...], preferred_element_type=F32) + br_ref[...]

    mixed = mix(blocks[0])
    for k, rows in enumerate(blocks):
        following = mix(blocks[k + 1]) if k + 1 < len(blocks) else None
        finish(rows, mixed)
        mixed = following


def _outproj(a, c, x2d, aw, cw, wo_bf16, fw, w_router, b_router):
    t = x2d.shape[0]
    tm = OUTPROJ_TM
    row = lambda i: (i, 0)
    fixed = lambda i: (0, 0)
    return pl.pallas_call(
        _outproj_body,
        grid=(t // tm,),
        in_specs=[pl.BlockSpec((tm, ATTN_WIDTH), row),
                  pl.BlockSpec((tm, CONV_WIDTH), row),
                  pl.BlockSpec((tm, D_MODEL), row),
                  pl.BlockSpec((1, ATTN_WIDTH), fixed),
                  pl.BlockSpec((1, CONV_WIDTH), fixed),
                  pl.BlockSpec((D_MODEL, D_MODEL), fixed, pipeline_mode=pl.Buffered(1)),
                  pl.BlockSpec((1, D_MODEL), fixed),
                  pl.BlockSpec((D_MODEL, LANES), fixed),
                  pl.BlockSpec((1, LANES), fixed)],
        out_specs=[pl.BlockSpec((tm, D_MODEL), row),
                   pl.BlockSpec((tm, D_MODEL // 2), row),
                   pl.BlockSpec((tm, LANES), row)],
        out_shape=[jax.ShapeDtypeStruct((t, D_MODEL), F32),
                   jax.ShapeDtypeStruct((t, D_MODEL // 2), jnp.uint32),
                   jax.ShapeDtypeStruct((t, LANES), F32)],
        compiler_params=_params("arbitrary"),
        name="outproj",
    )(a, c, x2d, aw.reshape(1, -1), cw.reshape(1, -1), wo_bf16, fw.reshape(1, -1), w_router, b_router)


def _route_body(lg_ref, idx_ref, wt_ref, cnt_ref, run_ref, seg_ref):
    final_pass = pl.program_id(0) == 1
    first_tile = pl.program_id(1) == 0

    @pl.when(first_tile & jnp.logical_not(final_pass))
    def _():
        run_ref[...] = jnp.zeros_like(run_ref)
        seg_ref[...] = jnp.zeros_like(seg_ref)

    @pl.when(first_tile & final_pass)
    def _():
        counts = run_ref[...]
        rb = float(EXPERT_ROW_BLOCK)
        padded = jnp.ceil(counts * (1.0 / rb)) * rb
        lane8 = lax.broadcasted_iota(jnp.int32, counts.shape, 1)
        ends = padded
        shift = 1
        while shift < LANES:
            ends = ends + jnp.where(lane8 >= shift, pltpu.roll(ends, shift, 1), 0.0)
            shift *= 2
        seg_ref[...] = ends - padded
        run_ref[...] = jnp.zeros_like(run_ref)

    logits = lg_ref[...]
    tm = logits.shape[0]
    lane = lax.broadcasted_iota(jnp.int32, (tm, LANES), 1)
    neg = -jnp.inf

    def first_argmax(v):
        m = jnp.max(v, axis=-1, keepdims=True)
        first = jnp.min(jnp.where(v == m, lane.astype(F32), float(LANES)), axis=-1, keepdims=True)
        return m, first.astype(jnp.int32)

    gl = jnp.where(lane < N_GROUPS, logits, neg)
    gmax, gidx = first_argmax(gl)
    g_w = 1.0 / jnp.sum(jnp.exp(gl - gmax), axis=-1, keepdims=True)
    first = N_GROUPS + gidx * EXPERTS_PER_GROUP
    el = jnp.where((lane >= first) & (lane < first + EXPERTS_PER_GROUP), logits, neg)
    m0, j0 = first_argmax(el)
    m1, j1 = first_argmax(jnp.where(lane == j0, neg, el))
    p1 = jnp.exp(m1 - m0)
    w0 = g_w / (1.0 + p1)
    w1 = g_w * p1 / (1.0 + p1)
    e0 = j0 - N_GROUPS
    e1 = j1 - N_GROUPS

    onehot = ((lane == e0) | (lane == e1)).astype(BF16)
    seen = run_ref[0:1, :]
    run = seen + jnp.sum(onehot.astype(F32), axis=0, keepdims=True)
    run_ref[...] = jnp.broadcast_to(run, run_ref.shape)

    @pl.when(final_pass)
    def _():
        tri = (lax.broadcasted_iota(jnp.int32, (tm, tm), 0)
               > lax.broadcasted_iota(jnp.int32, (tm, tm), 1)).astype(BF16)
        place = jnp.dot(tri, onehot, preferred_element_type=F32) + seen + seg_ref[0:1, :]
        d0 = jnp.sum(jnp.where(lane == e0, place, 0.0), axis=-1, keepdims=True).astype(jnp.int32)
        d1 = jnp.sum(jnp.where(lane == e1, place, 0.0), axis=-1, keepdims=True).astype(jnp.int32)
        idx_ref[...] = jnp.where(lane == 0, d0, jnp.where(lane == 1, d1, jnp.zeros_like(lane)))
        wt_ref[...] = jnp.where(lane == 0, w0, jnp.where(lane == 1, w1, 0.0))
        cnt_ref[...] = jnp.broadcast_to(run, cnt_ref.shape)


def _route(logits):
    t = logits.shape[0]
    tm = ROUTE_TM
    return pl.pallas_call(
        _route_body,
        grid=(2, t // tm),
        in_specs=[pl.BlockSpec((tm, LANES), lambda p, i: (i, 0))],
        out_specs=[pl.BlockSpec((tm, LANES), lambda p, i: (i * p, 0)),
                   pl.BlockSpec((tm, LANES), lambda p, i: (i * p, 0)),
                   pl.BlockSpec((8, LANES), lambda p, i: (0, 0))],
        out_shape=[jax.ShapeDtypeStruct((t, LANES), jnp.int32),
                   jax.ShapeDtypeStruct((t, LANES), F32),
                   jax.ShapeDtypeStruct((8, LANES), F32)],
        scratch_shapes=[pltpu.VMEM((8, LANES), F32)] * 2,
        compiler_params=_params("arbitrary", "arbitrary"),
        name="route",
    )(logits)


def _expert_body(sbe, sbs, sbn, sbr, dest, seg_fill, tail, hn_hbm, wg_ref, wu_ref, wd_ref,
                 y_hbm, x32, xb, acc, tok, gsem, osem):
    s = pl.program_id(0)
    c = pl.program_id(1)
    nsb = pl.num_programs(0)
    nch = EXPERT_FF // EXPERT_FF_CHUNK
    rb = EXPERT_ROW_BLOCK
    half = D_MODEL // 2
    n = sbn[s]
    start = sbs[s]
    nblk = n // rb
    slot = s % 2

    def for_each(count, fn):
        def body(i, _):
            fn(i)
            return 0
        lax.fori_loop(0, count, body, 0)

    def block_rows(j):
        return pl.ds(pl.multiple_of(j * rb, rb), rb)

    def gather_row(sb_first, buf, i):
        src = hn_hbm.at[pl.ds(tok[sb_first + i], 1)]
        pltpu.make_async_copy(src, x32.at[buf, pl.ds(i, 1)], gsem.at[buf]).start()

    def gather_rows(sb_first, buf, first_row, count):
        def group(g):
            for k in range(GATHER_UNROLL):
                gather_row(sb_first, buf, first_row + g * GATHER_UNROLL + k)
        for_each(count // GATHER_UNROLL, group)

    def wait_gathered(buf, count):
        rows = pl.ds(0, pl.multiple_of(count, GATHER_UNROLL))
        pltpu.make_async_copy(hn_hbm.at[rows], x32.at[buf, rows], gsem.at[buf]).wait()

    def out_copy(first_row, j):
        dst = pl.ds(pl.multiple_of(first_row + j * rb, rb), rb)
        return pltpu.make_async_copy(acc.at[block_rows(j)], y_hbm.at[dst], osem)

    @pl.when((s == 0) & (c == 0))
    def _():
        def clear(j):
            for buf in range(2):
                x32[buf, block_rows(j), :] = jnp.zeros((rb, half), jnp.uint32)
            acc[block_rows(j), :] = jnp.zeros((rb, D_MODEL), F32)
        for_each(EXPERT_CAP // rb, clear)

        def pad_rows(e):
            for k in range(GATHER_UNROLL - 1):
                tok[jnp.minimum(seg_fill[e] + k, tok.shape[0] - 1)] = 0
        for_each(N_EXPERTS, pad_rows)

        def invert(g):
            for k in range(GATHER_UNROLL):
                tok[dest[g * GATHER_UNROLL + k]] = g * (GATHER_UNROLL // TOP_K) + k // TOP_K
        for_each(dest.shape[0] // GATHER_UNROLL, invert)

        gather_rows(sbs[0], 0, 0, sbr[0])

    prev = jnp.maximum(s - 1, 0)
    nxt = jnp.minimum(s + 1, nsb - 1)
    prev_rows = jnp.where(s > 0, sbn[prev], 0)
    prev_pending = (c == 0) & (prev_rows > 0)

    @pl.when((c == 0) & (sbr[s] > 0))
    def _():
        wait_gathered(slot, sbr[s])

    def wait_prev_output():
        for_each(prev_rows // rb, lambda j: out_copy(sbs[prev], j).wait())

    @pl.when(prev_pending & (n == 0))
    def _():
        wait_prev_output()

    @pl.when((s == nsb - 1) & (c == 0))
    def _():
        first = tail[0]
        nfill = (y_hbm.shape[0] - first) // rb
        acc[0:rb, :] = jnp.zeros((rb, D_MODEL), F32)

        def fill_copy(j):
            dst = pl.ds(pl.multiple_of(first + j * rb, rb), rb)
            return pltpu.make_async_copy(acc.at[0:rb], y_hbm.at[dst], osem)

        for_each(nfill, lambda j: fill_copy(j).start())
        for_each(nfill, lambda j: fill_copy(j).wait())

    @pl.when(n > 0)
    def _():
        @pl.when(c == 0)
        def _():
            def unpack(j):
                rows = block_rows(j)
                lo, hi = _unpack_bf16_pairs(x32[slot, rows, :])
                xb[rows, 0:half] = lo
                xb[rows, half:D_MODEL] = hi
            for_each(nblk, unpack)

            @pl.when((s + 1 < nsb) & (sbr[nxt] > 0))
            def _():
                gather_rows(sbs[nxt], 1 - slot, 0, sbr[nxt])

            @pl.when(prev_pending)
            def _():
                wait_prev_output()

        def mlp(first_row, m):
            rows = pl.ds(pl.multiple_of(first_row, rb), m)
            x = xb[rows, :]
            g = jnp.dot(x, wg_ref[0].astype(BF16), preferred_element_type=F32)
            u = jnp.dot(x, wu_ref[0].astype(BF16), preferred_element_type=F32)
            h = (jax.nn.silu(g) * u).astype(BF16)
            y = jnp.dot(h, wd_ref[0].astype(BF16), preferred_element_type=F32)
            acc[rows, :] = jnp.where(c == 0, y, acc[rows, :] + y)

        done = 0
        for m in EXPERT_BLOCKS[:-1]:
            count = (n - done) // m
            for_each(count, lambda j, done=done, m=m: mlp(done + j * m, m))
            done = done + count * m

        @pl.when(done < n)
        def _():
            mlp(done, rb)

        @pl.when(c == nch - 1)
        def _():
            for_each(nblk, lambda j: out_copy(start, j).start())


def _experts(hn_packed, w_gate, w_up, w_down, sb_expert, sb_start, sb_rows, sb_real, dest, seg_fill, tail,
             n_rows):
    n_sb = sb_expert.shape[0]
    nch = EXPERT_FF // EXPERT_FF_CHUNK
    fc = EXPERT_FF_CHUNK

    def chunk(s, c, sbn):
        return jnp.where(sbn[s] > 0, c, nch - 1)

    def up_map(s, c, sbe, sbs, sbn, *_):
        return (sbe[s], 0, chunk(s, c, sbn))

    def down_map(s, c, sbe, sbs, sbn, *_):
        return (sbe[s], chunk(s, c, sbn), 0)

    grid_spec = pltpu.PrefetchScalarGridSpec(
        num_scalar_prefetch=7,
        grid=(n_sb, nch),
        in_specs=[pl.BlockSpec(memory_space=pl.ANY),
                  pl.BlockSpec((1, D_MODEL, fc), up_map),
                  pl.BlockSpec((1, D_MODEL, fc), up_map),
                  pl.BlockSpec((1, fc, D_MODEL), down_map)],
        out_specs=pl.BlockSpec(memory_space=pl.ANY),
        scratch_shapes=[pltpu.VMEM((2, EXPERT_CAP, D_MODEL // 2), jnp.uint32),
                        pltpu.VMEM((EXPERT_CAP, D_MODEL), BF16),
                        pltpu.VMEM((EXPERT_CAP, D_MODEL), F32),
                        pltpu.SMEM((n_rows,), jnp.int32),
                        pltpu.SemaphoreType.DMA((2,)),
                        pltpu.SemaphoreType.DMA(())],
    )
    return pl.pallas_call(
        _expert_body,
        grid_spec=grid_spec,
        out_shape=jax.ShapeDtypeStruct((n_rows, D_MODEL), F32),
        compiler_params=_params("arbitrary", "arbitrary"),
        name="experts",
    )(sb_expert, sb_start, sb_rows, sb_real, dest, seg_fill, tail, hn_packed, w_gate, w_up, w_down)


def _combine_body(dest, h1_ref, wt_ref, y_hbm, o_ref, g, sem):
    i = pl.program_id(0)
    nsteps = pl.num_programs(0)
    tm = h1_ref.shape[0]

    def issue(step, slot):
        def f(q, _):
            for u in range(GATHER_UNROLL):
                r = q * GATHER_UNROLL + u
                for k in range(TOP_K):
                    src = y_hbm.at[pl.ds(dest[(step * tm + r) * TOP_K + k], 1)]
                    pltpu.make_async_copy(src, g.at[slot, k, pl.ds(r, 1)], sem.at[slot]).start()
            return 0
        lax.fori_loop(0, tm // GATHER_UNROLL, f, 0)

    @pl.when(i == 0)
    def _():
        issue(0, 0)

    @pl.when(i + 1 < nsteps)
    def _():
        issue(i + 1, (i + 1) % 2)

    slot = i % 2
    for k in range(TOP_K):
        pltpu.make_async_copy(y_hbm.at[pl.ds(0, tm)], g.at[slot, k], sem.at[slot]).wait()

    w = wt_ref[...]
    o_ref[...] = h1_ref[...] + (w[:, 0:1] * g[slot, 0] + w[:, 1:2] * g[slot, 1])


def _combine(h1, wts, y_buf, dest_flat):
    t = h1.shape[0]
    tm = COMBINE_TM
    grid_spec = pltpu.PrefetchScalarGridSpec(
        num_scalar_prefetch=1,
        grid=(t // tm,),
        in_specs=[pl.BlockSpec((tm, D_MODEL), lambda i, d: (i, 0)),
                  pl.BlockSpec((tm, LANES), lambda i, d: (i, 0)),
                  pl.BlockSpec(memory_space=pl.ANY)],
        out_specs=pl.BlockSpec((tm, D_MODEL), lambda i, d: (i, 0)),
        scratch_shapes=[pltpu.VMEM((2, TOP_K, tm, D_MODEL), F32),
                        pltpu.SemaphoreType.DMA((2,))],
    )
    return pl.pallas_call(
        _combine_body,
        grid_spec=grid_spec,
        out_shape=jax.ShapeDtypeStruct((t, D_MODEL), F32),
        compiler_params=_params("arbitrary"),
        name="combine",
    )(dest_flat, h1, wts, y_buf)


def _dispatch_tables(idx, cnt, t):
    rb, cap = EXPERT_ROW_BLOCK, EXPERT_CAP
    n_assign = t * TOP_K
    n_rows = -(-(n_assign + N_EXPERTS * (rb - 1)) // rb) * rb
    n_sb = (n_rows + N_EXPERTS * (cap - rb)) // cap
    dest = idx[:, 0:TOP_K].reshape(-1)
    counts = cnt[0, :N_EXPERTS].astype(jnp.int32)
    padded = (counts + rb - 1) // rb * rb
    seg_end = jnp.cumsum(padded)
    seg_start = (seg_end - padded).astype(jnp.int32)
    seg_fill = (seg_start + counts).astype(jnp.int32)

    sb_per_expert = (padded + cap - 1) // cap
    sb_end = jnp.cumsum(sb_per_expert)
    total = sb_end[-1]
    s = jnp.arange(n_sb, dtype=jnp.int32)
    s_eff = jnp.minimum(s, total - 1)
    e = jnp.minimum(jnp.sum(sb_end[None, :] <= s_eff[:, None], axis=1), N_EXPERTS - 1).astype(jnp.int32)
    local = s_eff - (sb_end[e] - sb_per_expert[e])
    sb_start = (seg_start[e] + local * cap).astype(jnp.int32)
    sb_rows = jnp.where(s < total, jnp.clip(padded[e] - local * cap, 0, cap), 0).astype(jnp.int32)
    gathered = (counts + GATHER_UNROLL - 1) // GATHER_UNROLL * GATHER_UNROLL
    sb_real = jnp.where(s < total, jnp.clip(gathered[e] - local * cap, 0, cap), 0).astype(jnp.int32)
    tail = seg_end[-1:].astype(jnp.int32)
    return dest, seg_fill, e, sb_start, sb_rows, sb_real, tail, n_rows


def kernel(x, meta_tokens, mix_norm_w, w_in, q_norm_w, k_norm_w, rel_bias, meta_bias, conv_w,
           attn_out_norm_w, conv_out_norm_w, w_out, ffn_norm_w, w_router_group, b_router_group,
           w_router_expert, b_router_expert, w_gate, w_up, w_down):
    bsz, seq, d = x.shape
    depth = mix_norm_w.shape[0]
    assert depth == 1 and d == D_MODEL and seq % GRID_W == 0
    t = bsz * seq
    x2d = x.reshape(t, d)
    l = 0

    w_in_b = w_in[l].astype(BF16)
    proj = _inproj(x2d, mix_norm_w[l], w_in_b, INPROJ_TM)
    proj_meta = _inproj(meta_tokens.astype(x.dtype), mix_norm_w[l], w_in_b, N_META)

    a, c = _mixers(proj, proj_meta, q_norm_w[l], k_norm_w[l], rel_bias[l], meta_bias[l], conv_w[l], bsz, seq)

    spare = LANES - N_GROUPS - N_EXPERTS
    w_router = jnp.concatenate([w_router_group[l].astype(F32), w_router_expert[l].astype(F32),
                                jnp.zeros((d, spare), F32)], axis=1)
    b_router = jnp.concatenate([b_router_group[l].astype(F32), b_router_expert[l].astype(F32),
                                jnp.zeros((spare,), F32)]).reshape(1, LANES)
    h1, hn, logits = _outproj(a, c, x2d, attn_out_norm_w[l], conv_out_norm_w[l], w_out[l].astype(BF16),
                              ffn_norm_w[l], w_router.astype(BF16), b_router)

    idx, wts, cnt = _route(logits)
    dest, seg_fill, sb_expert, sb_start, sb_rows, sb_real, tail, n_rows = _dispatch_tables(idx, cnt, t)
    y_buf = _experts(hn, w_gate.reshape(N_EXPERTS, d, EXPERT_FF), w_up.reshape(N_EXPERTS, d, EXPERT_FF),
                     w_down.reshape(N_EXPERTS, EXPERT_FF, d), sb_expert, sb_start, sb_rows, sb_real,
                     dest, seg_fill, tail, n_rows)
    out = _combine(h1, wts, y_buf, dest)
    return out.reshape(bsz, seq, d)
```

```python
import functools

import jax
import jax.numpy as jnp
from jax import lax
from jax.experimental import pallas as pl
from jax.experimental.pallas import tpu as pltpu

F32 = jnp.float32
BF16 = jnp.bfloat16

D_MODEL = 2048
N_META = 16
GRID_W = 64
N_HEADS = 16
HEAD_DIM = 64
ATTN_WIDTH = N_HEADS * HEAD_DIM
CONV_WIDTH = D_MODEL - ATTN_WIDTH
PROJ_TOTAL = 3 * ATTN_WIDTH + 3 * CONV_WIDTH
WIN_ROWS = 8
WIN_COLS = 16
N_GROUPS = 4
EXPERTS_PER_GROUP = 8
N_EXPERTS = N_GROUPS * EXPERTS_PER_GROUP
TOP_K = 2
EXPERT_FF = 1024
EPS = 1e-6

LANES = 128
SUBLANES = 8
VMEM_LIMIT = 52 * 1024 * 1024

ATTN_ROWS_PER_STEP = 4
INPROJ_TM = 1024
INPROJ_TN = 1024
OUTPROJ_TM = 512
OUTPROJ_SUB = 256
ROUTE_TM = 1024
EXPERT_ROW_BLOCK = 128
EXPERT_BLOCKS = (512, 256, 128)
EXPERT_CAP = 1024
EXPERT_FF_CHUNK = 512
GATHER_UNROLL = 8
COMBINE_TM = 512


def _params(*sem):
    return pltpu.CompilerParams(dimension_semantics=sem, vmem_limit_bytes=VMEM_LIMIT)


def _inproj_body(x_ref, nw_ref, w_ref, o_ref, xn_ref):
    @pl.when(pl.program_id(1) == 0)
    def _():
        x = x_ref[...]
        ms = jnp.mean(x * x, axis=-1, keepdims=True)
        xn_ref[...] = (x * lax.rsqrt(ms + EPS) * nw_ref[...]).astype(BF16)

    o_ref[...] = jnp.dot(xn_ref[...], w_ref[...], preferred_element_type=F32)


def _inproj(x2d, norm_w, w_bf16, tm):
    m = x2d.shape[0]
    tn = INPROJ_TN
    return pl.pallas_call(
        _inproj_body,
        grid=(m // tm, PROJ_TOTAL // tn),
        in_specs=[pl.BlockSpec((tm, D_MODEL), lambda i, j: (i, 0)),
                  pl.BlockSpec((1, D_MODEL), lambda i, j: (0, 0)),
                  pl.BlockSpec((D_MODEL, tn), lambda i, j: (0, j))],
        out_specs=pl.BlockSpec((tm, tn), lambda i, j: (i, j)),
        out_shape=jax.ShapeDtypeStruct((m, PROJ_TOTAL), F32),
        scratch_shapes=[pltpu.VMEM((tm, D_MODEL), BF16)],
        compiler_params=_params("arbitrary", "arbitrary"),
        name="inproj",
    )(x2d, norm_w.reshape(1, D_MODEL), w_bf16)


def _head_norm(x, w, lo):
    x2 = x * x
    s_lo = jnp.sum(jnp.where(lo, x2, 0.0), axis=-1, keepdims=True)
    s_hi = jnp.sum(jnp.where(lo, 0.0, x2), axis=-1, keepdims=True)
    ms = jnp.where(lo, s_lo, s_hi) * (1.0 / HEAD_DIM)
    return x * lax.rsqrt(ms + EPS) * w


def _attn_body(q_ref, k_ref, v_ref, km_ref, vm_ref, qw_ref, kw_ref, bias_ref, mb_ref,
               gb_ref, gc_ref, hc_ref, gcm_ref, hcm_ref, cw_ref, o_ref, c_ref,
               qs, ks, vs, sc, smc, pr, pmr, *, rows):
    _conv_body(gb_ref, gc_ref, hc_ref, gcm_ref, hcm_ref, cw_ref, c_ref)
    lo = lax.broadcasted_iota(jnp.int32, (1, LANES), 1) < HEAD_DIM
    scale = HEAD_DIM ** -0.5
    chunk = 256
    seq = rows * GRID_W

    kmb = _head_norm(km_ref[...], kw_ref[...], lo).astype(BF16)
    vmb = vm_ref[...].astype(BF16)
    contract_last = (((1,), (1,)), ((), ()))
    head_masks = (lo, jnp.logical_not(lo))

    def one_head(x, h):
        return jnp.where(head_masks[h], x, jnp.zeros_like(x))

    def prep(i, _):
        sl = pl.ds(pl.multiple_of(i * chunk, chunk), chunk)
        qs[sl, :] = (_head_norm(q_ref[sl, :], qw_ref[...], lo) * scale).astype(BF16)
        ks[sl, :] = _head_norm(k_ref[sl, :], kw_ref[...], lo).astype(BF16)
        vs[sl, :] = v_ref[sl, :].astype(BF16)
        return 0

    lax.fori_loop(0, seq // chunk, prep, 0, unroll=2)
    wr = min(WIN_ROWS, rows)
    nk = wr * GRID_W

    def window_start(r):
        return jnp.clip(r - wr // 2, 0, rows - wr)

    def row_slice(r, n):
        return pl.ds(pl.multiple_of(r * GRID_W, GRID_W), n)

    def store_scores(r, s_ref, sm_ref):
        rs = window_start(r)
        si = rs - r + (WIN_ROWS - 1)
        q_r = qs[row_slice(r, GRID_W), :]
        kwin = ks[row_slice(rs, nk), :]
        for h in range(2):
            qh = one_head(q_r, h)
            bias = jnp.concatenate([bias_ref[h, si + w] for w in range(0, wr, 2)], axis=-1)
            s_ref[h] = lax.dot_general(qh, kwin, contract_last, preferred_element_type=F32) + bias
            sm_ref[h] = lax.dot_general(qh, kmb, contract_last, preferred_element_type=F32) + mb_ref[h]

    def store_softmax(s_ref, sm_ref, p_ref, pm_ref):
        for h in range(2):
            s = s_ref[h]
            sm = sm_ref[h]
            m = jnp.maximum(jnp.max(s, axis=-1, keepdims=True), jnp.max(sm, axis=-1, keepdims=True))
            p = jnp.exp(s - m)
            pm = jnp.exp(sm - m)
            inv = 1.0 / (jnp.sum(p, axis=-1, keepdims=True) + jnp.sum(pm, axis=-1, keepdims=True))
            p_ref[h] = (p * inv).astype(BF16)
            pm_ref[h] = (pm * inv).astype(BF16)

    def weighted_values(r, p_ref, pm_ref):
        vwin = vs[row_slice(window_start(r), nk), :]
        outs = [jnp.dot(p_ref[h], vwin, preferred_element_type=F32)
                + jnp.dot(pm_ref[h], vmb, preferred_element_type=F32) for h in range(2)]
        o_ref[row_slice(r, GRID_W), :] = jnp.where(lo, outs[0], outs[1])

    per = ATTN_ROWS_PER_STEP
    groups = rows // per

    def step(j, parity, do_scores, do_softmax, do_values):
        if do_scores:
            for k in range(per):
                store_scores(per * j + k, sc.at[parity, k], smc.at[parity, k])
        if do_values:
            for k in range(per):
                weighted_values(per * (j - 2) + k, pr.at[1 - parity, k], pmr.at[1 - parity, k])
        if do_softmax:
            for k in range(per):
                store_softmax(sc.at[1 - parity, k], smc.at[1 - parity, k], pr.at[parity, k], pmr.at[parity, k])

    step(0, 0, True, False, False)
    step(1, 1, True, True, False)

    def two_steps(i, _):
        step(2 * i, 0, True, True, True)
        step(2 * i + 1, 1, True, True, True)
        return 0

    lax.fori_loop(1, groups // 2, two_steps, 0)
    step(groups, 0, False, True, True)
    step(groups + 1, 1, False, False, True)


def _bias_table(rel_bias):
    c = jnp.arange(GRID_W)
    col_start = jnp.clip(c - WIN_COLS // 2, 0, GRID_W - WIN_COLS)
    col_mask = (c[None, :] >= col_start[:, None]) & (c[None, :] < col_start[:, None] + WIN_COLS)
    dc = jnp.clip(c[None, :] - c[:, None], -(WIN_COLS - 1), WIN_COLS - 1) + (WIN_COLS - 1)
    ncol = 2 * WIN_COLS - 1
    pair_bias = jnp.concatenate([rel_bias[:, :-1], rel_bias[:, 1:]], axis=-1).astype(F32)
    onehot = (dc[None] == jnp.arange(ncol)[:, None, None]).astype(F32)
    zeros = jnp.zeros_like(onehot)
    pair_onehot = jnp.concatenate([jnp.concatenate([onehot, zeros], axis=-1),
                                   jnp.concatenate([zeros, onehot], axis=-1)], axis=0)
    table = jnp.einsum('hdm,mqn->hdqn', pair_bias, pair_onehot, precision=lax.Precision.HIGHEST)
    pair_mask = jnp.concatenate([col_mask, col_mask], axis=-1)
    return jnp.where(pair_mask[None, None], table, -jnp.inf)


def _mixers(proj, proj_meta, q_norm_w, k_norm_w, rel_bias, meta_bias, conv_w, bsz, seq):
    rows = seq // GRID_W
    assert rows >= WIN_ROWS and rows % (2 * ATTN_ROWS_PER_STEP) == 0
    nk = WIN_ROWS * GRID_W
    npairs = N_HEADS // 2
    assert CONV_WIDTH // LANES == npairs
    conv_base = 3 * ATTN_WIDTH // LANES
    qw = jnp.tile(q_norm_w.astype(F32), 2).reshape(1, LANES)
    kw = jnp.tile(k_norm_w.astype(F32), 2).reshape(1, LANES)
    bias = _bias_table(rel_bias)
    mb = meta_bias.astype(F32).reshape(N_HEADS, 1, N_META)
    return pl.pallas_call(
        functools.partial(_attn_body, rows=rows),
        grid=(bsz, npairs),
        in_specs=[pl.BlockSpec((seq, LANES), lambda b, p: (b, p)),
                  pl.BlockSpec((seq, LANES), lambda b, p: (b, npairs + p)),
                  pl.BlockSpec((seq, LANES), lambda b, p: (b, 2 * npairs + p)),
                  pl.BlockSpec((N_META, LANES), lambda b, p: (0, npairs + p)),
                  pl.BlockSpec((N_META, LANES), lambda b, p: (0, 2 * npairs + p)),
                  pl.BlockSpec((1, LANES), lambda b, p: (0, 0)),
                  pl.BlockSpec((1, LANES), lambda b, p: (0, 0)),
                  pl.BlockSpec((2, 2 * WIN_ROWS - 2, GRID_W, LANES), lambda b, p: (p, 0, 0, 0)),
                  pl.BlockSpec((2, 1, N_META), lambda b, p: (p, 0, 0)),
                  pl.BlockSpec((seq, LANES), lambda b, p: (b, conv_base + p)),
                  pl.BlockSpec((seq, LANES), lambda b, p: (b, conv_base + npairs + p)),
                  pl.BlockSpec((seq, LANES), lambda b, p: (b, conv_base + 2 * npairs + p)),
                  pl.BlockSpec((N_META, LANES), lambda b, p: (0, conv_base + npairs + p)),
                  pl.BlockSpec((N_META, LANES), lambda b, p: (0, conv_base + 2 * npairs + p)),
                  pl.BlockSpec((3, LANES), lambda b, p: (0, p))],
        out_specs=[pl.BlockSpec((seq, LANES), lambda b, p: (b, p)),
                   pl.BlockSpec((seq, LANES), lambda b, p: (b, p))],
        out_shape=[jax.ShapeDtypeStruct((bsz * seq, ATTN_WIDTH), F32),
                   jax.ShapeDtypeStruct((bsz * seq, CONV_WIDTH), F32)],
        scratch_shapes=([pltpu.VMEM((seq, LANES), BF16)] * 3
                        + [pltpu.VMEM((2, ATTN_ROWS_PER_STEP, 2, GRID_W, nk), F32),
                           pltpu.VMEM((2, ATTN_ROWS_PER_STEP, 2, GRID_W, N_META), F32),
                           pltpu.VMEM((2, ATTN_ROWS_PER_STEP, 2, GRID_W, nk), BF16),
                           pltpu.VMEM((2, ATTN_ROWS_PER_STEP, 2, GRID_W, N_META), BF16)]),
        compiler_params=_params("arbitrary", "arbitrary"),
        name="mixers",
    )(proj, proj, proj, proj_meta, proj_meta, qw, kw, bias, mb,
      proj, proj, proj, proj_meta, proj_meta, conv_w.astype(F32))


def _conv_body(gb_ref, gc_ref, hc_ref, gcm_ref, hcm_ref, w_ref, o_ref):
    seq = gb_ref.shape[0]
    u = gc_ref[...] * hc_ref[...]
    u_meta_last = gcm_ref[N_META - 1:N_META, :] * hcm_ref[N_META - 1:N_META, :]
    row = lax.broadcasted_iota(jnp.int32, (seq, 1), 0)
    u_prev = jnp.where(row == 0, u_meta_last, pltpu.roll(u, 1, 0))
    u_next = jnp.where(row == seq - 1, 0.0, pltpu.roll(u, seq - 1, 0))
    w = w_ref[...]
    y = u_prev * w[0:1] + u * w[1:2] + u_next * w[2:3]
    o_ref[...] = gb_ref[...] * y


def _rms(x, w):
    ms = jnp.mean(x * x, axis=-1, keepdims=True)
    return x * lax.rsqrt(ms + EPS) * w


def _pack_bf16_pairs(x):
    w = x.shape[1] // 2
    lo = lax.bitcast_convert_type(x[:, :w].astype(BF16).astype(F32), jnp.uint32)
    hi = lax.bitcast_convert_type(x[:, w:].astype(BF16).astype(F32), jnp.uint32)
    return (hi & jnp.uint32(0xFFFF0000)) | (lo >> 16)


def _unpack_bf16_pairs(p):
    lo = lax.bitcast_convert_type(p << 16, F32).astype(BF16)
    hi = lax.bitcast_convert_type(p & jnp.uint32(0xFFFF0000), F32).astype(BF16)
    return lo, hi


def _outproj_body(a_ref, c_ref, x_ref, aw_ref, cw_ref, wo_ref, fw_ref, wr_ref, br_ref,
                  h1_ref, hn_ref, lg_ref):
    sub = OUTPROJ_SUB
    blocks = [pl.ds(k * sub, sub) for k in range(a_ref.shape[0] // sub)]

    def mix(rows):
        an = _rms(a_ref[rows, :], aw_ref[...]).astype(BF16)
        cn = _rms(c_ref[rows, :], cw_ref[...]).astype(BF16)
        return (jnp.dot(an, wo_ref[0:ATTN_WIDTH, :], preferred_element_type=F32)
                + jnp.dot(cn, wo_ref[ATTN_WIDTH:D_MODEL, :], preferred_element_type=F32))

    def finish(rows, mixed):
        h1 = x_ref[rows, :] + mixed
        h1_ref[rows, :] = h1
        hn = _rms(h1, fw_ref[...])
        hn_ref[rows, :] = _pack_bf16_pairs(hn)
        lg_ref[rows, :] = jnp.dot(hn.astype(BF16), wr_ref[...], preferred_element_type=F32) + br_ref[...]

    mixed = mix(blocks[0])
    for k, rows in enumerate(blocks):
        following = mix(blocks[k + 1]) if k + 1 < len(blocks) else None
        finish(rows, mixed)
        mixed = following


def _outproj(a, c, x2d, aw, cw, wo_bf16, fw, w_router, b_router):
    t = x2d.shape[0]
    tm = OUTPROJ_TM
    row = lambda i: (i, 0)
    fixed = lambda i: (0, 0)
    return pl.pallas_call(
        _outproj_body,
        grid=(t // tm,),
        in_specs=[pl.BlockSpec((tm, ATTN_WIDTH), row),
                  pl.BlockSpec((tm, CONV_WIDTH), row),
                  pl.BlockSpec((tm, D_MODEL), row),
                  pl.BlockSpec((1, ATTN_WIDTH), fixed),
                  pl.BlockSpec((1, CONV_WIDTH), fixed),
                  pl.BlockSpec((D_MODEL, D_MODEL), fixed, pipeline_mode=pl.Buffered(1)),
                  pl.BlockSpec((1, D_MODEL), fixed),
                  pl.BlockSpec((D_MODEL, LANES), fixed),
                  pl.BlockSpec((1, LANES), fixed)],
        out_specs=[pl.BlockSpec((tm, D_MODEL), row),
                   pl.BlockSpec((tm, D_MODEL // 2), row),
                   pl.BlockSpec((tm, LANES), row)],
        out_shape=[jax.ShapeDtypeStruct((t, D_MODEL), F32),
                   jax.ShapeDtypeStruct((t, D_MODEL // 2), jnp.uint32),
                   jax.ShapeDtypeStruct((t, LANES), F32)],
        compiler_params=_params("arbitrary"),
        name="outproj",
    )(a, c, x2d, aw.reshape(1, -1), cw.reshape(1, -1), wo_bf16, fw.reshape(1, -1), w_router, b_router)


def _route_body(lg_ref, idx_ref, wt_ref, cnt_ref, run_ref, seg_ref):
    final_pass = pl.program_id(0) == 1
    first_tile = pl.program_id(1) == 0

    @pl.when(first_tile & jnp.logical_not(final_pass))
    def _():
        run_ref[...] = jnp.zeros_like(run_ref)
        seg_ref[...] = jnp.zeros_like(seg_ref)

    @pl.when(first_tile & final_pass)
    def _():
        counts = run_ref[...]
        rb = float(EXPERT_ROW_BLOCK)
        padded = jnp.ceil(counts * (1.0 / rb)) * rb
        lane8 = lax.broadcasted_iota(jnp.int32, counts.shape, 1)
        ends = padded
        shift = 1
        while shift < LANES:
            ends = ends + jnp.where(lane8 >= shift, pltpu.roll(ends, shift, 1), 0.0)
            shift *= 2
        seg_ref[...] = ends - padded
        run_ref[...] = jnp.zeros_like(run_ref)

    logits = lg_ref[...]
    tm = logits.shape[0]
    lane = lax.broadcasted_iota(jnp.int32, (tm, LANES), 1)
    neg = -jnp.inf

    def first_argmax(v):
        m = jnp.max(v, axis=-1, keepdims=True)
        first = jnp.min(jnp.where(v == m, lane.astype(F32), float(LANES)), axis=-1, keepdims=True)
        return m, first.astype(jnp.int32)

    gl = jnp.where(lane < N_GROUPS, logits, neg)
    gmax, gidx = first_argmax(gl)
    g_w = 1.0 / jnp.sum(jnp.exp(gl - gmax), axis=-1, keepdims=True)
    first = N_GROUPS + gidx * EXPERTS_PER_GROUP
    el = jnp.where((lane >= first) & (lane < first + EXPERTS_PER_GROUP), logits, neg)
    m0, j0 = first_argmax(el)
    m1, j1 = first_argmax(jnp.where(lane == j0, neg, el))
    p1 = jnp.exp(m1 - m0)
    w0 = g_w / (1.0 + p1)
    w1 = g_w * p1 / (1.0 + p1)
    e0 = j0 - N_GROUPS
    e1 = j1 - N_GROUPS

    onehot = ((lane == e0) | (lane == e1)).astype(BF16)
    seen = run_ref[0:1, :]
    run = seen + jnp.sum(onehot.astype(F32), axis=0, keepdims=True)
    run_ref[...] = jnp.broadcast_to(run, run_ref.shape)

    @pl.when(final_pass)
    def _():
        tri = (lax.broadcasted_iota(jnp.int32, (tm, tm), 0)
               > lax.broadcasted_iota(jnp.int32, (tm, tm), 1)).astype(BF16)
        place = jnp.dot(tri, onehot, preferred_element_type=F32) + seen + seg_ref[0:1, :]
        d0 = jnp.sum(jnp.where(lane == e0, place, 0.0), axis=-1, keepdims=True).astype(jnp.int32)
        d1 = jnp.sum(jnp.where(lane == e1, place, 0.0), axis=-1, keepdims=True).astype(jnp.int32)
        idx_ref[...] = jnp.where(lane == 0, d0, jnp.where(lane == 1, d1, jnp.zeros_like(lane)))
        wt_ref[...] = jnp.where(lane == 0, w0, jnp.where(lane == 1, w1, 0.0))
        cnt_ref[...] = jnp.broadcast_to(run, cnt_ref.shape)


def _route(logits):
    t = logits.shape[0]
    tm = ROUTE_TM
    return pl.pallas_call(
        _route_body,
        grid=(2, t // tm),
        in_specs=[pl.BlockSpec((tm, LANES), lambda p, i: (i, 0))],
        out_specs=[pl.BlockSpec((tm, LANES), lambda p, i: (i * p, 0)),
                   pl.BlockSpec((tm, LANES), lambda p, i: (i * p, 0)),
                   pl.BlockSpec((8, LANES), lambda p, i: (0, 0))],
        out_shape=[jax.ShapeDtypeStruct((t, LANES), jnp.int32),
                   jax.ShapeDtypeStruct((t, LANES), F32),
                   jax.ShapeDtypeStruct((8, LANES), F32)],
        scratch_shapes=[pltpu.VMEM((8, LANES), F32)] * 2,
        compiler_params=_params("arbitrary", "arbitrary"),
        name="route",
    )(logits)


def _expert_body(sbe, sbs, sbn, sbr, dest, seg_fill, tail, hn_hbm, wg_ref, wu_ref, wd_ref,
                 y_hbm, x32, xb, acc, tok, gsem, osem):
    s = pl.program_id(0)
    c = pl.program_id(1)
    nsb = pl.num_programs(0)
    nch = EXPERT_FF // EXPERT_FF_CHUNK
    rb = EXPERT_ROW_BLOCK
    half = D_MODEL // 2
    n = sbn[s]
    start = sbs[s]
    nblk = n // rb
    slot = s % 2

    def for_each(count, fn):
        def body(i, _):
            fn(i)
            return 0
        lax.fori_loop(0, count, body, 0)

    def block_rows(j):
        return pl.ds(pl.multiple_of(j * rb, rb), rb)

    def gather_row(sb_first, buf, i):
        src = hn_hbm.at[pl.ds(tok[sb_first + i], 1)]
        pltpu.make_async_copy(src, x32.at[buf, pl.ds(i, 1)], gsem.at[buf]).start()

    def gather_rows(sb_first, buf, first_row, count):
        def group(g):
            for k in range(GATHER_UNROLL):
                gather_row(sb_first, buf, first_row + g * GATHER_UNROLL + k)
        for_each(count // GATHER_UNROLL, group)

    def wait_gathered(buf, count):
        rows = pl.ds(0, pl.multiple_of(count, GATHER_UNROLL))
        pltpu.make_async_copy(hn_hbm.at[rows], x32.at[buf, rows], gsem.at[buf]).wait()

    def out_copy(first_row, j):
        dst = pl.ds(pl.multiple_of(first_row + j * rb, rb), rb)
        return pltpu.make_async_copy(acc.at[block_rows(j)], y_hbm.at[dst], osem)

    @pl.when((s == 0) & (c == 0))
    def _():
        def clear(j):
            for buf in range(2):
                x32[buf, block_rows(j), :] = jnp.zeros((rb, half), jnp.uint32)
            acc[block_rows(j), :] = jnp.zeros((rb, D_MODEL), F32)
        for_each(EXPERT_CAP // rb, clear)

        def pad_rows(e):
            for k in range(GATHER_UNROLL - 1):
                tok[jnp.minimum(seg_fill[e] + k, tok.shape[0] - 1)] = 0
        for_each(N_EXPERTS, pad_rows)

        def invert(g):
            for k in range(GATHER_UNROLL):
                tok[dest[g * GATHER_UNROLL + k]] = g * (GATHER_UNROLL // TOP_K) + k // TOP_K
        for_each(dest.shape[0] // GATHER_UNROLL, invert)

        gather_rows(sbs[0], 0, 0, sbr[0])

    prev = jnp.maximum(s - 1, 0)
    nxt = jnp.minimum(s + 1, nsb - 1)
    prev_rows = jnp.where(s > 0, sbn[prev], 0)
    prev_pending = (c == 0) & (prev_rows > 0)

    @pl.when((c == 0) & (sbr[s] > 0))
    def _():
        wait_gathered(slot, sbr[s])

    def wait_prev_output():
        for_each(prev_rows // rb, lambda j: out_copy(sbs[prev], j).wait())

    @pl.when(prev_pending & (n == 0))
    def _():
        wait_prev_output()

    @pl.when((s == nsb - 1) & (c == 0))
    def _():
        first = tail[0]
        nfill = (y_hbm.shape[0] - first) // rb
        acc[0:rb, :] = jnp.zeros((rb, D_MODEL), F32)

        def fill_copy(j):
            dst = pl.ds(pl.multiple_of(first + j * rb, rb), rb)
            return pltpu.make_async_copy(acc.at[0:rb], y_hbm.at[dst], osem)

        for_each(nfill, lambda j: fill_copy(j).start())
        for_each(nfill, lambda j: fill_copy(j).wait())

    @pl.when(n > 0)
    def _():
        @pl.when(c == 0)
        def _():
            def unpack(j):
                rows = block_rows(j)
                lo, hi = _unpack_bf16_pairs(x32[slot, rows, :])
                xb[rows, 0:half] = lo
                xb[rows, half:D_MODEL] = hi
            for_each(nblk, unpack)

            @pl.when((s + 1 < nsb) & (sbr[nxt] > 0))
            def _():
                gather_rows(sbs[nxt], 1 - slot, 0, sbr[nxt])

            @pl.when(prev_pending)
            def _():
                wait_prev_output()

        def mlp(first_row, m):
            rows = pl.ds(pl.multiple_of(first_row, rb), m)
            x = xb[rows, :]
            g = jnp.dot(x, wg_ref[0].astype(BF16), preferred_element_type=F32)
            u = jnp.dot(x, wu_ref[0].astype(BF16), preferred_element_type=F32)
            h = (jax.nn.silu(g) * u).astype(BF16)
            y = jnp.dot(h, wd_ref[0].astype(BF16), preferred_element_type=F32)
            acc[rows, :] = jnp.where(c == 0, y, acc[rows, :] + y)

        done = 0
        for m in EXPERT_BLOCKS[:-1]:
            count = (n - done) // m
            for_each(count, lambda j, done=done, m=m: mlp(done + j * m, m))
            done = done + count * m

        @pl.when(done < n)
        def _():
            mlp(done, rb)

        @pl.when(c == nch - 1)
        def _():
            for_each(nblk, lambda j: out_copy(start, j).start())


def _experts(hn_packed, w_gate, w_up, w_down, sb_expert, sb_start, sb_rows, sb_real, dest, seg_fill, tail,
             n_rows):
    n_sb = sb_expert.shape[0]
    nch = EXPERT_FF // EXPERT_FF_CHUNK
    fc = EXPERT_FF_CHUNK

    def chunk(s, c, sbn):
        return jnp.where(sbn[s] > 0, c, nch - 1)

    def up_map(s, c, sbe, sbs, sbn, *_):
        return (sbe[s], 0, chunk(s, c, sbn))

    def down_map(s, c, sbe, sbs, sbn, *_):
        return (sbe[s], chunk(s, c, sbn), 0)

    grid_spec = pltpu.PrefetchScalarGridSpec(
        num_scalar_prefetch=7,
        grid=(n_sb, nch),
        in_specs=[pl.BlockSpec(memory_space=pl.ANY),
                  pl.BlockSpec((1, D_MODEL, fc), up_map),
                  pl.BlockSpec((1, D_MODEL, fc), up_map),
                  pl.BlockSpec((1, fc, D_MODEL), down_map)],
        out_specs=pl.BlockSpec(memory_space=pl.ANY),
        scratch_shapes=[pltpu.VMEM((2, EXPERT_CAP, D_MODEL // 2), jnp.uint32),
                        pltpu.VMEM((EXPERT_CAP, D_MODEL), BF16),
                        pltpu.VMEM((EXPERT_CAP, D_MODEL), F32),
                        pltpu.SMEM((n_rows,), jnp.int32),
                        pltpu.SemaphoreType.DMA((2,)),
                        pltpu.SemaphoreType.DMA(())],
    )
    return pl.pallas_call(
        _expert_body,
        grid_spec=grid_spec,
        out_shape=jax.ShapeDtypeStruct((n_rows, D_MODEL), F32),
        compiler_params=_params("arbitrary", "arbitrary"),
        name="experts",
    )(sb_expert, sb_start, sb_rows, sb_real, dest, seg_fill, tail, hn_packed, w_gate, w_up, w_down)


def _combine_body(dest, h1_ref, wt_ref, y_hbm, o_ref, g, sem):
    i = pl.program_id(0)
    nsteps = pl.num_programs(0)
    groups = h1_ref.shape[0]
    tm = groups * SUBLANES

    def issue(step, slot):
        def f(q, _):
            for u in range(SUBLANES):
                for k in range(TOP_K):
                    row = dest[(step * tm + q * SUBLANES + u) * TOP_K + k]
                    src = y_hbm.at[row >> 3, pl.ds(row & (SUBLANES - 1), 1)]
                    pltpu.make_async_copy(src, g.at[slot, k, q, pl.ds(u, 1)], sem.at[slot]).start()
            return 0
        lax.fori_loop(0, groups, f, 0)

    @pl.when(i == 0)
    def _():
        issue(0, 0)

    @pl.when(i + 1 < nsteps)
    def _():
        issue(i + 1, (i + 1) % 2)

    slot = i % 2
    for k in range(TOP_K):
        pltpu.make_async_copy(y_hbm.at[pl.ds(0, groups)], g.at[slot, k], sem.at[slot]).wait()

    w = wt_ref[...]
    o_ref[...] = h1_ref[...] + (w[:, :, 0:1] * g[slot, 0] + w[:, :, 1:2] * g[slot, 1])


def _combine(h1, wts, y_buf, dest_flat):
    t = h1.shape[0]
    groups = COMBINE_TM // SUBLANES
    by_group = lambda a: a.reshape(a.shape[0] // SUBLANES, SUBLANES, a.shape[1])
    grid_spec = pltpu.PrefetchScalarGridSpec(
        num_scalar_prefetch=1,
        grid=(t // COMBINE_TM,),
        in_specs=[pl.BlockSpec((groups, SUBLANES, D_MODEL), lambda i, d: (i, 0, 0)),
                  pl.BlockSpec((groups, SUBLANES, LANES), lambda i, d: (i, 0, 0)),
                  pl.BlockSpec(memory_space=pl.ANY)],
        out_specs=pl.BlockSpec((groups, SUBLANES, D_MODEL), lambda i, d: (i, 0, 0)),
        scratch_shapes=[pltpu.VMEM((2, TOP_K, groups, SUBLANES, D_MODEL), F32),
                        pltpu.SemaphoreType.DMA((2,))],
    )
    out = pl.pallas_call(
        _combine_body,
        grid_spec=grid_spec,
        out_shape=jax.ShapeDtypeStruct((t // SUBLANES, SUBLANES, D_MODEL), F32),
        compiler_params=_params("arbitrary"),
        name="combine",
    )(dest_flat, by_group(h1), by_group(wts), by_group(y_buf))
    return out.reshape(t, D_MODEL)


def _dispatch_tables(idx, cnt, t):
    rb, cap = EXPERT_ROW_BLOCK, EXPERT_CAP
    n_assign = t * TOP_K
    n_rows = -(-(n_assign + N_EXPERTS * (rb - 1)) // rb) * rb
    n_sb = (n_rows + N_EXPERTS * (cap - rb)) // cap
    dest = idx[:, 0:TOP_K].reshape(-1)
    counts = cnt[0, :N_EXPERTS].astype(jnp.int32)
    padded = (counts + rb - 1) // rb * rb
    seg_end = jnp.cumsum(padded)
    seg_start = (seg_end - padded).astype(jnp.int32)
    seg_fill = (seg_start + counts).astype(jnp.int32)

    sb_per_expert = (padded + cap - 1) // cap
    sb_end = jnp.cumsum(sb_per_expert)
    total = sb_end[-1]
    s = jnp.arange(n_sb, dtype=jnp.int32)
    s_eff = jnp.minimum(s, total - 1)
    e = jnp.minimum(jnp.sum(sb_end[None, :] <= s_eff[:, None], axis=1), N_EXPERTS - 1).astype(jnp.int32)
    local = s_eff - (sb_end[e] - sb_per_expert[e])
    sb_start = (seg_start[e] + local * cap).astype(jnp.int32)
    sb_rows = jnp.where(s < total, jnp.clip(padded[e] - local * cap, 0, cap), 0).astype(jnp.int32)
    gathered = (counts + GATHER_UNROLL - 1) // GATHER_UNROLL * GATHER_UNROLL
    sb_real = jnp.where(s < total, jnp.clip(gathered[e] - local * cap, 0, cap), 0).astype(jnp.int32)
    tail = seg_end[-1:].astype(jnp.int32)
    return dest, seg_fill, e, sb_start, sb_rows, sb_real, tail, n_rows


def kernel(x, meta_tokens, mix_norm_w, w_in, q_norm_w, k_norm_w, rel_bias, meta_bias, conv_w,
           attn_out_norm_w, conv_out_norm_w, w_out, ffn_norm_w, w_router_group, b_router_group,
           w_router_expert, b_router_expert, w_gate, w_up, w_down):
    bsz, seq, d = x.shape
    depth = mix_norm_w.shape[0]
    assert depth == 1 and d == D_MODEL and seq % GRID_W == 0
    t = bsz * seq
    x2d = x.reshape(t, d)
    l = 0

    w_in_b = w_in[l].astype(BF16)
    proj = _inproj(x2d, mix_norm_w[l], w_in_b, INPROJ_TM)
    proj_meta = _inproj(meta_tokens.astype(x.dtype), mix_norm_w[l], w_in_b, N_META)

    a, c = _mixers(proj, proj_meta, q_norm_w[l], k_norm_w[l], rel_bias[l], meta_bias[l], conv_w[l], bsz, seq)

    spare = LANES - N_GROUPS - N_EXPERTS
    w_router = jnp.concatenate([w_router_group[l].astype(F32), w_router_expert[l].astype(F32),
                                jnp.zeros((d, spare), F32)], axis=1)
    b_router = jnp.concatenate([b_router_group[l].astype(F32), b_router_expert[l].astype(F32),
                                jnp.zeros((spare,), F32)]).reshape(1, LANES)
    h1, hn, logits = _outproj(a, c, x2d, attn_out_norm_w[l], conv_out_norm_w[l], w_out[l].astype(BF16),
                              ffn_norm_w[l], w_router.astype(BF16), b_router)

    idx, wts, cnt = _route(logits)
    dest, seg_fill, sb_expert, sb_start, sb_rows, sb_real, tail, n_rows = _dispatch_tables(idx, cnt, t)
    y_buf = _experts(hn, w_gate.reshape(N_EXPERTS, d, EXPERT_FF), w_up.reshape(N_EXPERTS, d, EXPERT_FF),
                     w_down.reshape(N_EXPERTS, EXPERT_FF, d), sb_expert, sb_start, sb_rows, sb_real,
                     dest, seg_fill, tail, n_rows)
    out = _combine(h1, wts, y_buf, dest)
    return out.reshape(bsz, seq, d)
```

```python
import functools

import jax
import jax.numpy as jnp
from jax import lax
from jax.experimental import pallas as pl
from jax.experimental.pallas import tpu as pltpu

F32 = jnp.float32
BF16 = jnp.bfloat16

D_MODEL = 2048
N_META = 16
GRID_W = 64
N_HEADS = 16
HEAD_DIM = 64
ATTN_WIDTH = N_HEADS * HEAD_DIM
CONV_WIDTH = D_MODEL - ATTN_WIDTH
PROJ_TOTAL = 3 * ATTN_WIDTH + 3 * CONV_WIDTH
WIN_ROWS = 8
WIN_COLS = 16
N_GROUPS = 4
EXPERTS_PER_GROUP = 8
N_EXPERTS = N_GROUPS * EXPERTS_PER_GROUP
TOP_K = 2
EXPERT_FF = 1024
EPS = 1e-6

LANES = 128
SUBLANES = 8
VMEM_LIMIT = 52 * 1024 * 1024

ATTN_ROWS_PER_STEP = 4
INPROJ_TM = 1024
INPROJ_TN = 1024
OUTPROJ_TM = 512
OUTPROJ_SUB = 256
ROUTE_TM = 1024
EXPERT_ROW_BLOCK = 128
EXPERT_BLOCKS = (512, 256, 128)
EXPERT_CAP = 1024
EXPERT_FF_CHUNK = 512
GATHER_UNROLL = 8
COMBINE_TM = 512


def _params(*sem):
    return pltpu.CompilerParams(dimension_semantics=sem, vmem_limit_bytes=VMEM_LIMIT)


def _inproj_body(x_ref, nw_ref, w_ref, o_ref, xn_ref):
    @pl.when(pl.program_id(1) == 0)
    def _():
        x = x_ref[...]
        ms = jnp.mean(x * x, axis=-1, keepdims=True)
        xn_ref[...] = (x * lax.rsqrt(ms + EPS) * nw_ref[...]).astype(BF16)

    o_ref[...] = jnp.dot(xn_ref[...], w_ref[...], preferred_element_type=F32)


def _inproj(x2d, norm_w, w_bf16, tm):
    m = x2d.shape[0]
    tn = INPROJ_TN
    return pl.pallas_call(
        _inproj_body,
        grid=(m // tm, PROJ_TOTAL // tn),
        in_specs=[pl.BlockSpec((tm, D_MODEL), lambda i, j: (i, 0)),
                  pl.BlockSpec((1, D_MODEL), lambda i, j: (0, 0)),
                  pl.BlockSpec((D_MODEL, tn), lambda i, j: (0, j))],
        out_specs=pl.BlockSpec((tm, tn), lambda i, j: (i, j)),
        out_shape=jax.ShapeDtypeStruct((m, PROJ_TOTAL), F32),
        scratch_shapes=[pltpu.VMEM((tm, D_MODEL), BF16)],
        compiler_params=_params("arbitrary", "arbitrary"),
        name="inproj",
    )(x2d, norm_w.reshape(1, D_MODEL), w_bf16)


def _head_norm(x, w, lo):
    x2 = x * x
    s_lo = jnp.sum(jnp.where(lo, x2, 0.0), axis=-1, keepdims=True)
    s_hi = jnp.sum(jnp.where(lo, 0.0, x2), axis=-1, keepdims=True)
    ms = jnp.where(lo, s_lo, s_hi) * (1.0 / HEAD_DIM)
    return x * lax.rsqrt(ms + EPS) * w


def _attn_body(q_ref, k_ref, v_ref, km_ref, vm_ref, qw_ref, kw_ref, bias_ref, mb_ref,
               gb_ref, gc_ref, hc_ref, gcm_ref, hcm_ref, cw_ref, o_ref, c_ref,
               qs, ks, vs, sc, smc, pr, pmr, *, rows):
    _conv_body(gb_ref, gc_ref, hc_ref, gcm_ref, hcm_ref, cw_ref, c_ref)
    lo = lax.broadcasted_iota(jnp.int32, (1, LANES), 1) < HEAD_DIM
    scale = HEAD_DIM ** -0.5
    chunk = 256
    seq = rows * GRID_W

    kmb = _head_norm(km_ref[...], kw_ref[...], lo).astype(BF16)
    vmb = vm_ref[...].astype(BF16)
    contract_last = (((1,), (1,)), ((), ()))
    head_masks = (lo, jnp.logical_not(lo))

    def one_head(x, h):
        return jnp.where(head_masks[h], x, jnp.zeros_like(x))

    def prep(i, _):
        sl = pl.ds(pl.multiple_of(i * chunk, chunk), chunk)
        qs[sl, :] = (_head_norm(q_ref[sl, :], qw_ref[...], lo) * scale).astype(BF16)
        ks[sl, :] = _head_norm(k_ref[sl, :], kw_ref[...], lo).astype(BF16)
        vs[sl, :] = v_ref[sl, :].astype(BF16)
        return 0

    lax.fori_loop(0, seq // chunk, prep, 0, unroll=2)
    wr = min(WIN_ROWS, rows)
    nk = wr * GRID_W

    def window_start(r):
        return jnp.clip(r - wr // 2, 0, rows - wr)

    def row_slice(r, n):
        return pl.ds(pl.multiple_of(r * GRID_W, GRID_W), n)

    def store_scores(r, s_ref, sm_ref):
        rs = window_start(r)
        si = rs - r + (WIN_ROWS - 1)
        q_r = qs[row_slice(r, GRID_W), :]
        kwin = ks[row_slice(rs, nk), :]
        for h in range(2):
            qh = one_head(q_r, h)
            bias = jnp.concatenate([bias_ref[h, si + w] for w in range(0, wr, 2)], axis=-1)
            s_ref[h] = lax.dot_general(qh, kwin, contract_last, preferred_element_type=F32) + bias
            sm_ref[h] = lax.dot_general(qh, kmb, contract_last, preferred_element_type=F32) + mb_ref[h]

    def store_softmax(s_ref, sm_ref, p_ref, pm_ref):
        for h in range(2):
            s = s_ref[h]
            sm = sm_ref[h]
            m = jnp.maximum(jnp.max(s, axis=-1, keepdims=True), jnp.max(sm, axis=-1, keepdims=True))
            p = jnp.exp(s - m)
            pm = jnp.exp(sm - m)
            inv = 1.0 / (jnp.sum(p, axis=-1, keepdims=True) + jnp.sum(pm, axis=-1, keepdims=True))
            p_ref[h] = (p * inv).astype(BF16)
            pm_ref[h] = (pm * inv).astype(BF16)

    def weighted_values(r, p_ref, pm_ref):
        vwin = vs[row_slice(window_start(r), nk), :]
        outs = [jnp.dot(p_ref[h], vwin, preferred_element_type=F32)
                + jnp.dot(pm_ref[h], vmb, preferred_element_type=F32) for h in range(2)]
        o_ref[row_slice(r, GRID_W), :] = jnp.where(lo, outs[0], outs[1])

    per = ATTN_ROWS_PER_STEP
    groups = rows // per

    def step(j, parity, do_scores, do_softmax, do_values):
        if do_scores:
            for k in range(per):
                store_scores(per * j + k, sc.at[parity, k], smc.at[parity, k])
        if do_values:
            for k in range(per):
                weighted_values(per * (j - 2) + k, pr.at[1 - parity, k], pmr.at[1 - parity, k])
        if do_softmax:
            for k in range(per):
                store_softmax(sc.at[1 - parity, k], smc.at[1 - parity, k], pr.at[parity, k], pmr.at[parity, k])

    step(0, 0, True, False, False)
    step(1, 1, True, True, False)

    def two_steps(i, _):
        step(2 * i, 0, True, True, True)
        step(2 * i + 1, 1, True, True, True)
        return 0

    lax.fori_loop(1, groups // 2, two_steps, 0)
    step(groups, 0, False, True, True)
    step(groups + 1, 1, False, False, True)


def _bias_table(rel_bias):
    c = jnp.arange(GRID_W)
    col_start = jnp.clip(c - WIN_COLS // 2, 0, GRID_W - WIN_COLS)
    col_mask = (c[None, :] >= col_start[:, None]) & (c[None, :] < col_start[:, None] + WIN_COLS)
    dc = jnp.clip(c[None, :] - c[:, None], -(WIN_COLS - 1), WIN_COLS - 1) + (WIN_COLS - 1)
    ncol = 2 * WIN_COLS - 1
    pair_bias = jnp.concatenate([rel_bias[:, :-1], rel_bias[:, 1:]], axis=-1).astype(F32)
    onehot = (dc[None] == jnp.arange(ncol)[:, None, None]).astype(F32)
    zeros = jnp.zeros_like(onehot)
    pair_onehot = jnp.concatenate([jnp.concatenate([onehot, zeros], axis=-1),
                                   jnp.concatenate([zeros, onehot], axis=-1)], axis=0)
    table = jnp.einsum('hdm,mqn->hdqn', pair_bias, pair_onehot, precision=lax.Precision.HIGHEST)
    pair_mask = jnp.concatenate([col_mask, col_mask], axis=-1)
    return jnp.where(pair_mask[None, None], table, -jnp.inf)


def _mixers(proj, proj_meta, q_norm_w, k_norm_w, rel_bias, meta_bias, conv_w, bsz, seq):
    rows = seq // GRID_W
    assert rows >= WIN_ROWS and rows % (2 * ATTN_ROWS_PER_STEP) == 0
    nk = WIN_ROWS * GRID_W
    npairs = N_HEADS // 2
    assert CONV_WIDTH // LANES == npairs
    conv_base = 3 * ATTN_WIDTH // LANES
    qw = jnp.tile(q_norm_w.astype(F32), 2).reshape(1, LANES)
    kw = jnp.tile(k_norm_w.astype(F32), 2).reshape(1, LANES)
    bias = _bias_table(rel_bias)
    mb = meta_bias.astype(F32).reshape(N_HEADS, 1, N_META)
    return pl.pallas_call(
        functools.partial(_attn_body, rows=rows),
        grid=(bsz, npairs),
        in_specs=[pl.BlockSpec((seq, LANES), lambda b, p: (b, p)),
                  pl.BlockSpec((seq, LANES), lambda b, p: (b, npairs + p)),
                  pl.BlockSpec((seq, LANES), lambda b, p: (b, 2 * npairs + p)),
                  pl.BlockSpec((N_META, LANES), lambda b, p: (0, npairs + p)),
                  pl.BlockSpec((N_META, LANES), lambda b, p: (0, 2 * npairs + p)),
                  pl.BlockSpec((1, LANES), lambda b, p: (0, 0)),
                  pl.BlockSpec((1, LANES), lambda b, p: (0, 0)),
                  pl.BlockSpec((2, 2 * WIN_ROWS - 2, GRID_W, LANES), lambda b, p: (p, 0, 0, 0)),
                  pl.BlockSpec((2, 1, N_META), lambda b, p: (p, 0, 0)),
                  pl.BlockSpec((seq, LANES), lambda b, p: (b, conv_base + p)),
                  pl.BlockSpec((seq, LANES), lambda b, p: (b, conv_base + npairs + p)),
                  pl.BlockSpec((seq, LANES), lambda b, p: (b, conv_base + 2 * npairs + p)),
                  pl.BlockSpec((N_META, LANES), lambda b, p: (0, conv_base + npairs + p)),
                  pl.BlockSpec((N_META, LANES), lambda b, p: (0, conv_base + 2 * npairs + p)),
                  pl.BlockSpec((3, LANES), lambda b, p: (0, p))],
        out_specs=[pl.BlockSpec((seq, LANES), lambda b, p: (b, p)),
                   pl.BlockSpec((seq, LANES), lambda b, p: (b, p))],
        out_shape=[jax.ShapeDtypeStruct((bsz * seq, ATTN_WIDTH), F32),
                   jax.ShapeDtypeStruct((bsz * seq, CONV_WIDTH), F32)],
        scratch_shapes=([pltpu.VMEM((seq, LANES), BF16)] * 3
                        + [pltpu.VMEM((2, ATTN_ROWS_PER_STEP, 2, GRID_W, nk), F32),
                           pltpu.VMEM((2, ATTN_ROWS_PER_STEP, 2, GRID_W, N_META), F32),
                           pltpu.VMEM((2, ATTN_ROWS_PER_STEP, 2, GRID_W, nk), BF16),
                           pltpu.VMEM((2, ATTN_ROWS_PER_STEP, 2, GRID_W, N_META), BF16)]),
        compiler_params=_params("arbitrary", "arbitrary"),
        name="mixers",
    )(proj, proj, proj, proj_meta, proj_meta, qw, kw, bias, mb,
      proj, proj, proj, proj_meta, proj_meta, conv_w.astype(F32))


def _conv_body(gb_ref, gc_ref, hc_ref, gcm_ref, hcm_ref, w_ref, o_ref):
    seq = gb_ref.shape[0]
    u = gc_ref[...] * hc_ref[...]
    u_meta_last = gcm_ref[N_META - 1:N_META, :] * hcm_ref[N_META - 1:N_META, :]
    row = lax.broadcasted_iota(jnp.int32, (seq, 1), 0)
    u_prev = jnp.where(row == 0, u_meta_last, pltpu.roll(u, 1, 0))
    u_next = jnp.where(row == seq - 1, 0.0, pltpu.roll(u, seq - 1, 0))
    w = w_ref[...]
    y = u_prev * w[0:1] + u * w[1:2] + u_next * w[2:3]
    o_ref[...] = gb_ref[...] * y


def _rms(x, w):
    ms = jnp.mean(x * x, axis=-1, keepdims=True)
    return x * lax.rsqrt(ms + EPS) * w


def _pack_bf16_pairs(x):
    w = x.shape[1] // 2
    lo = lax.bitcast_convert_type(x[:, :w].astype(BF16).astype(F32), jnp.uint32)
    hi = lax.bitcast_convert_type(x[:, w:].astype(BF16).astype(F32), jnp.uint32)
    return (hi & jnp.uint32(0xFFFF0000)) | (lo >> 16)


def _unpack_bf16_pairs(p):
    lo = lax.bitcast_convert_type(p << 16, F32).astype(BF16)
    hi = lax.bitcast_convert_type(p & jnp.uint32(0xFFFF0000), F32).astype(BF16)
    return lo, hi


def _outproj_body(a_ref, c_ref, x_ref, aw_ref, cw_ref, wo_ref, fw_ref, wr_ref, br_ref,
                  h1_ref, hn_ref, lg_ref):
    sub = OUTPROJ_SUB
    blocks = [pl.ds(k * sub, sub) for k in range(a_ref.shape[0] // sub)]

    def mix(rows):
        an = _rms(a_ref[rows, :], aw_ref[...]).astype(BF16)
        cn = _rms(c_ref[rows, :], cw_ref[...]).astype(BF16)
        return (jnp.dot(an, wo_ref[0:ATTN_WIDTH, :], preferred_element_type=F32)
                + jnp.dot(cn, wo_ref[ATTN_WIDTH:D_MODEL, :], preferred_element_type=F32))

    def finish(rows, mixed):
        h1 = x_ref[rows, :] + mixed
        h1_ref[rows, :] = h1
        hn = _rms(h1, fw_ref[...])
        hn_ref[rows, :] = _pack_bf16_pairs(hn)
        lg_ref[rows, :] = jnp.dot(hn.astype(BF16), wr_ref[...], preferred_element_type=F32) + br_ref[...]

    mixed = mix(blocks[0])
    for k, rows in enumerate(blocks):
        following = mix(blocks[k + 1]) if k + 1 < len(blocks) else None
        finish(rows, mixed)
        mixed = following


def _outproj(a, c, x2d, aw, cw, wo_bf16, fw, w_router, b_router):
    t = x2d.shape[0]
    tm = OUTPROJ_TM
    row = lambda i: (i, 0)
    fixed = lambda i: (0, 0)
    return pl.pallas_call(
        _outproj_body,
        grid=(t // tm,),
        in_specs=[pl.BlockSpec((tm, ATTN_WIDTH), row),
                  pl.BlockSpec((tm, CONV_WIDTH), row),
                  pl.BlockSpec((tm, D_MODEL), row),
                  pl.BlockSpec((1, ATTN_WIDTH), fixed),
                  pl.BlockSpec((1, CONV_WIDTH), fixed),
                  pl.BlockSpec((D_MODEL, D_MODEL), fixed, pipeline_mode=pl.Buffered(1)),
                  pl.BlockSpec((1, D_MODEL), fixed),
                  pl.BlockSpec((D_MODEL, LANES), fixed),
                  pl.BlockSpec((1, LANES), fixed)],
        out_specs=[pl.BlockSpec((tm, D_MODEL), row),
                   pl.BlockSpec((tm, D_MODEL // 2), row),
                   pl.BlockSpec((tm, LANES), row)],
        out_shape=[jax.ShapeDtypeStruct((t, D_MODEL), F32),
                   jax.ShapeDtypeStruct((t, D_MODEL // 2), jnp.uint32),
                   jax.ShapeDtypeStruct((t, LANES), F32)],
        compiler_params=_params("arbitrary"),
        name="outproj",
    )(a, c, x2d, aw.reshape(1, -1), cw.reshape(1, -1), wo_bf16, fw.reshape(1, -1), w_router, b_router)


def _route_body(lg_ref, idx_ref, wt_ref, cnt_ref, run_ref, seg_ref):
    final_pass = pl.program_id(0) == 1
    first_tile = pl.program_id(1) == 0

    @pl.when(first_tile & jnp.logical_not(final_pass))
    def _():
        run_ref[...] = jnp.zeros_like(run_ref)
        seg_ref[...] = jnp.zeros_like(seg_ref)

    @pl.when(first_tile & final_pass)
    def _():
        counts = run_ref[...]
        rb = float(EXPERT_ROW_BLOCK)
        padded = jnp.ceil(counts * (1.0 / rb)) * rb
        lane8 = lax.broadcasted_iota(jnp.int32, counts.shape, 1)
        ends = padded
        shift = 1
        while shift < LANES:
            ends = ends + jnp.where(lane8 >= shift, pltpu.roll(ends, shift, 1), 0.0)
            shift *= 2
        seg_ref[...] = ends - padded
        run_ref[...] = jnp.zeros_like(run_ref)

    logits = lg_ref[...]
    tm = logits.shape[0]
    lane = lax.broadcasted_iota(jnp.int32, (tm, LANES), 1)
    neg = -jnp.inf

    def first_argmax(v):
        m = jnp.max(v, axis=-1, keepdims=True)
        first = jnp.min(jnp.where(v == m, lane.astype(F32), float(LANES)), axis=-1, keepdims=True)
        return m, first.astype(jnp.int32)

    gl = jnp.where(lane < N_GROUPS, logits, neg)
    gmax, gidx = first_argmax(gl)
    g_w = 1.0 / jnp.sum(jnp.exp(gl - gmax), axis=-1, keepdims=True)
    first = N_GROUPS + gidx * EXPERTS_PER_GROUP
    el = jnp.where((lane >= first) & (lane < first + EXPERTS_PER_GROUP), logits, neg)
    m0, j0 = first_argmax(el)
    m1, j1 = first_argmax(jnp.where(lane == j0, neg, el))
    p1 = jnp.exp(m1 - m0)
    w0 = g_w / (1.0 + p1)
    w1 = g_w * p1 / (1.0 + p1)
    e0 = j0 - N_GROUPS
    e1 = j1 - N_GROUPS

    onehot = ((lane == e0) | (lane == e1)).astype(BF16)
    seen = run_ref[0:1, :]
    run = seen + jnp.sum(onehot.astype(F32), axis=0, keepdims=True)
    run_ref[...] = jnp.broadcast_to(run, run_ref.shape)

    @pl.when(final_pass)
    def _():
        tri = (lax.broadcasted_iota(jnp.int32, (tm, tm), 0)
               > lax.broadcasted_iota(jnp.int32, (tm, tm), 1)).astype(BF16)
        place = jnp.dot(tri, onehot, preferred_element_type=F32) + seen + seg_ref[0:1, :]
        d0 = jnp.sum(jnp.where(lane == e0, place, 0.0), axis=-1, keepdims=True).astype(jnp.int32)
        d1 = jnp.sum(jnp.where(lane == e1, place, 0.0), axis=-1, keepdims=True).astype(jnp.int32)
        idx_ref[...] = jnp.where(lane == 0, d0, jnp.where(lane == 1, d1, jnp.zeros_like(lane)))
        wt_ref[...] = jnp.where(lane == 0, w0, jnp.where(lane == 1, w1, 0.0))
        cnt_ref[...] = jnp.broadcast_to(run, cnt_ref.shape)


def _route(logits):
    t = logits.shape[0]
    tm = ROUTE_TM
    return pl.pallas_call(
        _route_body,
        grid=(2, t // tm),
        in_specs=[pl.BlockSpec((tm, LANES), lambda p, i: (i, 0))],
        out_specs=[pl.BlockSpec((tm, LANES), lambda p, i: (i * p, 0)),
                   pl.BlockSpec((tm, LANES), lambda p, i: (i * p, 0)),
                   pl.BlockSpec((8, LANES), lambda p, i: (0, 0))],
        out_shape=[jax.ShapeDtypeStruct((t, LANES), jnp.int32),
                   jax.ShapeDtypeStruct((t, LANES), F32),
                   jax.ShapeDtypeStruct((8, LANES), F32)],
        scratch_shapes=[pltpu.VMEM((8, LANES), F32)] * 2,
        compiler_params=_params("arbitrary", "arbitrary"),
        name="route",
    )(logits)


def _expert_body(sbe, sbs, sbn, sbr, dest, seg_fill, tail, hn_hbm, wg_ref, wu_ref, wd_ref,
                 y_hbm, x32, xb, acc, tok, gsem, osem):
    s = pl.program_id(0)
    c = pl.program_id(1)
    nsb = pl.num_programs(0)
    nch = EXPERT_FF // EXPERT_FF_CHUNK
    rb = EXPERT_ROW_BLOCK
    half = D_MODEL // 2
    n = sbn[s]
    start = sbs[s]
    nblk = n // rb
    slot = s % 2

    def for_each(count, fn):
        def body(i, _):
            fn(i)
            return 0
        lax.fori_loop(0, count, body, 0)

    def block_rows(j):
        return pl.ds(pl.multiple_of(j * rb, rb), rb)

    def gather_row(sb_first, buf, i):
        src = hn_hbm.at[pl.ds(tok[sb_first + i], 1)]
        pltpu.make_async_copy(src, x32.at[buf, pl.ds(i, 1)], gsem.at[buf]).start()

    def gather_rows(sb_first, buf, first_row, count):
        def group(g):
            for k in range(GATHER_UNROLL):
                gather_row(sb_first, buf, first_row + g * GATHER_UNROLL + k)
        for_each(count // GATHER_UNROLL, group)

    def wait_gathered(buf, count):
        rows = pl.ds(0, pl.multiple_of(count, GATHER_UNROLL))
        pltpu.make_async_copy(hn_hbm.at[rows], x32.at[buf, rows], gsem.at[buf]).wait()

    def out_copy(first_row, j):
        dst = pl.ds(pl.multiple_of(first_row + j * rb, rb), rb)
        return pltpu.make_async_copy(acc.at[block_rows(j)], y_hbm.at[dst], osem)

    @pl.when((s == 0) & (c == 0))
    def _():
        def clear(j):
            for buf in range(2):
                x32[buf, block_rows(j), :] = jnp.zeros((rb, half), jnp.uint32)
            acc[block_rows(j), :] = jnp.zeros((rb, D_MODEL), F32)
        for_each(EXPERT_CAP // rb, clear)

        def pad_rows(e):
            for k in range(GATHER_UNROLL - 1):
                tok[jnp.minimum(seg_fill[e] + k, tok.shape[0] - 1)] = 0
        for_each(N_EXPERTS, pad_rows)

        def invert(g):
            for k in range(GATHER_UNROLL):
                tok[dest[g * GATHER_UNROLL + k]] = g * (GATHER_UNROLL // TOP_K) + k // TOP_K
        for_each(dest.shape[0] // GATHER_UNROLL, invert)

        gather_rows(sbs[0], 0, 0, sbr[0])

    prev = jnp.maximum(s - 1, 0)
    nxt = jnp.minimum(s + 1, nsb - 1)
    prev_rows = jnp.where(s > 0, sbn[prev], 0)
    prev_pending = (c == 0) & (prev_rows > 0)

    @pl.when((c == 0) & (sbr[s] > 0))
    def _():
        wait_gathered(slot, sbr[s])

    def wait_prev_output():
        for_each(prev_rows // rb, lambda j: out_copy(sbs[prev], j).wait())

    @pl.when(prev_pending & (n == 0))
    def _():
        wait_prev_output()

    @pl.when((s == nsb - 1) & (c == 0))
    def _():
        first = tail[0]
        nfill = (y_hbm.shape[0] - first) // rb
        acc[0:rb, :] = jnp.zeros((rb, D_MODEL), F32)

        def fill_copy(j):
            dst = pl.ds(pl.multiple_of(first + j * rb, rb), rb)
            return pltpu.make_async_copy(acc.at[0:rb], y_hbm.at[dst], osem)

        for_each(nfill, lambda j: fill_copy(j).start())
        for_each(nfill, lambda j: fill_copy(j).wait())

    @pl.when(n > 0)
    def _():
        @pl.when(c == 0)
        def _():
            def unpack(j):
                rows = block_rows(j)
                lo, hi = _unpack_bf16_pairs(x32[slot, rows, :])
                xb[rows, 0:half] = lo
                xb[rows, half:D_MODEL] = hi
            for_each(nblk, unpack)

            @pl.when(prev_pending)
            def _():
                wait_prev_output()

        @pl.when((c == nch - 1) & (s + 1 < nsb) & (sbr[nxt] > 0))
        def _():
            gather_rows(sbs[nxt], 1 - slot, 0, sbr[nxt])

        def mlp(first_row, m):
            rows = pl.ds(pl.multiple_of(first_row, rb), m)
            x = xb[rows, :]
            g = jnp.dot(x, wg_ref[0].astype(BF16), preferred_element_type=F32)
            u = jnp.dot(x, wu_ref[0].astype(BF16), preferred_element_type=F32)
            h = (jax.nn.silu(g) * u).astype(BF16)
            y = jnp.dot(h, wd_ref[0].astype(BF16), preferred_element_type=F32)
            acc[rows, :] = jnp.where(c == 0, y, acc[rows, :] + y)

        done = 0
        for m in EXPERT_BLOCKS[:-1]:
            count = (n - done) // m
            for_each(count, lambda j, done=done, m=m: mlp(done + j * m, m))
            done = done + count * m

        @pl.when(done < n)
        def _():
            mlp(done, rb)

        @pl.when(c == nch - 1)
        def _():
            for_each(nblk, lambda j: out_copy(start, j).start())


def _experts(hn_packed, w_gate, w_up, w_down, sb_expert, sb_start, sb_rows, sb_real, dest, seg_fill, tail,
             n_rows):
    n_sb = sb_expert.shape[0]
    nch = EXPERT_FF // EXPERT_FF_CHUNK
    fc = EXPERT_FF_CHUNK

    def chunk(s, c, sbn):
        return jnp.where(sbn[s] > 0, c, nch - 1)

    def up_map(s, c, sbe, sbs, sbn, *_):
        return (sbe[s], 0, chunk(s, c, sbn))

    def down_map(s, c, sbe, sbs, sbn, *_):
        return (sbe[s], chunk(s, c, sbn), 0)

    grid_spec = pltpu.PrefetchScalarGridSpec(
        num_scalar_prefetch=7,
        grid=(n_sb, nch),
        in_specs=[pl.BlockSpec(memory_space=pl.ANY),
                  pl.BlockSpec((1, D_MODEL, fc), up_map),
                  pl.BlockSpec((1, D_MODEL, fc), up_map),
                  pl.BlockSpec((1, fc, D_MODEL), down_map)],
        out_specs=pl.BlockSpec(memory_space=pl.ANY),
        scratch_shapes=[pltpu.VMEM((2, EXPERT_CAP, D_MODEL // 2), jnp.uint32),
                        pltpu.VMEM((EXPERT_CAP, D_MODEL), BF16),
                        pltpu.VMEM((EXPERT_CAP, D_MODEL), F32),
                        pltpu.SMEM((n_rows,), jnp.int32),
                        pltpu.SemaphoreType.DMA((2,)),
                        pltpu.SemaphoreType.DMA(())],
    )
    return pl.pallas_call(
        _expert_body,
        grid_spec=grid_spec,
        out_shape=jax.ShapeDtypeStruct((n_rows, D_MODEL), F32),
        compiler_params=_params("arbitrary", "arbitrary"),
        name="experts",
    )(sb_expert, sb_start, sb_rows, sb_real, dest, seg_fill, tail, hn_packed, w_gate, w_up, w_down)


def _combine_body(dest, h1_ref, wt_ref, y_hbm, o_ref, g, sem):
    i = pl.program_id(0)
    nsteps = pl.num_programs(0)
    groups = h1_ref.shape[0]
    tm = groups * SUBLANES

    def issue(step, slot):
        def f(q, _):
            for u in range(SUBLANES):
                for k in range(TOP_K):
                    row = dest[(step * tm + q * SUBLANES + u) * TOP_K + k]
                    src = y_hbm.at[row >> 3, pl.ds(row & (SUBLANES - 1), 1)]
                    pltpu.make_async_copy(src, g.at[slot, k, q, pl.ds(u, 1)], sem.at[slot]).start()
            return 0
        lax.fori_loop(0, groups, f, 0)

    @pl.when(i == 0)
    def _():
        issue(0, 0)

    @pl.when(i + 1 < nsteps)
    def _():
        issue(i + 1, (i + 1) % 2)

    slot = i % 2
    for k in range(TOP_K):
        pltpu.make_async_copy(y_hbm.at[pl.ds(0, groups)], g.at[slot, k], sem.at[slot]).wait()

    w = wt_ref[...]
    o_ref[...] = h1_ref[...] + (w[:, :, 0:1] * g[slot, 0] + w[:, :, 1:2] * g[slot, 1])


def _combine(h1, wts, y_buf, dest_flat):
    t = h1.shape[0]
    groups = COMBINE_TM // SUBLANES
    by_group = lambda a: a.reshape(a.shape[0] // SUBLANES, SUBLANES, a.shape[1])
    grid_spec = pltpu.PrefetchScalarGridSpec(
        num_scalar_prefetch=1,
        grid=(t // COMBINE_TM,),
        in_specs=[pl.BlockSpec((groups, SUBLANES, D_MODEL), lambda i, d: (i, 0, 0)),
                  pl.BlockSpec((groups, SUBLANES, LANES), lambda i, d: (i, 0, 0)),
                  pl.BlockSpec(memory_space=pl.ANY)],
        out_specs=pl.BlockSpec((groups, SUBLANES, D_MODEL), lambda i, d: (i, 0, 0)),
        scratch_shapes=[pltpu.VMEM((2, TOP_K, groups, SUBLANES, D_MODEL), F32),
                        pltpu.SemaphoreType.DMA((2,))],
    )
    out = pl.pallas_call(
        _combine_body,
        grid_spec=grid_spec,
        out_shape=jax.ShapeDtypeStruct((t // SUBLANES, SUBLANES, D_MODEL), F32),
        compiler_params=_params("arbitrary"),
        name="combine",
    )(dest_flat, by_group(h1), by_group(wts), by_group(y_buf))
    return out.reshape(t, D_MODEL)


def _dispatch_tables(idx, cnt, t):
    rb, cap = EXPERT_ROW_BLOCK, EXPERT_CAP
    n_assign = t * TOP_K
    n_rows = -(-(n_assign + N_EXPERTS * (rb - 1)) // rb) * rb
    n_sb = (n_rows + N_EXPERTS * (cap - rb)) // cap
    dest = idx[:, 0:TOP_K].reshape(-1)
    counts = cnt[0, :N_EXPERTS].astype(jnp.int32)
    padded = (counts + rb - 1) // rb * rb
    seg_end = jnp.cumsum(padded)
    seg_start = (seg_end - padded).astype(jnp.int32)
    seg_fill = (seg_start + counts).astype(jnp.int32)

    sb_per_expert = (padded + cap - 1) // cap
    sb_end = jnp.cumsum(sb_per_expert)
    total = sb_end[-1]
    s = jnp.arange(n_sb, dtype=jnp.int32)
    s_eff = jnp.minimum(s, total - 1)
    e = jnp.minimum(jnp.sum(sb_end[None, :] <= s_eff[:, None], axis=1), N_EXPERTS - 1).astype(jnp.int32)
    local = s_eff - (sb_end[e] - sb_per_expert[e])
    sb_start = (seg_start[e] + local * cap).astype(jnp.int32)
    sb_rows = jnp.where(s < total, jnp.clip(padded[e] - local * cap, 0, cap), 0).astype(jnp.int32)
    gathered = (counts + GATHER_UNROLL - 1) // GATHER_UNROLL * GATHER_UNROLL
    sb_real = jnp.where(s < total, jnp.clip(gathered[e] - local * cap, 0, cap), 0).astype(jnp.int32)
    tail = seg_end[-1:].astype(jnp.int32)
    return dest, seg_fill, e, sb_start, sb_rows, sb_real, tail, n_rows


def kernel(x, meta_tokens, mix_norm_w, w_in, q_norm_w, k_norm_w, rel_bias, meta_bias, conv_w,
           attn_out_norm_w, conv_out_norm_w, w_out, ffn_norm_w, w_router_group, b_router_group,
           w_router_expert, b_router_expert, w_gate, w_up, w_down):
    bsz, seq, d = x.shape
    depth = mix_norm_w.shape[0]
    assert depth == 1 and d == D_MODEL and seq % GRID_W == 0
    t = bsz * seq
    x2d = x.reshape(t, d)
    l = 0

    w_in_b = w_in[l].astype(BF16)
    proj = _inproj(x2d, mix_norm_w[l], w_in_b, INPROJ_TM)
    proj_meta = _inproj(meta_tokens.astype(x.dtype), mix_norm_w[l], w_in_b, N_META)

    a, c = _mixers(proj, proj_meta, q_norm_w[l], k_norm_w[l], rel_bias[l], meta_bias[l], conv_w[l], bsz, seq)

    spare = LANES - N_GROUPS - N_EXPERTS
    w_router = jnp.concatenate([w_router_group[l].astype(F32), w_router_expert[l].astype(F32),
                                jnp.zeros((d, spare), F32)], axis=1)
    b_router = jnp.concatenate([b_router_group[l].astype(F32), b_router_expert[l].astype(F32),
                                jnp.zeros((spare,), F32)]).reshape(1, LANES)
    h1, hn, logits = _outproj(a, c, x2d, attn_out_norm_w[l], conv_out_norm_w[l], w_out[l].astype(BF16),
                              ffn_norm_w[l], w_router.astype(BF16), b_router)

    idx, wts, cnt = _route(logits)
    dest, seg_fill, sb_expert, sb_start, sb_rows, sb_real, tail, n_rows = _dispatch_tables(idx, cnt, t)
    y_buf = _experts(hn, w_gate.reshape(N_EXPERTS, d, EXPERT_FF), w_up.reshape(N_EXPERTS, d, EXPERT_FF),
                     w_down.reshape(N_EXPERTS, EXPERT_FF, d), sb_expert, sb_start, sb_rows, sb_real,
                     dest, seg_fill, tail, n_rows)
    out = _combine(h1, wts, y_buf, dest)
    return out.reshape(bsz, seq, d)
```

```python
import functools

import jax
import jax.numpy as jnp
from jax import lax
from jax.experimental import pallas as pl
from jax.experimental.pallas import tpu as pltpu

F32 = jnp.float32
BF16 = jnp.bfloat16

D_MODEL = 2048
N_META = 16
GRID_W = 64
N_HEADS = 16
HEAD_DIM = 64
ATTN_WIDTH = N_HEADS * HEAD_DIM
CONV_WIDTH = D_MODEL - ATTN_WIDTH
PROJ_TOTAL = 3 * ATTN_WIDTH + 3 * CONV_WIDTH
WIN_ROWS = 8
WIN_COLS = 16
N_GROUPS = 4
EXPERTS_PER_GROUP = 8
N_EXPERTS = N_GROUPS * EXPERTS_PER_GROUP
TOP_K = 2
EXPERT_FF = 1024
EPS = 1e-6

LANES = 128
SUBLANES = 8
VMEM_LIMIT = 52 * 1024 * 1024

ATTN_ROWS_PER_STEP = 4
CONV_CHUNK = 256
INPROJ_TM = 1024
INPROJ_TN = 1024
OUTPROJ_TM = 512
OUTPROJ_SUB = 256
ROUTE_TM = 1024
EXPERT_ROW_BLOCK = 128
EXPERT_BLOCKS = (512, 256, 128)
EXPERT_CAP = 1024
EXPERT_FF_CHUNK = 512
GATHER_UNROLL = 8
COMBINE_TM = 512


def _params(*sem):
    return pltpu.CompilerParams(dimension_semantics=sem, vmem_limit_bytes=VMEM_LIMIT)


def _inproj_body(x_ref, meta_ref, nw_ref, w_ref, o_ref, om_ref, xn_ref, mn_ref):
    first_tile = pl.program_id(0) == 0

    def normed(v):
        ms = jnp.mean(v * v, axis=-1, keepdims=True)
        return (v * lax.rsqrt(ms + EPS) * nw_ref[...]).astype(BF16)

    @pl.when(pl.program_id(1) == 0)
    def _():
        xn_ref[...] = normed(x_ref[...])

    @pl.when(first_tile & (pl.program_id(1) == 0))
    def _():
        mn_ref[...] = normed(meta_ref[...])

    o_ref[...] = jnp.dot(xn_ref[...], w_ref[...], preferred_element_type=F32)

    @pl.when(first_tile)
    def _():
        om_ref[...] = jnp.dot(mn_ref[...], w_ref[...], preferred_element_type=F32)


def _inproj(x2d, meta, norm_w, w_bf16):
    m = x2d.shape[0]
    tm, tn = INPROJ_TM, INPROJ_TN
    return pl.pallas_call(
        _inproj_body,
        grid=(m // tm, PROJ_TOTAL // tn),
        in_specs=[pl.BlockSpec((tm, D_MODEL), lambda i, j: (i, 0)),
                  pl.BlockSpec((N_META, D_MODEL), lambda i, j: (0, 0)),
                  pl.BlockSpec((1, D_MODEL), lambda i, j: (0, 0)),
                  pl.BlockSpec((D_MODEL, tn), lambda i, j: (0, j))],
        out_specs=[pl.BlockSpec((tm, tn), lambda i, j: (i, j)),
                   pl.BlockSpec((N_META, tn), lambda i, j: (0, jnp.where(i == 0, j, PROJ_TOTAL // tn - 1)))],
        out_shape=[jax.ShapeDtypeStruct((m, PROJ_TOTAL), F32),
                   jax.ShapeDtypeStruct((N_META, PROJ_TOTAL), F32)],
        scratch_shapes=[pltpu.VMEM((tm, D_MODEL), BF16),
                        pltpu.VMEM((N_META, D_MODEL), BF16)],
        compiler_params=_params("arbitrary", "arbitrary"),
        name="inproj",
    )(x2d, meta, norm_w.reshape(1, D_MODEL), w_bf16)


def _head_norm(x, w, lo):
    x2 = x * x
    s_lo = jnp.sum(jnp.where(lo, x2, 0.0), axis=-1, keepdims=True)
    s_hi = jnp.sum(jnp.where(lo, 0.0, x2), axis=-1, keepdims=True)
    ms = jnp.where(lo, s_lo, s_hi) * (1.0 / HEAD_DIM)
    return x * lax.rsqrt(ms + EPS) * w


def _attn_body(q_ref, k_ref, v_ref, km_ref, vm_ref, qw_ref, kw_ref, bias_ref, mb_ref,
               gb_ref, gc_ref, hc_ref, gcm_ref, hcm_ref, cw_ref, o_ref, c_ref,
               qs, ks, vs, sc, smc, pr, pmr, *, rows):
    _conv_body(gb_ref, gc_ref, hc_ref, gcm_ref, hcm_ref, cw_ref, c_ref)
    lo = lax.broadcasted_iota(jnp.int32, (1, LANES), 1) < HEAD_DIM
    scale = HEAD_DIM ** -0.5
    chunk = 256
    seq = rows * GRID_W

    kmb = _head_norm(km_ref[...], kw_ref[...], lo).astype(BF16)
    vmb = vm_ref[...].astype(BF16)
    contract_last = (((1,), (1,)), ((), ()))
    head_masks = (lo, jnp.logical_not(lo))

    def one_head(x, h):
        return jnp.where(head_masks[h], x, jnp.zeros_like(x))

    def prep(i, _):
        sl = pl.ds(pl.multiple_of(i * chunk, chunk), chunk)
        qs[sl, :] = (_head_norm(q_ref[sl, :], qw_ref[...], lo) * scale).astype(BF16)
        ks[sl, :] = _head_norm(k_ref[sl, :], kw_ref[...], lo).astype(BF16)
        vs[sl, :] = v_ref[sl, :].astype(BF16)
        return 0

    lax.fori_loop(0, seq // chunk, prep, 0, unroll=2)
    wr = min(WIN_ROWS, rows)
    nk = wr * GRID_W

    def window_start(r):
        return jnp.clip(r - wr // 2, 0, rows - wr)

    def row_slice(r, n):
        return pl.ds(pl.multiple_of(r * GRID_W, GRID_W), n)

    def store_scores(r, s_ref, sm_ref):
        rs = window_start(r)
        si = rs - r + (WIN_ROWS - 1)
        q_r = qs[row_slice(r, GRID_W), :]
        kwin = ks[row_slice(rs, nk), :]
        for h in range(2):
            qh = one_head(q_r, h)
            bias = jnp.concatenate([bias_ref[h, si + w] for w in range(0, wr, 2)], axis=-1)
            s_ref[h] = lax.dot_general(qh, kwin, contract_last, preferred_element_type=F32) + bias
            sm_ref[h] = lax.dot_general(qh, kmb, contract_last, preferred_element_type=F32) + mb_ref[h]

    def store_softmax(s_ref, sm_ref, p_ref, pm_ref):
        for h in range(2):
            s = s_ref[h]
            sm = sm_ref[h]
            m = jnp.maximum(jnp.max(s, axis=-1, keepdims=True), jnp.max(sm, axis=-1, keepdims=True))
            p = jnp.exp(s - m)
            pm = jnp.exp(sm - m)
            inv = 1.0 / (jnp.sum(p, axis=-1, keepdims=True) + jnp.sum(pm, axis=-1, keepdims=True))
            p_ref[h] = (p * inv).astype(BF16)
            pm_ref[h] = (pm * inv).astype(BF16)

    def weighted_values(r, p_ref, pm_ref):
        vwin = vs[row_slice(window_start(r), nk), :]
        outs = [jnp.dot(p_ref[h], vwin, preferred_element_type=F32)
                + jnp.dot(pm_ref[h], vmb, preferred_element_type=F32) for h in range(2)]
        o_ref[row_slice(r, GRID_W), :] = jnp.where(lo, outs[0], outs[1])

    per = ATTN_ROWS_PER_STEP
    groups = rows // per

    def step(j, parity, do_scores, do_softmax, do_values):
        if do_scores:
            for k in range(per):
                store_scores(per * j + k, sc.at[parity, k], smc.at[parity, k])
        if do_values:
            for k in range(per):
                weighted_values(per * (j - 2) + k, pr.at[1 - parity, k], pmr.at[1 - parity, k])
        if do_softmax:
            for k in range(per):
                store_softmax(sc.at[1 - parity, k], smc.at[1 - parity, k], pr.at[parity, k], pmr.at[parity, k])

    step(0, 0, True, False, False)
    step(1, 1, True, True, False)

    def two_steps(i, _):
        step(2 * i, 0, True, True, True)
        step(2 * i + 1, 1, True, True, True)
        return 0

    lax.fori_loop(1, groups // 2, two_steps, 0)
    step(groups, 0, False, True, True)
    step(groups + 1, 1, False, False, True)


def _bias_table(rel_bias):
    c = jnp.arange(GRID_W)
    col_start = jnp.clip(c - WIN_COLS // 2, 0, GRID_W - WIN_COLS)
    col_mask = (c[None, :] >= col_start[:, None]) & (c[None, :] < col_start[:, None] + WIN_COLS)
    dc = jnp.clip(c[None, :] - c[:, None], -(WIN_COLS - 1), WIN_COLS - 1) + (WIN_COLS - 1)
    ncol = 2 * WIN_COLS - 1
    pair_bias = jnp.concatenate([rel_bias[:, :-1], rel_bias[:, 1:]], axis=-1).astype(F32)
    onehot = (dc[None] == jnp.arange(ncol)[:, None, None]).astype(F32)
    zeros = jnp.zeros_like(onehot)
    pair_onehot = jnp.concatenate([jnp.concatenate([onehot, zeros], axis=-1),
                                   jnp.concatenate([zeros, onehot], axis=-1)], axis=0)
    table = jnp.einsum('hdm,mqn->hdqn', pair_bias, pair_onehot, precision=lax.Precision.HIGHEST)
    pair_mask = jnp.concatenate([col_mask, col_mask], axis=-1)
    return jnp.where(pair_mask[None, None], table, -jnp.inf)


def _mixers(proj, proj_meta, q_norm_w, k_norm_w, rel_bias, meta_bias, conv_w, bsz, seq):
    rows = seq // GRID_W
    assert rows >= WIN_ROWS and rows % (2 * ATTN_ROWS_PER_STEP) == 0
    nk = WIN_ROWS * GRID_W
    npairs = N_HEADS // 2
    assert CONV_WIDTH // LANES == npairs
    conv_base = 3 * ATTN_WIDTH // LANES
    qw = jnp.tile(q_norm_w.astype(F32), 2).reshape(1, LANES)
    kw = jnp.tile(k_norm_w.astype(F32), 2).reshape(1, LANES)
    bias = _bias_table(rel_bias)
    mb = meta_bias.astype(F32).reshape(N_HEADS, 1, N_META)
    return pl.pallas_call(
        functools.partial(_attn_body, rows=rows),
        grid=(bsz, npairs),
        in_specs=[pl.BlockSpec((seq, LANES), lambda b, p: (b, p)),
                  pl.BlockSpec((seq, LANES), lambda b, p: (b, npairs + p)),
                  pl.BlockSpec((seq, LANES), lambda b, p: (b, 2 * npairs + p)),
                  pl.BlockSpec((N_META, LANES), lambda b, p: (0, npairs + p)),
                  pl.BlockSpec((N_META, LANES), lambda b, p: (0, 2 * npairs + p)),
                  pl.BlockSpec((1, LANES), lambda b, p: (0, 0)),
                  pl.BlockSpec((1, LANES), lambda b, p: (0, 0)),
                  pl.BlockSpec((2, 2 * WIN_ROWS - 2, GRID_W, LANES), lambda b, p: (p, 0, 0, 0)),
                  pl.BlockSpec((2, 1, N_META), lambda b, p: (p, 0, 0)),
                  pl.BlockSpec((seq, LANES), lambda b, p: (b, conv_base + p)),
                  pl.BlockSpec((seq, LANES), lambda b, p: (b, conv_base + npairs + p)),
                  pl.BlockSpec((seq, LANES), lambda b, p: (b, conv_base + 2 * npairs + p)),
                  pl.BlockSpec((N_META, LANES), lambda b, p: (0, conv_base + npairs + p)),
                  pl.BlockSpec((N_META, LANES), lambda b, p: (0, conv_base + 2 * npairs + p)),
                  pl.BlockSpec((3, LANES), lambda b, p: (0, p))],
        out_specs=[pl.BlockSpec((seq, LANES), lambda b, p: (b, p)),
                   pl.BlockSpec((seq, LANES), lambda b, p: (b, p))],
        out_shape=[jax.ShapeDtypeStruct((bsz * seq, ATTN_WIDTH), F32),
                   jax.ShapeDtypeStruct((bsz * seq, CONV_WIDTH), F32)],
        scratch_shapes=([pltpu.VMEM((seq, LANES), BF16)] * 3
                        + [pltpu.VMEM((2, ATTN_ROWS_PER_STEP, 2, GRID_W, nk), F32),
                           pltpu.VMEM((2, ATTN_ROWS_PER_STEP, 2, GRID_W, N_META), F32),
                           pltpu.VMEM((2, ATTN_ROWS_PER_STEP, 2, GRID_W, nk), BF16),
                           pltpu.VMEM((2, ATTN_ROWS_PER_STEP, 2, GRID_W, N_META), BF16)]),
        compiler_params=_params("arbitrary", "arbitrary"),
        name="mixers",
    )(proj, proj, proj, proj_meta, proj_meta, qw, kw, bias, mb,
      proj, proj, proj, proj_meta, proj_meta, conv_w.astype(F32))


def _conv_body(gb_ref, gc_ref, hc_ref, gcm_ref, hcm_ref, w_ref, o_ref):
    seq = gb_ref.shape[0]
    ch = CONV_CHUNK
    nchunks = seq // ch
    w = w_ref[...]
    u_meta_last = gcm_ref[N_META - 1:N_META, :] * hcm_ref[N_META - 1:N_META, :]
    row = lax.broadcasted_iota(jnp.int32, (ch, 1), 0)

    def chunk(i, _):
        r0 = pl.multiple_of(i * ch, ch)
        rows = pl.ds(r0, ch)
        u = gc_ref[rows, :] * hc_ref[rows, :]
        before = pl.ds(pl.multiple_of(jnp.maximum(r0 - SUBLANES, 0), SUBLANES), SUBLANES)
        after = pl.ds(pl.multiple_of(jnp.minimum(r0 + ch, seq - SUBLANES), SUBLANES), SUBLANES)
        u_before = (gc_ref[before, :] * hc_ref[before, :])[SUBLANES - 1:SUBLANES]
        u_after = (gc_ref[after, :] * hc_ref[after, :])[0:1]
        u_before = jnp.where(i == 0, u_meta_last, u_before)
        u_after = jnp.where(i == nchunks - 1, 0.0, u_after)
        u_prev = jnp.where(row == 0, u_before, pltpu.roll(u, 1, 0))
        u_next = jnp.where(row == ch - 1, u_after, pltpu.roll(u, ch - 1, 0))
        y = u_prev * w[0:1] + u * w[1:2] + u_next * w[2:3]
        o_ref[rows, :] = gb_ref[rows, :] * y
        return 0

    lax.fori_loop(0, nchunks, chunk, 0)


def _rms(x, w):
    ms = jnp.mean(x * x, axis=-1, keepdims=True)
    return x * lax.rsqrt(ms + EPS) * w


def _pack_bf16_pairs(x):
    w = x.shape[1] // 2
    lo = lax.bitcast_convert_type(x[:, :w].astype(BF16).astype(F32), jnp.uint32)
    hi = lax.bitcast_convert_type(x[:, w:].astype(BF16).astype(F32), jnp.uint32)
    return (hi & jnp.uint32(0xFFFF0000)) | (lo >> 16)


def _unpack_bf16_pairs(p):
    lo = lax.bitcast_convert_type(p << 16, F32).astype(BF16)
    hi = lax.bitcast_convert_type(p & jnp.uint32(0xFFFF0000), F32).astype(BF16)
    return lo, hi


def _outproj_body(a_ref, c_ref, x_ref, aw_ref, cw_ref, wo_ref, fw_ref, wr_ref, br_ref,
                  h1_ref, hn_ref, lg_ref):
    sub = OUTPROJ_SUB
    blocks = [pl.ds(k * sub, sub) for k in range(a_ref.shape[0] // sub)]

    def mix(rows):
        an = _rms(a_ref[rows, :], aw_ref[...]).astype(BF16)
        cn = _rms(c_ref[rows, :], cw_ref[...]).astype(BF16)
        return (jnp.dot(an, wo_ref[0:ATTN_WIDTH, :], preferred_element_type=F32)
                + jnp.dot(cn, wo_ref[ATTN_WIDTH:D_MODEL, :], preferred_element_type=F32))

    def finish(rows, mixed):
        h1 = x_ref[rows, :] + mixed
        h1_ref[rows, :] = h1
        hn = _rms(h1, fw_ref[...])
        hn_ref[rows, :] = _pack_bf16_pairs(hn)
        lg_ref[rows, :] = jnp.dot(hn.astype(BF16), wr_ref[...], preferred_element_type=F32) + br_ref[...]

    mixed = mix(blocks[0])
    for k, rows in enumerate(blocks):
        following = mix(blocks[k + 1]) if k + 1 < len(blocks) else None
        finish(rows, mixed)
        mixed = following


def _outproj(a, c, x2d, aw, cw, wo_bf16, fw, w_router, b_router):
    t = x2d.shape[0]
    tm = OUTPROJ_TM
    row = lambda i: (i, 0)
    fixed = lambda i: (0, 0)
    return pl.pallas_call(
        _outproj_body,
        grid=(t // tm,),
        in_specs=[pl.BlockSpec((tm, ATTN_WIDTH), row),
                  pl.BlockSpec((tm, CONV_WIDTH), row),
                  pl.BlockSpec((tm, D_MODEL), row),
                  pl.BlockSpec((1, ATTN_WIDTH), fixed),
                  pl.BlockSpec((1, CONV_WIDTH), fixed),
                  pl.BlockSpec((D_MODEL, D_MODEL), fixed, pipeline_mode=pl.Buffered(1)),
                  pl.BlockSpec((1, D_MODEL), fixed),
                  pl.BlockSpec((D_MODEL, LANES), fixed),
                  pl.BlockSpec((1, LANES), fixed)],
        out_specs=[pl.BlockSpec((tm, D_MODEL), row),
                   pl.BlockSpec((tm, D_MODEL // 2), row),
                   pl.BlockSpec((tm, LANES), row)],
        out_shape=[jax.ShapeDtypeStruct((t, D_MODEL), F32),
                   jax.ShapeDtypeStruct((t, D_MODEL // 2), jnp.uint32),
                   jax.ShapeDtypeStruct((t, LANES), F32)],
        compiler_params=_params("arbitrary"),
        name="outproj",
    )(a, c, x2d, aw.reshape(1, -1), cw.reshape(1, -1), wo_bf16, fw.reshape(1, -1), w_router, b_router)


def _route_body(lg_ref, idx_ref, wt_ref, cnt_ref, run_ref, seg_ref):
    final_pass = pl.program_id(0) == 1
    first_tile = pl.program_id(1) == 0

    @pl.when(first_tile & jnp.logical_not(final_pass))
    def _():
        run_ref[...] = jnp.zeros_like(run_ref)
        seg_ref[...] = jnp.zeros_like(seg_ref)

    @pl.when(first_tile & final_pass)
    def _():
        counts = run_ref[...]
        rb = float(EXPERT_ROW_BLOCK)
        padded = jnp.ceil(counts * (1.0 / rb)) * rb
        lane8 = lax.broadcasted_iota(jnp.int32, counts.shape, 1)
        ends = padded
        shift = 1
        while shift < LANES:
            ends = ends + jnp.where(lane8 >= shift, pltpu.roll(ends, shift, 1), 0.0)
            shift *= 2
        seg_ref[...] = ends - padded
        run_ref[...] = jnp.zeros_like(run_ref)

    logits = lg_ref[...]
    tm = logits.shape[0]
    lane = lax.broadcasted_iota(jnp.int32, (tm, LANES), 1)
    neg = -jnp.inf

    def first_argmax(v):
        m = jnp.max(v, axis=-1, keepdims=True)
        first = jnp.min(jnp.where(v == m, lane.astype(F32), float(LANES)), axis=-1, keepdims=True)
        return m, first.astype(jnp.int32)

    gl = jnp.where(lane < N_GROUPS, logits, neg)
    gmax, gidx = first_argmax(gl)
    g_w = 1.0 / jnp.sum(jnp.exp(gl - gmax), axis=-1, keepdims=True)
    first = N_GROUPS + gidx * EXPERTS_PER_GROUP
    el = jnp.where((lane >= first) & (lane < first + EXPERTS_PER_GROUP), logits, neg)
    m0, j0 = first_argmax(el)
    m1, j1 = first_argmax(jnp.where(lane == j0, neg, el))
    p1 = jnp.exp(m1 - m0)
    w0 = g_w / (1.0 + p1)
    w1 = g_w * p1 / (1.0 + p1)
    e0 = j0 - N_GROUPS
    e1 = j1 - N_GROUPS

    onehot = ((lane == e0) | (lane == e1)).astype(BF16)
    seen = run_ref[0:1, :]
    run = seen + jnp.sum(onehot.astype(F32), axis=0, keepdims=True)
    run_ref[...] = jnp.broadcast_to(run, run_ref.shape)

    @pl.when(final_pass)
    def _():
        tri = (lax.broadcasted_iota(jnp.int32, (tm, tm), 0)
               > lax.broadcasted_iota(jnp.int32, (tm, tm), 1)).astype(BF16)
        place = jnp.dot(tri, onehot, preferred_element_type=F32) + seen + seg_ref[0:1, :]
        d0 = jnp.sum(jnp.where(lane == e0, place, 0.0), axis=-1, keepdims=True).astype(jnp.int32)
        d1 = jnp.sum(jnp.where(lane == e1, place, 0.0), axis=-1, keepdims=True).astype(jnp.int32)
        idx_ref[...] = jnp.where(lane == 0, d0, jnp.where(lane == 1, d1, jnp.zeros_like(lane)))
        wt_ref[...] = jnp.where(lane == 0, w0, jnp.where(lane == 1, w1, 0.0))
        cnt_ref[...] = jnp.broadcast_to(run, cnt_ref.shape)


def _route(logits):
    t = logits.shape[0]
    tm = ROUTE_TM
    return pl.pallas_call(
        _route_body,
        grid=(2, t // tm),
        in_specs=[pl.BlockSpec((tm, LANES), lambda p, i: (i, 0))],
        out_specs=[pl.BlockSpec((tm, LANES), lambda p, i: (i * p, 0)),
                   pl.BlockSpec((tm, LANES), lambda p, i: (i * p, 0)),
                   pl.BlockSpec((8, LANES), lambda p, i: (0, 0))],
        out_shape=[jax.ShapeDtypeStruct((t, LANES), jnp.int32),
                   jax.ShapeDtypeStruct((t, LANES), F32),
                   jax.ShapeDtypeStruct((8, LANES), F32)],
        scratch_shapes=[pltpu.VMEM((8, LANES), F32)] * 2,
        compiler_params=_params("arbitrary", "arbitrary"),
        name="route",
    )(logits)


def _expert_body(sbe, sbs, sbn, sbr, dest, seg_fill, tail, hn_hbm, wg_ref, wu_ref, wd_ref,
                 y_hbm, x32, xb, acc, tok, gsem, osem):
    s = pl.program_id(0)
    c = pl.program_id(1)
    nsb = pl.num_programs(0)
    nch = EXPERT_FF // EXPERT_FF_CHUNK
    rb = EXPERT_ROW_BLOCK
    half = D_MODEL // 2
    n = sbn[s]
    start = sbs[s]
    nblk = n // rb
    slot = s % 2

    def for_each(count, fn):
        def body(i, _):
            fn(i)
            return 0
        lax.fori_loop(0, count, body, 0)

    def block_rows(j):
        return pl.ds(pl.multiple_of(j * rb, rb), rb)

    def gather_row(sb_first, buf, i):
        src = hn_hbm.at[pl.ds(tok[sb_first + i], 1)]
        pltpu.make_async_copy(src, x32.at[buf, pl.ds(i, 1)], gsem.at[buf]).start()

    def gather_rows(sb_first, buf, first_row, count):
        def group(g):
            for k in range(GATHER_UNROLL):
                gather_row(sb_first, buf, first_row + g * GATHER_UNROLL + k)
        for_each(count // GATHER_UNROLL, group)

    def wait_gathered(buf, count):
        rows = pl.ds(0, pl.multiple_of(count, GATHER_UNROLL))
        pltpu.make_async_copy(hn_hbm.at[rows], x32.at[buf, rows], gsem.at[buf]).wait()

    def out_copy(first_row, j):
        dst = pl.ds(pl.multiple_of(first_row + j * rb, rb), rb)
        return pltpu.make_async_copy(acc.at[block_rows(j)], y_hbm.at[dst], osem)

    @pl.when((s == 0) & (c == 0))
    def _():
        def clear(j):
            for buf in range(2):
                x32[buf, block_rows(j), :] = jnp.zeros((rb, half), jnp.uint32)
            acc[block_rows(j), :] = jnp.zeros((rb, D_MODEL), F32)
        for_each(EXPERT_CAP // rb, clear)

        def pad_rows(e):
            for k in range(GATHER_UNROLL - 1):
                tok[jnp.minimum(seg_fill[e] + k, tok.shape[0] - 1)] = 0
        for_each(N_EXPERTS, pad_rows)

        def invert(g):
            for k in range(GATHER_UNROLL):
                tok[dest[g * GATHER_UNROLL + k]] = g * (GATHER_UNROLL // TOP_K) + k // TOP_K
        for_each(dest.shape[0] // GATHER_UNROLL, invert)

        gather_rows(sbs[0], 0, 0, sbr[0])

    prev = jnp.maximum(s - 1, 0)
    nxt = jnp.minimum(s + 1, nsb - 1)
    prev_rows = jnp.where(s > 0, sbn[prev], 0)
    prev_pending = (c == 0) & (prev_rows > 0)

    @pl.when((c == 0) & (sbr[s] > 0))
    def _():
        wait_gathered(slot, sbr[s])

    def wait_prev_output():
        for_each(prev_rows // rb, lambda j: out_copy(sbs[prev], j).wait())

    @pl.when(prev_pending & (n == 0))
    def _():
        wait_prev_output()

    @pl.when((s == nsb - 1) & (c == 0))
    def _():
        first = tail[0]
        nfill = (y_hbm.shape[0] - first) // rb
        acc[0:rb, :] = jnp.zeros((rb, D_MODEL), F32)

        def fill_copy(j):
            dst = pl.ds(pl.multiple_of(first + j * rb, rb), rb)
            return pltpu.make_async_copy(acc.at[0:rb], y_hbm.at[dst], osem)

        for_each(nfill, lambda j: fill_copy(j).start())
        for_each(nfill, lambda j: fill_copy(j).wait())

    @pl.when(n > 0)
    def _():
        @pl.when(c == 0)
        def _():
            def unpack(j):
                rows = block_rows(j)
                lo, hi = _unpack_bf16_pairs(x32[slot, rows, :])
                xb[rows, 0:half] = lo
                xb[rows, half:D_MODEL] = hi
            for_each(nblk, unpack)

            @pl.when((s + 1 < nsb) & (sbr[nxt] > 0))
            def _():
                gather_rows(sbs[nxt], 1 - slot, 0, sbr[nxt])

            @pl.when(prev_pending)
            def _():
                wait_prev_output()

        def mlp(first_row, m):
            rows = pl.ds(pl.multiple_of(first_row, rb), m)
            x = xb[rows, :]
            g = jnp.dot(x, wg_ref[0].astype(BF16), preferred_element_type=F32)
            u = jnp.dot(x, wu_ref[0].astype(BF16), preferred_element_type=F32)
            h = (jax.nn.silu(g) * u).astype(BF16)
            y = jnp.dot(h, wd_ref[0].astype(BF16), preferred_element_type=F32)
            acc[rows, :] = jnp.where(c == 0, y, acc[rows, :] + y)

        done = 0
        for m in EXPERT_BLOCKS[:-1]:
            count = (n - done) // m
            for_each(count, lambda j, done=done, m=m: mlp(done + j * m, m))
            done = done + count * m

        @pl.when(done < n)
        def _():
            mlp(done, rb)

        @pl.when(c == nch - 1)
        def _():
            for_each(nblk, lambda j: out_copy(start, j).start())


def _experts(hn_packed, w_gate, w_up, w_down, sb_expert, sb_start, sb_rows, sb_real, dest, seg_fill, tail,
             n_rows):
    n_sb = sb_expert.shape[0]
    nch = EXPERT_FF // EXPERT_FF_CHUNK
    fc = EXPERT_FF_CHUNK

    def chunk(s, c, sbn):
        return jnp.where(sbn[s] > 0, c, nch - 1)

    def up_map(s, c, sbe, sbs, sbn, *_):
        return (sbe[s], 0, chunk(s, c, sbn))

    def down_map(s, c, sbe, sbs, sbn, *_):
        return (sbe[s], chunk(s, c, sbn), 0)

    grid_spec = pltpu.PrefetchScalarGridSpec(
        num_scalar_prefetch=7,
        grid=(n_sb, nch),
        in_specs=[pl.BlockSpec(memory_space=pl.ANY),
                  pl.BlockSpec((1, D_MODEL, fc), up_map),
                  pl.BlockSpec((1, D_MODEL, fc), up_map),
                  pl.BlockSpec((1, fc, D_MODEL), down_map)],
        out_specs=pl.BlockSpec(memory_space=pl.ANY),
        scratch_shapes=[pltpu.VMEM((2, EXPERT_CAP, D_MODEL // 2), jnp.uint32),
                        pltpu.VMEM((EXPERT_CAP, D_MODEL), BF16),
                        pltpu.VMEM((EXPERT_CAP, D_MODEL), F32),
                        pltpu.SMEM((n_rows,), jnp.int32),
                        pltpu.SemaphoreType.DMA((2,)),
                        pltpu.SemaphoreType.DMA(())],
    )
    return pl.pallas_call(
        _expert_body,
        grid_spec=grid_spec,
        out_shape=jax.ShapeDtypeStruct((n_rows, D_MODEL), F32),
        compiler_params=_params("arbitrary", "arbitrary"),
        name="experts",
    )(sb_expert, sb_start, sb_rows, sb_real, dest, seg_fill, tail, hn_packed, w_gate, w_up, w_down)


def _combine_body(dest, h1_ref, wt_ref, y_hbm, o_ref, g, sem):
    i = pl.program_id(0)
    nsteps = pl.num_programs(0)
    groups = h1_ref.shape[0]
    tm = groups * SUBLANES

    def issue(step, slot):
        def f(q, _):
            for u in range(SUBLANES):
                for k in range(TOP_K):
                    row = dest[(step * tm + q * SUBLANES + u) * TOP_K + k]
                    src = y_hbm.at[row >> 3, pl.ds(row & (SUBLANES - 1), 1)]
                    pltpu.make_async_copy(src, g.at[slot, k, q, pl.ds(u, 1)], sem.at[slot]).start()
            return 0
        lax.fori_loop(0, groups, f, 0)

    @pl.when(i == 0)
    def _():
        issue(0, 0)

    @pl.when(i + 1 < nsteps)
    def _():
        issue(i + 1, (i + 1) % 2)

    slot = i % 2
    for k in range(TOP_K):
        pltpu.make_async_copy(y_hbm.at[pl.ds(0, groups)], g.at[slot, k], sem.at[slot]).wait()

    w = wt_ref[...]
    o_ref[...] = h1_ref[...] + (w[:, :, 0:1] * g[slot, 0] + w[:, :, 1:2] * g[slot, 1])


def _combine(h1, wts, y_buf, dest_flat):
    t = h1.shape[0]
    groups = COMBINE_TM // SUBLANES
    by_group = lambda a: a.reshape(a.shape[0] // SUBLANES, SUBLANES, a.shape[1])
    grid_spec = pltpu.PrefetchScalarGridSpec(
        num_scalar_prefetch=1,
        grid=(t // COMBINE_TM,),
        in_specs=[pl.BlockSpec((groups, SUBLANES, D_MODEL), lambda i, d: (i, 0, 0)),
                  pl.BlockSpec((groups, SUBLANES, LANES), lambda i, d: (i, 0, 0)),
                  pl.BlockSpec(memory_space=pl.ANY)],
        out_specs=pl.BlockSpec((groups, SUBLANES, D_MODEL), lambda i, d: (i, 0, 0)),
        scratch_shapes=[pltpu.VMEM((2, TOP_K, groups, SUBLANES, D_MODEL), F32),
                        pltpu.SemaphoreType.DMA((2,))],
    )
    out = pl.pallas_call(
        _combine_body,
        grid_spec=grid_spec,
        out_shape=jax.ShapeDtypeStruct((t // SUBLANES, SUBLANES, D_MODEL), F32),
        compiler_params=_params("arbitrary"),
        name="combine",
    )(dest_flat, by_group(h1), by_group(wts), by_group(y_buf))
    return out.reshape(t, D_MODEL)


def _dispatch_tables(idx, cnt, t):
    rb, cap = EXPERT_ROW_BLOCK, EXPERT_CAP
    n_assign = t * TOP_K
    n_rows = -(-(n_assign + N_EXPERTS * (rb - 1)) // rb) * rb
    n_sb = (n_rows + N_EXPERTS * (cap - rb)) // cap
    dest = idx[:, 0:TOP_K].reshape(-1)
    counts = cnt[0, :N_EXPERTS].astype(jnp.int32)
    padded = (counts + rb - 1) // rb * rb
    seg_end = jnp.cumsum(padded)
    seg_start = (seg_end - padded).astype(jnp.int32)
    seg_fill = (seg_start + counts).astype(jnp.int32)

    sb_per_expert = (padded + cap - 1) // cap
    sb_end = jnp.cumsum(sb_per_expert)
    total = sb_end[-1]
    s = jnp.arange(n_sb, dtype=jnp.int32)
    s_eff = jnp.minimum(s, total - 1)
    e = jnp.minimum(jnp.sum(sb_end[None, :] <= s_eff[:, None], axis=1), N_EXPERTS - 1).astype(jnp.int32)
    local = s_eff - (sb_end[e] - sb_per_expert[e])
    sb_start = (seg_start[e] + local * cap).astype(jnp.int32)
    sb_rows = jnp.where(s < total, jnp.clip(padded[e] - local * cap, 0, cap), 0).astype(jnp.int32)
    gathered = (counts + GATHER_UNROLL - 1) // GATHER_UNROLL * GATHER_UNROLL
    sb_real = jnp.where(s < total, jnp.clip(gathered[e] - local * cap, 0, cap), 0).astype(jnp.int32)
    tail = seg_end[-1:].astype(jnp.int32)
    return dest, seg_fill, e, sb_start, sb_rows, sb_real, tail, n_rows


def kernel(x, meta_tokens, mix_norm_w, w_in, q_norm_w, k_norm_w, rel_bias, meta_bias, conv_w,
           attn_out_norm_w, conv_out_norm_w, w_out, ffn_norm_w, w_router_group, b_router_group,
           w_router_expert, b_router_expert, w_gate, w_up, w_down):
    bsz, seq, d = x.shape
    depth = mix_norm_w.shape[0]
    assert depth == 1 and d == D_MODEL and seq % GRID_W == 0
    t = bsz * seq
    x2d = x.reshape(t, d)
    l = 0

    w_in_b = w_in[l].astype(BF16)
    proj, proj_meta = _inproj(x2d, meta_tokens.astype(x.dtype), mix_norm_w[l], w_in_b)

    a, c = _mixers(proj, proj_meta, q_norm_w[l], k_norm_w[l], rel_bias[l], meta_bias[l], conv_w[l], bsz, seq)

    spare = LANES - N_GROUPS - N_EXPERTS
    w_router = jnp.concatenate([w_router_group[l].astype(F32), w_router_expert[l].astype(F32),
                                jnp.zeros((d, spare), F32)], axis=1)
    b_router = jnp.concatenate([b_router_group[l].astype(F32), b_router_expert[l].astype(F32),
                                jnp.zeros((spare,), F32)]).reshape(1, LANES)
    h1, hn, logits = _outproj(a, c, x2d, attn_out_norm_w[l], conv_out_norm_w[l], w_out[l].astype(BF16),
                              ffn_norm_w[l], w_router.astype(BF16), b_router)

    idx, wts, cnt = _route(logits)
    dest, seg_fill, sb_expert, sb_start, sb_rows, sb_real, tail, n_rows = _dispatch_tables(idx, cnt, t)
    y_buf = _experts(hn, w_gate.reshape(N_EXPERTS, d, EXPERT_FF), w_up.reshape(N_EXPERTS, d, EXPERT_FF),
                     w_down.reshape(N_EXPERTS, EXPERT_FF, d), sb_expert, sb_start, sb_rows, sb_real,
                     dest, seg_fill, tail, n_rows)
    out = _combine(h1, wts, y_buf, dest)
    return out.reshape(bsz, seq, d)
```

```python
import functools

import jax
import jax.numpy as jnp
from jax import lax
from jax.experimental import pallas as pl
from jax.experimental.pallas import tpu as pltpu

F32 = jnp.float32
BF16 = jnp.bfloat16

D_MODEL = 2048
N_META = 16
GRID_W = 64
N_HEADS = 16
HEAD_DIM = 64
ATTN_WIDTH = N_HEADS * HEAD_DIM
CONV_WIDTH = D_MODEL - ATTN_WIDTH
PROJ_TOTAL = 3 * ATTN_WIDTH + 3 * CONV_WIDTH
WIN_ROWS = 8
WIN_COLS = 16
N_GROUPS = 4
EXPERTS_PER_GROUP = 8
N_EXPERTS = N_GROUPS * EXPERTS_PER_GROUP
TOP_K = 2
EXPERT_FF = 1024
EPS = 1e-6

LANES = 128
SUBLANES = 8
VMEM_LIMIT = 52 * 1024 * 1024

ATTN_ROWS_PER_STEP = 4
CONV_CHUNK = 256
INPROJ_TM = 1024
INPROJ_TN = 1024
OUTPROJ_TM = 512
OUTPROJ_SUB = 256
ROUTE_TM = 1024
EXPERT_ROW_BLOCK = 128
EXPERT_BLOCKS = (512, 256, 128)
EXPERT_CAP = 1024
EXPERT_FF_CHUNK = 512
GATHER_UNROLL = 8
GATHER_AHEAD = 2
GATHER_BUFFERS = GATHER_AHEAD + 1
COMBINE_TM = 512


def _params(*sem):
    return pltpu.CompilerParams(dimension_semantics=sem, vmem_limit_bytes=VMEM_LIMIT)


def _inproj_body(x_ref, meta_ref, nw_ref, w_ref, o_ref, om_ref, wb_ref):
    def normed(v):
        ms = jnp.mean(v * v, axis=-1, keepdims=True)
        return (v * lax.rsqrt(ms + EPS) * nw_ref[...]).astype(BF16)

    @pl.when(pl.program_id(1) == 0)
    def _():
        wb_ref[...] = w_ref[...].astype(BF16)
        om_ref[...] = jnp.dot(normed(meta_ref[...]), wb_ref[...], preferred_element_type=F32)

    o_ref[...] = jnp.dot(normed(x_ref[...]), wb_ref[...], preferred_element_type=F32)


def _inproj(x2d, meta, norm_w, w):
    m = x2d.shape[0]
    tm, tn = INPROJ_TM, INPROJ_TN
    return pl.pallas_call(
        _inproj_body,
        grid=(PROJ_TOTAL // tn, m // tm),
        in_specs=[pl.BlockSpec((tm, D_MODEL), lambda j, i: (i, 0)),
                  pl.BlockSpec((N_META, D_MODEL), lambda j, i: (0, 0)),
                  pl.BlockSpec((1, D_MODEL), lambda j, i: (0, 0)),
                  pl.BlockSpec((D_MODEL, tn), lambda j, i: (0, j))],
        out_specs=[pl.BlockSpec((tm, tn), lambda j, i: (i, j)),
                   pl.BlockSpec((N_META, tn), lambda j, i: (0, j))],
        out_shape=[jax.ShapeDtypeStruct((m, PROJ_TOTAL), F32),
                   jax.ShapeDtypeStruct((N_META, PROJ_TOTAL), F32)],
        scratch_shapes=[pltpu.VMEM((D_MODEL, tn), BF16)],
        compiler_params=_params("arbitrary", "arbitrary"),
        name="inproj",
    )(x2d, meta, norm_w.reshape(1, D_MODEL), w)


def _head_norm(x, w, lo):
    x2 = x * x
    s_lo = jnp.sum(jnp.where(lo, x2, 0.0), axis=-1, keepdims=True)
    s_hi = jnp.sum(jnp.where(lo, 0.0, x2), axis=-1, keepdims=True)
    ms = jnp.where(lo, s_lo, s_hi) * (1.0 / HEAD_DIM)
    return x * lax.rsqrt(ms + EPS) * w


def _attn_body(q_ref, k_ref, v_ref, km_ref, vm_ref, qw_ref, kw_ref, bias_ref, mb_ref,
               gb_ref, gc_ref, hc_ref, gcm_ref, hcm_ref, cw_ref, o_ref, c_ref,
               qs, ks, vs, sc, smc, pr, pmr, *, rows):
    _conv_body(gb_ref, gc_ref, hc_ref, gcm_ref, hcm_ref, cw_ref, c_ref)
    lo = lax.broadcasted_iota(jnp.int32, (1, LANES), 1) < HEAD_DIM
    scale = HEAD_DIM ** -0.5
    chunk = 256
    seq = rows * GRID_W

    kmb = _head_norm(km_ref[...], kw_ref[...], lo).astype(BF16)
    vmb = vm_ref[...].astype(BF16)
    contract_last = (((1,), (1,)), ((), ()))
    head_masks = (lo, jnp.logical_not(lo))

    def one_head(x, h):
        return jnp.where(head_masks[h], x, jnp.zeros_like(x))

    def prep(i, _):
        sl = pl.ds(pl.multiple_of(i * chunk, chunk), chunk)
        qs[sl, :] = (_head_norm(q_ref[sl, :], qw_ref[...], lo) * scale).astype(BF16)
        ks[sl, :] = _head_norm(k_ref[sl, :], kw_ref[...], lo).astype(BF16)
        vs[sl, :] = v_ref[sl, :].astype(BF16)
        return 0

    lax.fori_loop(0, seq // chunk, prep, 0, unroll=2)
    wr = min(WIN_ROWS, rows)
    nk = wr * GRID_W

    def window_start(r):
        return jnp.clip(r - wr // 2, 0, rows - wr)

    def row_slice(r, n):
        return pl.ds(pl.multiple_of(r * GRID_W, GRID_W), n)

    def store_scores(r, s_ref, sm_ref):
        rs = window_start(r)
        si = rs - r + (WIN_ROWS - 1)
        q_r = qs[row_slice(r, GRID_W), :]
        kwin = ks[row_slice(rs, nk), :]
        for h in range(2):
            qh = one_head(q_r, h)
            bias = jnp.concatenate([bias_ref[h, si + w] for w in range(0, wr, 2)], axis=-1)
            s_ref[h] = lax.dot_general(qh, kwin, contract_last, preferred_element_type=F32) + bias
            sm_ref[h] = lax.dot_general(qh, kmb, contract_last, preferred_element_type=F32) + mb_ref[h]

    def store_softmax(s_ref, sm_ref, p_ref, pm_ref):
        for h in range(2):
            s = s_ref[h]
            sm = sm_ref[h]
            m = jnp.maximum(jnp.max(s, axis=-1, keepdims=True), jnp.max(sm, axis=-1, keepdims=True))
            p = jnp.exp(s - m)
            pm = jnp.exp(sm - m)
            inv = 1.0 / (jnp.sum(p, axis=-1, keepdims=True) + jnp.sum(pm, axis=-1, keepdims=True))
            p_ref[h] = (p * inv).astype(BF16)
            pm_ref[h] = (pm * inv).astype(BF16)

    def weighted_values(r, p_ref, pm_ref):
        vwin = vs[row_slice(window_start(r), nk), :]
        outs = [jnp.dot(p_ref[h], vwin, preferred_element_type=F32)
                + jnp.dot(pm_ref[h], vmb, preferred_element_type=F32) for h in range(2)]
        o_ref[row_slice(r, GRID_W), :] = jnp.where(lo, outs[0], outs[1])

    per = ATTN_ROWS_PER_STEP
    groups = rows // per

    def step(j, parity, do_scores, do_softmax, do_values):
        if do_scores:
            for k in range(per):
                store_scores(per * j + k, sc.at[parity, k], smc.at[parity, k])
        if do_values:
            for k in range(per):
                weighted_values(per * (j - 2) + k, pr.at[1 - parity, k], pmr.at[1 - parity, k])
        if do_softmax:
            for k in range(per):
                store_softmax(sc.at[1 - parity, k], smc.at[1 - parity, k], pr.at[parity, k], pmr.at[parity, k])

    step(0, 0, True, False, False)
    step(1, 1, True, True, False)

    def two_steps(i, _):
        step(2 * i, 0, True, True, True)
        step(2 * i + 1, 1, True, True, True)
        return 0

    lax.fori_loop(1, groups // 2, two_steps, 0)
    step(groups, 0, False, True, True)
    step(groups + 1, 1, False, False, True)


def _bias_table(rel_bias):
    c = jnp.arange(GRID_W)
    col_start = jnp.clip(c - WIN_COLS // 2, 0, GRID_W - WIN_COLS)
    col_mask = (c[None, :] >= col_start[:, None]) & (c[None, :] < col_start[:, None] + WIN_COLS)
    dc = jnp.clip(c[None, :] - c[:, None], -(WIN_COLS - 1), WIN_COLS - 1) + (WIN_COLS - 1)
    ncol = 2 * WIN_COLS - 1
    pair_bias = jnp.concatenate([rel_bias[:, :-1], rel_bias[:, 1:]], axis=-1).astype(F32)
    onehot = (dc[None] == jnp.arange(ncol)[:, None, None]).astype(F32)
    zeros = jnp.zeros_like(onehot)
    pair_onehot = jnp.concatenate([jnp.concatenate([onehot, zeros], axis=-1),
                                   jnp.concatenate([zeros, onehot], axis=-1)], axis=0)
    table = jnp.einsum('hdm,mqn->hdqn', pair_bias, pair_onehot, precision=lax.Precision.HIGHEST)
    pair_mask = jnp.concatenate([col_mask, col_mask], axis=-1)
    return jnp.where(pair_mask[None, None], table, -jnp.inf)


def _mixers(proj, proj_meta, q_norm_w, k_norm_w, rel_bias, meta_bias, conv_w, bsz, seq):
    rows = seq // GRID_W
    assert rows >= WIN_ROWS and rows % (2 * ATTN_ROWS_PER_STEP) == 0
    nk = WIN_ROWS * GRID_W
    npairs = N_HEADS // 2
    assert CONV_WIDTH // LANES == npairs
    conv_base = 3 * ATTN_WIDTH // LANES
    qw = jnp.tile(q_norm_w.astype(F32), 2).reshape(1, LANES)
    kw = jnp.tile(k_norm_w.astype(F32), 2).reshape(1, LANES)
    bias = _bias_table(rel_bias)
    mb = meta_bias.astype(F32).reshape(N_HEADS, 1, N_META)
    return pl.pallas_call(
        functools.partial(_attn_body, rows=rows),
        grid=(bsz, npairs),
        in_specs=[pl.BlockSpec((seq, LANES), lambda b, p: (b, p)),
                  pl.BlockSpec((seq, LANES), lambda b, p: (b, npairs + p)),
                  pl.BlockSpec((seq, LANES), lambda b, p: (b, 2 * npairs + p)),
                  pl.BlockSpec((N_META, LANES), lambda b, p: (0, npairs + p)),
                  pl.BlockSpec((N_META, LANES), lambda b, p: (0, 2 * npairs + p)),
                  pl.BlockSpec((1, LANES), lambda b, p: (0, 0)),
                  pl.BlockSpec((1, LANES), lambda b, p: (0, 0)),
                  pl.BlockSpec((2, 2 * WIN_ROWS - 2, GRID_W, LANES), lambda b, p: (p, 0, 0, 0)),
                  pl.BlockSpec((2, 1, N_META), lambda b, p: (p, 0, 0)),
                  pl.BlockSpec((seq, LANES), lambda b, p: (b, conv_base + p)),
                  pl.BlockSpec((seq, LANES), lambda b, p: (b, conv_base + npairs + p)),
                  pl.BlockSpec((seq, LANES), lambda b, p: (b, conv_base + 2 * npairs + p)),
                  pl.BlockSpec((N_META, LANES), lambda b, p: (0, conv_base + npairs + p)),
                  pl.BlockSpec((N_META, LANES), lambda b, p: (0, conv_base + 2 * npairs + p)),
                  pl.BlockSpec((3, LANES), lambda b, p: (0, p))],
        out_specs=[pl.BlockSpec((seq, LANES), lambda b, p: (b, p)),
                   pl.BlockSpec((seq, LANES), lambda b, p: (b, p))],
        out_shape=[jax.ShapeDtypeStruct((bsz * seq, ATTN_WIDTH), F32),
                   jax.ShapeDtypeStruct((bsz * seq, CONV_WIDTH), F32)],
        scratch_shapes=([pltpu.VMEM((seq, LANES), BF16)] * 3
                        + [pltpu.VMEM((2, ATTN_ROWS_PER_STEP, 2, GRID_W, nk), F32),
                           pltpu.VMEM((2, ATTN_ROWS_PER_STEP, 2, GRID_W, N_META), F32),
                           pltpu.VMEM((2, ATTN_ROWS_PER_STEP, 2, GRID_W, nk), BF16),
                           pltpu.VMEM((2, ATTN_ROWS_PER_STEP, 2, GRID_W, N_META), BF16)]),
        compiler_params=_params("arbitrary", "arbitrary"),
        name="mixers",
    )(proj, proj, proj, proj_meta, proj_meta, qw, kw, bias, mb,
      proj, proj, proj, proj_meta, proj_meta, conv_w.astype(F32))


def _conv_body(gb_ref, gc_ref, hc_ref, gcm_ref, hcm_ref, w_ref, o_ref):
    seq = gb_ref.shape[0]
    ch = CONV_CHUNK
    nchunks = seq // ch
    w = w_ref[...]
    u_meta_last = gcm_ref[N_META - 1:N_META, :] * hcm_ref[N_META - 1:N_META, :]
    row = lax.broadcasted_iota(jnp.int32, (ch, 1), 0)

    def chunk(i, _):
        r0 = pl.multiple_of(i * ch, ch)
        rows = pl.ds(r0, ch)
        u = gc_ref[rows, :] * hc_ref[rows, :]
        before = pl.ds(pl.multiple_of(jnp.maximum(r0 - SUBLANES, 0), SUBLANES), SUBLANES)
        after = pl.ds(pl.multiple_of(jnp.minimum(r0 + ch, seq - SUBLANES), SUBLANES), SUBLANES)
        u_before = (gc_ref[before, :] * hc_ref[before, :])[SUBLANES - 1:SUBLANES]
        u_after = (gc_ref[after, :] * hc_ref[after, :])[0:1]
        u_before = jnp.where(i == 0, u_meta_last, u_before)
        u_after = jnp.where(i == nchunks - 1, 0.0, u_after)
        u_prev = jnp.where(row == 0, u_before, pltpu.roll(u, 1, 0))
        u_next = jnp.where(row == ch - 1, u_after, pltpu.roll(u, ch - 1, 0))
        y = u_prev * w[0:1] + u * w[1:2] + u_next * w[2:3]
        o_ref[rows, :] = gb_ref[rows, :] * y
        return 0

    lax.fori_loop(0, nchunks, chunk, 0)


def _rms(x, w):
    ms = jnp.mean(x * x, axis=-1, keepdims=True)
    return x * lax.rsqrt(ms + EPS) * w


def _pack_bf16_pairs(x):
    w = x.shape[1] // 2
    lo = lax.bitcast_convert_type(x[:, :w].astype(BF16).astype(F32), jnp.uint32)
    hi = lax.bitcast_convert_type(x[:, w:].astype(BF16).astype(F32), jnp.uint32)
    return (hi & jnp.uint32(0xFFFF0000)) | (lo >> 16)


def _unpack_bf16_pairs(p):
    lo = lax.bitcast_convert_type(p << 16, F32).astype(BF16)
    hi = lax.bitcast_convert_type(p & jnp.uint32(0xFFFF0000), F32).astype(BF16)
    return lo, hi


def _outproj_body(a_ref, c_ref, x_ref, aw_ref, cw_ref, wo_ref, fw_ref, wr_ref, br_ref,
                  h1_ref, hn_ref, lg_ref):
    sub = OUTPROJ_SUB
    blocks = [pl.ds(k * sub, sub) for k in range(a_ref.shape[0] // sub)]

    def mix(rows):
        an = _rms(a_ref[rows, :], aw_ref[...]).astype(BF16)
        cn = _rms(c_ref[rows, :], cw_ref[...]).astype(BF16)
        return (jnp.dot(an, wo_ref[0:ATTN_WIDTH, :], preferred_element_type=F32)
                + jnp.dot(cn, wo_ref[ATTN_WIDTH:D_MODEL, :], preferred_element_type=F32))

    def finish(rows, mixed):
        h1 = x_ref[rows, :] + mixed
        h1_ref[rows, :] = h1
        hn = _rms(h1, fw_ref[...])
        hn_ref[rows, :] = _pack_bf16_pairs(hn)
        lg_ref[rows, :] = jnp.dot(hn.astype(BF16), wr_ref[...], preferred_element_type=F32) + br_ref[...]

    mixed = mix(blocks[0])
    for k, rows in enumerate(blocks):
        following = mix(blocks[k + 1]) if k + 1 < len(blocks) else None
        finish(rows, mixed)
        mixed = following


def _outproj(a, c, x2d, aw, cw, wo_bf16, fw, w_router, b_router):
    t = x2d.shape[0]
    tm = OUTPROJ_TM
    row = lambda i: (i, 0)
    fixed = lambda i: (0, 0)
    return pl.pallas_call(
        _outproj_body,
        grid=(t // tm,),
        in_specs=[pl.BlockSpec((tm, ATTN_WIDTH), row),
                  pl.BlockSpec((tm, CONV_WIDTH), row),
                  pl.BlockSpec((tm, D_MODEL), row),
                  pl.BlockSpec((1, ATTN_WIDTH), fixed),
                  pl.BlockSpec((1, CONV_WIDTH), fixed),
                  pl.BlockSpec((D_MODEL, D_MODEL), fixed, pipeline_mode=pl.Buffered(1)),
                  pl.BlockSpec((1, D_MODEL), fixed),
                  pl.BlockSpec((D_MODEL, LANES), fixed),
                  pl.BlockSpec((1, LANES), fixed)],
        out_specs=[pl.BlockSpec((tm, D_MODEL), row),
                   pl.BlockSpec((tm, D_MODEL // 2), row),
                   pl.BlockSpec((tm, LANES), row)],
        out_shape=[jax.ShapeDtypeStruct((t, D_MODEL), F32),
                   jax.ShapeDtypeStruct((t, D_MODEL // 2), jnp.uint32),
                   jax.ShapeDtypeStruct((t, LANES), F32)],
        compiler_params=_params("arbitrary"),
        name="outproj",
    )(a, c, x2d, aw.reshape(1, -1), cw.reshape(1, -1), wo_bf16, fw.reshape(1, -1), w_router, b_router)


def _route_body(lg_ref, idx_ref, wt_ref, cnt_ref, run_ref, seg_ref):
    final_pass = pl.program_id(0) == 1
    first_tile = pl.program_id(1) == 0

    @pl.when(first_tile & jnp.logical_not(final_pass))
    def _():
        run_ref[...] = jnp.zeros_like(run_ref)
        seg_ref[...] = jnp.zeros_like(seg_ref)

    @pl.when(first_tile & final_pass)
    def _():
        counts = run_ref[...]
        rb = float(EXPERT_ROW_BLOCK)
        padded = jnp.ceil(counts * (1.0 / rb)) * rb
        lane8 = lax.broadcasted_iota(jnp.int32, counts.shape, 1)
        ends = padded
        shift = 1
        while shift < LANES:
            ends = ends + jnp.where(lane8 >= shift, pltpu.roll(ends, shift, 1), 0.0)
            shift *= 2
        seg_ref[...] = ends - padded
        run_ref[...] = jnp.zeros_like(run_ref)

    logits = lg_ref[...]
    tm = logits.shape[0]
    lane = lax.broadcasted_iota(jnp.int32, (tm, LANES), 1)
    neg = -jnp.inf

    def first_argmax(v):
        m = jnp.max(v, axis=-1, keepdims=True)
        first = jnp.min(jnp.where(v == m, lane.astype(F32), float(LANES)), axis=-1, keepdims=True)
        return m, first.astype(jnp.int32)

    gl = jnp.where(lane < N_GROUPS, logits, neg)
    gmax, gidx = first_argmax(gl)
    g_w = 1.0 / jnp.sum(jnp.exp(gl - gmax), axis=-1, keepdims=True)
    first = N_GROUPS + gidx * EXPERTS_PER_GROUP
    el = jnp.where((lane >= first) & (lane < first + EXPERTS_PER_GROUP), logits, neg)
    m0, j0 = first_argmax(el)
    m1, j1 = first_argmax(jnp.where(lane == j0, neg, el))
    p1 = jnp.exp(m1 - m0)
    w0 = g_w / (1.0 + p1)
    w1 = g_w * p1 / (1.0 + p1)
    e0 = j0 - N_GROUPS
    e1 = j1 - N_GROUPS

    onehot = ((lane == e0) | (lane == e1)).astype(BF16)
    seen = run_ref[0:1, :]
    run = seen + jnp.sum(onehot.astype(F32), axis=0, keepdims=True)
    run_ref[...] = jnp.broadcast_to(run, run_ref.shape)

    @pl.when(final_pass)
    def _():
        tri = (lax.broadcasted_iota(jnp.int32, (tm, tm), 0)
               > lax.broadcasted_iota(jnp.int32, (tm, tm), 1)).astype(BF16)
        place = jnp.dot(tri, onehot, preferred_element_type=F32) + seen + seg_ref[0:1, :]
        d0 = jnp.sum(jnp.where(lane == e0, place, 0.0), axis=-1, keepdims=True).astype(jnp.int32)
        d1 = jnp.sum(jnp.where(lane == e1, place, 0.0), axis=-1, keepdims=True).astype(jnp.int32)
        idx_ref[...] = jnp.where(lane == 0, d0, jnp.where(lane == 1, d1, jnp.zeros_like(lane)))
        wt_ref[...] = jnp.where(lane == 0, w0, jnp.where(lane == 1, w1, 0.0))
        cnt_ref[...] = jnp.broadcast_to(run, cnt_ref.shape)


def _route(logits):
    t = logits.shape[0]
    tm = ROUTE_TM
    return pl.pallas_call(
        _route_body,
        grid=(2, t // tm),
        in_specs=[pl.BlockSpec((tm, LANES), lambda p, i: (i, 0))],
        out_specs=[pl.BlockSpec((tm, LANES), lambda p, i: (i * p, 0)),
                   pl.BlockSpec((tm, LANES), lambda p, i: (i * p, 0)),
                   pl.BlockSpec((8, LANES), lambda p, i: (0, 0))],
        out_shape=[jax.ShapeDtypeStruct((t, LANES), jnp.int32),
                   jax.ShapeDtypeStruct((t, LANES), F32),
                   jax.ShapeDtypeStruct((8, LANES), F32)],
        scratch_shapes=[pltpu.VMEM((8, LANES), F32)] * 2,
        compiler_params=_params("arbitrary", "arbitrary"),
        name="route",
    )(logits)


def _expert_body(sbe, sbs, sbn, sbr, dest, seg_fill, tail, hn_hbm, wg_ref, wu_ref, wd_ref,
                 y_hbm, x32, xb, acc, tok, gsem, osem):
    s = pl.program_id(0)
    c = pl.program_id(1)
    nsb = pl.num_programs(0)
    nch = EXPERT_FF // EXPERT_FF_CHUNK
    rb = EXPERT_ROW_BLOCK
    half = D_MODEL // 2
    n = sbn[s]
    start = sbs[s]
    nblk = n // rb
    slot = s % GATHER_BUFFERS

    def for_each(count, fn):
        def body(i, _):
            fn(i)
            return 0
        lax.fori_loop(0, count, body, 0)

    def block_rows(j):
        return pl.ds(pl.multiple_of(j * rb, rb), rb)

    def gather_row(sb_first, buf, i):
        src = hn_hbm.at[pl.ds(tok[sb_first + i], 1)]
        pltpu.make_async_copy(src, x32.at[buf, pl.ds(i, 1)], gsem.at[buf]).start()

    def gather_rows(sb_first, buf, first_row, count):
        def group(g):
            for k in range(GATHER_UNROLL):
                gather_row(sb_first, buf, first_row + g * GATHER_UNROLL + k)
        for_each(count // GATHER_UNROLL, group)

    def wait_gathered(buf, count):
        rows = pl.ds(0, pl.multiple_of(count, GATHER_UNROLL))
        pltpu.make_async_copy(hn_hbm.at[rows], x32.at[buf, rows], gsem.at[buf]).wait()

    def out_copy(first_row, j):
        dst = pl.ds(pl.multiple_of(first_row + j * rb, rb), rb)
        return pltpu.make_async_copy(acc.at[block_rows(j)], y_hbm.at[dst], osem)

    @pl.when((s == 0) & (c == 0))
    def _():
        def clear(j):
            for buf in range(GATHER_BUFFERS):
                x32[buf, block_rows(j), :] = jnp.zeros((rb, half), jnp.uint32)
            acc[block_rows(j), :] = jnp.zeros((rb, D_MODEL), F32)
        for_each(EXPERT_CAP // rb, clear)

        def pad_rows(e):
            for k in range(GATHER_UNROLL - 1):
                tok[jnp.minimum(seg_fill[e] + k, tok.shape[0] - 1)] = 0
        for_each(N_EXPERTS, pad_rows)

        def invert(g):
            for k in range(GATHER_UNROLL):
                tok[dest[g * GATHER_UNROLL + k]] = g * (GATHER_UNROLL // TOP_K) + k // TOP_K
        for_each(dest.shape[0] // GATHER_UNROLL, invert)

        for first in range(GATHER_AHEAD):
            gather_rows(sbs[first], first, 0, sbr[first])

    prev = jnp.maximum(s - 1, 0)
    prev_rows = jnp.where(s > 0, sbn[prev], 0)
    prev_pending = (c == 0) & (prev_rows > 0)

    @pl.when((c == 0) & (sbr[s] > 0))
    def _():
        wait_gathered(slot, sbr[s])

    def wait_prev_output():
        for_each(prev_rows // rb, lambda j: out_copy(sbs[prev], j).wait())

    @pl.when(prev_pending & (n == 0))
    def _():
        wait_prev_output()

    @pl.when((s == nsb - 1) & (c == 0))
    def _():
        first = tail[0]
        nfill = (y_hbm.shape[0] - first) // rb
        acc[0:rb, :] = jnp.zeros((rb, D_MODEL), F32)

        def fill_copy(j):
            dst = pl.ds(pl.multiple_of(first + j * rb, rb), rb)
            return pltpu.make_async_copy(acc.at[0:rb], y_hbm.at[dst], osem)

        for_each(nfill, lambda j: fill_copy(j).start())
        for_each(nfill, lambda j: fill_copy(j).wait())

    @pl.when(n > 0)
    def _():
        @pl.when(c == 0)
        def _():
            def unpack(j):
                rows = block_rows(j)
                lo, hi = _unpack_bf16_pairs(x32[slot, rows, :])
                xb[rows, 0:half] = lo
                xb[rows, half:D_MODEL] = hi
            for_each(nblk, unpack)

            @pl.when(prev_pending)
            def _():
                wait_prev_output()

        ahead = jnp.minimum(s + GATHER_AHEAD, nsb - 1)

        @pl.when((c == nch - 1) & (s + GATHER_AHEAD < nsb) & (sbr[ahead] > 0))
        def _():
            gather_rows(sbs[ahead], (s + GATHER_AHEAD) % GATHER_BUFFERS, 0, sbr[ahead])

        def mlp(first_row, m):
            rows = pl.ds(pl.multiple_of(first_row, rb), m)
            x = xb[rows, :]
            g = jnp.dot(x, wg_ref[0].astype(BF16), preferred_element_type=F32)
            u = jnp.dot(x, wu_ref[0].astype(BF16), preferred_element_type=F32)
            h = (jax.nn.silu(g) * u).astype(BF16)
            y = jnp.dot(h, wd_ref[0].astype(BF16), preferred_element_type=F32)
            acc[rows, :] = jnp.where(c == 0, y, acc[rows, :] + y)

        done = 0
        for m in EXPERT_BLOCKS[:-1]:
            count = (n - done) // m
            for_each(count, lambda j, done=done, m=m: mlp(done + j * m, m))
            done = done + count * m

        @pl.when(done < n)
        def _():
            mlp(done, rb)

        @pl.when(c == nch - 1)
        def _():
            for_each(nblk, lambda j: out_copy(start, j).start())


def _experts(hn_packed, w_gate, w_up, w_down, sb_expert, sb_start, sb_rows, sb_real, dest, seg_fill, tail,
             n_rows):
    n_sb = sb_expert.shape[0]
    nch = EXPERT_FF // EXPERT_FF_CHUNK
    fc = EXPERT_FF_CHUNK

    def chunk(s, c, sbn):
        return jnp.where(sbn[s] > 0, c, nch - 1)

    def up_map(s, c, sbe, sbs, sbn, *_):
        return (sbe[s], 0, chunk(s, c, sbn))

    def down_map(s, c, sbe, sbs, sbn, *_):
        return (sbe[s], chunk(s, c, sbn), 0)

    grid_spec = pltpu.PrefetchScalarGridSpec(
        num_scalar_prefetch=7,
        grid=(n_sb, nch),
        in_specs=[pl.BlockSpec(memory_space=pl.ANY),
                  pl.BlockSpec((1, D_MODEL, fc), up_map),
                  pl.BlockSpec((1, D_MODEL, fc), up_map),
                  pl.BlockSpec((1, fc, D_MODEL), down_map)],
        out_specs=pl.BlockSpec(memory_space=pl.ANY),
        scratch_shapes=[pltpu.VMEM((GATHER_BUFFERS, EXPERT_CAP, D_MODEL // 2), jnp.uint32),
                        pltpu.VMEM((EXPERT_CAP, D_MODEL), BF16),
                        pltpu.VMEM((EXPERT_CAP, D_MODEL), F32),
                        pltpu.SMEM((n_rows,), jnp.int32),
                        pltpu.SemaphoreType.DMA((GATHER_BUFFERS,)),
                        pltpu.SemaphoreType.DMA(())],
    )
    return pl.pallas_call(
        _expert_body,
        grid_spec=grid_spec,
        out_shape=jax.ShapeDtypeStruct((n_rows, D_MODEL), F32),
        compiler_params=_params("arbitrary", "arbitrary"),
        name="experts",
    )(sb_expert, sb_start, sb_rows, sb_real, dest, seg_fill, tail, hn_packed, w_gate, w_up, w_down)


def _combine_body(dest, h1_ref, wt_ref, y_hbm, o_ref, g, sem):
    i = pl.program_id(0)
    nsteps = pl.num_programs(0)
    groups = h1_ref.shape[0]
    tm = groups * SUBLANES

    def issue(step, slot):
        def f(q, _):
            for u in range(SUBLANES):
                for k in range(TOP_K):
                    row = dest[(step * tm + q * SUBLANES + u) * TOP_K + k]
                    src = y_hbm.at[row >> 3, pl.ds(row & (SUBLANES - 1), 1)]
                    pltpu.make_async_copy(src, g.at[slot, k, q, pl.ds(u, 1)], sem.at[slot]).start()
            return 0
        lax.fori_loop(0, groups, f, 0)

    @pl.when(i == 0)
    def _():
        issue(0, 0)

    @pl.when(i + 1 < nsteps)
    def _():
        issue(i + 1, (i + 1) % 2)

    slot = i % 2
    for k in range(TOP_K):
        pltpu.make_async_copy(y_hbm.at[pl.ds(0, groups)], g.at[slot, k], sem.at[slot]).wait()

    w = wt_ref[...]
    o_ref[...] = h1_ref[...] + (w[:, :, 0:1] * g[slot, 0] + w[:, :, 1:2] * g[slot, 1])


def _combine(h1, wts, y_buf, dest_flat):
    t = h1.shape[0]
    groups = COMBINE_TM // SUBLANES
    by_group = lambda a: a.reshape(a.shape[0] // SUBLANES, SUBLANES, a.shape[1])
    grid_spec = pltpu.PrefetchScalarGridSpec(
        num_scalar_prefetch=1,
        grid=(t // COMBINE_TM,),
        in_specs=[pl.BlockSpec((groups, SUBLANES, D_MODEL), lambda i, d: (i, 0, 0)),
                  pl.BlockSpec((groups, SUBLANES, LANES), lambda i, d: (i, 0, 0)),
                  pl.BlockSpec(memory_space=pl.ANY)],
        out_specs=pl.BlockSpec((groups, SUBLANES, D_MODEL), lambda i, d: (i, 0, 0)),
        scratch_shapes=[pltpu.VMEM((2, TOP_K, groups, SUBLANES, D_MODEL), F32),
                        pltpu.SemaphoreType.DMA((2,))],
    )
    out = pl.pallas_call(
        _combine_body,
        grid_spec=grid_spec,
        out_shape=jax.ShapeDtypeStruct((t // SUBLANES, SUBLANES, D_MODEL), F32),
        compiler_params=_params("arbitrary"),
        name="combine",
    )(dest_flat, by_group(h1), by_group(wts), by_group(y_buf))
    return out.reshape(t, D_MODEL)


def _dispatch_tables(idx, cnt, t):
    rb, cap = EXPERT_ROW_BLOCK, EXPERT_CAP
    n_assign = t * TOP_K
    n_rows = -(-(n_assign + N_EXPERTS * (rb - 1)) // rb) * rb
    n_sb = (n_rows + N_EXPERTS * (cap - rb)) // cap
    dest = idx[:, 0:TOP_K].reshape(-1)
    counts = cnt[0, :N_EXPERTS].astype(jnp.int32)
    padded = (counts + rb - 1) // rb * rb
    seg_end = jnp.cumsum(padded)
    seg_start = (seg_end - padded).astype(jnp.int32)
    seg_fill = (seg_start + counts).astype(jnp.int32)

    sb_per_expert = (padded + cap - 1) // cap
    sb_end = jnp.cumsum(sb_per_expert)
    total = sb_end[-1]
    s = jnp.arange(n_sb, dtype=jnp.int32)
    s_eff = jnp.minimum(s, total - 1)
    e = jnp.minimum(jnp.sum(sb_end[None, :] <= s_eff[:, None], axis=1), N_EXPERTS - 1).astype(jnp.int32)
    local = s_eff - (sb_end[e] - sb_per_expert[e])
    sb_start = (seg_start[e] + local * cap).astype(jnp.int32)
    sb_rows = jnp.where(s < total, jnp.clip(padded[e] - local * cap, 0, cap), 0).astype(jnp.int32)
    gathered = (counts + GATHER_UNROLL - 1) // GATHER_UNROLL * GATHER_UNROLL
    sb_real = jnp.where(s < total, jnp.clip(gathered[e] - local * cap, 0, cap), 0).astype(jnp.int32)
    tail = seg_end[-1:].astype(jnp.int32)
    return dest, seg_fill, e, sb_start, sb_rows, sb_real, tail, n_rows


def kernel(x, meta_tokens, mix_norm_w, w_in, q_norm_w, k_norm_w, rel_bias, meta_bias, conv_w,
           attn_out_norm_w, conv_out_norm_w, w_out, ffn_norm_w, w_router_group, b_router_group,
           w_router_expert, b_router_expert, w_gate, w_up, w_down):
    bsz, seq, d = x.shape
    depth = mix_norm_w.shape[0]
    assert depth == 1 and d == D_MODEL and seq % GRID_W == 0
    t = bsz * seq
    x2d = x.reshape(t, d)
    l = 0

    proj, proj_meta = _inproj(x2d, meta_tokens.astype(x.dtype), mix_norm_w[l], w_in.reshape(d, PROJ_TOTAL))

    a, c = _mixers(proj, proj_meta, q_norm_w[l], k_norm_w[l], rel_bias[l], meta_bias[l], conv_w[l], bsz, seq)

    spare = LANES - N_GROUPS - N_EXPERTS
    w_router = jnp.concatenate([w_router_group[l].astype(F32), w_router_expert[l].astype(F32),
                                jnp.zeros((d, spare), F32)], axis=1)
    b_router = jnp.concatenate([b_router_group[l].astype(F32), b_router_expert[l].astype(F32),
                                jnp.zeros((spare,), F32)]).reshape(1, LANES)
    h1, hn, logits = _outproj(a, c, x2d, attn_out_norm_w[l], conv_out_norm_w[l], w_out[l].astype(BF16),
                              ffn_norm_w[l], w_router.astype(BF16), b_router)

    idx, wts, cnt = _route(logits)
    dest, seg_fill, sb_expert, sb_start, sb_rows, sb_real, tail, n_rows = _dispatch_tables(idx, cnt, t)
    y_buf = _experts(hn, w_gate.reshape(N_EXPERTS, d, EXPERT_FF), w_up.reshape(N_EXPERTS, d, EXPERT_FF),
                     w_down.reshape(N_EXPERTS, EXPERT_FF, d), sb_expert, sb_start, sb_rows, sb_real,
                     dest, seg_fill, tail, n_rows)
    out = _combine(h1, wts, y_buf, dest)
    return out.reshape(bsz, seq, d)
```

```python
import functools

import jax
import jax.numpy as jnp
from jax import lax
from jax.experimental import pallas as pl
from jax.experimental.pallas import tpu as pltpu

F32 = jnp.float32
BF16 = jnp.bfloat16

D_MODEL = 2048
N_META = 16
GRID_W = 64
N_HEADS = 16
HEAD_DIM = 64
ATTN_WIDTH = N_HEADS * HEAD_DIM
CONV_WIDTH = D_MODEL - ATTN_WIDTH
PROJ_TOTAL = 3 * ATTN_WIDTH + 3 * CONV_WIDTH
WIN_ROWS = 8
WIN_COLS = 16
N_GROUPS = 4
EXPERTS_PER_GROUP = 8
N_EXPERTS = N_GROUPS * EXPERTS_PER_GROUP
TOP_K = 2
EXPERT_FF = 1024
EPS = 1e-6

LANES = 128
SUBLANES = 8
VMEM_LIMIT = 52 * 1024 * 1024

ATTN_ROWS_PER_STEP = 4
CONV_CHUNK = 256
INPROJ_TM = 1024
INPROJ_TN = 1024
OUTPROJ_TM = 512
OUTPROJ_SUB = 256
ROUTE_TM = 1024
EXPERT_ROW_BLOCK = 128
EXPERT_BLOCKS = (512, 256, 128)
EXPERT_CAP = 1024
EXPERT_FF_CHUNK = 512
GATHER_UNROLL = 8
COMBINE_TM = 512


def _params(*sem):
    return pltpu.CompilerParams(dimension_semantics=sem, vmem_limit_bytes=VMEM_LIMIT)


def _inproj_body(x_ref, meta_ref, nw_ref, w_ref, o_ref, om_ref, wb_ref):
    def normed(v):
        ms = jnp.mean(v * v, axis=-1, keepdims=True)
        return (v * lax.rsqrt(ms + EPS) * nw_ref[...]).astype(BF16)

    @pl.when(pl.program_id(1) == 0)
    def _():
        wb_ref[...] = w_ref[...].astype(BF16)
        om_ref[...] = jnp.dot(normed(meta_ref[...]), wb_ref[...], preferred_element_type=F32)

    o_ref[...] = jnp.dot(normed(x_ref[...]), wb_ref[...], preferred_element_type=F32)


def _inproj(x2d, meta, norm_w, w):
    m = x2d.shape[0]
    tm, tn = INPROJ_TM, INPROJ_TN
    return pl.pallas_call(
        _inproj_body,
        grid=(PROJ_TOTAL // tn, m // tm),
        in_specs=[pl.BlockSpec((tm, D_MODEL), lambda j, i: (i, 0)),
                  pl.BlockSpec((N_META, D_MODEL), lambda j, i: (0, 0)),
                  pl.BlockSpec((1, D_MODEL), lambda j, i: (0, 0)),
                  pl.BlockSpec((D_MODEL, tn), lambda j, i: (0, j))],
        out_specs=[pl.BlockSpec((tm, tn), lambda j, i: (i, j)),
                   pl.BlockSpec((N_META, tn), lambda j, i: (0, j))],
        out_shape=[jax.ShapeDtypeStruct((m, PROJ_TOTAL), F32),
                   jax.ShapeDtypeStruct((N_META, PROJ_TOTAL), F32)],
        scratch_shapes=[pltpu.VMEM((D_MODEL, tn), BF16)],
        compiler_params=_params("arbitrary", "arbitrary"),
        name="inproj",
    )(x2d, meta, norm_w.reshape(1, D_MODEL), w)


def _head_norm(x, w, lo):
    x2 = x * x
    s_lo = jnp.sum(jnp.where(lo, x2, 0.0), axis=-1, keepdims=True)
    s_hi = jnp.sum(jnp.where(lo, 0.0, x2), axis=-1, keepdims=True)
    ms = jnp.where(lo, s_lo, s_hi) * (1.0 / HEAD_DIM)
    return x * lax.rsqrt(ms + EPS) * w


def _attn_body(q_ref, k_ref, v_ref, km_ref, vm_ref, qw_ref, kw_ref, bias_ref, mb_ref,
               gb_ref, gc_ref, hc_ref, gcm_ref, hcm_ref, cw_ref, o_ref, c_ref,
               qs, ks, vs, sc, smc, pr, pmr, *, rows):
    _conv_body(gb_ref, gc_ref, hc_ref, gcm_ref, hcm_ref, cw_ref, c_ref)
    lo = lax.broadcasted_iota(jnp.int32, (1, LANES), 1) < HEAD_DIM
    scale = HEAD_DIM ** -0.5
    chunk = 256
    seq = rows * GRID_W

    kmb = _head_norm(km_ref[...], kw_ref[...], lo).astype(BF16)
    vmb = vm_ref[...].astype(BF16)
    contract_last = (((1,), (1,)), ((), ()))
    head_masks = (lo, jnp.logical_not(lo))

    def one_head(x, h):
        return jnp.where(head_masks[h], x, jnp.zeros_like(x))

    def prep(i, _):
        sl = pl.ds(pl.multiple_of(i * chunk, chunk), chunk)
        qs[sl, :] = (_head_norm(q_ref[sl, :], qw_ref[...], lo) * scale).astype(BF16)
        ks[sl, :] = _head_norm(k_ref[sl, :], kw_ref[...], lo).astype(BF16)
        vs[sl, :] = v_ref[sl, :].astype(BF16)
        return 0

    lax.fori_loop(0, seq // chunk, prep, 0, unroll=2)
    wr = min(WIN_ROWS, rows)
    nk = wr * GRID_W

    def window_start(r):
        return jnp.clip(r - wr // 2, 0, rows - wr)

    def row_slice(r, n):
        return pl.ds(pl.multiple_of(r * GRID_W, GRID_W), n)

    def store_scores(r, s_ref, sm_ref):
        rs = window_start(r)
        si = rs - r + (WIN_ROWS - 1)
        q_r = qs[row_slice(r, GRID_W), :]
        kwin = ks[row_slice(rs, nk), :]
        for h in range(2):
            qh = one_head(q_r, h)
            bias = jnp.concatenate([bias_ref[h, si + w] for w in range(0, wr, 2)], axis=-1)
            s_ref[h] = lax.dot_general(qh, kwin, contract_last, preferred_element_type=F32) + bias
            sm_ref[h] = lax.dot_general(qh, kmb, contract_last, preferred_element_type=F32) + mb_ref[h]

    def store_softmax(s_ref, sm_ref, p_ref, pm_ref):
        for h in range(2):
            s = s_ref[h]
            sm = sm_ref[h]
            m = jnp.maximum(jnp.max(s, axis=-1, keepdims=True), jnp.max(sm, axis=-1, keepdims=True))
            p = jnp.exp(s - m)
            pm = jnp.exp(sm - m)
            inv = 1.0 / (jnp.sum(p, axis=-1, keepdims=True) + jnp.sum(pm, axis=-1, keepdims=True))
            p_ref[h] = (p * inv).astype(BF16)
            pm_ref[h] = (pm * inv).astype(BF16)

    def weighted_values(r, p_ref, pm_ref):
        vwin = vs[row_slice(window_start(r), nk), :]
        outs = [jnp.dot(p_ref[h], vwin, preferred_element_type=F32)
                + jnp.dot(pm_ref[h], vmb, preferred_element_type=F32) for h in range(2)]
        o_ref[row_slice(r, GRID_W), :] = jnp.where(lo, outs[0], outs[1])

    per = ATTN_ROWS_PER_STEP
    groups = rows // per

    def step(j, parity, do_scores, do_softmax, do_values):
        for k in range(per):
            if do_scores:
                store_scores(per * j + k, sc.at[parity, k], smc.at[parity, k])
            if do_softmax:
                store_softmax(sc.at[1 - parity, k], smc.at[1 - parity, k], pr.at[parity, k], pmr.at[parity, k])
            if do_values:
                weighted_values(per * (j - 2) + k, pr.at[1 - parity, k], pmr.at[1 - parity, k])

    step(0, 0, True, False, False)
    step(1, 1, True, True, False)

    def two_steps(i, _):
        step(2 * i, 0, True, True, True)
        step(2 * i + 1, 1, True, True, True)
        return 0

    lax.fori_loop(1, groups // 2, two_steps, 0)
    step(groups, 0, False, True, True)
    step(groups + 1, 1, False, False, True)


def _bias_table(rel_bias):
    c = jnp.arange(GRID_W)
    col_start = jnp.clip(c - WIN_COLS // 2, 0, GRID_W - WIN_COLS)
    col_mask = (c[None, :] >= col_start[:, None]) & (c[None, :] < col_start[:, None] + WIN_COLS)
    dc = jnp.clip(c[None, :] - c[:, None], -(WIN_COLS - 1), WIN_COLS - 1) + (WIN_COLS - 1)
    ncol = 2 * WIN_COLS - 1
    pair_bias = jnp.concatenate([rel_bias[:, :-1], rel_bias[:, 1:]], axis=-1).astype(F32)
    onehot = (dc[None] == jnp.arange(ncol)[:, None, None]).astype(F32)
    zeros = jnp.zeros_like(onehot)
    pair_onehot = jnp.concatenate([jnp.concatenate([onehot, zeros], axis=-1),
                                   jnp.concatenate([zeros, onehot], axis=-1)], axis=0)
    table = jnp.einsum('hdm,mqn->hdqn', pair_bias, pair_onehot, precision=lax.Precision.HIGHEST)
    pair_mask = jnp.concatenate([col_mask, col_mask], axis=-1)
    return jnp.where(pair_mask[None, None], table, -jnp.inf)


def _mixers(proj, proj_meta, q_norm_w, k_norm_w, rel_bias, meta_bias, conv_w, bsz, seq):
    rows = seq // GRID_W
    assert rows >= WIN_ROWS and rows % (2 * ATTN_ROWS_PER_STEP) == 0
    nk = WIN_ROWS * GRID_W
    npairs = N_HEADS // 2
    assert CONV_WIDTH // LANES == npairs
    conv_base = 3 * ATTN_WIDTH // LANES
    qw = jnp.tile(q_norm_w.astype(F32), 2).reshape(1, LANES)
    kw = jnp.tile(k_norm_w.astype(F32), 2).reshape(1, LANES)
    bias = _bias_table(rel_bias)
    mb = meta_bias.astype(F32).reshape(N_HEADS, 1, N_META)
    return pl.pallas_call(
        functools.partial(_attn_body, rows=rows),
        grid=(bsz, npairs),
        in_specs=[pl.BlockSpec((seq, LANES), lambda b, p: (b, p)),
                  pl.BlockSpec((seq, LANES), lambda b, p: (b, npairs + p)),
                  pl.BlockSpec((seq, LANES), lambda b, p: (b, 2 * npairs + p)),
                  pl.BlockSpec((N_META, LANES), lambda b, p: (0, npairs + p)),
                  pl.BlockSpec((N_META, LANES), lambda b, p: (0, 2 * npairs + p)),
                  pl.BlockSpec((1, LANES), lambda b, p: (0, 0)),
                  pl.BlockSpec((1, LANES), lambda b, p: (0, 0)),
                  pl.BlockSpec((2, 2 * WIN_ROWS - 2, GRID_W, LANES), lambda b, p: (p, 0, 0, 0)),
                  pl.BlockSpec((2, 1, N_META), lambda b, p: (p, 0, 0)),
                  pl.BlockSpec((seq, LANES), lambda b, p: (b, conv_base + p)),
                  pl.BlockSpec((seq, LANES), lambda b, p: (b, conv_base + npairs + p)),
                  pl.BlockSpec((seq, LANES), lambda b, p: (b, conv_base + 2 * npairs + p)),
                  pl.BlockSpec((N_META, LANES), lambda b, p: (0, conv_base + npairs + p)),
                  pl.BlockSpec((N_META, LANES), lambda b, p: (0, conv_base + 2 * npairs + p)),
                  pl.BlockSpec((3, LANES), lambda b, p: (0, p))],
        out_specs=[pl.BlockSpec((seq, LANES), lambda b, p: (b, p)),
                   pl.BlockSpec((seq, LANES), lambda b, p: (b, p))],
        out_shape=[jax.ShapeDtypeStruct((bsz * seq, ATTN_WIDTH), F32),
                   jax.ShapeDtypeStruct((bsz * seq, CONV_WIDTH), F32)],
        scratch_shapes=([pltpu.VMEM((seq, LANES), BF16)] * 3
                        + [pltpu.VMEM((2, ATTN_ROWS_PER_STEP, 2, GRID_W, nk), F32),
                           pltpu.VMEM((2, ATTN_ROWS_PER_STEP, 2, GRID_W, N_META), F32),
                           pltpu.VMEM((2, ATTN_ROWS_PER_STEP, 2, GRID_W, nk), BF16),
                           pltpu.VMEM((2, ATTN_ROWS_PER_STEP, 2, GRID_W, N_META), BF16)]),
        compiler_params=_params("arbitrary", "arbitrary"),
        name="mixers",
    )(proj, proj, proj, proj_meta, proj_meta, qw, kw, bias, mb,
      proj, proj, proj, proj_meta, proj_meta, conv_w.astype(F32))


def _conv_body(gb_ref, gc_ref, hc_ref, gcm_ref, hcm_ref, w_ref, o_ref):
    seq = gb_ref.shape[0]
    ch = CONV_CHUNK
    nchunks = seq // ch
    w = w_ref[...]
    u_meta_last = gcm_ref[N_META - 1:N_META, :] * hcm_ref[N_META - 1:N_META, :]
    row = lax.broadcasted_iota(jnp.int32, (ch, 1), 0)

    def chunk(i, _):
        r0 = pl.multiple_of(i * ch, ch)
        rows = pl.ds(r0, ch)
        u = gc_ref[rows, :] * hc_ref[rows, :]
        before = pl.ds(pl.multiple_of(jnp.maximum(r0 - SUBLANES, 0), SUBLANES), SUBLANES)
        after = pl.ds(pl.multiple_of(jnp.minimum(r0 + ch, seq - SUBLANES), SUBLANES), SUBLANES)
        u_before = (gc_ref[before, :] * hc_ref[before, :])[SUBLANES - 1:SUBLANES]
        u_after = (gc_ref[after, :] * hc_ref[after, :])[0:1]
        u_before = jnp.where(i == 0, u_meta_last, u_before)
        u_after = jnp.where(i == nchunks - 1, 0.0, u_after)
        u_prev = jnp.where(row == 0, u_before, pltpu.roll(u, 1, 0))
        u_next = jnp.where(row == ch - 1, u_after, pltpu.roll(u, ch - 1, 0))
        y = u_prev * w[0:1] + u * w[1:2] + u_next * w[2:3]
        o_ref[rows, :] = gb_ref[rows, :] * y
        return 0

    lax.fori_loop(0, nchunks, chunk, 0)


def _rms(x, w):
    ms = jnp.mean(x * x, axis=-1, keepdims=True)
    return x * lax.rsqrt(ms + EPS) * w


def _pack_bf16_pairs(x):
    w = x.shape[1] // 2
    lo = lax.bitcast_convert_type(x[:, :w].astype(BF16).astype(F32), jnp.uint32)
    hi = lax.bitcast_convert_type(x[:, w:].astype(BF16).astype(F32), jnp.uint32)
    return (hi & jnp.uint32(0xFFFF0000)) | (lo >> 16)


def _unpack_bf16_pairs(p):
    lo = lax.bitcast_convert_type(p << 16, F32).astype(BF16)
    hi = lax.bitcast_convert_type(p & jnp.uint32(0xFFFF0000), F32).astype(BF16)
    return lo, hi


def _outproj_body(a_ref, c_ref, x_ref, aw_ref, cw_ref, wo_ref, fw_ref, wr_ref, br_ref,
                  h1_ref, hn_ref, lg_ref):
    sub = OUTPROJ_SUB
    blocks = [pl.ds(k * sub, sub) for k in range(a_ref.shape[0] // sub)]

    def mix(rows):
        an = _rms(a_ref[rows, :], aw_ref[...]).astype(BF16)
        cn = _rms(c_ref[rows, :], cw_ref[...]).astype(BF16)
        return (jnp.dot(an, wo_ref[0:ATTN_WIDTH, :], preferred_element_type=F32)
                + jnp.dot(cn, wo_ref[ATTN_WIDTH:D_MODEL, :], preferred_element_type=F32))

    def finish(rows, mixed):
        h1 = x_ref[rows, :] + mixed
        h1_ref[rows, :] = h1
        hn = _rms(h1, fw_ref[...])
        hn_ref[rows, :] = _pack_bf16_pairs(hn)
        lg_ref[rows, :] = jnp.dot(hn.astype(BF16), wr_ref[...], preferred_element_type=F32) + br_ref[...]

    mixed = mix(blocks[0])
    for k, rows in enumerate(blocks):
        following = mix(blocks[k + 1]) if k + 1 < len(blocks) else None
        finish(rows, mixed)
        mixed = following


def _outproj(a, c, x2d, aw, cw, wo_bf16, fw, w_router, b_router):
    t = x2d.shape[0]
    tm = OUTPROJ_TM
    row = lambda i: (i, 0)
    fixed = lambda i: (0, 0)
    return pl.pallas_call(
        _outproj_body,
        grid=(t // tm,),
        in_specs=[pl.BlockSpec((tm, ATTN_WIDTH), row),
                  pl.BlockSpec((tm, CONV_WIDTH), row),
                  pl.BlockSpec((tm, D_MODEL), row),
                  pl.BlockSpec((1, ATTN_WIDTH), fixed),
                  pl.BlockSpec((1, CONV_WIDTH), fixed),
                  pl.BlockSpec((D_MODEL, D_MODEL), fixed, pipeline_mode=pl.Buffered(1)),
                  pl.BlockSpec((1, D_MODEL), fixed),
                  pl.BlockSpec((D_MODEL, LANES), fixed),
                  pl.BlockSpec((1, LANES), fixed)],
        out_specs=[pl.BlockSpec((tm, D_MODEL), row),
                   pl.BlockSpec((tm, D_MODEL // 2), row),
                   pl.BlockSpec((tm, LANES), row)],
        out_shape=[jax.ShapeDtypeStruct((t, D_MODEL), F32),
                   jax.ShapeDtypeStruct((t, D_MODEL // 2), jnp.uint32),
                   jax.ShapeDtypeStruct((t, LANES), F32)],
        compiler_params=_params("arbitrary"),
        name="outproj",
    )(a, c, x2d, aw.reshape(1, -1), cw.reshape(1, -1), wo_bf16, fw.reshape(1, -1), w_router, b_router)


def _route_body(lg_ref, idx_ref, wt_ref, cnt_ref, run_ref, seg_ref):
    final_pass = pl.program_id(0) == 1
    first_tile = pl.program_id(1) == 0

    @pl.when(first_tile & jnp.logical_not(final_pass))
    def _():
        run_ref[...] = jnp.zeros_like(run_ref)
        seg_ref[...] = jnp.zeros_like(seg_ref)

    @pl.when(first_tile & final_pass)
    def _():
        counts = run_ref[...]
        rb = float(EXPERT_ROW_BLOCK)
        padded = jnp.ceil(counts * (1.0 / rb)) * rb
        lane8 = lax.broadcasted_iota(jnp.int32, counts.shape, 1)
        ends = padded
        shift = 1
        while shift < LANES:
            ends = ends + jnp.where(lane8 >= shift, pltpu.roll(ends, shift, 1), 0.0)
            shift *= 2
        seg_ref[...] = ends - padded
        run_ref[...] = jnp.zeros_like(run_ref)

    logits = lg_ref[...]
    tm = logits.shape[0]
    lane = lax.broadcasted_iota(jnp.int32, (tm, LANES), 1)
    neg = -jnp.inf

    def first_argmax(v):
        m = jnp.max(v, axis=-1, keepdims=True)
        first = jnp.min(jnp.where(v == m, lane.astype(F32), float(LANES)), axis=-1, keepdims=True)
        return m, first.astype(jnp.int32)

    gl = jnp.where(lane < N_GROUPS, logits, neg)
    gmax, gidx = first_argmax(gl)
    g_w = 1.0 / jnp.sum(jnp.exp(gl - gmax), axis=-1, keepdims=True)
    first = N_GROUPS + gidx * EXPERTS_PER_GROUP
    el = jnp.where((lane >= first) & (lane < first + EXPERTS_PER_GROUP), logits, neg)
    m0, j0 = first_argmax(el)
    m1, j1 = first_argmax(jnp.where(lane == j0, neg, el))
    p1 = jnp.exp(m1 - m0)
    w0 = g_w / (1.0 + p1)
    w1 = g_w * p1 / (1.0 + p1)
    e0 = j0 - N_GROUPS
    e1 = j1 - N_GROUPS

    onehot = ((lane == e0) | (lane == e1)).astype(BF16)
    seen = run_ref[0:1, :]
    run = seen + jnp.sum(onehot.astype(F32), axis=0, keepdims=True)
    run_ref[...] = jnp.broadcast_to(run, run_ref.shape)

    @pl.when(final_pass)
    def _():
        tri = (lax.broadcasted_iota(jnp.int32, (tm, tm), 0)
               > lax.broadcasted_iota(jnp.int32, (tm, tm), 1)).astype(BF16)
        place = jnp.dot(tri, onehot, preferred_element_type=F32) + seen + seg_ref[0:1, :]
        d0 = jnp.sum(jnp.where(lane == e0, place, 0.0), axis=-1, keepdims=True).astype(jnp.int32)
        d1 = jnp.sum(jnp.where(lane == e1, place, 0.0), axis=-1, keepdims=True).astype(jnp.int32)
        idx_ref[...] = jnp.where(lane == 0, d0, jnp.where(lane == 1, d1, jnp.zeros_like(lane)))
        wt_ref[...] = jnp.where(lane == 0, w0, jnp.where(lane == 1, w1, 0.0))
        cnt_ref[...] = jnp.broadcast_to(run, cnt_ref.shape)


def _route(logits):
    t = logits.shape[0]
    tm = ROUTE_TM
    return pl.pallas_call(
        _route_body,
        grid=(2, t // tm),
        in_specs=[pl.BlockSpec((tm, LANES), lambda p, i: (i, 0))],
        out_specs=[pl.BlockSpec((tm, LANES), lambda p, i: (i * p, 0)),
                   pl.BlockSpec((tm, LANES), lambda p, i: (i * p, 0)),
                   pl.BlockSpec((8, LANES), lambda p, i: (0, 0))],
        out_shape=[jax.ShapeDtypeStruct((t, LANES), jnp.int32),
                   jax.ShapeDtypeStruct((t, LANES), F32),
                   jax.ShapeDtypeStruct((8, LANES), F32)],
        scratch_shapes=[pltpu.VMEM((8, LANES), F32)] * 2,
        compiler_params=_params("arbitrary", "arbitrary"),
        name="route",
    )(logits)


def _expert_body(sbe, sbs, sbn, sbr, dest, seg_fill, tail, hn_hbm, wg_ref, wu_ref, wd_ref,
                 y_hbm, x32, xb, acc, tok, gsem, osem):
    s = pl.program_id(0)
    c = pl.program_id(1)
    nsb = pl.num_programs(0)
    nch = EXPERT_FF // EXPERT_FF_CHUNK
    rb = EXPERT_ROW_BLOCK
    half = D_MODEL // 2
    n = sbn[s]
    start = sbs[s]
    nblk = n // rb
    slot = s % 2

    def for_each(count, fn):
        def body(i, _):
            fn(i)
            return 0
        lax.fori_loop(0, count, body, 0)

    def block_rows(j):
        return pl.ds(pl.multiple_of(j * rb, rb), rb)

    def gather_row(sb_first, buf, i):
        src = hn_hbm.at[pl.ds(tok[sb_first + i], 1)]
        pltpu.make_async_copy(src, x32.at[buf, pl.ds(i, 1)], gsem.at[buf]).start()

    def gather_rows(sb_first, buf, first_row, count):
        def group(g):
            for k in range(GATHER_UNROLL):
                gather_row(sb_first, buf, first_row + g * GATHER_UNROLL + k)
        for_each(count // GATHER_UNROLL, group)

    def wait_gathered(buf, count):
        rows = pl.ds(0, pl.multiple_of(count, GATHER_UNROLL))
        pltpu.make_async_copy(hn_hbm.at[rows], x32.at[buf, rows], gsem.at[buf]).wait()

    def out_copy(first_row, j):
        dst = pl.ds(pl.multiple_of(first_row + j * rb, rb), rb)
        return pltpu.make_async_copy(acc.at[block_rows(j)], y_hbm.at[dst], osem)

    @pl.when((s == 0) & (c == 0))
    def _():
        def clear(j):
            for buf in range(2):
                x32[buf, block_rows(j), :] = jnp.zeros((rb, half), jnp.uint32)
            acc[block_rows(j), :] = jnp.zeros((rb, D_MODEL), F32)
        for_each(EXPERT_CAP // rb, clear)

        def pad_rows(e):
            for k in range(GATHER_UNROLL - 1):
                tok[jnp.minimum(seg_fill[e] + k, tok.shape[0] - 1)] = 0
        for_each(N_EXPERTS, pad_rows)

        def invert(g):
            for k in range(GATHER_UNROLL):
                tok[dest[g * GATHER_UNROLL + k]] = g * (GATHER_UNROLL // TOP_K) + k // TOP_K
        for_each(dest.shape[0] // GATHER_UNROLL, invert)

        gather_rows(sbs[0], 0, 0, sbr[0])

    prev = jnp.maximum(s - 1, 0)
    nxt = jnp.minimum(s + 1, nsb - 1)
    prev_rows = jnp.where(s > 0, sbn[prev], 0)
    prev_pending = (c == 0) & (prev_rows > 0)

    @pl.when((c == 0) & (sbr[s] > 0))
    def _():
        wait_gathered(slot, sbr[s])

    def wait_prev_output():
        for_each(prev_rows // rb, lambda j: out_copy(sbs[prev], j).wait())

    @pl.when(prev_pending & (n == 0))
    def _():
        wait_prev_output()

    @pl.when((s == nsb - 1) & (c == 0))
    def _():
        first = tail[0]
        nfill = (y_hbm.shape[0] - first) // rb
        acc[0:rb, :] = jnp.zeros((rb, D_MODEL), F32)

        def fill_copy(j):
            dst = pl.ds(pl.multiple_of(first + j * rb, rb), rb)
            return pltpu.make_async_copy(acc.at[0:rb], y_hbm.at[dst], osem)

        for_each(nfill, lambda j: fill_copy(j).start())
        for_each(nfill, lambda j: fill_copy(j).wait())

    @pl.when(n > 0)
    def _():
        @pl.when(c == 0)
        def _():
            def unpack(j):
                rows = block_rows(j)
                lo, hi = _unpack_bf16_pairs(x32[slot, rows, :])
                xb[rows, 0:half] = lo
                xb[rows, half:D_MODEL] = hi
            for_each(nblk, unpack)

            @pl.when((s + 1 < nsb) & (sbr[nxt] > 0))
            def _():
                gather_rows(sbs[nxt], 1 - slot, 0, sbr[nxt])

            @pl.when(prev_pending)
            def _():
                wait_prev_output()

        def mlp(first_row, m):
            rows = pl.ds(pl.multiple_of(first_row, rb), m)
            x = xb[rows, :]
            g = jnp.dot(x, wg_ref[0].astype(BF16), preferred_element_type=F32)
            u = jnp.dot(x, wu_ref[0].astype(BF16), preferred_element_type=F32)
            h = (jax.nn.silu(g) * u).astype(BF16)
            y = jnp.dot(h, wd_ref[0].astype(BF16), preferred_element_type=F32)
            acc[rows, :] = jnp.where(c == 0, y, acc[rows, :] + y)

        done = 0
        for m in EXPERT_BLOCKS[:-1]:
            count = (n - done) // m
            for_each(count, lambda j, done=done, m=m: mlp(done + j * m, m))
            done = done + count * m

        @pl.when(done < n)
        def _():
            mlp(done, rb)

        @pl.when(c == nch - 1)
        def _():
            for_each(nblk, lambda j: out_copy(start, j).start())


def _experts(hn_packed, w_gate, w_up, w_down, sb_expert, sb_start, sb_rows, sb_real, dest, seg_fill, tail,
             n_rows):
    n_sb = sb_expert.shape[0]
    nch = EXPERT_FF // EXPERT_FF_CHUNK
    fc = EXPERT_FF_CHUNK

    def chunk(s, c, sbn):
        return jnp.where(sbn[s] > 0, c, nch - 1)

    def up_map(s, c, sbe, sbs, sbn, *_):
        return (sbe[s], 0, chunk(s, c, sbn))

    def down_map(s, c, sbe, sbs, sbn, *_):
        return (sbe[s], chunk(s, c, sbn), 0)

    grid_spec = pltpu.PrefetchScalarGridSpec(
        num_scalar_prefetch=7,
        grid=(n_sb, nch),
        in_specs=[pl.BlockSpec(memory_space=pl.ANY),
                  pl.BlockSpec((1, D_MODEL, fc), up_map),
                  pl.BlockSpec((1, D_MODEL, fc), up_map),
                  pl.BlockSpec((1, fc, D_MODEL), down_map)],
        out_specs=pl.BlockSpec(memory_space=pl.ANY),
        scratch_shapes=[pltpu.VMEM((2, EXPERT_CAP, D_MODEL // 2), jnp.uint32),
                        pltpu.VMEM((EXPERT_CAP, D_MODEL), BF16),
                        pltpu.VMEM((EXPERT_CAP, D_MODEL), F32),
                        pltpu.SMEM((n_rows,), jnp.int32),
                        pltpu.SemaphoreType.DMA((2,)),
                        pltpu.SemaphoreType.DMA(())],
    )
    return pl.pallas_call(
        _expert_body,
        grid_spec=grid_spec,
        out_shape=jax.ShapeDtypeStruct((n_rows, D_MODEL), F32),
        compiler_params=_params("arbitrary", "arbitrary"),
        name="experts",
    )(sb_expert, sb_start, sb_rows, sb_real, dest, seg_fill, tail, hn_packed, w_gate, w_up, w_down)


def _combine_body(dest, h1_ref, wt_ref, y_hbm, o_ref, g, sem):
    i = pl.program_id(0)
    nsteps = pl.num_programs(0)
    groups = h1_ref.shape[0]
    tm = groups * SUBLANES

    def issue(step, slot):
        def f(q, _):
            for u in range(SUBLANES):
                for k in range(TOP_K):
                    row = dest[(step * tm + q * SUBLANES + u) * TOP_K + k]
                    src = y_hbm.at[row >> 3, pl.ds(row & (SUBLANES - 1), 1)]
                    pltpu.make_async_copy(src, g.at[slot, k, q, pl.ds(u, 1)], sem.at[slot]).start()
            return 0
        lax.fori_loop(0, groups, f, 0)

    @pl.when(i == 0)
    def _():
        issue(0, 0)

    @pl.when(i + 1 < nsteps)
    def _():
        issue(i + 1, (i + 1) % 2)

    slot = i % 2
    for k in range(TOP_K):
        pltpu.make_async_copy(y_hbm.at[pl.ds(0, groups)], g.at[slot, k], sem.at[slot]).wait()

    w = wt_ref[...]
    o_ref[...] = h1_ref[...] + (w[:, :, 0:1] * g[slot, 0] + w[:, :, 1:2] * g[slot, 1])


def _combine(h1, wts, y_buf, dest_flat):
    t = h1.shape[0]
    groups = COMBINE_TM // SUBLANES
    by_group = lambda a: a.reshape(a.shape[0] // SUBLANES, SUBLANES, a.shape[1])
    grid_spec = pltpu.PrefetchScalarGridSpec(
        num_scalar_prefetch=1,
        grid=(t // COMBINE_TM,),
        in_specs=[pl.BlockSpec((groups, SUBLANES, D_MODEL), lambda i, d: (i, 0, 0)),
                  pl.BlockSpec((groups, SUBLANES, LANES), lambda i, d: (i, 0, 0)),
                  pl.BlockSpec(memory_space=pl.ANY)],
        out_specs=pl.BlockSpec((groups, SUBLANES, D_MODEL), lambda i, d: (i, 0, 0)),
        scratch_shapes=[pltpu.VMEM((2, TOP_K, groups, SUBLANES, D_MODEL), F32),
                        pltpu.SemaphoreType.DMA((2,))],
    )
    out = pl.pallas_call(
        _combine_body,
        grid_spec=grid_spec,
        out_shape=jax.ShapeDtypeStruct((t // SUBLANES, SUBLANES, D_MODEL), F32),
        compiler_params=_params("arbitrary"),
        name="combine",
    )(dest_flat, by_group(h1), by_group(wts), by_group(y_buf))
    return out.reshape(t, D_MODEL)


def _dispatch_tables(idx, cnt, t):
    rb, cap = EXPERT_ROW_BLOCK, EXPERT_CAP
    n_assign = t * TOP_K
    n_rows = -(-(n_assign + N_EXPERTS * (rb - 1)) // rb) * rb
    n_sb = (n_rows + N_EXPERTS * (cap - rb)) // cap
    dest = idx[:, 0:TOP_K].reshape(-1)
    counts = cnt[0, :N_EXPERTS].astype(jnp.int32)
    padded = (counts + rb - 1) // rb * rb
    seg_end = jnp.cumsum(padded)
    seg_start = (seg_end - padded).astype(jnp.int32)
    seg_fill = (seg_start + counts).astype(jnp.int32)

    sb_per_expert = (padded + cap - 1) // cap
    sb_end = jnp.cumsum(sb_per_expert)
    total = sb_end[-1]
    s = jnp.arange(n_sb, dtype=jnp.int32)
    s_eff = jnp.minimum(s, total - 1)
    e = jnp.minimum(jnp.sum(sb_end[None, :] <= s_eff[:, None], axis=1), N_EXPERTS - 1).astype(jnp.int32)
    local = s_eff - (sb_end[e] - sb_per_expert[e])
    sb_start = (seg_start[e] + local * cap).astype(jnp.int32)
    sb_rows = jnp.where(s < total, jnp.clip(padded[e] - local * cap, 0, cap), 0).astype(jnp.int32)
    gathered = (counts + GATHER_UNROLL - 1) // GATHER_UNROLL * GATHER_UNROLL
    sb_real = jnp.where(s < total, jnp.clip(gathered[e] - local * cap, 0, cap), 0).astype(jnp.int32)
    tail = seg_end[-1:].astype(jnp.int32)
    return dest, seg_fill, e, sb_start, sb_rows, sb_real, tail, n_rows


def kernel(x, meta_tokens, mix_norm_w, w_in, q_norm_w, k_norm_w, rel_bias, meta_bias, conv_w,
           attn_out_norm_w, conv_out_norm_w, w_out, ffn_norm_w, w_router_group, b_router_group,
           w_router_expert, b_router_expert, w_gate, w_up, w_down):
    bsz, seq, d = x.shape
    depth = mix_norm_w.shape[0]
    assert depth == 1 and d == D_MODEL and seq % GRID_W == 0
    t = bsz * seq
    x2d = x.reshape(t, d)
    l = 0

    proj, proj_meta = _inproj(x2d, meta_tokens.astype(x.dtype), mix_norm_w[l], w_in.reshape(d, PROJ_TOTAL))

    a, c = _mixers(proj, proj_meta, q_norm_w[l], k_norm_w[l], rel_bias[l], meta_bias[l], conv_w[l], bsz, seq)

    spare = LANES - N_GROUPS - N_EXPERTS
    w_router = jnp.concatenate([w_router_group[l].astype(F32), w_router_expert[l].astype(F32),
                                jnp.zeros((d, spare), F32)], axis=1)
    b_router = jnp.concatenate([b_router_group[l].astype(F32), b_router_expert[l].astype(F32),
                                jnp.zeros((spare,), F32)]).reshape(1, LANES)
    h1, hn, logits = _outproj(a, c, x2d, attn_out_norm_w[l], conv_out_norm_w[l], w_out[l].astype(BF16),
                              ffn_norm_w[l], w_router.astype(BF16), b_router)

    idx, wts, cnt = _route(logits)
    dest, seg_fill, sb_expert, sb_start, sb_rows, sb_real, tail, n_rows = _dispatch_tables(idx, cnt, t)
    y_buf = _experts(hn, w_gate.reshape(N_EXPERTS, d, EXPERT_FF), w_up.reshape(N_EXPERTS, d, EXPERT_FF),
                     w_down.reshape(N_EXPERTS, EXPERT_FF, d), sb_expert, sb_start, sb_rows, sb_real,
                     dest, seg_fill, tail, n_rows)
    out = _combine(h1, wts, y_buf, dest)
    return out.reshape(bsz, seq, d)
```

```python
import functools

import jax
import jax.numpy as jnp
from jax import lax
from jax.experimental import pallas as pl
from jax.experimental.pallas import tpu as pltpu

F32 = jnp.float32
BF16 = jnp.bfloat16

D_MODEL = 2048
N_META = 16
GRID_W = 64
N_HEADS = 16
HEAD_DIM = 64
ATTN_WIDTH = N_HEADS * HEAD_DIM
CONV_WIDTH = D_MODEL - ATTN_WIDTH
PROJ_TOTAL = 3 * ATTN_WIDTH + 3 * CONV_WIDTH
WIN_ROWS = 8
WIN_COLS = 16
N_GROUPS = 4
EXPERTS_PER_GROUP = 8
N_EXPERTS = N_GROUPS * EXPERTS_PER_GROUP
TOP_K = 2
EXPERT_FF = 1024
EPS = 1e-6

LANES = 128
SUBLANES = 8
VMEM_LIMIT = 54 * 1024 * 1024

ATTN_ROWS_PER_STEP = 4
CONV_CHUNK = 256
INPROJ_TM = 1024
INPROJ_TN = 1024
OUTPROJ_TM = 512
OUTPROJ_SUB = 256
ROUTE_TM = 1024
EXPERT_ROW_BLOCK = 128
EXPERT_BLOCKS = (512, 256, 128)
EXPERT_CAP = 1024
EXPERT_FF_CHUNK = 512
GATHER_UNROLL = 8
COMBINE_TM = 512


def _params(*sem):
    return pltpu.CompilerParams(dimension_semantics=sem, vmem_limit_bytes=VMEM_LIMIT)


def _inproj_body(x_ref, meta_ref, nw_ref, w_ref, o_ref, om_ref, wb_ref):
    def normed(v):
        ms = jnp.mean(v * v, axis=-1, keepdims=True)
        return (v * lax.rsqrt(ms + EPS) * nw_ref[...]).astype(BF16)

    @pl.when(pl.program_id(1) == 0)
    def _():
        wb_ref[...] = w_ref[...].astype(BF16)
        om_ref[...] = jnp.dot(normed(meta_ref[...]), wb_ref[...], preferred_element_type=F32)

    o_ref[...] = jnp.dot(normed(x_ref[...]), wb_ref[...], preferred_element_type=F32)


def _inproj(x2d, meta, norm_w, w):
    m = x2d.shape[0]
    tm, tn = INPROJ_TM, INPROJ_TN
    return pl.pallas_call(
        _inproj_body,
        grid=(PROJ_TOTAL // tn, m // tm),
        in_specs=[pl.BlockSpec((tm, D_MODEL), lambda j, i: (i, 0)),
                  pl.BlockSpec((N_META, D_MODEL), lambda j, i: (0, 0)),
                  pl.BlockSpec((1, D_MODEL), lambda j, i: (0, 0)),
                  pl.BlockSpec((D_MODEL, tn), lambda j, i: (0, j))],
        out_specs=[pl.BlockSpec((tm, tn), lambda j, i: (i, j)),
                   pl.BlockSpec((N_META, tn), lambda j, i: (0, j))],
        out_shape=[jax.ShapeDtypeStruct((m, PROJ_TOTAL), F32),
                   jax.ShapeDtypeStruct((N_META, PROJ_TOTAL), F32)],
        scratch_shapes=[pltpu.VMEM((D_MODEL, tn), BF16)],
        compiler_params=_params("arbitrary", "arbitrary"),
        name="inproj",
    )(x2d, meta, norm_w.reshape(1, D_MODEL), w)


def _head_norm(x, w, lo):
    x2 = x * x
    s_lo = jnp.sum(jnp.where(lo, x2, 0.0), axis=-1, keepdims=True)
    s_hi = jnp.sum(jnp.where(lo, 0.0, x2), axis=-1, keepdims=True)
    ms = jnp.where(lo, s_lo, s_hi) * (1.0 / HEAD_DIM)
    return x * lax.rsqrt(ms + EPS) * w


def _attn_body(q_ref, k_ref, v_ref, km_ref, vm_ref, qw_ref, kw_ref, bias_ref, mb_ref,
               gb_ref, gc_ref, hc_ref, gcm_ref, hcm_ref, cw_ref, o_ref, c_ref,
               qs, ks, vs, sc, smc, pr, pmr, *, rows):
    _conv_body(gb_ref, gc_ref, hc_ref, gcm_ref, hcm_ref, cw_ref, c_ref)
    lo = lax.broadcasted_iota(jnp.int32, (1, LANES), 1) < HEAD_DIM
    scale = HEAD_DIM ** -0.5
    chunk = 256
    seq = rows * GRID_W

    kmb = _head_norm(km_ref[...], kw_ref[...], lo).astype(BF16)
    vmb = vm_ref[...].astype(BF16)
    contract_last = (((1,), (1,)), ((), ()))
    head_masks = (lo, jnp.logical_not(lo))

    def one_head(x, h):
        return jnp.where(head_masks[h], x, jnp.zeros_like(x))

    def prep(i, _):
        sl = pl.ds(pl.multiple_of(i * chunk, chunk), chunk)
        qs[sl, :] = (_head_norm(q_ref[sl, :], qw_ref[...], lo) * scale).astype(BF16)
        ks[sl, :] = _head_norm(k_ref[sl, :], kw_ref[...], lo).astype(BF16)
        vs[sl, :] = v_ref[sl, :].astype(BF16)
        return 0

    lax.fori_loop(0, seq // chunk, prep, 0, unroll=2)
    wr = min(WIN_ROWS, rows)
    nk = wr * GRID_W

    def window_start(r):
        return jnp.clip(r - wr // 2, 0, rows - wr)

    def row_slice(r, n):
        return pl.ds(pl.multiple_of(r * GRID_W, GRID_W), n)

    def store_scores(r, s_ref, sm_ref):
        rs = window_start(r)
        si = rs - r + (WIN_ROWS - 1)
        q_r = qs[row_slice(r, GRID_W), :]
        kwin = ks[row_slice(rs, nk), :]
        for h in range(2):
            qh = one_head(q_r, h)
            bias = jnp.concatenate([bias_ref[h, si + w] for w in range(0, wr, 2)], axis=-1)
            s_ref[h] = lax.dot_general(qh, kwin, contract_last, preferred_element_type=F32) + bias
            sm_ref[h] = lax.dot_general(qh, kmb, contract_last, preferred_element_type=F32) + mb_ref[h]

    def store_softmax(s_ref, sm_ref, p_ref, pm_ref):
        for h in range(2):
            s = s_ref[h]
            sm = sm_ref[h]
            m = jnp.maximum(jnp.max(s, axis=-1, keepdims=True), jnp.max(sm, axis=-1, keepdims=True))
            p = jnp.exp(s - m)
            pm = jnp.exp(sm - m)
            inv = 1.0 / (jnp.sum(p, axis=-1, keepdims=True) + jnp.sum(pm, axis=-1, keepdims=True))
            p_ref[h] = (p * inv).astype(BF16)
            pm_ref[h] = (pm * inv).astype(BF16)

    def weighted_values(r, p_ref, pm_ref):
        vwin = vs[row_slice(window_start(r), nk), :]
        outs = [jnp.dot(p_ref[h], vwin, preferred_element_type=F32)
                + jnp.dot(pm_ref[h], vmb, preferred_element_type=F32) for h in range(2)]
        o_ref[row_slice(r, GRID_W), :] = jnp.where(lo, outs[0], outs[1])

    per = ATTN_ROWS_PER_STEP
    groups = rows // per

    def step(j, parity, do_scores, do_softmax, do_values):
        for k in range(per):
            if do_scores:
                store_scores(per * j + k, sc.at[parity, k], smc.at[parity, k])
            if do_softmax:
                store_softmax(sc.at[1 - parity, k], smc.at[1 - parity, k], pr.at[parity, k], pmr.at[parity, k])
            if do_values:
                weighted_values(per * (j - 2) + k, pr.at[1 - parity, k], pmr.at[1 - parity, k])

    step(0, 0, True, False, False)
    step(1, 1, True, True, False)

    def two_steps(i, _):
        step(2 * i, 0, True, True, True)
        step(2 * i + 1, 1, True, True, True)
        return 0

    lax.fori_loop(1, groups // 2, two_steps, 0)
    step(groups, 0, False, True, True)
    step(groups + 1, 1, False, False, True)


def _bias_table(rel_bias):
    c = jnp.arange(GRID_W)
    col_start = jnp.clip(c - WIN_COLS // 2, 0, GRID_W - WIN_COLS)
    col_mask = (c[None, :] >= col_start[:, None]) & (c[None, :] < col_start[:, None] + WIN_COLS)
    dc = jnp.clip(c[None, :] - c[:, None], -(WIN_COLS - 1), WIN_COLS - 1) + (WIN_COLS - 1)
    ncol = 2 * WIN_COLS - 1
    pair_bias = jnp.concatenate([rel_bias[:, :-1], rel_bias[:, 1:]], axis=-1).astype(F32)
    onehot = (dc[None] == jnp.arange(ncol)[:, None, None]).astype(F32)
    zeros = jnp.zeros_like(onehot)
    pair_onehot = jnp.concatenate([jnp.concatenate([onehot, zeros], axis=-1),
                                   jnp.concatenate([zeros, onehot], axis=-1)], axis=0)
    table = jnp.einsum('hdm,mqn->hdqn', pair_bias, pair_onehot, precision=lax.Precision.HIGHEST)
    pair_mask = jnp.concatenate([col_mask, col_mask], axis=-1)
    return jnp.where(pair_mask[None, None], table, -jnp.inf)


def _mixers(proj, proj_meta, q_norm_w, k_norm_w, rel_bias, meta_bias, conv_w, bsz, seq):
    rows = seq // GRID_W
    assert rows >= WIN_ROWS and rows % (2 * ATTN_ROWS_PER_STEP) == 0
    nk = WIN_ROWS * GRID_W
    npairs = N_HEADS // 2
    assert CONV_WIDTH // LANES == npairs
    conv_base = 3 * ATTN_WIDTH // LANES
    qw = jnp.tile(q_norm_w.astype(F32), 2).reshape(1, LANES)
    kw = jnp.tile(k_norm_w.astype(F32), 2).reshape(1, LANES)
    bias = _bias_table(rel_bias)
    mb = meta_bias.astype(F32).reshape(N_HEADS, 1, N_META)
    return pl.pallas_call(
        functools.partial(_attn_body, rows=rows),
        grid=(bsz, npairs),
        in_specs=[pl.BlockSpec((seq, LANES), lambda b, p: (b, p)),
                  pl.BlockSpec((seq, LANES), lambda b, p: (b, npairs + p)),
                  pl.BlockSpec((seq, LANES), lambda b, p: (b, 2 * npairs + p)),
                  pl.BlockSpec((N_META, LANES), lambda b, p: (0, npairs + p)),
                  pl.BlockSpec((N_META, LANES), lambda b, p: (0, 2 * npairs + p)),
                  pl.BlockSpec((1, LANES), lambda b, p: (0, 0)),
                  pl.BlockSpec((1, LANES), lambda b, p: (0, 0)),
                  pl.BlockSpec((2, 2 * WIN_ROWS - 2, GRID_W, LANES), lambda b, p: (p, 0, 0, 0)),
                  pl.BlockSpec((2, 1, N_META), lambda b, p: (p, 0, 0)),
                  pl.BlockSpec((seq, LANES), lambda b, p: (b, conv_base + p)),
                  pl.BlockSpec((seq, LANES), lambda b, p: (b, conv_base + npairs + p)),
                  pl.BlockSpec((seq, LANES), lambda b, p: (b, conv_base + 2 * npairs + p)),
                  pl.BlockSpec((N_META, LANES), lambda b, p: (0, conv_base + npairs + p)),
                  pl.BlockSpec((N_META, LANES), lambda b, p: (0, conv_base + 2 * npairs + p)),
                  pl.BlockSpec((3, LANES), lambda b, p: (0, p))],
        out_specs=[pl.BlockSpec((seq, LANES), lambda b, p: (b, p)),
                   pl.BlockSpec((seq, LANES), lambda b, p: (b, p))],
        out_shape=[jax.ShapeDtypeStruct((bsz * seq, ATTN_WIDTH), F32),
                   jax.ShapeDtypeStruct((bsz * seq, CONV_WIDTH), F32)],
        scratch_shapes=([pltpu.VMEM((seq, LANES), BF16)] * 3
                        + [pltpu.VMEM((2, ATTN_ROWS_PER_STEP, 2, GRID_W, nk), F32),
                           pltpu.VMEM((2, ATTN_ROWS_PER_STEP, 2, GRID_W, N_META), F32),
                           pltpu.VMEM((2, ATTN_ROWS_PER_STEP, 2, GRID_W, nk), BF16),
                           pltpu.VMEM((2, ATTN_ROWS_PER_STEP, 2, GRID_W, N_META), BF16)]),
        compiler_params=_params("arbitrary", "arbitrary"),
        name="mixers",
    )(proj, proj, proj, proj_meta, proj_meta, qw, kw, bias, mb,
      proj, proj, proj, proj_meta, proj_meta, conv_w.astype(F32))


def _conv_body(gb_ref, gc_ref, hc_ref, gcm_ref, hcm_ref, w_ref, o_ref):
    seq = gb_ref.shape[0]
    ch = CONV_CHUNK
    nchunks = seq // ch
    w = w_ref[...]
    u_meta_last = gcm_ref[N_META - 1:N_META, :] * hcm_ref[N_META - 1:N_META, :]
    row = lax.broadcasted_iota(jnp.int32, (ch, 1), 0)

    def chunk(i, _):
        r0 = pl.multiple_of(i * ch, ch)
        rows = pl.ds(r0, ch)
        u = gc_ref[rows, :] * hc_ref[rows, :]
        before = pl.ds(pl.multiple_of(jnp.maximum(r0 - SUBLANES, 0), SUBLANES), SUBLANES)
        after = pl.ds(pl.multiple_of(jnp.minimum(r0 + ch, seq - SUBLANES), SUBLANES), SUBLANES)
        u_before = (gc_ref[before, :] * hc_ref[before, :])[SUBLANES - 1:SUBLANES]
        u_after = (gc_ref[after, :] * hc_ref[after, :])[0:1]
        u_before = jnp.where(i == 0, u_meta_last, u_before)
        u_after = jnp.where(i == nchunks - 1, 0.0, u_after)
        u_prev = jnp.where(row == 0, u_before, pltpu.roll(u, 1, 0))
        u_next = jnp.where(row == ch - 1, u_after, pltpu.roll(u, ch - 1, 0))
        y = u_prev * w[0:1] + u * w[1:2] + u_next * w[2:3]
        o_ref[rows, :] = gb_ref[rows, :] * y
        return 0

    lax.fori_loop(0, nchunks, chunk, 0)


def _rms(x, w):
    ms = jnp.mean(x * x, axis=-1, keepdims=True)
    return x * lax.rsqrt(ms + EPS) * w


def _pack_bf16_pairs(x):
    w = x.shape[1] // 2
    lo = lax.bitcast_convert_type(x[:, :w].astype(BF16).astype(F32), jnp.uint32)
    hi = lax.bitcast_convert_type(x[:, w:].astype(BF16).astype(F32), jnp.uint32)
    return (hi & jnp.uint32(0xFFFF0000)) | (lo >> 16)


def _unpack_bf16_pairs(p):
    lo = lax.bitcast_convert_type(p << 16, F32).astype(BF16)
    hi = lax.bitcast_convert_type(p & jnp.uint32(0xFFFF0000), F32).astype(BF16)
    return lo, hi


def _outproj_body(a_ref, c_ref, x_ref, aw_ref, cw_ref, wo_ref, fw_ref, wr_ref, br_ref,
                  h1_ref, hn_ref, lg_ref):
    sub = OUTPROJ_SUB
    blocks = [pl.ds(k * sub, sub) for k in range(a_ref.shape[0] // sub)]

    def mix(rows):
        an = _rms(a_ref[rows, :], aw_ref[...]).astype(BF16)
        cn = _rms(c_ref[rows, :], cw_ref[...]).astype(BF16)
        return (jnp.dot(an, wo_ref[0:ATTN_WIDTH, :].astype(BF16), preferred_element_type=F32)
                + jnp.dot(cn, wo_ref[ATTN_WIDTH:D_MODEL, :].astype(BF16), preferred_element_type=F32))

    def finish(rows, mixed):
        h1 = x_ref[rows, :] + mixed
        h1_ref[rows, :] = h1
        hn = _rms(h1, fw_ref[...])
        hn_ref[rows, :] = _pack_bf16_pairs(hn)
        lg_ref[rows, :] = jnp.dot(hn.astype(BF16), wr_ref[...], preferred_element_type=F32) + br_ref[...]

    mixed = mix(blocks[0])
    for k, rows in enumerate(blocks):
        following = mix(blocks[k + 1]) if k + 1 < len(blocks) else None
        finish(rows, mixed)
        mixed = following


def _outproj(a, c, x2d, aw, cw, w_out, fw, w_router, b_router):
    t = x2d.shape[0]
    tm = OUTPROJ_TM
    row = lambda i: (i, 0)
    fixed = lambda i: (0, 0)
    return pl.pallas_call(
        _outproj_body,
        grid=(t // tm,),
        in_specs=[pl.BlockSpec((tm, ATTN_WIDTH), row),
                  pl.BlockSpec((tm, CONV_WIDTH), row),
                  pl.BlockSpec((tm, D_MODEL), row),
                  pl.BlockSpec((1, ATTN_WIDTH), fixed),
                  pl.BlockSpec((1, CONV_WIDTH), fixed),
                  pl.BlockSpec((D_MODEL, D_MODEL), fixed, pipeline_mode=pl.Buffered(1)),
                  pl.BlockSpec((1, D_MODEL), fixed),
                  pl.BlockSpec((D_MODEL, LANES), fixed),
                  pl.BlockSpec((1, LANES), fixed)],
        out_specs=[pl.BlockSpec((tm, D_MODEL), row),
                   pl.BlockSpec((tm, D_MODEL // 2), row),
                   pl.BlockSpec((tm, LANES), row)],
        out_shape=[jax.ShapeDtypeStruct((t, D_MODEL), F32),
                   jax.ShapeDtypeStruct((t, D_MODEL // 2), jnp.uint32),
                   jax.ShapeDtypeStruct((t, LANES), F32)],
        compiler_params=_params("arbitrary"),
        name="outproj",
    )(a, c, x2d, aw.reshape(1, -1), cw.reshape(1, -1), w_out, fw.reshape(1, -1), w_router, b_router)


def _route_body(lg_ref, idx_ref, wt_ref, cnt_ref, run_ref, seg_ref):
    final_pass = pl.program_id(0) == 1
    first_tile = pl.program_id(1) == 0

    @pl.when(first_tile & jnp.logical_not(final_pass))
    def _():
        run_ref[...] = jnp.zeros_like(run_ref)
        seg_ref[...] = jnp.zeros_like(seg_ref)

    @pl.when(first_tile & final_pass)
    def _():
        counts = run_ref[...]
        rb = float(EXPERT_ROW_BLOCK)
        padded = jnp.ceil(counts * (1.0 / rb)) * rb
        lane8 = lax.broadcasted_iota(jnp.int32, counts.shape, 1)
        ends = padded
        shift = 1
        while shift < LANES:
            ends = ends + jnp.where(lane8 >= shift, pltpu.roll(ends, shift, 1), 0.0)
            shift *= 2
        seg_ref[...] = ends - padded
        run_ref[...] = jnp.zeros_like(run_ref)

    logits = lg_ref[...]
    tm = logits.shape[0]
    lane = lax.broadcasted_iota(jnp.int32, (tm, LANES), 1)
    neg = -jnp.inf

    def first_argmax(v):
        m = jnp.max(v, axis=-1, keepdims=True)
        first = jnp.min(jnp.where(v == m, lane.astype(F32), float(LANES)), axis=-1, keepdims=True)
        return m, first.astype(jnp.int32)

    gl = jnp.where(lane < N_GROUPS, logits, neg)
    gmax, gidx = first_argmax(gl)
    g_w = 1.0 / jnp.sum(jnp.exp(gl - gmax), axis=-1, keepdims=True)
    first = N_GROUPS + gidx * EXPERTS_PER_GROUP
    el = jnp.where((lane >= first) & (lane < first + EXPERTS_PER_GROUP), logits, neg)
    m0, j0 = first_argmax(el)
    m1, j1 = first_argmax(jnp.where(lane == j0, neg, el))
    p1 = jnp.exp(m1 - m0)
    w0 = g_w / (1.0 + p1)
    w1 = g_w * p1 / (1.0 + p1)
    e0 = j0 - N_GROUPS
    e1 = j1 - N_GROUPS

    onehot = ((lane == e0) | (lane == e1)).astype(BF16)
    seen = run_ref[0:1, :]
    run = seen + jnp.sum(onehot.astype(F32), axis=0, keepdims=True)
    run_ref[...] = jnp.broadcast_to(run, run_ref.shape)

    @pl.when(final_pass)
    def _():
        tri = (lax.broadcasted_iota(jnp.int32, (tm, tm), 0)
               > lax.broadcasted_iota(jnp.int32, (tm, tm), 1)).astype(BF16)
        place = jnp.dot(tri, onehot, preferred_element_type=F32) + seen + seg_ref[0:1, :]
        d0 = jnp.sum(jnp.where(lane == e0, place, 0.0), axis=-1, keepdims=True).astype(jnp.int32)
        d1 = jnp.sum(jnp.where(lane == e1, place, 0.0), axis=-1, keepdims=True).astype(jnp.int32)
        idx_ref[...] = jnp.where(lane == 0, d0, jnp.where(lane == 1, d1, jnp.zeros_like(lane)))
        wt_ref[...] = jnp.where(lane == 0, w0, jnp.where(lane == 1, w1, 0.0))
        cnt_ref[...] = jnp.broadcast_to(run, cnt_ref.shape)


def _route(logits):
    t = logits.shape[0]
    tm = ROUTE_TM
    return pl.pallas_call(
        _route_body,
        grid=(2, t // tm),
        in_specs=[pl.BlockSpec((tm, LANES), lambda p, i: (i, 0))],
        out_specs=[pl.BlockSpec((tm, LANES), lambda p, i: (i * p, 0)),
                   pl.BlockSpec((tm, LANES), lambda p, i: (i * p, 0)),
                   pl.BlockSpec((8, LANES), lambda p, i: (0, 0))],
        out_shape=[jax.ShapeDtypeStruct((t, LANES), jnp.int32),
                   jax.ShapeDtypeStruct((t, LANES), F32),
                   jax.ShapeDtypeStruct((8, LANES), F32)],
        scratch_shapes=[pltpu.VMEM((8, LANES), F32)] * 2,
        compiler_params=_params("arbitrary", "arbitrary"),
        name="route",
    )(logits)


def _expert_body(sbe, sbs, sbn, sbr, dest, seg_fill, tail, hn_hbm, wg_ref, wu_ref, wd_ref,
                 y_hbm, x32, xb, acc, tok, gsem, osem):
    s = pl.program_id(0)
    c = pl.program_id(1)
    nsb = pl.num_programs(0)
    nch = EXPERT_FF // EXPERT_FF_CHUNK
    rb = EXPERT_ROW_BLOCK
    half = D_MODEL // 2
    n = sbn[s]
    start = sbs[s]
    nblk = n // rb
    slot = s % 2

    def for_each(count, fn):
        def body(i, _):
            fn(i)
            return 0
        lax.fori_loop(0, count, body, 0)

    def block_rows(j):
        return pl.ds(pl.multiple_of(j * rb, rb), rb)

    def gather_row(sb_first, buf, i):
        src = hn_hbm.at[pl.ds(tok[sb_first + i], 1)]
        pltpu.make_async_copy(src, x32.at[buf, pl.ds(i, 1)], gsem.at[buf]).start()

    def gather_rows(sb_first, buf, first_row, count):
        def group(g):
            for k in range(GATHER_UNROLL):
                gather_row(sb_first, buf, first_row + g * GATHER_UNROLL + k)
        for_each(count // GATHER_UNROLL, group)

    def wait_gathered(buf, count):
        rows = pl.ds(0, pl.multiple_of(count, GATHER_UNROLL))
        pltpu.make_async_copy(hn_hbm.at[rows], x32.at[buf, rows], gsem.at[buf]).wait()

    def out_copy(first_row, j):
        dst = pl.ds(pl.multiple_of(first_row + j * rb, rb), rb)
        return pltpu.make_async_copy(acc.at[block_rows(j)], y_hbm.at[dst], osem)

    @pl.when((s == 0) & (c == 0))
    def _():
        def clear(j):
            for buf in range(2):
                x32[buf, block_rows(j), :] = jnp.zeros((rb, half), jnp.uint32)
            acc[block_rows(j), :] = jnp.zeros((rb, D_MODEL), F32)
        for_each(EXPERT_CAP // rb, clear)

        def pad_rows(e):
            for k in range(GATHER_UNROLL - 1):
                tok[jnp.minimum(seg_fill[e] + k, tok.shape[0] - 1)] = 0
        for_each(N_EXPERTS, pad_rows)

        def invert(g):
            for k in range(GATHER_UNROLL):
                tok[dest[g * GATHER_UNROLL + k]] = g * (GATHER_UNROLL // TOP_K) + k // TOP_K
        for_each(dest.shape[0] // GATHER_UNROLL, invert)

        gather_rows(sbs[0], 0, 0, sbr[0])

    prev = jnp.maximum(s - 1, 0)
    nxt = jnp.minimum(s + 1, nsb - 1)
    prev_rows = jnp.where(s > 0, sbn[prev], 0)
    prev_pending = (c == 0) & (prev_rows > 0)

    @pl.when((c == 0) & (sbr[s] > 0))
    def _():
        wait_gathered(slot, sbr[s])

    def wait_prev_output():
        for_each(prev_rows // rb, lambda j: out_copy(sbs[prev], j).wait())

    @pl.when(prev_pending & (n == 0))
    def _():
        wait_prev_output()

    @pl.when((s == nsb - 1) & (c == 0))
    def _():
        first = tail[0]
        nfill = (y_hbm.shape[0] - first) // rb
        acc[0:rb, :] = jnp.zeros((rb, D_MODEL), F32)

        def fill_copy(j):
            dst = pl.ds(pl.multiple_of(first + j * rb, rb), rb)
            return pltpu.make_async_copy(acc.at[0:rb], y_hbm.at[dst], osem)

        for_each(nfill, lambda j: fill_copy(j).start())
        for_each(nfill, lambda j: fill_copy(j).wait())

    @pl.when(n > 0)
    def _():
        @pl.when(c == 0)
        def _():
            def unpack(j):
                rows = block_rows(j)
                lo, hi = _unpack_bf16_pairs(x32[slot, rows, :])
                xb[rows, 0:half] = lo
                xb[rows, half:D_MODEL] = hi
            for_each(nblk, unpack)

            @pl.when((s + 1 < nsb) & (sbr[nxt] > 0))
            def _():
                gather_rows(sbs[nxt], 1 - slot, 0, sbr[nxt])

            @pl.when(prev_pending)
            def _():
                wait_prev_output()

        def mlp(first_row, m):
            rows = pl.ds(pl.multiple_of(first_row, rb), m)
            x = xb[rows, :]
            g = jnp.dot(x, wg_ref[0].astype(BF16), preferred_element_type=F32)
            u = jnp.dot(x, wu_ref[0].astype(BF16), preferred_element_type=F32)
            h = (jax.nn.silu(g) * u).astype(BF16)
            y = jnp.dot(h, wd_ref[0].astype(BF16), preferred_element_type=F32)
            acc[rows, :] = jnp.where(c == 0, y, acc[rows, :] + y)

        done = 0
        for m in EXPERT_BLOCKS[:-1]:
            count = (n - done) // m
            for_each(count, lambda j, done=done, m=m: mlp(done + j * m, m))
            done = done + count * m

        @pl.when(done < n)
        def _():
            mlp(done, rb)

        @pl.when(c == nch - 1)
        def _():
            for_each(nblk, lambda j: out_copy(start, j).start())


def _experts(hn_packed, w_gate, w_up, w_down, sb_expert, sb_start, sb_rows, sb_real, dest, seg_fill, tail,
             n_rows):
    n_sb = sb_expert.shape[0]
    nch = EXPERT_FF // EXPERT_FF_CHUNK
    fc = EXPERT_FF_CHUNK

    def chunk(s, c, sbn):
        return jnp.where(sbn[s] > 0, c, nch - 1)

    def up_map(s, c, sbe, sbs, sbn, *_):
        return (sbe[s], 0, chunk(s, c, sbn))

    def down_map(s, c, sbe, sbs, sbn, *_):
        return (sbe[s], chunk(s, c, sbn), 0)

    grid_spec = pltpu.PrefetchScalarGridSpec(
        num_scalar_prefetch=7,
        grid=(n_sb, nch),
        in_specs=[pl.BlockSpec(memory_space=pl.ANY),
                  pl.BlockSpec((1, D_MODEL, fc), up_map),
                  pl.BlockSpec((1, D_MODEL, fc), up_map),
                  pl.BlockSpec((1, fc, D_MODEL), down_map)],
        out_specs=pl.BlockSpec(memory_space=pl.ANY),
        scratch_shapes=[pltpu.VMEM((2, EXPERT_CAP, D_MODEL // 2), jnp.uint32),
                        pltpu.VMEM((EXPERT_CAP, D_MODEL), BF16),
                        pltpu.VMEM((EXPERT_CAP, D_MODEL), F32),
                        pltpu.SMEM((n_rows,), jnp.int32),
                        pltpu.SemaphoreType.DMA((2,)),
                        pltpu.SemaphoreType.DMA(())],
    )
    return pl.pallas_call(
        _expert_body,
        grid_spec=grid_spec,
        out_shape=jax.ShapeDtypeStruct((n_rows, D_MODEL), F32),
        compiler_params=_params("arbitrary", "arbitrary"),
        name="experts",
    )(sb_expert, sb_start, sb_rows, sb_real, dest, seg_fill, tail, hn_packed, w_gate, w_up, w_down)


def _combine_body(dest, h1_ref, wt_ref, y_hbm, o_ref, g, sem):
    i = pl.program_id(0)
    nsteps = pl.num_programs(0)
    groups = h1_ref.shape[0]
    tm = groups * SUBLANES

    def issue(step, slot):
        def f(q, _):
            for u in range(SUBLANES):
                for k in range(TOP_K):
                    row = dest[(step * tm + q * SUBLANES + u) * TOP_K + k]
                    src = y_hbm.at[row >> 3, pl.ds(row & (SUBLANES - 1), 1)]
                    pltpu.make_async_copy(src, g.at[slot, k, q, pl.ds(u, 1)], sem.at[slot]).start()
            return 0
        lax.fori_loop(0, groups, f, 0)

    @pl.when(i == 0)
    def _():
        issue(0, 0)

    @pl.when(i + 1 < nsteps)
    def _():
        issue(i + 1, (i + 1) % 2)

    slot = i % 2
    for k in range(TOP_K):
        pltpu.make_async_copy(y_hbm.at[pl.ds(0, groups)], g.at[slot, k], sem.at[slot]).wait()

    w = wt_ref[...]
    o_ref[...] = h1_ref[...] + (w[:, :, 0:1] * g[slot, 0] + w[:, :, 1:2] * g[slot, 1])


def _combine(h1, wts, y_buf, dest_flat):
    t = h1.shape[0]
    groups = COMBINE_TM // SUBLANES
    by_group = lambda a: a.reshape(a.shape[0] // SUBLANES, SUBLANES, a.shape[1])
    grid_spec = pltpu.PrefetchScalarGridSpec(
        num_scalar_prefetch=1,
        grid=(t // COMBINE_TM,),
        in_specs=[pl.BlockSpec((groups, SUBLANES, D_MODEL), lambda i, d: (i, 0, 0)),
                  pl.BlockSpec((groups, SUBLANES, LANES), lambda i, d: (i, 0, 0)),
                  pl.BlockSpec(memory_space=pl.ANY)],
        out_specs=pl.BlockSpec((groups, SUBLANES, D_MODEL), lambda i, d: (i, 0, 0)),
        scratch_shapes=[pltpu.VMEM((2, TOP_K, groups, SUBLANES, D_MODEL), F32),
                        pltpu.SemaphoreType.DMA((2,))],
    )
    out = pl.pallas_call(
        _combine_body,
        grid_spec=grid_spec,
        out_shape=jax.ShapeDtypeStruct((t // SUBLANES, SUBLANES, D_MODEL), F32),
        compiler_params=_params("arbitrary"),
        name="combine",
    )(dest_flat, by_group(h1), by_group(wts), by_group(y_buf))
    return out.reshape(t, D_MODEL)


def _dispatch_tables(idx, cnt, t):
    rb, cap = EXPERT_ROW_BLOCK, EXPERT_CAP
    n_assign = t * TOP_K
    n_rows = -(-(n_assign + N_EXPERTS * (rb - 1)) // rb) * rb
    n_sb = (n_rows + N_EXPERTS * (cap - rb)) // cap
    dest = idx[:, 0:TOP_K].reshape(-1)
    counts = cnt[0, :N_EXPERTS].astype(jnp.int32)
    padded = (counts + rb - 1) // rb * rb
    seg_end = jnp.cumsum(padded)
    seg_start = (seg_end - padded).astype(jnp.int32)
    seg_fill = (seg_start + counts).astype(jnp.int32)

    sb_per_expert = (padded + cap - 1) // cap
    sb_end = jnp.cumsum(sb_per_expert)
    total = sb_end[-1]
    s = jnp.arange(n_sb, dtype=jnp.int32)
    s_eff = jnp.minimum(s, total - 1)
    e = jnp.minimum(jnp.sum(sb_end[None, :] <= s_eff[:, None], axis=1), N_EXPERTS - 1).astype(jnp.int32)
    local = s_eff - (sb_end[e] - sb_per_expert[e])
    sb_start = (seg_start[e] + local * cap).astype(jnp.int32)
    sb_rows = jnp.where(s < total, jnp.clip(padded[e] - local * cap, 0, cap), 0).astype(jnp.int32)
    gathered = (counts + GATHER_UNROLL - 1) // GATHER_UNROLL * GATHER_UNROLL
    sb_real = jnp.where(s < total, jnp.clip(gathered[e] - local * cap, 0, cap), 0).astype(jnp.int32)
    tail = seg_end[-1:].astype(jnp.int32)
    return dest, seg_fill, e, sb_start, sb_rows, sb_real, tail, n_rows


def kernel(x, meta_tokens, mix_norm_w, w_in, q_norm_w, k_norm_w, rel_bias, meta_bias, conv_w,
           attn_out_norm_w, conv_out_norm_w, w_out, ffn_norm_w, w_router_group, b_router_group,
           w_router_expert, b_router_expert, w_gate, w_up, w_down):
    bsz, seq, d = x.shape
    depth = mix_norm_w.shape[0]
    assert depth == 1 and d == D_MODEL and seq % GRID_W == 0
    t = bsz * seq
    x2d = x.reshape(t, d)
    l = 0

    proj, proj_meta = _inproj(x2d, meta_tokens.astype(x.dtype), mix_norm_w[l], w_in.reshape(d, PROJ_TOTAL))

    a, c = _mixers(proj, proj_meta, q_norm_w[l], k_norm_w[l], rel_bias[l], meta_bias[l], conv_w[l], bsz, seq)

    spare = LANES - N_GROUPS - N_EXPERTS
    w_router = jnp.concatenate([w_router_group[l].astype(F32), w_router_expert[l].astype(F32),
                                jnp.zeros((d, spare), F32)], axis=1)
    b_router = jnp.concatenate([b_router_group[l].astype(F32), b_router_expert[l].astype(F32),
                                jnp.zeros((spare,), F32)]).reshape(1, LANES)
    h1, hn, logits = _outproj(a, c, x2d, attn_out_norm_w[l], conv_out_norm_w[l], w_out.reshape(d, d),
                              ffn_norm_w[l], w_router.astype(BF16), b_router)

    idx, wts, cnt = _route(logits)
    dest, seg_fill, sb_expert, sb_start, sb_rows, sb_real, tail, n_rows = _dispatch_tables(idx, cnt, t)
    y_buf = _experts(hn, w_gate.reshape(N_EXPERTS, d, EXPERT_FF), w_up.reshape(N_EXPERTS, d, EXPERT_FF),
                     w_down.reshape(N_EXPERTS, EXPERT_FF, d), sb_expert, sb_start, sb_rows, sb_real,
                     dest, seg_fill, tail, n_rows)
    out = _combine(h1, wts, y_buf, dest)
    return out.reshape(bsz, seq, d)
```

```python
import functools

import jax
import jax.numpy as jnp
from jax import lax
from jax.experimental import pallas as pl
from jax.experimental.pallas import tpu as pltpu

F32 = jnp.float32
BF16 = jnp.bfloat16

D_MODEL = 2048
N_META = 16
GRID_W = 64
N_HEADS = 16
HEAD_DIM = 64
ATTN_WIDTH = N_HEADS * HEAD_DIM
CONV_WIDTH = D_MODEL - ATTN_WIDTH
PROJ_TOTAL = 3 * ATTN_WIDTH + 3 * CONV_WIDTH
WIN_ROWS = 8
WIN_COLS = 16
N_GROUPS = 4
EXPERTS_PER_GROUP = 8
N_EXPERTS = N_GROUPS * EXPERTS_PER_GROUP
TOP_K = 2
EXPERT_FF = 1024
EPS = 1e-6

LANES = 128
SUBLANES = 8
VMEM_LIMIT = 52 * 1024 * 1024

ATTN_ROWS_PER_STEP = 4
CONV_CHUNK = 256
INPROJ_TM = 1024
INPROJ_TN = 1024
OUTPROJ_TM = 512
OUTPROJ_SUB = 256
ROUTE_TM = 1024
EXPERT_ROW_BLOCK = 128
EXPERT_BLOCKS = (512, 256, 128)
EXPERT_CAP = 1024
EXPERT_FF_CHUNK = 512
GATHER_UNROLL = 8
COMBINE_TM = 512


def _params(*sem):
    return pltpu.CompilerParams(dimension_semantics=sem, vmem_limit_bytes=VMEM_LIMIT)


def _inproj_body(x_ref, meta_ref, nw_ref, w_ref, o_ref, om_ref):
    def normed(v):
        ms = jnp.mean(v * v, axis=-1, keepdims=True)
        return (v * lax.rsqrt(ms + EPS) * nw_ref[...]).astype(BF16)

    o_ref[...] = jnp.dot(normed(x_ref[...]), w_ref[...], preferred_element_type=F32)

    @pl.when(pl.program_id(0) == 0)
    def _():
        om_ref[...] = jnp.dot(normed(meta_ref[...]), w_ref[...], preferred_element_type=F32)


def _inproj(x2d, meta, norm_w, w_bf16):
    m = x2d.shape[0]
    tm, tn = INPROJ_TM, INPROJ_TN
    last = PROJ_TOTAL // tn - 1
    return pl.pallas_call(
        _inproj_body,
        grid=(m // tm, PROJ_TOTAL // tn),
        in_specs=[pl.BlockSpec((tm, D_MODEL), lambda i, j: (i, 0)),
                  pl.BlockSpec((N_META, D_MODEL), lambda i, j: (0, 0)),
                  pl.BlockSpec((1, D_MODEL), lambda i, j: (0, 0)),
                  pl.BlockSpec((D_MODEL, tn), lambda i, j: (0, j))],
        out_specs=[pl.BlockSpec((tm, tn), lambda i, j: (i, j)),
                   pl.BlockSpec((N_META, tn), lambda i, j: (0, jnp.where(i == 0, j, last)))],
        out_shape=[jax.ShapeDtypeStruct((m, PROJ_TOTAL), F32),
                   jax.ShapeDtypeStruct((N_META, PROJ_TOTAL), F32)],
        compiler_params=_params("arbitrary", "arbitrary"),
        name="inproj",
    )(x2d, meta, norm_w.reshape(1, D_MODEL), w_bf16)


def _head_norm(x, w, lo):
    x2 = x * x
    s_lo = jnp.sum(jnp.where(lo, x2, 0.0), axis=-1, keepdims=True)
    s_hi = jnp.sum(jnp.where(lo, 0.0, x2), axis=-1, keepdims=True)
    ms = jnp.where(lo, s_lo, s_hi) * (1.0 / HEAD_DIM)
    return x * lax.rsqrt(ms + EPS) * w


def _attn_body(q_ref, k_ref, v_ref, km_ref, vm_ref, qw_ref, kw_ref, bias_ref, mb_ref,
               gb_ref, gc_ref, hc_ref, gcm_ref, hcm_ref, cw_ref, o_ref, c_ref,
               qs, ks, vs, sc, smc, pr, pmr, *, rows):
    _conv_body(gb_ref, gc_ref, hc_ref, gcm_ref, hcm_ref, cw_ref, c_ref)
    lo = lax.broadcasted_iota(jnp.int32, (1, LANES), 1) < HEAD_DIM
    scale = HEAD_DIM ** -0.5
    chunk = 256
    seq = rows * GRID_W

    kmb = _head_norm(km_ref[...], kw_ref[...], lo).astype(BF16)
    vmb = vm_ref[...].astype(BF16)
    contract_last = (((1,), (1,)), ((), ()))
    head_masks = (lo, jnp.logical_not(lo))

    def one_head(x, h):
        return jnp.where(head_masks[h], x, jnp.zeros_like(x))

    def prep(i, _):
        sl = pl.ds(pl.multiple_of(i * chunk, chunk), chunk)
        qs[sl, :] = (_head_norm(q_ref[sl, :], qw_ref[...], lo) * scale).astype(BF16)
        ks[sl, :] = _head_norm(k_ref[sl, :], kw_ref[...], lo).astype(BF16)
        vs[sl, :] = v_ref[sl, :].astype(BF16)
        return 0

    lax.fori_loop(0, seq // chunk, prep, 0, unroll=2)
    wr = min(WIN_ROWS, rows)
    nk = wr * GRID_W

    def window_start(r):
        return jnp.clip(r - wr // 2, 0, rows - wr)

    def row_slice(r, n):
        return pl.ds(pl.multiple_of(r * GRID_W, GRID_W), n)

    def store_scores(r, s_ref, sm_ref):
        rs = window_start(r)
        si = rs - r + (WIN_ROWS - 1)
        q_r = qs[row_slice(r, GRID_W), :]
        kwin = ks[row_slice(rs, nk), :]
        for h in range(2):
            qh = one_head(q_r, h)
            bias = jnp.concatenate([bias_ref[h, si + w] for w in range(0, wr, 2)], axis=-1)
            s_ref[h] = lax.dot_general(qh, kwin, contract_last, preferred_element_type=F32) + bias
            sm_ref[h] = lax.dot_general(qh, kmb, contract_last, preferred_element_type=F32) + mb_ref[h]

    def store_softmax(s_ref, sm_ref, p_ref, pm_ref):
        for h in range(2):
            s = s_ref[h]
            sm = sm_ref[h]
            m = jnp.maximum(jnp.max(s, axis=-1, keepdims=True), jnp.max(sm, axis=-1, keepdims=True))
            p = jnp.exp(s - m)
            pm = jnp.exp(sm - m)
            inv = 1.0 / (jnp.sum(p, axis=-1, keepdims=True) + jnp.sum(pm, axis=-1, keepdims=True))
            p_ref[h] = (p * inv).astype(BF16)
            pm_ref[h] = (pm * inv).astype(BF16)

    def weighted_values(r, p_ref, pm_ref):
        vwin = vs[row_slice(window_start(r), nk), :]
        outs = [jnp.dot(p_ref[h], vwin, preferred_element_type=F32)
                + jnp.dot(pm_ref[h], vmb, preferred_element_type=F32) for h in range(2)]
        o_ref[row_slice(r, GRID_W), :] = jnp.where(lo, outs[0], outs[1])

    per = ATTN_ROWS_PER_STEP
    groups = rows // per

    def step(j, parity, do_scores, do_softmax, do_values):
        for k in range(per):
            if do_scores:
                store_scores(per * j + k, sc.at[parity, k], smc.at[parity, k])
            if do_softmax:
                store_softmax(sc.at[1 - parity, k], smc.at[1 - parity, k], pr.at[parity, k], pmr.at[parity, k])
            if do_values:
                weighted_values(per * (j - 2) + k, pr.at[1 - parity, k], pmr.at[1 - parity, k])

    step(0, 0, True, False, False)
    step(1, 1, True, True, False)

    def two_steps(i, _):
        step(2 * i, 0, True, True, True)
        step(2 * i + 1, 1, True, True, True)
        return 0

    lax.fori_loop(1, groups // 2, two_steps, 0)
    step(groups, 0, False, True, True)
    step(groups + 1, 1, False, False, True)


def _bias_table(rel_bias):
    c = jnp.arange(GRID_W)
    col_start = jnp.clip(c - WIN_COLS // 2, 0, GRID_W - WIN_COLS)
    col_mask = (c[None, :] >= col_start[:, None]) & (c[None, :] < col_start[:, None] + WIN_COLS)
    dc = jnp.clip(c[None, :] - c[:, None], -(WIN_COLS - 1), WIN_COLS - 1) + (WIN_COLS - 1)
    ncol = 2 * WIN_COLS - 1
    pair_bias = jnp.concatenate([rel_bias[:, :-1], rel_bias[:, 1:]], axis=-1).astype(F32)
    onehot = (dc[None] == jnp.arange(ncol)[:, None, None]).astype(F32)
    zeros = jnp.zeros_like(onehot)
    pair_onehot = jnp.concatenate([jnp.concatenate([onehot, zeros], axis=-1),
                                   jnp.concatenate([zeros, onehot], axis=-1)], axis=0)
    table = jnp.einsum('hdm,mqn->hdqn', pair_bias, pair_onehot, precision=lax.Precision.HIGHEST)
    pair_mask = jnp.concatenate([col_mask, col_mask], axis=-1)
    return jnp.where(pair_mask[None, None], table, -jnp.inf)


def _mixers(proj, proj_meta, q_norm_w, k_norm_w, rel_bias, meta_bias, conv_w, bsz, seq):
    rows = seq // GRID_W
    assert rows >= WIN_ROWS and rows % (2 * ATTN_ROWS_PER_STEP) == 0
    nk = WIN_ROWS * GRID_W
    npairs = N_HEADS // 2
    assert CONV_WIDTH // LANES == npairs
    conv_base = 3 * ATTN_WIDTH // LANES
    qw = jnp.tile(q_norm_w.astype(F32), 2).reshape(1, LANES)
    kw = jnp.tile(k_norm_w.astype(F32), 2).reshape(1, LANES)
    bias = _bias_table(rel_bias)
    mb = meta_bias.astype(F32).reshape(N_HEADS, 1, N_META)
    return pl.pallas_call(
        functools.partial(_attn_body, rows=rows),
        grid=(bsz, npairs),
        in_specs=[pl.BlockSpec((seq, LANES), lambda b, p: (b, p)),
                  pl.BlockSpec((seq, LANES), lambda b, p: (b, npairs + p)),
                  pl.BlockSpec((seq, LANES), lambda b, p: (b, 2 * npairs + p)),
                  pl.BlockSpec((N_META, LANES), lambda b, p: (0, npairs + p)),
                  pl.BlockSpec((N_META, LANES), lambda b, p: (0, 2 * npairs + p)),
                  pl.BlockSpec((1, LANES), lambda b, p: (0, 0)),
                  pl.BlockSpec((1, LANES), lambda b, p: (0, 0)),
                  pl.BlockSpec((2, 2 * WIN_ROWS - 2, GRID_W, LANES), lambda b, p: (p, 0, 0, 0)),
                  pl.BlockSpec((2, 1, N_META), lambda b, p: (p, 0, 0)),
                  pl.BlockSpec((seq, LANES), lambda b, p: (b, conv_base + p)),
                  pl.BlockSpec((seq, LANES), lambda b, p: (b, conv_base + npairs + p)),
                  pl.BlockSpec((seq, LANES), lambda b, p: (b, conv_base + 2 * npairs + p)),
                  pl.BlockSpec((N_META, LANES), lambda b, p: (0, conv_base + npairs + p)),
                  pl.BlockSpec((N_META, LANES), lambda b, p: (0, conv_base + 2 * npairs + p)),
                  pl.BlockSpec((3, LANES), lambda b, p: (0, p))],
        out_specs=[pl.BlockSpec((seq, LANES), lambda b, p: (b, p)),
                   pl.BlockSpec((seq, LANES), lambda b, p: (b, p))],
        out_shape=[jax.ShapeDtypeStruct((bsz * seq, ATTN_WIDTH), F32),
                   jax.ShapeDtypeStruct((bsz * seq, CONV_WIDTH), F32)],
        scratch_shapes=([pltpu.VMEM((seq, LANES), BF16)] * 3
                        + [pltpu.VMEM((2, ATTN_ROWS_PER_STEP, 2, GRID_W, nk), F32),
                           pltpu.VMEM((2, ATTN_ROWS_PER_STEP, 2, GRID_W, N_META), F32),
                           pltpu.VMEM((2, ATTN_ROWS_PER_STEP, 2, GRID_W, nk), BF16),
                           pltpu.VMEM((2, ATTN_ROWS_PER_STEP, 2, GRID_W, N_META), BF16)]),
        compiler_params=_params("arbitrary", "arbitrary"),
        name="mixers",
    )(proj, proj, proj, proj_meta, proj_meta, qw, kw, bias, mb,
      proj, proj, proj, proj_meta, proj_meta, conv_w.astype(F32))


def _conv_body(gb_ref, gc_ref, hc_ref, gcm_ref, hcm_ref, w_ref, o_ref):
    seq = gb_ref.shape[0]
    ch = CONV_CHUNK
    nchunks = seq // ch
    w = w_ref[...]
    u_meta_last = gcm_ref[N_META - 1:N_META, :] * hcm_ref[N_META - 1:N_META, :]
    row = lax.broadcasted_iota(jnp.int32, (ch, 1), 0)

    def chunk(i, _):
        r0 = pl.multiple_of(i * ch, ch)
        rows = pl.ds(r0, ch)
        u = gc_ref[rows, :] * hc_ref[rows, :]
        before = pl.ds(pl.multiple_of(jnp.maximum(r0 - SUBLANES, 0), SUBLANES), SUBLANES)
        after = pl.ds(pl.multiple_of(jnp.minimum(r0 + ch, seq - SUBLANES), SUBLANES), SUBLANES)
        u_before = (gc_ref[before, :] * hc_ref[before, :])[SUBLANES - 1:SUBLANES]
        u_after = (gc_ref[after, :] * hc_ref[after, :])[0:1]
        u_before = jnp.where(i == 0, u_meta_last, u_before)
        u_after = jnp.where(i == nchunks - 1, 0.0, u_after)
        u_prev = jnp.where(row == 0, u_before, pltpu.roll(u, 1, 0))
        u_next = jnp.where(row == ch - 1, u_after, pltpu.roll(u, ch - 1, 0))
        y = u_prev * w[0:1] + u * w[1:2] + u_next * w[2:3]
        o_ref[rows, :] = gb_ref[rows, :] * y
        return 0

    lax.fori_loop(0, nchunks, chunk, 0)


def _rms(x, w):
    ms = jnp.mean(x * x, axis=-1, keepdims=True)
    return x * lax.rsqrt(ms + EPS) * w


def _pack_bf16_pairs(x):
    w = x.shape[1] // 2
    lo = lax.bitcast_convert_type(x[:, :w].astype(BF16).astype(F32), jnp.uint32)
    hi = lax.bitcast_convert_type(x[:, w:].astype(BF16).astype(F32), jnp.uint32)
    return (hi & jnp.uint32(0xFFFF0000)) | (lo >> 16)


def _unpack_bf16_pairs(p):
    lo = lax.bitcast_convert_type(p << 16, F32).astype(BF16)
    hi = lax.bitcast_convert_type(p & jnp.uint32(0xFFFF0000), F32).astype(BF16)
    return lo, hi


def _outproj_body(a_ref, c_ref, x_ref, aw_ref, cw_ref, wo_ref, fw_ref, wr_ref, br_ref,
                  h1_ref, hn_ref, lg_ref):
    sub = OUTPROJ_SUB
    blocks = [pl.ds(k * sub, sub) for k in range(a_ref.shape[0] // sub)]

    def mix(rows):
        an = _rms(a_ref[rows, :], aw_ref[...]).astype(BF16)
        cn = _rms(c_ref[rows, :], cw_ref[...]).astype(BF16)
        return (jnp.dot(an, wo_ref[0:ATTN_WIDTH, :], preferred_element_type=F32)
                + jnp.dot(cn, wo_ref[ATTN_WIDTH:D_MODEL, :], preferred_element_type=F32))

    def finish(rows, mixed):
        h1 = x_ref[rows, :] + mixed
        h1_ref[rows, :] = h1
        hn = _rms(h1, fw_ref[...])
        hn_ref[rows, :] = _pack_bf16_pairs(hn)
        lg_ref[rows, :] = jnp.dot(hn.astype(BF16), wr_ref[...], preferred_element_type=F32) + br_ref[...]

    mixed = mix(blocks[0])
    for k, rows in enumerate(blocks):
        following = mix(blocks[k + 1]) if k + 1 < len(blocks) else None
        finish(rows, mixed)
        mixed = following


def _outproj(a, c, x2d, aw, cw, wo_bf16, fw, w_router, b_router):
    t = x2d.shape[0]
    tm = OUTPROJ_TM
    row = lambda i: (i, 0)
    fixed = lambda i: (0, 0)
    return pl.pallas_call(
        _outproj_body,
        grid=(t // tm,),
        in_specs=[pl.BlockSpec((tm, ATTN_WIDTH), row),
                  pl.BlockSpec((tm, CONV_WIDTH), row),
                  pl.BlockSpec((tm, D_MODEL), row),
                  pl.BlockSpec((1, ATTN_WIDTH), fixed),
                  pl.BlockSpec((1, CONV_WIDTH), fixed),
                  pl.BlockSpec((D_MODEL, D_MODEL), fixed, pipeline_mode=pl.Buffered(1)),
                  pl.BlockSpec((1, D_MODEL), fixed),
                  pl.BlockSpec((D_MODEL, LANES), fixed),
                  pl.BlockSpec((1, LANES), fixed)],
        out_specs=[pl.BlockSpec((tm, D_MODEL), row),
                   pl.BlockSpec((tm, D_MODEL // 2), row),
                   pl.BlockSpec((tm, LANES), row)],
        out_shape=[jax.ShapeDtypeStruct((t, D_MODEL), F32),
                   jax.ShapeDtypeStruct((t, D_MODEL // 2), jnp.uint32),
                   jax.ShapeDtypeStruct((t, LANES), F32)],
        compiler_params=_params("arbitrary"),
        name="outproj",
    )(a, c, x2d, aw.reshape(1, -1), cw.reshape(1, -1), wo_bf16, fw.reshape(1, -1), w_router, b_router)


def _route_body(lg_ref, idx_ref, wt_ref, cnt_ref, run_ref, seg_ref):
    final_pass = pl.program_id(0) == 1
    first_tile = pl.program_id(1) == 0

    @pl.when(first_tile & jnp.logical_not(final_pass))
    def _():
        run_ref[...] = jnp.zeros_like(run_ref)
        seg_ref[...] = jnp.zeros_like(seg_ref)

    @pl.when(first_tile & final_pass)
    def _():
        counts = run_ref[...]
        rb = float(EXPERT_ROW_BLOCK)
        padded = jnp.ceil(counts * (1.0 / rb)) * rb
        lane8 = lax.broadcasted_iota(jnp.int32, counts.shape, 1)
        ends = padded
        shift = 1
        while shift < LANES:
            ends = ends + jnp.where(lane8 >= shift, pltpu.roll(ends, shift, 1), 0.0)
            shift *= 2
        seg_ref[...] = ends - padded
        run_ref[...] = jnp.zeros_like(run_ref)

    logits = lg_ref[...]
    tm = logits.shape[0]
    lane = lax.broadcasted_iota(jnp.int32, (tm, LANES), 1)
    neg = -jnp.inf

    def first_argmax(v):
        m = jnp.max(v, axis=-1, keepdims=True)
        first = jnp.min(jnp.where(v == m, lane.astype(F32), float(LANES)), axis=-1, keepdims=True)
        return m, first.astype(jnp.int32)

    gl = jnp.where(lane < N_GROUPS, logits, neg)
    gmax, gidx = first_argmax(gl)
    g_w = 1.0 / jnp.sum(jnp.exp(gl - gmax), axis=-1, keepdims=True)
    first = N_GROUPS + gidx * EXPERTS_PER_GROUP
    el = jnp.where((lane >= first) & (lane < first + EXPERTS_PER_GROUP), logits, neg)
    m0, j0 = first_argmax(el)
    m1, j1 = first_argmax(jnp.where(lane == j0, neg, el))
    p1 = jnp.exp(m1 - m0)
    w0 = g_w / (1.0 + p1)
    w1 = g_w * p1 / (1.0 + p1)
    e0 = j0 - N_GROUPS
    e1 = j1 - N_GROUPS

    onehot = ((lane == e0) | (lane == e1)).astype(BF16)
    seen = run_ref[0:1, :]
    run = seen + jnp.sum(onehot.astype(F32), axis=0, keepdims=True)
    run_ref[...] = jnp.broadcast_to(run, run_ref.shape)

    @pl.when(final_pass)
    def _():
        tri = (lax.broadcasted_iota(jnp.int32, (tm, tm), 0)
               > lax.broadcasted_iota(jnp.int32, (tm, tm), 1)).astype(BF16)
        place = jnp.dot(tri, onehot, preferred_element_type=F32) + seen + seg_ref[0:1, :]
        d0 = jnp.sum(jnp.where(lane == e0, place, 0.0), axis=-1, keepdims=True).astype(jnp.int32)
        d1 = jnp.sum(jnp.where(lane == e1, place, 0.0), axis=-1, keepdims=True).astype(jnp.int32)
        idx_ref[...] = jnp.where(lane == 0, d0, jnp.where(lane == 1, d1, jnp.zeros_like(lane)))
        wt_ref[...] = jnp.where(lane == 0, w0, jnp.where(lane == 1, w1, 0.0))
        cnt_ref[...] = jnp.broadcast_to(run, cnt_ref.shape)


def _route(logits):
    t = logits.shape[0]
    tm = ROUTE_TM
    return pl.pallas_call(
        _route_body,
        grid=(2, t // tm),
        in_specs=[pl.BlockSpec((tm, LANES), lambda p, i: (i, 0))],
        out_specs=[pl.BlockSpec((tm, LANES), lambda p, i: (i * p, 0)),
                   pl.BlockSpec((tm, LANES), lambda p, i: (i * p, 0)),
                   pl.BlockSpec((8, LANES), lambda p, i: (0, 0))],
        out_shape=[jax.ShapeDtypeStruct((t, LANES), jnp.int32),
                   jax.ShapeDtypeStruct((t, LANES), F32),
                   jax.ShapeDtypeStruct((8, LANES), F32)],
        scratch_shapes=[pltpu.VMEM((8, LANES), F32)] * 2,
        compiler_params=_params("arbitrary", "arbitrary"),
        name="route",
    )(logits)


def _expert_body(sbe, sbs, sbn, sbr, dest, seg_fill, tail, hn_hbm, wg_ref, wu_ref, wd_ref,
                 y_hbm, x32, xb, acc, tok, gsem, osem):
    s = pl.program_id(0)
    c = pl.program_id(1)
    nsb = pl.num_programs(0)
    nch = EXPERT_FF // EXPERT_FF_CHUNK
    rb = EXPERT_ROW_BLOCK
    half = D_MODEL // 2
    n = sbn[s]
    start = sbs[s]
    nblk = n // rb
    slot = s % 2

    def for_each(count, fn):
        def body(i, _):
            fn(i)
            return 0
        lax.fori_loop(0, count, body, 0)

    def block_rows(j):
        return pl.ds(pl.multiple_of(j * rb, rb), rb)

    def gather_row(sb_first, buf, i):
        src = hn_hbm.at[pl.ds(tok[sb_first + i], 1)]
        pltpu.make_async_copy(src, x32.at[buf, pl.ds(i, 1)], gsem.at[buf]).start()

    def gather_rows(sb_first, buf, first_row, count):
        def group(g):
            for k in range(GATHER_UNROLL):
                gather_row(sb_first, buf, first_row + g * GATHER_UNROLL + k)
        for_each(count // GATHER_UNROLL, group)

    def wait_gathered(buf, count):
        rows = pl.ds(0, pl.multiple_of(count, GATHER_UNROLL))
        pltpu.make_async_copy(hn_hbm.at[rows], x32.at[buf, rows], gsem.at[buf]).wait()

    def out_copy(first_row, j):
        dst = pl.ds(pl.multiple_of(first_row + j * rb, rb), rb)
        return pltpu.make_async_copy(acc.at[block_rows(j)], y_hbm.at[dst], osem)

    @pl.when((s == 0) & (c == 0))
    def _():
        def clear(j):
            for buf in range(2):
                x32[buf, block_rows(j), :] = jnp.zeros((rb, half), jnp.uint32)
            acc[block_rows(j), :] = jnp.zeros((rb, D_MODEL), F32)
        for_each(EXPERT_CAP // rb, clear)

        def pad_rows(e):
            for k in range(GATHER_UNROLL - 1):
                tok[jnp.minimum(seg_fill[e] + k, tok.shape[0] - 1)] = 0
        for_each(N_EXPERTS, pad_rows)

        def invert(g):
            for k in range(GATHER_UNROLL):
                tok[dest[g * GATHER_UNROLL + k]] = g * (GATHER_UNROLL // TOP_K) + k // TOP_K
        for_each(dest.shape[0] // GATHER_UNROLL, invert)

        gather_rows(sbs[0], 0, 0, sbr[0])

    prev = jnp.maximum(s - 1, 0)
    nxt = jnp.minimum(s + 1, nsb - 1)
    prev_rows = jnp.where(s > 0, sbn[prev], 0)
    prev_pending = (c == 0) & (prev_rows > 0)

    @pl.when((c == 0) & (sbr[s] > 0))
    def _():
        wait_gathered(slot, sbr[s])

    def wait_prev_output():
        for_each(prev_rows // rb, lambda j: out_copy(sbs[prev], j).wait())

    @pl.when(prev_pending & (n == 0))
    def _():
        wait_prev_output()

    @pl.when((s == nsb - 1) & (c == 0))
    def _():
        first = tail[0]
        nfill = (y_hbm.shape[0] - first) // rb
        acc[0:rb, :] = jnp.zeros((rb, D_MODEL), F32)

        def fill_copy(j):
            dst = pl.ds(pl.multiple_of(first + j * rb, rb), rb)
            return pltpu.make_async_copy(acc.at[0:rb], y_hbm.at[dst], osem)

        for_each(nfill, lambda j: fill_copy(j).start())
        for_each(nfill, lambda j: fill_copy(j).wait())

    @pl.when(n > 0)
    def _():
        @pl.when(c == 0)
        def _():
            def unpack(j):
                rows = block_rows(j)
                lo, hi = _unpack_bf16_pairs(x32[slot, rows, :])
                xb[rows, 0:half] = lo
                xb[rows, half:D_MODEL] = hi
            for_each(nblk, unpack)

            @pl.when((s + 1 < nsb) & (sbr[nxt] > 0))
            def _():
                gather_rows(sbs[nxt], 1 - slot, 0, sbr[nxt])

            @pl.when(prev_pending)
            def _():
                wait_prev_output()

        def mlp(first_row, m):
            rows = pl.ds(pl.multiple_of(first_row, rb), m)
            x = xb[rows, :]
            g = jnp.dot(x, wg_ref[0].astype(BF16), preferred_element_type=F32)
            u = jnp.dot(x, wu_ref[0].astype(BF16), preferred_element_type=F32)
            h = (jax.nn.silu(g) * u).astype(BF16)
            y = jnp.dot(h, wd_ref[0].astype(BF16), preferred_element_type=F32)
            acc[rows, :] = jnp.where(c == 0, y, acc[rows, :] + y)

        done = 0
        for m in EXPERT_BLOCKS[:-1]:
            count = (n - done) // m
            for_each(count, lambda j, done=done, m=m: mlp(done + j * m, m))
            done = done + count * m

        @pl.when(done < n)
        def _():
            mlp(done, rb)

        @pl.when(c == nch - 1)
        def _():
            for_each(nblk, lambda j: out_copy(start, j).start())


def _experts(hn_packed, w_gate, w_up, w_down, sb_expert, sb_start, sb_rows, sb_real, dest, seg_fill, tail,
             n_rows):
    n_sb = sb_expert.shape[0]
    nch = EXPERT_FF // EXPERT_FF_CHUNK
    fc = EXPERT_FF_CHUNK

    def chunk(s, c, sbn):
        return jnp.where(sbn[s] > 0, c, nch - 1)

    def up_map(s, c, sbe, sbs, sbn, *_):
        return (sbe[s], 0, chunk(s, c, sbn))

    def down_map(s, c, sbe, sbs, sbn, *_):
        return (sbe[s], chunk(s, c, sbn), 0)

    grid_spec = pltpu.PrefetchScalarGridSpec(
        num_scalar_prefetch=7,
        grid=(n_sb, nch),
        in_specs=[pl.BlockSpec(memory_space=pl.ANY),
                  pl.BlockSpec((1, D_MODEL, fc), up_map),
                  pl.BlockSpec((1, D_MODEL, fc), up_map),
                  pl.BlockSpec((1, fc, D_MODEL), down_map)],
        out_specs=pl.BlockSpec(memory_space=pl.ANY),
        scratch_shapes=[pltpu.VMEM((2, EXPERT_CAP, D_MODEL // 2), jnp.uint32),
                        pltpu.VMEM((EXPERT_CAP, D_MODEL), BF16),
                        pltpu.VMEM((EXPERT_CAP, D_MODEL), F32),
                        pltpu.SMEM((n_rows,), jnp.int32),
                        pltpu.SemaphoreType.DMA((2,)),
                        pltpu.SemaphoreType.DMA(())],
    )
    return pl.pallas_call(
        _expert_body,
        grid_spec=grid_spec,
        out_shape=jax.ShapeDtypeStruct((n_rows, D_MODEL), F32),
        compiler_params=_params("arbitrary", "arbitrary"),
        name="experts",
    )(sb_expert, sb_start, sb_rows, sb_real, dest, seg_fill, tail, hn_packed, w_gate, w_up, w_down)


def _combine_body(dest, h1_ref, wt_ref, y_hbm, o_ref, g, sem):
    i = pl.program_id(0)
    nsteps = pl.num_programs(0)
    groups = h1_ref.shape[0]
    tm = groups * SUBLANES

    def issue(step, slot):
        def f(q, _):
            for u in range(SUBLANES):
                for k in range(TOP_K):
                    row = dest[(step * tm + q * SUBLANES + u) * TOP_K + k]
                    src = y_hbm.at[row >> 3, pl.ds(row & (SUBLANES - 1), 1)]
                    pltpu.make_async_copy(src, g.at[slot, k, q, pl.ds(u, 1)], sem.at[slot]).start()
            return 0
        lax.fori_loop(0, groups, f, 0)

    @pl.when(i == 0)
    def _():
        issue(0, 0)

    @pl.when(i + 1 < nsteps)
    def _():
        issue(i + 1, (i + 1) % 2)

    slot = i % 2
    for k in range(TOP_K):
        pltpu.make_async_copy(y_hbm.at[pl.ds(0, groups)], g.at[slot, k], sem.at[slot]).wait()

    w = wt_ref[...]
    o_ref[...] = h1_ref[...] + (w[:, :, 0:1] * g[slot, 0] + w[:, :, 1:2] * g[slot, 1])


def _combine(h1, wts, y_buf, dest_flat):
    t = h1.shape[0]
    groups = COMBINE_TM // SUBLANES
    by_group = lambda a: a.reshape(a.shape[0] // SUBLANES, SUBLANES, a.shape[1])
    grid_spec = pltpu.PrefetchScalarGridSpec(
        num_scalar_prefetch=1,
        grid=(t // COMBINE_TM,),
        in_specs=[pl.BlockSpec((groups, SUBLANES, D_MODEL), lambda i, d: (i, 0, 0)),
                  pl.BlockSpec((groups, SUBLANES, LANES), lambda i, d: (i, 0, 0)),
                  pl.BlockSpec(memory_space=pl.ANY)],
        out_specs=pl.BlockSpec((groups, SUBLANES, D_MODEL), lambda i, d: (i, 0, 0)),
        scratch_shapes=[pltpu.VMEM((2, TOP_K, groups, SUBLANES, D_MODEL), F32),
                        pltpu.SemaphoreType.DMA((2,))],
    )
    out = pl.pallas_call(
        _combine_body,
        grid_spec=grid_spec,
        out_shape=jax.ShapeDtypeStruct((t // SUBLANES, SUBLANES, D_MODEL), F32),
        compiler_params=_params("arbitrary"),
        name="combine",
    )(dest_flat, by_group(h1), by_group(wts), by_group(y_buf))
    return out.reshape(t, D_MODEL)


def _dispatch_tables(idx, cnt, t):
    rb, cap = EXPERT_ROW_BLOCK, EXPERT_CAP
    n_assign = t * TOP_K
    n_rows = -(-(n_assign + N_EXPERTS * (rb - 1)) // rb) * rb
    n_sb = (n_rows + N_EXPERTS * (cap - rb)) // cap
    dest = idx[:, 0:TOP_K].reshape(-1)
    counts = cnt[0, :N_EXPERTS].astype(jnp.int32)
    padded = (counts + rb - 1) // rb * rb
    seg_end = jnp.cumsum(padded)
    seg_start = (seg_end - padded).astype(jnp.int32)
    seg_fill = (seg_start + counts).astype(jnp.int32)

    sb_per_expert = (padded + cap - 1) // cap
    sb_end = jnp.cumsum(sb_per_expert)
    total = sb_end[-1]
    s = jnp.arange(n_sb, dtype=jnp.int32)
    s_eff = jnp.minimum(s, total - 1)
    e = jnp.minimum(jnp.sum(sb_end[None, :] <= s_eff[:, None], axis=1), N_EXPERTS - 1).astype(jnp.int32)
    local = s_eff - (sb_end[e] - sb_per_expert[e])
    sb_start = (seg_start[e] + local * cap).astype(jnp.int32)
    sb_rows = jnp.where(s < total, jnp.clip(padded[e] - local * cap, 0, cap), 0).astype(jnp.int32)
    gathered = (counts + GATHER_UNROLL - 1) // GATHER_UNROLL * GATHER_UNROLL
    sb_real = jnp.where(s < total, jnp.clip(gathered[e] - local * cap, 0, cap), 0).astype(jnp.int32)
    tail = seg_end[-1:].astype(jnp.int32)
    return dest, seg_fill, e, sb_start, sb_rows, sb_real, tail, n_rows


def kernel(x, meta_tokens, mix_norm_w, w_in, q_norm_w, k_norm_w, rel_bias, meta_bias, conv_w,
           attn_out_norm_w, conv_out_norm_w, w_out, ffn_norm_w, w_router_group, b_router_group,
           w_router_expert, b_router_expert, w_gate, w_up, w_down):
    bsz, seq, d = x.shape
    depth = mix_norm_w.shape[0]
    assert depth == 1 and d == D_MODEL and seq % GRID_W == 0
    t = bsz * seq
    x2d = x.reshape(t, d)
    l = 0

    proj, proj_meta = _inproj(x2d, meta_tokens.astype(x.dtype), mix_norm_w[l], w_in[l].astype(BF16))

    a, c = _mixers(proj, proj_meta, q_norm_w[l], k_norm_w[l], rel_bias[l], meta_bias[l], conv_w[l], bsz, seq)

    spare = LANES - N_GROUPS - N_EXPERTS
    w_router = jnp.concatenate([w_router_group[l].astype(F32), w_router_expert[l].astype(F32),
                                jnp.zeros((d, spare), F32)], axis=1)
    b_router = jnp.concatenate([b_router_group[l].astype(F32), b_router_expert[l].astype(F32),
                                jnp.zeros((spare,), F32)]).reshape(1, LANES)
    h1, hn, logits = _outproj(a, c, x2d, attn_out_norm_w[l], conv_out_norm_w[l], w_out[l].astype(BF16),
                              ffn_norm_w[l], w_router.astype(BF16), b_router)

    idx, wts, cnt = _route(logits)
    dest, seg_fill, sb_expert, sb_start, sb_rows, sb_real, tail, n_rows = _dispatch_tables(idx, cnt, t)
    y_buf = _experts(hn, w_gate.reshape(N_EXPERTS, d, EXPERT_FF), w_up.reshape(N_EXPERTS, d, EXPERT_FF),
                     w_down.reshape(N_EXPERTS, EXPERT_FF, d), sb_expert, sb_start, sb_rows, sb_real,
                     dest, seg_fill, tail, n_rows)
    out = _combine(h1, wts, y_buf, dest)
    return out.reshape(bsz, seq, d)
```

```python
import functools

import jax
import jax.numpy as jnp
from jax import lax
from jax.experimental import pallas as pl
from jax.experimental.pallas import tpu as pltpu

F32 = jnp.float32
BF16 = jnp.bfloat16

D_MODEL = 2048
N_META = 16
GRID_W = 64
N_HEADS = 16
HEAD_DIM = 64
ATTN_WIDTH = N_HEADS * HEAD_DIM
CONV_WIDTH = D_MODEL - ATTN_WIDTH
PROJ_TOTAL = 3 * ATTN_WIDTH + 3 * CONV_WIDTH
WIN_ROWS = 8
WIN_COLS = 16
N_GROUPS = 4
EXPERTS_PER_GROUP = 8
N_EXPERTS = N_GROUPS * EXPERTS_PER_GROUP
TOP_K = 2
EXPERT_FF = 1024
EPS = 1e-6

LANES = 128
SUBLANES = 8
VMEM_LIMIT = 52 * 1024 * 1024

ATTN_ROWS_PER_STEP = 4
CONV_CHUNK = 256
INPROJ_TM = 1024
INPROJ_TN = 1024
OUTPROJ_TM = 512
OUTPROJ_SUB = 256
ROUTE_TM = 1024
EXPERT_ROW_BLOCK = 128
EXPERT_BLOCKS = (512, 256, 128)
EXPERT_CAP = 1024
EXPERT_FF_CHUNK = 512
GATHER_UNROLL = 8
COMBINE_TM = 512


def _params(*sem):
    return pltpu.CompilerParams(dimension_semantics=sem, vmem_limit_bytes=VMEM_LIMIT)


def _inproj_body(x_ref, meta_ref, nw_ref, w_ref, o_ref, om_ref, wb_ref):
    def normed(v):
        ms = jnp.mean(v * v, axis=-1, keepdims=True)
        return (v * lax.rsqrt(ms + EPS) * nw_ref[...]).astype(BF16)

    @pl.when(pl.program_id(1) == 0)
    def _():
        wb_ref[...] = w_ref[...].astype(BF16)
        om_ref[...] = jnp.dot(normed(meta_ref[...]), wb_ref[...], preferred_element_type=F32)

    o_ref[...] = jnp.dot(normed(x_ref[...]), wb_ref[...], preferred_element_type=F32)


def _inproj(x2d, meta, norm_w, w):
    m = x2d.shape[0]
    tm, tn = INPROJ_TM, INPROJ_TN
    return pl.pallas_call(
        _inproj_body,
        grid=(PROJ_TOTAL // tn, m // tm),
        in_specs=[pl.BlockSpec((tm, D_MODEL), lambda j, i: (i, 0)),
                  pl.BlockSpec((N_META, D_MODEL), lambda j, i: (0, 0)),
                  pl.BlockSpec((1, D_MODEL), lambda j, i: (0, 0)),
                  pl.BlockSpec((D_MODEL, tn), lambda j, i: (0, j))],
        out_specs=[pl.BlockSpec((tm, tn), lambda j, i: (i, j)),
                   pl.BlockSpec((N_META, tn), lambda j, i: (0, j))],
        out_shape=[jax.ShapeDtypeStruct((m, PROJ_TOTAL), F32),
                   jax.ShapeDtypeStruct((N_META, PROJ_TOTAL), F32)],
        scratch_shapes=[pltpu.VMEM((D_MODEL, tn), BF16)],
        compiler_params=_params("arbitrary", "arbitrary"),
        name="inproj",
    )(x2d, meta, norm_w.reshape(1, D_MODEL), w)


def _head_norm(x, w, lo):
    x2 = x * x
    s_lo = jnp.sum(jnp.where(lo, x2, 0.0), axis=-1, keepdims=True)
    s_hi = jnp.sum(jnp.where(lo, 0.0, x2), axis=-1, keepdims=True)
    ms = jnp.where(lo, s_lo, s_hi) * (1.0 / HEAD_DIM)
    return x * lax.rsqrt(ms + EPS) * w


def _attn_body(q_ref, k_ref, v_ref, km_ref, vm_ref, qw_ref, kw_ref, bias_ref, mb_ref,
               gb_ref, gc_ref, hc_ref, gcm_ref, hcm_ref, cw_ref, o_ref, c_ref,
               qs, ks, vs, sc, smc, pr, pmr, *, rows):
    _conv_body(gb_ref, gc_ref, hc_ref, gcm_ref, hcm_ref, cw_ref, c_ref)
    lo = lax.broadcasted_iota(jnp.int32, (1, LANES), 1) < HEAD_DIM
    scale = HEAD_DIM ** -0.5
    chunk = 256
    seq = rows * GRID_W

    kmb = _head_norm(km_ref[...], kw_ref[...], lo).astype(BF16)
    vmb = vm_ref[...].astype(BF16)
    contract_last = (((1,), (1,)), ((), ()))
    head_masks = (lo, jnp.logical_not(lo))

    def one_head(x, h):
        return jnp.where(head_masks[h], x, jnp.zeros_like(x))

    def prep(i, _):
        sl = pl.ds(pl.multiple_of(i * chunk, chunk), chunk)
        qs[sl, :] = (_head_norm(q_ref[sl, :], qw_ref[...], lo) * scale).astype(BF16)
        ks[sl, :] = _head_norm(k_ref[sl, :], kw_ref[...], lo).astype(BF16)
        vs[sl, :] = v_ref[sl, :].astype(BF16)
        return 0

    lax.fori_loop(0, seq // chunk, prep, 0, unroll=2)
    wr = min(WIN_ROWS, rows)
    nk = wr * GRID_W

    def window_start(r):
        return jnp.clip(r - wr // 2, 0, rows - wr)

    def row_slice(r, n):
        return pl.ds(pl.multiple_of(r * GRID_W, GRID_W), n)

    def store_scores(r, s_ref, sm_ref):
        rs = window_start(r)
        si = rs - r + (WIN_ROWS - 1)
        q_r = qs[row_slice(r, GRID_W), :]
        kwin = ks[row_slice(rs, nk), :]
        for h in range(2):
            qh = one_head(q_r, h)
            bias = jnp.concatenate([bias_ref[h, si + w] for w in range(0, wr, 2)], axis=-1)
            s_ref[h] = lax.dot_general(qh, kwin, contract_last, preferred_element_type=F32) + bias
            sm_ref[h] = lax.dot_general(qh, kmb, contract_last, preferred_element_type=F32) + mb_ref[h]

    def store_softmax(s_ref, sm_ref, p_ref, pm_ref):
        for h in range(2):
            s = s_ref[h]
            sm = sm_ref[h]
            m = jnp.maximum(jnp.max(s, axis=-1, keepdims=True), jnp.max(sm, axis=-1, keepdims=True))
            p = jnp.exp(s - m)
            pm = jnp.exp(sm - m)
            inv = 1.0 / (jnp.sum(p, axis=-1, keepdims=True) + jnp.sum(pm, axis=-1, keepdims=True))
            p_ref[h] = (p * inv).astype(BF16)
            pm_ref[h] = (pm * inv).astype(BF16)

    def weighted_values(r, p_ref, pm_ref):
        vwin = vs[row_slice(window_start(r), nk), :]
        outs = [jnp.dot(p_ref[h], vwin, preferred_element_type=F32)
                + jnp.dot(pm_ref[h], vmb, preferred_element_type=F32) for h in range(2)]
        o_ref[row_slice(r, GRID_W), :] = jnp.where(lo, outs[0], outs[1])

    per = ATTN_ROWS_PER_STEP
    groups = rows // per

    def step(j, parity, do_scores, do_softmax, do_values):
        for k in range(per):
            if do_scores:
                store_scores(per * j + k, sc.at[parity, k], smc.at[parity, k])
            if do_softmax:
                store_softmax(sc.at[1 - parity, k], smc.at[1 - parity, k], pr.at[parity, k], pmr.at[parity, k])
            if do_values:
                weighted_values(per * (j - 2) + k, pr.at[1 - parity, k], pmr.at[1 - parity, k])

    step(0, 0, True, False, False)
    step(1, 1, True, True, False)

    def two_steps(i, _):
        step(2 * i, 0, True, True, True)
        step(2 * i + 1, 1, True, True, True)
        return 0

    lax.fori_loop(1, groups // 2, two_steps, 0)
    step(groups, 0, False, True, True)
    step(groups + 1, 1, False, False, True)


def _bias_table(rel_bias):
    c = jnp.arange(GRID_W)
    col_start = jnp.clip(c - WIN_COLS // 2, 0, GRID_W - WIN_COLS)
    col_mask = (c[None, :] >= col_start[:, None]) & (c[None, :] < col_start[:, None] + WIN_COLS)
    dc = jnp.clip(c[None, :] - c[:, None], -(WIN_COLS - 1), WIN_COLS - 1) + (WIN_COLS - 1)
    ncol = 2 * WIN_COLS - 1
    pair_bias = jnp.concatenate([rel_bias[:, :-1], rel_bias[:, 1:]], axis=-1).astype(F32)
    onehot = (dc[None] == jnp.arange(ncol)[:, None, None]).astype(F32)
    zeros = jnp.zeros_like(onehot)
    pair_onehot = jnp.concatenate([jnp.concatenate([onehot, zeros], axis=-1),
                                   jnp.concatenate([zeros, onehot], axis=-1)], axis=0)
    table = jnp.einsum('hdm,mqn->hdqn', pair_bias, pair_onehot, precision=lax.Precision.HIGHEST)
    pair_mask = jnp.concatenate([col_mask, col_mask], axis=-1)
    return jnp.where(pair_mask[None, None], table, -jnp.inf)


def _mixers(proj, proj_meta, q_norm_w, k_norm_w, rel_bias, meta_bias, conv_w, bsz, seq):
    rows = seq // GRID_W
    assert rows >= WIN_ROWS and rows % (2 * ATTN_ROWS_PER_STEP) == 0
    nk = WIN_ROWS * GRID_W
    npairs = N_HEADS // 2
    assert CONV_WIDTH // LANES == npairs
    conv_base = 3 * ATTN_WIDTH // LANES
    qw = jnp.tile(q_norm_w.astype(F32), 2).reshape(1, LANES)
    kw = jnp.tile(k_norm_w.astype(F32), 2).reshape(1, LANES)
    bias = _bias_table(rel_bias)
    mb = meta_bias.astype(F32).reshape(N_HEADS, 1, N_META)
    return pl.pallas_call(
        functools.partial(_attn_body, rows=rows),
        grid=(bsz, npairs),
        in_specs=[pl.BlockSpec((seq, LANES), lambda b, p: (b, p)),
                  pl.BlockSpec((seq, LANES), lambda b, p: (b, npairs + p)),
                  pl.BlockSpec((seq, LANES), lambda b, p: (b, 2 * npairs + p)),
                  pl.BlockSpec((N_META, LANES), lambda b, p: (0, npairs + p)),
                  pl.BlockSpec((N_META, LANES), lambda b, p: (0, 2 * npairs + p)),
                  pl.BlockSpec((1, LANES), lambda b, p: (0, 0)),
                  pl.BlockSpec((1, LANES), lambda b, p: (0, 0)),
                  pl.BlockSpec((2, 2 * WIN_ROWS - 2, GRID_W, LANES), lambda b, p: (p, 0, 0, 0)),
                  pl.BlockSpec((2, 1, N_META), lambda b, p: (p, 0, 0)),
                  pl.BlockSpec((seq, LANES), lambda b, p: (b, conv_base + p)),
                  pl.BlockSpec((seq, LANES), lambda b, p: (b, conv_base + npairs + p)),
                  pl.BlockSpec((seq, LANES), lambda b, p: (b, conv_base + 2 * npairs + p)),
                  pl.BlockSpec((N_META, LANES), lambda b, p: (0, conv_base + npairs + p)),
                  pl.BlockSpec((N_META, LANES), lambda b, p: (0, conv_base + 2 * npairs + p)),
                  pl.BlockSpec((3, LANES), lambda b, p: (0, p))],
        out_specs=[pl.BlockSpec((seq, LANES), lambda b, p: (b, p)),
                   pl.BlockSpec((seq, LANES), lambda b, p: (b, p))],
        out_shape=[jax.ShapeDtypeStruct((bsz * seq, ATTN_WIDTH), F32),
                   jax.ShapeDtypeStruct((bsz * seq, CONV_WIDTH), F32)],
        scratch_shapes=([pltpu.VMEM((seq, LANES), BF16)] * 3
                        + [pltpu.VMEM((2, ATTN_ROWS_PER_STEP, 2, GRID_W, nk), F32),
                           pltpu.VMEM((2, ATTN_ROWS_PER_STEP, 2, GRID_W, N_META), F32),
                           pltpu.VMEM((2, ATTN_ROWS_PER_STEP, 2, GRID_W, nk), BF16),
                           pltpu.VMEM((2, ATTN_ROWS_PER_STEP, 2, GRID_W, N_META), BF16)]),
        compiler_params=_params("arbitrary", "arbitrary"),
        name="mixers",
    )(proj, proj, proj, proj_meta, proj_meta, qw, kw, bias, mb,
      proj, proj, proj, proj_meta, proj_meta, conv_w.astype(F32))


def _conv_body(gb_ref, gc_ref, hc_ref, gcm_ref, hcm_ref, w_ref, o_ref):
    seq = gb_ref.shape[0]
    ch = CONV_CHUNK
    nchunks = seq // ch
    w = w_ref[...]
    u_meta_last = gcm_ref[N_META - 1:N_META, :] * hcm_ref[N_META - 1:N_META, :]
    row = lax.broadcasted_iota(jnp.int32, (ch, 1), 0)

    def chunk(i, _):
        r0 = pl.multiple_of(i * ch, ch)
        rows = pl.ds(r0, ch)
        u = gc_ref[rows, :] * hc_ref[rows, :]
        before = pl.ds(pl.multiple_of(jnp.maximum(r0 - SUBLANES, 0), SUBLANES), SUBLANES)
        after = pl.ds(pl.multiple_of(jnp.minimum(r0 + ch, seq - SUBLANES), SUBLANES), SUBLANES)
        u_before = (gc_ref[before, :] * hc_ref[before, :])[SUBLANES - 1:SUBLANES]
        u_after = (gc_ref[after, :] * hc_ref[after, :])[0:1]
        u_before = jnp.where(i == 0, u_meta_last, u_before)
        u_after = jnp.where(i == nchunks - 1, 0.0, u_after)
        u_prev = jnp.where(row == 0, u_before, pltpu.roll(u, 1, 0))
        u_next = jnp.where(row == ch - 1, u_after, pltpu.roll(u, ch - 1, 0))
        y = u_prev * w[0:1] + u * w[1:2] + u_next * w[2:3]
        o_ref[rows, :] = gb_ref[rows, :] * y
        return 0

    lax.fori_loop(0, nchunks, chunk, 0)


def _rms(x, w):
    ms = jnp.mean(x * x, axis=-1, keepdims=True)
    return x * lax.rsqrt(ms + EPS) * w


def _pack_bf16_pairs(x):
    w = x.shape[1] // 2
    lo = lax.bitcast_convert_type(x[:, :w].astype(BF16).astype(F32), jnp.uint32)
    hi = lax.bitcast_convert_type(x[:, w:].astype(BF16).astype(F32), jnp.uint32)
    return (hi & jnp.uint32(0xFFFF0000)) | (lo >> 16)


def _unpack_bf16_pairs(p):
    lo = lax.bitcast_convert_type(p << 16, F32).astype(BF16)
    hi = lax.bitcast_convert_type(p & jnp.uint32(0xFFFF0000), F32).astype(BF16)
    return lo, hi


def _outproj_body(a_ref, c_ref, x_ref, aw_ref, cw_ref, wo_ref, fw_ref, wr_ref, br_ref,
                  h1_ref, hn_ref, lg_ref):
    sub = OUTPROJ_SUB
    blocks = [pl.ds(k * sub, sub) for k in range(a_ref.shape[0] // sub)]

    def mix(rows):
        an = _rms(a_ref[rows, :], aw_ref[...]).astype(BF16)
        cn = _rms(c_ref[rows, :], cw_ref[...]).astype(BF16)
        return (jnp.dot(an, wo_ref[0:ATTN_WIDTH, :], preferred_element_type=F32)
                + jnp.dot(cn, wo_ref[ATTN_WIDTH:D_MODEL, :], preferred_element_type=F32))

    def finish(rows, mixed):
        h1 = x_ref[rows, :] + mixed
        h1_ref[rows, :] = h1
        hn = _rms(h1, fw_ref[...])
        hn_ref[rows, :] = _pack_bf16_pairs(hn)
        lg_ref[rows, :] = jnp.dot(hn.astype(BF16), wr_ref[...], preferred_element_type=F32) + br_ref[...]

    mixed = mix(blocks[0])
    for k, rows in enumerate(blocks):
        following = mix(blocks[k + 1]) if k + 1 < len(blocks) else None
        finish(rows, mixed)
        mixed = following


def _outproj(a, c, x2d, aw, cw, wo_bf16, fw, w_router, b_router):
    t = x2d.shape[0]
    tm = OUTPROJ_TM
    row = lambda i: (i, 0)
    fixed = lambda i: (0, 0)
    return pl.pallas_call(
        _outproj_body,
        grid=(t // tm,),
        in_specs=[pl.BlockSpec((tm, ATTN_WIDTH), row),
                  pl.BlockSpec((tm, CONV_WIDTH), row),
                  pl.BlockSpec((tm, D_MODEL), row),
                  pl.BlockSpec((1, ATTN_WIDTH), fixed),
                  pl.BlockSpec((1, CONV_WIDTH), fixed),
                  pl.BlockSpec((D_MODEL, D_MODEL), fixed, pipeline_mode=pl.Buffered(1)),
                  pl.BlockSpec((1, D_MODEL), fixed),
                  pl.BlockSpec((D_MODEL, LANES), fixed),
                  pl.BlockSpec((1, LANES), fixed)],
        out_specs=[pl.BlockSpec((tm, D_MODEL), row),
                   pl.BlockSpec((tm, D_MODEL // 2), row),
                   pl.BlockSpec((tm, LANES), row)],
        out_shape=[jax.ShapeDtypeStruct((t, D_MODEL), F32),
                   jax.ShapeDtypeStruct((t, D_MODEL // 2), jnp.uint32),
                   jax.ShapeDtypeStruct((t, LANES), F32)],
        compiler_params=_params("arbitrary"),
        name="outproj",
    )(a, c, x2d, aw.reshape(1, -1), cw.reshape(1, -1), wo_bf16, fw.reshape(1, -1), w_router, b_router)


def _route_body(lg_ref, idx_ref, wt_ref, cnt_ref, run_ref, seg_ref):
    final_pass = pl.program_id(0) == 1
    first_tile = pl.program_id(1) == 0

    @pl.when(first_tile & jnp.logical_not(final_pass))
    def _():
        run_ref[...] = jnp.zeros_like(run_ref)
        seg_ref[...] = jnp.zeros_like(seg_ref)

    @pl.when(first_tile & final_pass)
    def _():
        counts = run_ref[...]
        rb = float(EXPERT_ROW_BLOCK)
        padded = jnp.ceil(counts * (1.0 / rb)) * rb
        lane8 = lax.broadcasted_iota(jnp.int32, counts.shape, 1)
        ends = padded
        shift = 1
        while shift < LANES:
            ends = ends + jnp.where(lane8 >= shift, pltpu.roll(ends, shift, 1), 0.0)
            shift *= 2
        seg_ref[...] = ends - padded
        run_ref[...] = jnp.zeros_like(run_ref)

    logits = lg_ref[...]
    tm = logits.shape[0]
    lane = lax.broadcasted_iota(jnp.int32, (tm, LANES), 1)
    neg = -jnp.inf

    def first_argmax(v):
        m = jnp.max(v, axis=-1, keepdims=True)
        first = jnp.min(jnp.where(v == m, lane.astype(F32), float(LANES)), axis=-1, keepdims=True)
        return m, first.astype(jnp.int32)

    gl = jnp.where(lane < N_GROUPS, logits, neg)
    gmax, gidx = first_argmax(gl)
    g_w = 1.0 / jnp.sum(jnp.exp(gl - gmax), axis=-1, keepdims=True)
    first = N_GROUPS + gidx * EXPERTS_PER_GROUP
    el = jnp.where((lane >= first) & (lane < first + EXPERTS_PER_GROUP), logits, neg)
    m0, j0 = first_argmax(el)
    m1, j1 = first_argmax(jnp.where(lane == j0, neg, el))
    p1 = jnp.exp(m1 - m0)
    w0 = g_w / (1.0 + p1)
    w1 = g_w * p1 / (1.0 + p1)
    e0 = j0 - N_GROUPS
    e1 = j1 - N_GROUPS

    onehot = ((lane == e0) | (lane == e1)).astype(BF16)
    seen = run_ref[0:1, :]
    run = seen + jnp.sum(onehot.astype(F32), axis=0, keepdims=True)
    run_ref[...] = jnp.broadcast_to(run, run_ref.shape)

    @pl.when(final_pass)
    def _():
        tri = (lax.broadcasted_iota(jnp.int32, (tm, tm), 0)
               > lax.broadcasted_iota(jnp.int32, (tm, tm), 1)).astype(BF16)
        place = jnp.dot(tri, onehot, preferred_element_type=F32) + seen + seg_ref[0:1, :]
        d0 = jnp.sum(jnp.where(lane == e0, place, 0.0), axis=-1, keepdims=True).astype(jnp.int32)
        d1 = jnp.sum(jnp.where(lane == e1, place, 0.0), axis=-1, keepdims=True).astype(jnp.int32)
        idx_ref[...] = jnp.where(lane == 0, d0, jnp.where(lane == 1, d1, jnp.zeros_like(lane)))
        wt_ref[...] = jnp.where(lane == 0, w0, jnp.where(lane == 1, w1, 0.0))
        cnt_ref[...] = jnp.broadcast_to(run, cnt_ref.shape)


def _route(logits):
    t = logits.shape[0]
    tm = ROUTE_TM
    return pl.pallas_call(
        _route_body,
        grid=(2, t // tm),
        in_specs=[pl.BlockSpec((tm, LANES), lambda p, i: (i, 0))],
        out_specs=[pl.BlockSpec((tm, LANES), lambda p, i: (i * p, 0)),
                   pl.BlockSpec((tm, LANES), lambda p, i: (i * p, 0)),
                   pl.BlockSpec((8, LANES), lambda p, i: (0, 0))],
        out_shape=[jax.ShapeDtypeStruct((t, LANES), jnp.int32),
                   jax.ShapeDtypeStruct((t, LANES), F32),
                   jax.ShapeDtypeStruct((8, LANES), F32)],
        scratch_shapes=[pltpu.VMEM((8, LANES), F32)] * 2,
        compiler_params=_params("arbitrary", "arbitrary"),
        name="route",
    )(logits)


def _expert_body(sbe, sbs, sbn, sbr, dest, seg_fill, tail, hn_hbm, wg_ref, wu_ref, wd_ref,
                 y_hbm, x32, xb, acc, tok, gsem, osem):
    s = pl.program_id(0)
    c = pl.program_id(1)
    nsb = pl.num_programs(0)
    nch = EXPERT_FF // EXPERT_FF_CHUNK
    rb = EXPERT_ROW_BLOCK
    half = D_MODEL // 2
    n = sbn[s]
    start = sbs[s]
    nblk = n // rb
    slot = s % 2

    def for_each(count, fn):
        def body(i, _):
            fn(i)
            return 0
        lax.fori_loop(0, count, body, 0)

    def block_rows(j):
        return pl.ds(pl.multiple_of(j * rb, rb), rb)

    def gather_row(sb_first, buf, i):
        src = hn_hbm.at[pl.ds(tok[sb_first + i], 1)]
        pltpu.make_async_copy(src, x32.at[buf, pl.ds(i, 1)], gsem.at[buf]).start()

    def gather_rows(sb_first, buf, first_row, count):
        def group(g):
            for k in range(GATHER_UNROLL):
                gather_row(sb_first, buf, first_row + g * GATHER_UNROLL + k)
        for_each(count // GATHER_UNROLL, group)

    def wait_gathered(buf, count):
        rows = pl.ds(0, pl.multiple_of(count, GATHER_UNROLL))
        pltpu.make_async_copy(hn_hbm.at[rows], x32.at[buf, rows], gsem.at[buf]).wait()

    def out_copy(first_row, j):
        dst = pl.ds(pl.multiple_of(first_row + j * rb, rb), rb)
        return pltpu.make_async_copy(acc.at[block_rows(j)], y_hbm.at[dst], osem)

    @pl.when((s == 0) & (c == 0))
    def _():
        def clear(j):
            for buf in range(2):
                x32[buf, block_rows(j), :] = jnp.zeros((rb, half), jnp.uint32)
            acc[block_rows(j), :] = jnp.zeros((rb, D_MODEL), F32)
        for_each(EXPERT_CAP // rb, clear)

        def pad_rows(e):
            for k in range(GATHER_UNROLL - 1):
                tok[jnp.minimum(seg_fill[e] + k, tok.shape[0] - 1)] = 0
        for_each(N_EXPERTS, pad_rows)

        def invert(g):
            for k in range(GATHER_UNROLL):
                tok[dest[g * GATHER_UNROLL + k]] = g * (GATHER_UNROLL // TOP_K) + k // TOP_K
        for_each(dest.shape[0] // GATHER_UNROLL, invert)

        gather_rows(sbs[0], 0, 0, sbr[0])

    prev = jnp.maximum(s - 1, 0)
    nxt = jnp.minimum(s + 1, nsb - 1)
    prev_rows = jnp.where(s > 0, sbn[prev], 0)
    prev_pending = (c == 0) & (prev_rows > 0)

    @pl.when((c == 0) & (sbr[s] > 0))
    def _():
        wait_gathered(slot, sbr[s])

    def wait_prev_output():
        for_each(prev_rows // rb, lambda j: out_copy(sbs[prev], j).wait())

    @pl.when(prev_pending & (n == 0))
    def _():
        wait_prev_output()

    @pl.when((s == nsb - 1) & (c == 0))
    def _():
        first = tail[0]
        nfill = (y_hbm.shape[0] - first) // rb
        acc[0:rb, :] = jnp.zeros((rb, D_MODEL), F32)

        def fill_copy(j):
            dst = pl.ds(pl.multiple_of(first + j * rb, rb), rb)
            return pltpu.make_async_copy(acc.at[0:rb], y_hbm.at[dst], osem)

        for_each(nfill, lambda j: fill_copy(j).start())
        for_each(nfill, lambda j: fill_copy(j).wait())

    @pl.when(n > 0)
    def _():
        @pl.when(c == 0)
        def _():
            def unpack(j):
                rows = block_rows(j)
                lo, hi = _unpack_bf16_pairs(x32[slot, rows, :])
                xb[rows, 0:half] = lo
                xb[rows, half:D_MODEL] = hi
            for_each(nblk, unpack)

            @pl.when((s + 1 < nsb) & (sbr[nxt] > 0))
            def _():
                gather_rows(sbs[nxt], 1 - slot, 0, sbr[nxt])

            @pl.when(prev_pending)
            def _():
                wait_prev_output()

        def mlp(first_row, m):
            rows = pl.ds(pl.multiple_of(first_row, rb), m)
            x = xb[rows, :]
            g = jnp.dot(x, wg_ref[0].astype(BF16), preferred_element_type=F32)
            u = jnp.dot(x, wu_ref[0].astype(BF16), preferred_element_type=F32)
            h = (jax.nn.silu(g) * u).astype(BF16)
            y = jnp.dot(h, wd_ref[0].astype(BF16), preferred_element_type=F32)
            acc[rows, :] = jnp.where(c == 0, y, acc[rows, :] + y)

        done = 0
        for m in EXPERT_BLOCKS[:-1]:
            count = (n - done) // m
            for_each(count, lambda j, done=done, m=m: mlp(done + j * m, m))
            done = done + count * m

        @pl.when(done < n)
        def _():
            mlp(done, rb)

        @pl.when(c == nch - 1)
        def _():
            for_each(nblk, lambda j: out_copy(start, j).start())


def _experts(hn_packed, w_gate, w_up, w_down, sb_expert, sb_start, sb_rows, sb_real, dest, seg_fill, tail,
             n_rows):
    n_sb = sb_expert.shape[0]
    nch = EXPERT_FF // EXPERT_FF_CHUNK
    fc = EXPERT_FF_CHUNK

    def chunk(s, c, sbn):
        return jnp.where(sbn[s] > 0, c, nch - 1)

    def up_map(s, c, sbe, sbs, sbn, *_):
        return (sbe[s], 0, chunk(s, c, sbn))

    def down_map(s, c, sbe, sbs, sbn, *_):
        return (sbe[s], chunk(s, c, sbn), 0)

    grid_spec = pltpu.PrefetchScalarGridSpec(
        num_scalar_prefetch=7,
        grid=(n_sb, nch),
        in_specs=[pl.BlockSpec(memory_space=pl.ANY),
                  pl.BlockSpec((1, D_MODEL, fc), up_map),
                  pl.BlockSpec((1, D_MODEL, fc), up_map),
                  pl.BlockSpec((1, fc, D_MODEL), down_map)],
        out_specs=pl.BlockSpec(memory_space=pl.ANY),
        scratch_shapes=[pltpu.VMEM((2, EXPERT_CAP, D_MODEL // 2), jnp.uint32),
                        pltpu.VMEM((EXPERT_CAP, D_MODEL), BF16),
                        pltpu.VMEM((EXPERT_CAP, D_MODEL), F32),
                        pltpu.SMEM((n_rows,), jnp.int32),
                        pltpu.SemaphoreType.DMA((2,)),
                        pltpu.SemaphoreType.DMA(())],
    )
    return pl.pallas_call(
        _expert_body,
        grid_spec=grid_spec,
        out_shape=jax.ShapeDtypeStruct((n_rows, D_MODEL), F32),
        compiler_params=_params("arbitrary", "arbitrary"),
        name="experts",
    )(sb_expert, sb_start, sb_rows, sb_real, dest, seg_fill, tail, hn_packed, w_gate, w_up, w_down)


def _combine_body(dest, h1_ref, wt_ref, y_hbm, o_ref, g, sem):
    i = pl.program_id(0)
    nsteps = pl.num_programs(0)
    groups = h1_ref.shape[0]
    tm = groups * SUBLANES

    def issue(step, slot):
        def f(q, _):
            for u in range(SUBLANES):
                for k in range(TOP_K):
                    row = dest[(step * tm + q * SUBLANES + u) * TOP_K + k]
                    src = y_hbm.at[row >> 3, pl.ds(row & (SUBLANES - 1), 1)]
                    pltpu.make_async_copy(src, g.at[slot, k, q, pl.ds(u, 1)], sem.at[slot]).start(priority=k)
            return 0
        lax.fori_loop(0, groups, f, 0)

    @pl.when(i == 0)
    def _():
        issue(0, 0)

    @pl.when(i + 1 < nsteps)
    def _():
        issue(i + 1, (i + 1) % 2)

    slot = i % 2
    for k in range(TOP_K):
        pltpu.make_async_copy(y_hbm.at[pl.ds(0, groups)], g.at[slot, k], sem.at[slot]).wait()

    w = wt_ref[...]
    o_ref[...] = h1_ref[...] + (w[:, :, 0:1] * g[slot, 0] + w[:, :, 1:2] * g[slot, 1])


def _combine(h1, wts, y_buf, dest_flat):
    t = h1.shape[0]
    groups = COMBINE_TM // SUBLANES
    by_group = lambda a: a.reshape(a.shape[0] // SUBLANES, SUBLANES, a.shape[1])
    grid_spec = pltpu.PrefetchScalarGridSpec(
        num_scalar_prefetch=1,
        grid=(t // COMBINE_TM,),
        in_specs=[pl.BlockSpec((groups, SUBLANES, D_MODEL), lambda i, d: (i, 0, 0)),
                  pl.BlockSpec((groups, SUBLANES, LANES), lambda i, d: (i, 0, 0)),
                  pl.BlockSpec(memory_space=pl.ANY)],
        out_specs=pl.BlockSpec((groups, SUBLANES, D_MODEL), lambda i, d: (i, 0, 0)),
        scratch_shapes=[pltpu.VMEM((2, TOP_K, groups, SUBLANES, D_MODEL), F32),
                        pltpu.SemaphoreType.DMA((2,))],
    )
    out = pl.pallas_call(
        _combine_body,
        grid_spec=grid_spec,
        out_shape=jax.ShapeDtypeStruct((t // SUBLANES, SUBLANES, D_MODEL), F32),
        compiler_params=_params("arbitrary"),
        name="combine",
    )(dest_flat, by_group(h1), by_group(wts), by_group(y_buf))
    return out.reshape(t, D_MODEL)


def _dispatch_tables(idx, cnt, t):
    rb, cap = EXPERT_ROW_BLOCK, EXPERT_CAP
    n_assign = t * TOP_K
    n_rows = -(-(n_assign + N_EXPERTS * (rb - 1)) // rb) * rb
    n_sb = (n_rows + N_EXPERTS * (cap - rb)) // cap
    dest = idx[:, 0:TOP_K].reshape(-1)
    counts = cnt[0, :N_EXPERTS].astype(jnp.int32)
    padded = (counts + rb - 1) // rb * rb
    seg_end = jnp.cumsum(padded)
    seg_start = (seg_end - padded).astype(jnp.int32)
    seg_fill = (seg_start + counts).astype(jnp.int32)

    sb_per_expert = (padded + cap - 1) // cap
    sb_end = jnp.cumsum(sb_per_expert)
    total = sb_end[-1]
    s = jnp.arange(n_sb, dtype=jnp.int32)
    s_eff = jnp.minimum(s, total - 1)
    e = jnp.minimum(jnp.sum(sb_end[None, :] <= s_eff[:, None], axis=1), N_EXPERTS - 1).astype(jnp.int32)
    local = s_eff - (sb_end[e] - sb_per_expert[e])
    sb_start = (seg_start[e] + local * cap).astype(jnp.int32)
    sb_rows = jnp.where(s < total, jnp.clip(padded[e] - local * cap, 0, cap), 0).astype(jnp.int32)
    gathered = (counts + GATHER_UNROLL - 1) // GATHER_UNROLL * GATHER_UNROLL
    sb_real = jnp.where(s < total, jnp.clip(gathered[e] - local * cap, 0, cap), 0).astype(jnp.int32)
    tail = seg_end[-1:].astype(jnp.int32)
    return dest, seg_fill, e, sb_start, sb_rows, sb_real, tail, n_rows


def kernel(x, meta_tokens, mix_norm_w, w_in, q_norm_w, k_norm_w, rel_bias, meta_bias, conv_w,
           attn_out_norm_w, conv_out_norm_w, w_out, ffn_norm_w, w_router_group, b_router_group,
           w_router_expert, b_router_expert, w_gate, w_up, w_down):
    bsz, seq, d = x.shape
    depth = mix_norm_w.shape[0]
    assert depth == 1 and d == D_MODEL and seq % GRID_W == 0
    t = bsz * seq
    x2d = x.reshape(t, d)
    l = 0

    proj, proj_meta = _inproj(x2d, meta_tokens.astype(x.dtype), mix_norm_w[l], w_in.reshape(d, PROJ_TOTAL))

    a, c = _mixers(proj, proj_meta, q_norm_w[l], k_norm_w[l], rel_bias[l], meta_bias[l], conv_w[l], bsz, seq)

    spare = LANES - N_GROUPS - N_EXPERTS
    w_router = jnp.concatenate([w_router_group[l].astype(F32), w_router_expert[l].astype(F32),
                                jnp.zeros((d, spare), F32)], axis=1)
    b_router = jnp.concatenate([b_router_group[l].astype(F32), b_router_expert[l].astype(F32),
                                jnp.zeros((spare,), F32)]).reshape(1, LANES)
    h1, hn, logits = _outproj(a, c, x2d, attn_out_norm_w[l], conv_out_norm_w[l], w_out[l].astype(BF16),
                              ffn_norm_w[l], w_router.astype(BF16), b_router)

    idx, wts, cnt = _route(logits)
    dest, seg_fill, sb_expert, sb_start, sb_rows, sb_real, tail, n_rows = _dispatch_tables(idx, cnt, t)
    y_buf = _experts(hn, w_gate.reshape(N_EXPERTS, d, EXPERT_FF), w_up.reshape(N_EXPERTS, d, EXPERT_FF),
                     w_down.reshape(N_EXPERTS, EXPERT_FF, d), sb_expert, sb_start, sb_rows, sb_real,
                     dest, seg_fill, tail, n_rows)
    out = _combine(h1, wts, y_buf, dest)
    return out.reshape(bsz, seq, d)
```
